```python
import math
import jax
import jax.numpy as jnp
from jax import lax
import numpy as np

D_MODEL = 1024
BATCH = 32
SEQ = 2048
DEPTH = 4

N_MIXERS = 2
NORM_EPS = 1e-6

D_FF = 2816
MACARON_WEIGHT = 0.5

A_PATTERNS = ((128, 1), (512, 4), (2048, 16))
A_GROUPS = len(A_PATTERNS)
A_HEADS = 8
A_HEAD_DIM = 64
A_GROUP_WIDTH = A_HEADS * A_HEAD_DIM
A_IN_WIDTH = A_GROUPS * 3 * A_GROUP_WIDTH
A_BLOCK = 128
NEG_INF = -1e30

NUM_BUCKETS = 32
MAX_DISTANCE = 2048

B_HEADS = 8
B_HEAD_DIM = 128
B_WIDTH = B_HEADS * B_HEAD_DIM
B_IN_WIDTH = 4 * B_WIDTH + 2 * B_HEADS
B_CONV = 4
B_CHUNK = 64

N_A_LAYERS = len(range(0, DEPTH, N_MIXERS))
N_B_LAYERS = len(range(1, DEPTH, N_MIXERS))

kernel_name = "hybrid_dilated_attn_gated_deltanet_macaron"


def rms_norm(x, g):
    xf = x.astype(jnp.float32)
    y = xf * lax.rsqrt(jnp.mean(xf * xf, axis=-1, keepdims=True) + NORM_EPS)
    return (y * g.astype(jnp.float32)).astype(x.dtype)


def swiglu(h, w_gate, w_up, w_down):
    return (jax.nn.silu(h @ w_gate) * (h @ w_up)) @ w_down


def t5_bucket(distance):
    max_exact = NUM_BUCKETS // 2
    n = distance.astype(jnp.float32)
    large = max_exact + (jnp.log(jnp.maximum(n, 1.0) / max_exact)
                         / math.log(MAX_DISTANCE / max_exact) * (NUM_BUCKETS - max_exact))
    large = jnp.minimum(large.astype(jnp.int32), NUM_BUCKETS - 1)
    return jnp.where(distance < max_exact, distance, large)


def dilated_group_attention(q, k, v, bias_h, window, dilation):
    B, S, H, dh = q.shape
    L = S // dilation
    nb = -(-L // A_BLOCK)
    Lp = nb * A_BLOCK
    steps = window // dilation

    def to_blocks(t):
        t = t.reshape(B, L, dilation, H, dh).transpose(0, 2, 1, 3, 4)
        t = jnp.pad(t, ((0, 0), (0, 0), (0, Lp - L), (0, 0), (0, 0)))
        return t.reshape(B, dilation, nb, A_BLOCK, H, dh)

    def with_prev(t):
        prev = jnp.pad(t, ((0, 0), (0, 0), (1, 0), (0, 0), (0, 0), (0, 0)))[:, :, :-1]
        return jnp.concatenate([prev, t], axis=3)

    qb = to_blocks(q)
    kk = with_prev(to_blocks(k))
    vv = with_prev(to_blocks(v))

    s = jnp.einsum('bcnqhe,bcnkhe->bcnhqk', qb, kk) * (A_HEAD_DIM ** -0.5)
    q_loc = jnp.arange(A_BLOCK)[:, None]
    k_loc = jnp.arange(2 * A_BLOCK)[None, :]
    rel = q_loc + A_BLOCK - k_loc
    bias = bias_h.astype(jnp.float32)[t5_bucket(jnp.maximum(rel, 0) * dilation)]
    s = s + bias.transpose(2, 0, 1)
    k_idx = jnp.arange(nb)[:, None] * A_BLOCK + k_loc - A_BLOCK
    valid = ((rel >= 0) & (rel <= steps))[None] & (k_idx >= 0)[:, None, :]
    s = jnp.where(valid[:, None], s, NEG_INF)

    m = jnp.max(s, axis=-1, keepdims=True)
    p = jnp.exp(s - m)
    den = jnp.sum(p, axis=-1)
    o = jnp.einsum('bcnhqk,bcnkhe->bcnqhe', p, vv) / den.transpose(0, 1, 2, 4, 3)[..., None]
    lse = m[..., 0] + jnp.log(den)

    o = o.reshape(B, dilation, Lp, H, dh)[:, :, :L].transpose(0, 2, 1, 3, 4).reshape(B, S, H, dh)
    lse = lse.transpose(0, 1, 2, 4, 3).reshape(B, dilation, Lp, H)[:, :, :L]
    lse = lse.transpose(0, 2, 1, 3).reshape(B, S, H)
    return o, lse


def dilated_attention_mixer(h, w_in, rel_bias, w_out):
    B, S, _ = h.shape
    proj = (h @ w_in).astype(jnp.float32).reshape(B, S, A_GROUPS, 3, A_HEADS, A_HEAD_DIM)
    outs, lses = [], []
    for g, (window, dilation) in enumerate(A_PATTERNS):
        o, lse = dilated_group_attention(proj[:, :, g, 0], proj[:, :, g, 1], proj[:, :, g, 2],
                                         rel_bias[:, g * A_HEADS:(g + 1) * A_HEADS],
                                         window, dilation)
        outs.append(o)
        lses.append(lse)
    alpha = jax.nn.softmax(jnp.stack(lses, 0), axis=0)
    o = jnp.einsum('gbsh,gbshe->bshe', alpha, jnp.stack(outs, 0))
    return o.reshape(B, S, A_GROUP_WIDTH).astype(h.dtype) @ w_out


def causal_depthwise_conv(x, w):
    K, C = w.shape
    return lax.conv_general_dilated(x, w[:, None, :].astype(x.dtype), window_strides=(1,),
                                    padding=[(K - 1, 0)],
                                    dimension_numbers=('NWC', 'WIO', 'NWC'),
                                    feature_group_count=C)


def l2_normalize(t):
    return t * lax.rsqrt(jnp.sum(t * t, axis=-1, keepdims=True) + NORM_EPS)


def chunk_gated_delta_rule(q, k, v, g, beta):
    B, S, H, dk = q.shape
    dv = v.shape[-1]
    C = B_CHUNK
    N = S // C

    def chunks(t):
        if t.ndim == 4:
            return t.reshape(B, N, C, H, t.shape[-1]).transpose(0, 3, 1, 2, 4)
        return t.reshape(B, N, C, H).transpose(0, 3, 1, 2)

    q, k, v, g, beta = map(chunks, (q, k, v, g, beta))
    gc = jnp.cumsum(g, axis=-1)
    idx = jnp.arange(C)
    tril = idx[:, None] >= idx[None, :]
    strict = idx[:, None] > idx[None, :]
    decay = jnp.exp(jnp.where(tril, gc[..., :, None] - gc[..., None, :], -jnp.inf))

    kb = k * beta[..., None]
    lower = jnp.where(strict, jnp.einsum('bhnid,bhnjd->bhnij', kb, k) * decay, 0.0)
    t_sys = jnp.eye(C, dtype=jnp.float32) + lower
    u = lax.linalg.triangular_solve(t_sys, v * beta[..., None], left_side=True, lower=True)
    w = lax.linalg.triangular_solve(t_sys, kb * jnp.exp(gc)[..., None], left_side=True, lower=True)

    attn = jnp.einsum('bhnid,bhnjd->bhnij', q, k) * decay
    q_dec = q * jnp.exp(gc)[..., None]
    k_tail = k * jnp.exp(gc[..., -1:] - gc)[..., None]
    chunk_dec = jnp.exp(gc[..., -1])

    def step(state, inp):
        u_n, w_n, attn_n, qd_n, kt_n, cd_n = inp
        v_new = u_n - jnp.einsum('bhcd,bhde->bhce', w_n, state)
        o_n = (jnp.einsum('bhcd,bhde->bhce', qd_n, state)
               + jnp.einsum('bhij,bhje->bhie', attn_n, v_new))
        state = state * cd_n[..., None, None] + jnp.einsum('bhcd,bhce->bhde', kt_n, v_new)
        return state, o_n

    xs = tuple(jnp.moveaxis(t, 2, 0) for t in (u, w, attn, q_dec, k_tail, chunk_dec))
    state0 = jnp.zeros((B, H, dk, dv), jnp.float32)
    _, o = lax.scan(step, state0, xs)
    return o.transpose(1, 0, 3, 2, 4).reshape(B, S, H, dv)


def gated_deltanet_mixer(h, w_in, conv_w, a_log, dt_bias, norm_w, w_out):
    B, S, _ = h.shape
    proj = h @ w_in
    qkv, z, a, b = jnp.split(proj, [3 * B_WIDTH, 4 * B_WIDTH, 4 * B_WIDTH + B_HEADS], axis=-1)
    qkv = jax.nn.silu(causal_depthwise_conv(qkv, conv_w).astype(jnp.float32))
    q, k, v = jnp.split(qkv, 3, axis=-1)
    q = l2_normalize(q.reshape(B, S, B_HEADS, B_HEAD_DIM)) * (B_HEAD_DIM ** -0.5)
    k = l2_normalize(k.reshape(B, S, B_HEADS, B_HEAD_DIM))
    v = v.reshape(B, S, B_HEADS, B_HEAD_DIM)
    beta = jax.nn.sigmoid(b.astype(jnp.float32))
    g = -jnp.exp(a_log.astype(jnp.float32)) * jax.nn.softplus(a.astype(jnp.float32)
                                                              + dt_bias.astype(jnp.float32))
    o = chunk_gated_delta_rule(q, k, v, g, beta)
    o = rms_norm(o, norm_w) * jax.nn.silu(z.astype(jnp.float32).reshape(B, S, B_HEADS, B_HEAD_DIM))
    return o.reshape(B, S, B_WIDTH).astype(h.dtype) @ w_out


def _fwd_setup_inputs(seed: int = 0) -> dict:
    key = jax.random.key(seed)
    ks = jax.random.split(key, 16)
    f32 = jnp.float32

    def nrm(k, shape, scale):
        return jax.random.normal(k, shape, f32) * scale

    x = nrm(ks[0], (BATCH, SEQ, D_MODEL), 1.0)
    norm_g = 1.0 + nrm(ks[1], (DEPTH, 3, D_MODEL), 0.02)
    ffn_w_gate = nrm(ks[2], (DEPTH, 2, D_MODEL, D_FF), D_MODEL ** -0.5)
    ffn_w_up = nrm(ks[3], (DEPTH, 2, D_MODEL, D_FF), D_MODEL ** -0.5)
    ffn_w_down = nrm(ks[4], (DEPTH, 2, D_FF, D_MODEL), D_FF ** -0.5)
    rel_bias = nrm(ks[5], (NUM_BUCKETS, A_GROUPS * A_HEADS), 0.5)
    a_w_in = nrm(ks[6], (N_A_LAYERS, D_MODEL, A_IN_WIDTH), D_MODEL ** -0.5)
    a_w_out = nrm(ks[7], (N_A_LAYERS, A_GROUP_WIDTH, D_MODEL), A_GROUP_WIDTH ** -0.5)
    b_w_in = nrm(ks[8], (N_B_LAYERS, D_MODEL, B_IN_WIDTH), D_MODEL ** -0.5)
    b_conv_w = nrm(ks[9], (N_B_LAYERS, B_CONV, 3 * B_WIDTH), B_CONV ** -0.5)
    b_a_log = jnp.log(jax.random.uniform(ks[10], (N_B_LAYERS, B_HEADS), f32, 1.0, 16.0))
    dt = jnp.exp(jax.random.uniform(ks[11], (N_B_LAYERS, B_HEADS), f32,
                                    math.log(1e-3), math.log(1e-1)))
    b_dt_bias = dt + jnp.log(-jnp.expm1(-dt))
    b_norm_w = 1.0 + nrm(ks[12], (N_B_LAYERS, B_HEAD_DIM), 0.02)
    b_w_out = nrm(ks[13], (N_B_LAYERS, B_WIDTH, D_MODEL), B_WIDTH ** -0.5)
    final_g = 1.0 + nrm(ks[14], (D_MODEL,), 0.02)
    return {"x": x, "norm_g": norm_g, "ffn_w_gate": ffn_w_gate, "ffn_w_up": ffn_w_up,
            "ffn_w_down": ffn_w_down, "rel_bias": rel_bias, "a_w_in": a_w_in,
            "a_w_out": a_w_out, "b_w_in": b_w_in, "b_conv_w": b_conv_w,
            "b_a_log": b_a_log, "b_dt_bias": b_dt_bias, "b_norm_w": b_norm_w,
            "b_w_out": b_w_out, "final_g": final_g}


def _fwd_reference(x, norm_g, ffn_w_gate, ffn_w_up, ffn_w_down, rel_bias, a_w_in, a_w_out,
              b_w_in, b_conv_w, b_a_log, b_dt_bias, b_norm_w, b_w_out, final_g):
    for i in range(DEPTH):
        j = i // N_MIXERS
        x = x + MACARON_WEIGHT * swiglu(rms_norm(x, norm_g[i, 0]),
                                        ffn_w_gate[i, 0], ffn_w_up[i, 0], ffn_w_down[i, 0])
        h = rms_norm(x, norm_g[i, 1])
        if i % N_MIXERS == 0:
            x = x + dilated_attention_mixer(h, a_w_in[j], rel_bias, a_w_out[j])
        else:
            x = x + gated_deltanet_mixer(h, b_w_in[j], b_conv_w[j], b_a_log[j], b_dt_bias[j],
                                         b_norm_w[j], b_w_out[j])
        x = x + MACARON_WEIGHT * swiglu(rms_norm(x, norm_g[i, 2]),
                                        ffn_w_gate[i, 1], ffn_w_up[i, 1], ffn_w_down[i, 1])
    return rms_norm(x, final_g)


import jax as _jax
import jax.numpy as _jnp

TWIN_FORMAT = 'train_step'
FWD_PARAMS = ['x', 'norm_g', 'ffn_w_gate', 'ffn_w_up', 'ffn_w_down', 'rel_bias', 'a_w_in', 'a_w_out', 'b_w_in', 'b_conv_w', 'b_a_log', 'b_dt_bias', 'b_norm_w', 'b_w_out', 'final_g']
TWIN_WEIGHTS = ['norm_g', 'ffn_w_gate', 'ffn_w_up', 'ffn_w_down', 'rel_bias', 'a_w_in', 'a_w_out', 'b_w_in', 'b_conv_w', 'b_a_log', 'b_dt_bias', 'b_norm_w', 'b_w_out', 'final_g']
TWIN_DIFF_INPUT = 'x'
TWIN_INPUTS = ['x', 'norm_g', 'ffn_w_gate', 'ffn_w_up', 'ffn_w_down', 'rel_bias', 'a_w_in', 'a_w_out', 'b_w_in', 'b_conv_w', 'b_a_log', 'b_dt_bias', 'b_norm_w', 'b_w_out', 'final_g', 'loss_target', 'm_norm_g', 'm_ffn_w_gate', 'm_ffn_w_up', 'm_ffn_w_down', 'm_rel_bias', 'm_a_w_in', 'm_a_w_out', 'm_b_w_in', 'm_b_conv_w', 'm_b_a_log', 'm_b_dt_bias', 'm_b_norm_w', 'm_b_w_out', 'm_final_g', 'v_norm_g', 'v_ffn_w_gate', 'v_ffn_w_up', 'v_ffn_w_down', 'v_rel_bias', 'v_a_w_in', 'v_a_w_out', 'v_b_w_in', 'v_b_conv_w', 'v_b_a_log', 'v_b_dt_bias', 'v_b_norm_w', 'v_b_w_out', 'v_final_g']
TWIN_OUTPUTS = ['loss', 'grad_x', 'grad_norm_g', 'grad_ffn_w_gate', 'grad_ffn_w_up', 'grad_ffn_w_down', 'grad_rel_bias', 'grad_a_w_in', 'grad_a_w_out', 'grad_b_w_in', 'grad_b_conv_w', 'grad_b_a_log', 'grad_b_dt_bias', 'grad_b_norm_w', 'grad_b_w_out', 'grad_final_g', 'delta_norm_g', 'delta_ffn_w_gate', 'delta_ffn_w_up', 'delta_ffn_w_down', 'delta_rel_bias', 'delta_a_w_in', 'delta_a_w_out', 'delta_b_w_in', 'delta_b_conv_w', 'delta_b_a_log', 'delta_b_dt_bias', 'delta_b_norm_w', 'delta_b_w_out', 'delta_final_g', 'new_m_norm_g', 'new_m_ffn_w_gate', 'new_m_ffn_w_up', 'new_m_ffn_w_down', 'new_m_rel_bias', 'new_m_a_w_in', 'new_m_a_w_out', 'new_m_b_w_in', 'new_m_b_conv_w', 'new_m_b_a_log', 'new_m_b_dt_bias', 'new_m_b_norm_w', 'new_m_b_w_out', 'new_m_final_g', 'new_v_norm_g', 'new_v_ffn_w_gate', 'new_v_ffn_w_up', 'new_v_ffn_w_down', 'new_v_rel_bias', 'new_v_a_w_in', 'new_v_a_w_out', 'new_v_b_w_in', 'new_v_b_conv_w', 'new_v_b_a_log', 'new_v_b_dt_bias', 'new_v_b_norm_w', 'new_v_b_w_out', 'new_v_final_g']
TWIN_LEAF_KINDS = {'loss': 'loss', 'grad_x': 'grad_x', 'grad_norm_g': 'grad_w', 'grad_ffn_w_gate': 'grad_w', 'grad_ffn_w_up': 'grad_w', 'grad_ffn_w_down': 'grad_w', 'grad_rel_bias': 'grad_w', 'grad_a_w_in': 'grad_w', 'grad_a_w_out': 'grad_w', 'grad_b_w_in': 'grad_w', 'grad_b_conv_w': 'grad_w', 'grad_b_a_log': 'grad_w', 'grad_b_dt_bias': 'grad_w', 'grad_b_norm_w': 'grad_w', 'grad_b_w_out': 'grad_w', 'grad_final_g': 'grad_w', 'delta_norm_g': 'delta_w', 'delta_ffn_w_gate': 'delta_w', 'delta_ffn_w_up': 'delta_w', 'delta_ffn_w_down': 'delta_w', 'delta_rel_bias': 'delta_w', 'delta_a_w_in': 'delta_w', 'delta_a_w_out': 'delta_w', 'delta_b_w_in': 'delta_w', 'delta_b_conv_w': 'delta_w', 'delta_b_a_log': 'delta_w', 'delta_b_dt_bias': 'delta_w', 'delta_b_norm_w': 'delta_w', 'delta_b_w_out': 'delta_w', 'delta_final_g': 'delta_w', 'new_m_norm_g': 'new_m', 'new_m_ffn_w_gate': 'new_m', 'new_m_ffn_w_up': 'new_m', 'new_m_ffn_w_down': 'new_m', 'new_m_rel_bias': 'new_m', 'new_m_a_w_in': 'new_m', 'new_m_a_w_out': 'new_m', 'new_m_b_w_in': 'new_m', 'new_m_b_conv_w': 'new_m', 'new_m_b_a_log': 'new_m', 'new_m_b_dt_bias': 'new_m', 'new_m_b_norm_w': 'new_m', 'new_m_b_w_out': 'new_m', 'new_m_final_g': 'new_m', 'new_v_norm_g': 'new_v', 'new_v_ffn_w_gate': 'new_v', 'new_v_ffn_w_up': 'new_v', 'new_v_ffn_w_down': 'new_v', 'new_v_rel_bias': 'new_v', 'new_v_a_w_in': 'new_v', 'new_v_a_w_out': 'new_v', 'new_v_b_w_in': 'new_v', 'new_v_b_conv_w': 'new_v', 'new_v_b_a_log': 'new_v', 'new_v_b_dt_bias': 'new_v', 'new_v_b_norm_w': 'new_v', 'new_v_b_w_out': 'new_v', 'new_v_final_g': 'new_v'}


def _forward(args):
    return _fwd_reference(*[args[k] for k in FWD_PARAMS])


def _output_shape():
    out = _jax.eval_shape(lambda: _forward(_fwd_setup_inputs(0)))
    return out.shape, out.dtype

N_MICROBATCH = 1
ADAM_LR = 0.001
ADAM_B1 = 0.9
ADAM_B2 = 0.999
ADAM_EPS = 1e-08
ADAM_WD = 0.01
ADAM_STEP = 10
PER_EXAMPLE_BATCH_AXIS = {'x': 0, 'loss_target': 0}
SHARED_INPUTS = []
_WEIGHT_DTYPES = {'norm_g': _jnp.float32, 'ffn_w_gate': _jnp.float32, 'ffn_w_up': _jnp.float32, 'ffn_w_down': _jnp.float32, 'rel_bias': _jnp.float32, 'a_w_in': _jnp.float32, 'a_w_out': _jnp.float32, 'b_w_in': _jnp.float32, 'b_conv_w': _jnp.float32, 'b_a_log': _jnp.float32, 'b_dt_bias': _jnp.float32, 'b_norm_w': _jnp.float32, 'b_w_out': _jnp.float32, 'final_g': _jnp.float32}
MOMENT_SCALE = {'norm_g': 1.367601e-01, 'ffn_w_gate': 5.187722e-02, 'ffn_w_up': 5.021597e-02, 'ffn_w_down': 8.327450e-02, 'rel_bias': 8.258042e-02, 'a_w_in': 4.815879e-02, 'a_w_out': 6.666936e-02, 'b_w_in': 1.062052e-01, 'b_conv_w': 1.005570e-01, 'b_a_log': 6.041945e-01, 'b_dt_bias': 5.769522e-01, 'b_norm_w': 3.828486e-01, 'b_w_out': 1.265710e-01, 'final_g': 6.400594e+01}


def _to_microbatches(a, axis):
    t = _jnp.moveaxis(a, axis, 0)
    t = t.reshape((N_MICROBATCH, t.shape[0] // N_MICROBATCH) + t.shape[1:])
    return _jnp.moveaxis(t, 1, axis + 1)


def setup_inputs(seed: int = 0) -> dict:
    inp = _fwd_setup_inputs(seed)
    key = _jax.random.fold_in(_jax.random.key(seed), 7919)
    shape, _ = _output_shape()
    out = dict(inp)
    out["loss_target"] = _jax.random.normal(_jax.random.fold_in(key, 0), shape, _jnp.float32)
    for i, name in enumerate(TWIN_WEIGHTS):
        w = inp[name].astype(_jnp.float32)
        if MOMENT_SCALE is None:
            s = _jnp.sqrt(_jnp.mean(_jnp.square(w)) + 1e-30)
        else:
            s = MOMENT_SCALE[name]
        km, kv = _jax.random.split(_jax.random.fold_in(key, i + 1))
        out[name] = w
        out["m_" + name] = s * _jax.random.normal(km, w.shape, _jnp.float32)
        out["v_" + name] = (s * s) * _jax.random.uniform(kv, w.shape, _jnp.float32, 0.5, 1.5)
    if N_MICROBATCH > 1:
        for name, axis in PER_EXAMPLE_BATCH_AXIS.items():
            out[name] = _to_microbatches(out[name], axis)
    return {'x': out['x'], 'norm_g': out['norm_g'], 'ffn_w_gate': out['ffn_w_gate'], 'ffn_w_up': out['ffn_w_up'], 'ffn_w_down': out['ffn_w_down'], 'rel_bias': out['rel_bias'], 'a_w_in': out['a_w_in'], 'a_w_out': out['a_w_out'], 'b_w_in': out['b_w_in'], 'b_conv_w': out['b_conv_w'], 'b_a_log': out['b_a_log'], 'b_dt_bias': out['b_dt_bias'], 'b_norm_w': out['b_norm_w'], 'b_w_out': out['b_w_out'], 'final_g': out['final_g'], 'loss_target': out['loss_target'], 'm_norm_g': out['m_norm_g'], 'm_ffn_w_gate': out['m_ffn_w_gate'], 'm_ffn_w_up': out['m_ffn_w_up'], 'm_ffn_w_down': out['m_ffn_w_down'], 'm_rel_bias': out['m_rel_bias'], 'm_a_w_in': out['m_a_w_in'], 'm_a_w_out': out['m_a_w_out'], 'm_b_w_in': out['m_b_w_in'], 'm_b_conv_w': out['m_b_conv_w'], 'm_b_a_log': out['m_b_a_log'], 'm_b_dt_bias': out['m_b_dt_bias'], 'm_b_norm_w': out['m_b_norm_w'], 'm_b_w_out': out['m_b_w_out'], 'm_final_g': out['m_final_g'], 'v_norm_g': out['v_norm_g'], 'v_ffn_w_gate': out['v_ffn_w_gate'], 'v_ffn_w_up': out['v_ffn_w_up'], 'v_ffn_w_down': out['v_ffn_w_down'], 'v_rel_bias': out['v_rel_bias'], 'v_a_w_in': out['v_a_w_in'], 'v_a_w_out': out['v_a_w_out'], 'v_b_w_in': out['v_b_w_in'], 'v_b_conv_w': out['v_b_conv_w'], 'v_b_a_log': out['v_b_a_log'], 'v_b_dt_bias': out['v_b_dt_bias'], 'v_b_norm_w': out['v_b_norm_w'], 'v_b_w_out': out['v_b_w_out'], 'v_final_g': out['v_final_g']}


def _loss(weights, diff, rest, loss_target):
    with _jax.named_scope("forward"):
        args = {**rest, TWIN_DIFF_INPUT: diff, **{k: w.astype(_WEIGHT_DTYPES[k]) for k, w in weights.items()}}
        y = _forward(args)
    with _jax.named_scope("loss_head"):
        err = _jnp.square(y.astype(_jnp.float32) - loss_target)
        return 0.5 * _jnp.sum(_jnp.mean(err, axis=-1)) if err.ndim else 0.5 * err


def _adamw(w, g, m, v):
    m = ADAM_B1 * m + (1.0 - ADAM_B1) * g
    v = ADAM_B2 * v + (1.0 - ADAM_B2) * _jnp.square(g)
    m_hat = m / (1.0 - ADAM_B1 ** ADAM_STEP)
    v_hat = v / (1.0 - ADAM_B2 ** ADAM_STEP)
    delta = -ADAM_LR * (m_hat / (_jnp.sqrt(v_hat) + ADAM_EPS) + ADAM_WD * w)
    return delta, m, v


def reference(x, norm_g, ffn_w_gate, ffn_w_up, ffn_w_down, rel_bias, a_w_in, a_w_out, b_w_in, b_conv_w, b_a_log, b_dt_bias, b_norm_w, b_w_out, final_g, loss_target, m_norm_g, m_ffn_w_gate, m_ffn_w_up, m_ffn_w_down, m_rel_bias, m_a_w_in, m_a_w_out, m_b_w_in, m_b_conv_w, m_b_a_log, m_b_dt_bias, m_b_norm_w, m_b_w_out, m_final_g, v_norm_g, v_ffn_w_gate, v_ffn_w_up, v_ffn_w_down, v_rel_bias, v_a_w_in, v_a_w_out, v_b_w_in, v_b_conv_w, v_b_a_log, v_b_dt_bias, v_b_norm_w, v_b_w_out, v_final_g):
    given = dict(x=x, norm_g=norm_g, ffn_w_gate=ffn_w_gate, ffn_w_up=ffn_w_up, ffn_w_down=ffn_w_down, rel_bias=rel_bias, a_w_in=a_w_in, a_w_out=a_w_out, b_w_in=b_w_in, b_conv_w=b_conv_w, b_a_log=b_a_log, b_dt_bias=b_dt_bias, b_norm_w=b_norm_w, b_w_out=b_w_out, final_g=final_g, loss_target=loss_target, m_norm_g=m_norm_g, m_ffn_w_gate=m_ffn_w_gate, m_ffn_w_up=m_ffn_w_up, m_ffn_w_down=m_ffn_w_down, m_rel_bias=m_rel_bias, m_a_w_in=m_a_w_in, m_a_w_out=m_a_w_out, m_b_w_in=m_b_w_in, m_b_conv_w=m_b_conv_w, m_b_a_log=m_b_a_log, m_b_dt_bias=m_b_dt_bias, m_b_norm_w=m_b_norm_w, m_b_w_out=m_b_w_out, m_final_g=m_final_g, v_norm_g=v_norm_g, v_ffn_w_gate=v_ffn_w_gate, v_ffn_w_up=v_ffn_w_up, v_ffn_w_down=v_ffn_w_down, v_rel_bias=v_rel_bias, v_a_w_in=v_a_w_in, v_a_w_out=v_a_w_out, v_b_w_in=v_b_w_in, v_b_conv_w=v_b_conv_w, v_b_a_log=v_b_a_log, v_b_dt_bias=v_b_dt_bias, v_b_norm_w=v_b_norm_w, v_b_w_out=v_b_w_out, v_final_g=v_final_g)
    weights = {n: given[n] for n in TWIN_WEIGHTS}
    shared = {n: given[n] for n in SHARED_INPUTS}
    per_example = {n: given[n] for n in ['x']}
    grad_fn = _jax.value_and_grad(_loss, argnums=(0, 1))

    def one_microbatch(ex, loss_target):
        ex = dict(ex)
        diff = ex.pop(TWIN_DIFF_INPUT)
        return grad_fn(weights, diff, {**shared, **ex}, loss_target)

    if N_MICROBATCH == 1:
        loss, (grad_w, grad_x) = one_microbatch(per_example, given["loss_target"])
    else:
        def body(carry, xs):
            loss_sum, grad_sum = carry
            l_k, (gw_k, gx_k) = one_microbatch(xs[0], xs[1])
            with _jax.named_scope("update"):
                return (loss_sum + l_k, _jax.tree.map(_jnp.add, grad_sum, gw_k)), gx_k

        init = (_jnp.zeros((), _jnp.float32), _jax.tree.map(_jnp.zeros_like, weights))
        (loss, grad_w), grad_x = _jax.lax.scan(body, init, (per_example, given["loss_target"]))
    with _jax.named_scope("update"):
        delta_w, new_m, new_v = {}, {}, {}
        for n in TWIN_WEIGHTS:
            delta_w[n], new_m[n], new_v[n] = _adamw(weights[n], grad_w[n], given["m_" + n], given["v_" + n])
    return (loss, grad_x, *[grad_w[n] for n in TWIN_WEIGHTS], *[delta_w[n] for n in TWIN_WEIGHTS],
            *[new_m[n] for n in TWIN_WEIGHTS], *[new_v[n] for n in TWIN_WEIGHTS])
```

```python
import functools
import math

import jax
import jax.numpy as jnp
from jax import lax
from jax.experimental import pallas as pl
from jax.experimental.pallas import tpu as pltpu

F32 = jnp.float32
BF16 = jnp.bfloat16

N_DEV = 8
NORM_EPS = 1e-6
MACARON_WEIGHT = 0.5
DEPTH = 4

A_PATTERNS = ((128, 1), (512, 4), (2048, 16))
A_HEADS = 8
A_HEAD_DIM = 64
A_GROUP_WIDTH = A_HEADS * A_HEAD_DIM
A_BLOCK = 128
NEG_INF = -1e30
NUM_BUCKETS = 32
MAX_DISTANCE = 2048

B_HEADS = 8
B_HEAD_DIM = 128
B_WIDTH = B_HEADS * B_HEAD_DIM
B_CONV = 4
B_CHUNK = 64
B_AB_PAD = 128

ADAM_LR = 0.001
ADAM_B1 = 0.9
ADAM_B2 = 0.999
ADAM_EPS = 1e-08
ADAM_WD = 0.01
ADAM_STEP = 10

LANES = 128
VMEM_LIMIT = 48 * 1024 * 1024
RS_DTYPE = F32
PACK_ROWS = 2048

MESH = pl.DeviceIdType.MESH


def _cparams(sem=None):
    return pltpu.CompilerParams(dimension_semantics=sem, vmem_limit_bytes=VMEM_LIMIT)


def _sigmoid(x):
    return 0.5 * (jnp.tanh(0.5 * x) + 1.0)


def _silu(x):
    return x * _sigmoid(x)


def _softplus(x):
    return jnp.maximum(x, 0.0) + jnp.log1p(jnp.exp(-jnp.abs(x)))


def _rms(x, g):
    return x * lax.rsqrt(jnp.mean(x * x, axis=-1, keepdims=True) + NORM_EPS) * g


def _dg(a, b, ca, cb, hp):
    dims = (((ca,), (cb,)), ((), ()))
    if hp:
        return lax.dot_general(a.astype(F32), b.astype(F32), dims, precision=lax.Precision.HIGHEST,
                               preferred_element_type=F32)
    return lax.dot_general(a.astype(BF16), b.astype(BF16), dims, preferred_element_type=F32)


def _make_dots(hp):
    @jax.custom_vjp
    def nn(a, b):
        return _dg(a, b, 1, 0, hp)

    def nn_f(a, b):
        return nn(a, b), (a, b)

    def nn_b(r, ct):
        a, b = r
        return _dg(ct, b, 1, 1, hp), _dg(a, ct, 0, 0, hp)

    nn.defvjp(nn_f, nn_b)

    @jax.custom_vjp
    def nt(a, b):
        return _dg(a, b, 1, 1, hp)

    def nt_f(a, b):
        return nt(a, b), (a, b)

    def nt_b(r, ct):
        a, b = r
        return _dg(ct, b, 1, 0, hp), _dg(ct, a, 0, 0, hp)

    nt.defvjp(nt_f, nt_b)

    @jax.custom_vjp
    def tn(a, b):
        return _dg(a, b, 0, 0, hp)

    def tn_f(a, b):
        return tn(a, b), (a, b)

    def tn_b(r, ct):
        a, b = r
        return _dg(b, ct, 1, 1, hp), _dg(a, ct, 1, 0, hp)

    tn.defvjp(tn_f, tn_b)
    return nn, nt, tn


bdot_nn, bdot_nt, bdot_tn = _make_dots(False)
hdot_nn, hdot_nt, hdot_tn = _make_dots(True)


def rmsnorm_fwd(x, g, tm):
    M, D = x.shape

    def body(x_ref, g_ref, h_ref):
        h_ref[...] = _rms(x_ref[...], g_ref[...]).astype(h_ref.dtype)

    return pl.pallas_call(
        body, name="rmsnorm_fwd", grid=(M // tm,),
        in_specs=[pl.BlockSpec((tm, D), lambda i: (i, 0)), pl.BlockSpec((1, D), lambda i: (0, 0))],
        out_specs=pl.BlockSpec((tm, D), lambda i: (i, 0)),
        out_shape=jax.ShapeDtypeStruct((M, D), BF16),
        compiler_params=_cparams(("arbitrary",)),
    )(x, g)


def rmsnorm_bwd(x, g, dh, dres, tm):
    M, D = x.shape

    def body(x_ref, g_ref, dh_ref, dres_ref, dx_ref, dg_ref):
        _, vjp = jax.vjp(_rms, x_ref[...], g_ref[...])
        dx, dg = vjp(dh_ref[...])
        dx_ref[...] = dres_ref[...] + dx

        @pl.when(pl.program_id(0) == 0)
        def _():
            dg_ref[...] = jnp.zeros_like(dg_ref)

        dg_ref[...] += dg

    row = pl.BlockSpec((tm, D), lambda i: (i, 0))
    vec = pl.BlockSpec((1, D), lambda i: (0, 0))
    return pl.pallas_call(
        body, name="rmsnorm_bwd", grid=(M // tm,),
        in_specs=[row, vec, row, row], out_specs=[row, vec],
        out_shape=[jax.ShapeDtypeStruct((M, D), F32), jax.ShapeDtypeStruct((1, D), F32)],
        compiler_params=_cparams(("arbitrary",)),
    )(x, g, dh, dres)


def loss_head(x, g, target, tm):
    M, D = x.shape

    def body(x_ref, g_ref, t_ref, dx_ref, dg_ref, loss_ref):
        t = t_ref[...]

        def f(xv, gv):
            err = _rms(xv, gv) - t
            return 0.5 * jnp.sum(jnp.mean(err * err, axis=-1, keepdims=True), axis=0, keepdims=True)

        val, vjp = jax.vjp(f, x_ref[...], g_ref[...])
        dx, dg = vjp(jnp.ones((1, 1), F32))
        dx_ref[...] = dx

        @pl.when(pl.program_id(0) == 0)
        def _():
            dg_ref[...] = jnp.zeros_like(dg_ref)
            loss_ref[...] = jnp.zeros_like(loss_ref)

        dg_ref[...] += dg
        loss_ref[...] += jnp.broadcast_to(val, loss_ref.shape)

    row = pl.BlockSpec((tm, D), lambda i: (i, 0))
    vec = pl.BlockSpec((1, D), lambda i: (0, 0))
    return pl.pallas_call(
        body, name="loss_head", grid=(M // tm,),
        in_specs=[row, vec, row],
        out_specs=[row, vec, pl.BlockSpec((1, LANES), lambda i: (0, 0))],
        out_shape=[jax.ShapeDtypeStruct((M, D), F32), jax.ShapeDtypeStruct((1, D), F32),
                   jax.ShapeDtypeStruct((1, LANES), F32)],
        compiler_params=_cparams(("arbitrary",)),
    )(x, g, target)


def _pick(n, cands):
    for c in cands:
        if n % c == 0:
            return c
    return n


def mm_nn(a, b, *, out_dtype=F32, res=None, res_scale=1.0, name="mm_nn"):
    M, K = a.shape
    N = b.shape[1]
    tm = _pick(M, (512, 256, 128))
    tn = _pick(N, (1408, 1536, 1024, 512, 256, 128))

    def body(*refs):
        if res is None:
            a_ref, b_ref, o_ref = refs
        else:
            a_ref, b_ref, r_ref, o_ref = refs
        acc = _dg(a_ref[...], b_ref[...], 1, 0, False)
        if res is not None:
            acc = r_ref[...] + res_scale * acc
        o_ref[...] = acc.astype(o_ref.dtype)

    in_specs = [pl.BlockSpec((tm, K), lambda j, i: (i, 0)), pl.BlockSpec((K, tn), lambda j, i: (0, j))]
    args = [a, b]
    if res is not None:
        in_specs.append(pl.BlockSpec((tm, tn), lambda j, i: (i, j)))
        args.append(res)
    return pl.pallas_call(
        body, name=name, grid=(N // tn, M // tm), in_specs=in_specs,
        out_specs=pl.BlockSpec((tm, tn), lambda j, i: (i, j)),
        out_shape=jax.ShapeDtypeStruct((M, N), out_dtype),
        compiler_params=_cparams(("arbitrary", "arbitrary")),
    )(*args)


def mm_nt(pairs, *, out_dtype=F32, name="mm_nt"):
    M = pairs[0][0].shape[0]
    N = pairs[0][1].shape[0]
    tm = _pick(M, (512, 256, 128))
    tn = _pick(N, (1024, 512, 256, 128))
    np_ = len(pairs)

    def body(*refs):
        o_ref = refs[-1]
        acc = None
        for p in range(np_):
            t = _dg(refs[2 * p][...], refs[2 * p + 1][...], 1, 1, False)
            acc = t if acc is None else acc + t
        o_ref[...] = acc.astype(o_ref.dtype)

    in_specs, args = [], []
    for a, b in pairs:
        K = a.shape[1]
        in_specs += [pl.BlockSpec((tm, K), lambda j, i: (i, 0)), pl.BlockSpec((tn, K), lambda j, i: (j, 0))]
        args += [a, b]
    return pl.pallas_call(
        body, name=name, grid=(N // tn, M // tm), in_specs=in_specs,
        out_specs=pl.BlockSpec((tm, tn), lambda j, i: (i, j)),
        out_shape=jax.ShapeDtypeStruct((M, N), out_dtype),
        compiler_params=_cparams(("arbitrary", "arbitrary")),
    )(*args)


def mm_tn(a, b, *, name="mm_tn"):
    M, K = a.shape
    N = b.shape[1]
    tm = _pick(M, (512, 256, 128))
    tk = _pick(K, (1408, 1024, 512, 256, 128))
    tn = _pick(N, (1408, 1536, 1024, 1056, 512, 256, 128))

    def body(a_ref, b_ref, o_ref):
        @pl.when(pl.program_id(2) == 0)
        def _():
            o_ref[...] = jnp.zeros_like(o_ref)

        o_ref[...] += _dg(a_ref[...], b_ref[...], 0, 0, False)

    return pl.pallas_call(
        body, name=name, grid=(K // tk, N // tn, M // tm),
        in_specs=[pl.BlockSpec((tm, tk), lambda k, j, m: (m, k)), pl.BlockSpec((tm, tn), lambda k, j, m: (m, j))],
        out_specs=pl.BlockSpec((tk, tn), lambda k, j, m: (k, j)),
        out_shape=jax.ShapeDtypeStruct((K, N), F32),
        compiler_params=_cparams(("arbitrary", "arbitrary", "arbitrary")),
    )(a, b)


def ffn_up(h, wg, wu):
    M, K = h.shape
    N = wg.shape[1]
    tm = _pick(M, (512, 256, 128))
    tn = _pick(N, (1408, 1024, 512, 256, 128))

    def body(h_ref, wg_ref, wu_ref, g_ref, u_ref, a_ref):
        hv = h_ref[...]
        g = _dg(hv, wg_ref[...], 1, 0, False)
        u = _dg(hv, wu_ref[...], 1, 0, False)
        g_ref[...] = g
        u_ref[...] = u
        a_ref[...] = (_silu(g) * u).astype(a_ref.dtype)

    wspec = pl.BlockSpec((K, tn), lambda j, i: (0, j))
    ospec = pl.BlockSpec((tm, tn), lambda j, i: (i, j))
    return pl.pallas_call(
        body, name="ffn_up", grid=(N // tn, M // tm),
        in_specs=[pl.BlockSpec((tm, K), lambda j, i: (i, 0)), wspec, wspec],
        out_specs=[ospec, ospec, ospec],
        out_shape=[jax.ShapeDtypeStruct((M, N), F32), jax.ShapeDtypeStruct((M, N), F32),
                   jax.ShapeDtypeStruct((M, N), BF16)],
        compiler_params=_cparams(("arbitrary", "arbitrary")),
    )(h, wg, wu)


def ffn_bwd_act(dout, wd, gate, up):
    M, D = dout.shape
    N = wd.shape[0]
    tm = _pick(M, (512, 256, 128))
    tn = _pick(N, (1408, 1024, 512, 256, 128))

    def body(d_ref, w_ref, g_ref, u_ref, dg_ref, du_ref):
        dact = _dg(d_ref[...], w_ref[...], 1, 1, False)
        _, vjp = jax.vjp(lambda g, u: _silu(g) * u, g_ref[...], u_ref[...])
        dg, du = vjp(dact)
        dg_ref[...] = dg.astype(dg_ref.dtype)
        du_ref[...] = du.astype(du_ref.dtype)

    ospec = pl.BlockSpec((tm, tn), lambda j, i: (i, j))
    return pl.pallas_call(
        body, name="ffn_bwd_act", grid=(N // tn, M // tm),
        in_specs=[pl.BlockSpec((tm, D), lambda j, i: (i, 0)), pl.BlockSpec((tn, D), lambda j, i: (j, 0)), ospec, ospec],
        out_specs=[ospec, ospec],
        out_shape=[jax.ShapeDtypeStruct((M, N), BF16), jax.ShapeDtypeStruct((M, N), BF16)],
        compiler_params=_cparams(("arbitrary", "arbitrary")),
    )(dout, wd, gate, up)


def scale_cast(x, scale, tm):
    M, D = x.shape

    def body(x_ref, o_ref):
        o_ref[...] = (scale * x_ref[...]).astype(o_ref.dtype)

    row = pl.BlockSpec((tm, D), lambda i: (i, 0))
    return pl.pallas_call(
        body, name="scale_cast", grid=(M // tm,), in_specs=[row], out_specs=row,
        out_shape=jax.ShapeDtypeStruct((M, D), BF16), compiler_params=_cparams(("arbitrary",)),
    )(x)


def _t5_bucket(distance):
    max_exact = NUM_BUCKETS // 2
    n = distance.astype(jnp.float32)
    large = max_exact + (jnp.log(jnp.maximum(n, 1.0) / max_exact)
                         / math.log(MAX_DISTANCE / max_exact) * (NUM_BUCKETS - max_exact))
    large = jnp.minimum(large.astype(jnp.int32), NUM_BUCKETS - 1)
    return jnp.where(distance < max_exact, distance, large)


def _attn_geometry(blk, nb):
    nk = 2 * blk if nb > 1 else blk
    off = blk if nb > 1 else 0
    return nk, off


def _bucket_index(blk, nb, dilation):
    nk, off = _attn_geometry(blk, nb)
    rel = jnp.arange(blk)[:, None] + off - jnp.arange(nk)[None, :]
    return _t5_bucket(jnp.maximum(rel, 0) * dilation)


def _attn_block(q, kk, vv, bias, valid, scale):
    s = bdot_nt(q, kk) * scale + bias
    s = jnp.where(valid, s, NEG_INF)
    m = lax.stop_gradient(jnp.max(s, axis=-1, keepdims=True))
    p = jnp.exp(s - m)
    den = jnp.sum(p, axis=-1, keepdims=True)
    o = bdot_nn(p, vv) / den
    return o, m + jnp.log(den)


def _attn_valid(blk, nk, off, steps, first):
    q_loc = lax.broadcasted_iota(jnp.int32, (blk, nk), 0)
    k_loc = lax.broadcasted_iota(jnp.int32, (blk, nk), 1)
    rel = q_loc + off - k_loc
    valid = (rel >= 0) & (rel <= steps)
    if off:
        valid = valid & ((k_loc >= off) | jnp.logical_not(first))
    return valid


def attn_fwd(q, k, v, bias, steps):
    BD, H, L, dh = q.shape
    blk = A_BLOCK
    nb = L // blk
    nk, off = _attn_geometry(blk, nb)
    scale = A_HEAD_DIM ** -0.5

    def body(*refs):
        if nb > 1:
            q_ref, kp_ref, kc_ref, vp_ref, vc_ref, b_ref, o_ref, l_ref = refs
        else:
            q_ref, kc_ref, vc_ref, b_ref, o_ref, l_ref = refs
        valid = _attn_valid(blk, nk, off, steps, pl.program_id(1) == 0)
        for h in range(H):
            if nb > 1:
                kk = jnp.concatenate([kp_ref[0, h], kc_ref[0, h]], axis=0)
                vv = jnp.concatenate([vp_ref[0, h], vc_ref[0, h]], axis=0)
            else:
                kk, vv = kc_ref[0, h], vc_ref[0, h]
            o, lse = _attn_block(q_ref[0, h], kk, vv, b_ref[h], valid, scale)
            o_ref[0, h] = o
            l_ref[0, h] = lse

    cur = pl.BlockSpec((1, H, blk, dh), lambda b, i: (b, 0, i, 0))
    prev = pl.BlockSpec((1, H, blk, dh), lambda b, i: (b, 0, jnp.maximum(i - 1, 0), 0))
    bspec = pl.BlockSpec((H, blk, nk), lambda b, i: (0, 0, 0))
    if nb > 1:
        in_specs, args = [cur, prev, cur, prev, cur, bspec], [q, k, k, v, v, bias]
    else:
        in_specs, args = [cur, cur, cur, bspec], [q, k, v, bias]
    return pl.pallas_call(
        body, name="attn_fwd", grid=(BD, nb), in_specs=in_specs,
        out_specs=[cur, pl.BlockSpec((1, H, blk, 1), lambda b, i: (b, 0, i, 0))],
        out_shape=[jax.ShapeDtypeStruct((BD, H, L, dh), F32), jax.ShapeDtypeStruct((BD, H, L, 1), F32)],
        compiler_params=_cparams(("arbitrary", "arbitrary")),
    )(*args)


def attn_bwd(q, k, v, bias, do, dlse, steps):
    BD, H, L, dh = q.shape
    blk = A_BLOCK
    nb = L // blk
    nk, off = _attn_geometry(blk, nb)
    scale = A_HEAD_DIM ** -0.5
    nsteps = nb + 1 if nb > 1 else 1

    def body(*refs):
        if nb > 1:
            (q_ref, kp_ref, kc_ref, vp_ref, vc_ref, b_ref, do_ref, dl_ref,
             dq_ref, dk_ref, dv_ref, db_ref, ck, cv) = refs
        else:
            q_ref, kc_ref, vc_ref, b_ref, do_ref, dl_ref, dq_ref, dk_ref, dv_ref, db_ref = refs
        b = pl.program_id(0)
        i = pl.program_id(1)

        @pl.when((b == 0) & (i == 0))
        def _():
            db_ref[...] = jnp.zeros_like(db_ref)

        if nb > 1:
            @pl.when(i == 0)
            def _():
                ck[...] = jnp.zeros_like(ck)
                cv[...] = jnp.zeros_like(cv)

        def work():
            valid = _attn_valid(blk, nk, off, steps, i == 0)
            for h in range(H):
                if nb > 1:
                    kk = jnp.concatenate([kp_ref[0, h], kc_ref[0, h]], axis=0)
                    vv = jnp.concatenate([vp_ref[0, h], vc_ref[0, h]], axis=0)
                else:
                    kk, vv = kc_ref[0, h], vc_ref[0, h]
                _, vjp = jax.vjp(lambda a, c, d, e: _attn_block(a, c, d, e, valid, scale),
                                 q_ref[0, h], kk, vv, b_ref[h])
                dq, dkk, dvv, dbias = vjp((do_ref[0, h], dl_ref[0, h]))
                dq_ref[0, h] = dq
                db_ref[h] += dbias
                if nb > 1:
                    dk_ref[0, h] = ck[h] + dkk[:blk]
                    dv_ref[0, h] = cv[h] + dvv[:blk]
                    ck[h] = dkk[blk:]
                    cv[h] = dvv[blk:]
                else:
                    dk_ref[0, h] = dkk
                    dv_ref[0, h] = dvv

        if nb > 1:
            pl.when(i < nb)(work)

            @pl.when(i == nb)
            def _():
                dk_ref[0] = ck[...]
                dv_ref[0] = cv[...]
        else:
            work()

    last = nb - 1
    cur = pl.BlockSpec((1, H, blk, dh), lambda b, i: (b, 0, jnp.minimum(i, last), 0))
    prev = pl.BlockSpec((1, H, blk, dh), lambda b, i: (b, 0, jnp.clip(i - 1, 0, last), 0))
    col = pl.BlockSpec((1, H, blk, 1), lambda b, i: (b, 0, jnp.minimum(i, last), 0))
    bspec = pl.BlockSpec((H, blk, nk), lambda b, i: (0, 0, 0))
    if nb > 1:
        in_specs, args = [cur, prev, cur, prev, cur, bspec, cur, col], [q, k, k, v, v, bias, do, dlse]
        out_specs = [cur, prev, prev, bspec]
        scratch = [pltpu.VMEM((H, blk, dh), F32), pltpu.VMEM((H, blk, dh), F32)]
    else:
        in_specs, args = [cur, cur, cur, bspec, cur, col], [q, k, v, bias, do, dlse]
        out_specs = [cur, cur, cur, bspec]
        scratch = []
    qs = jax.ShapeDtypeStruct((BD, H, L, dh), F32)
    return pl.pallas_call(
        body, name="attn_bwd", grid=(BD, nsteps), in_specs=in_specs, out_specs=out_specs,
        out_shape=[qs, qs, qs, jax.ShapeDtypeStruct((H, blk, nk), F32)],
        scratch_shapes=scratch,
        compiler_params=_cparams(("arbitrary", "arbitrary")),
    )(*args)


def _merge(o0, o1, o2, l0, l1, l2):
    m = lax.stop_gradient(jnp.maximum(jnp.maximum(l0, l1), l2))
    e0, e1, e2 = jnp.exp(l0 - m), jnp.exp(l1 - m), jnp.exp(l2 - m)
    inv = 1.0 / (e0 + e1 + e2)
    return (e0 * inv) * o0 + (e1 * inv) * o1 + (e2 * inv) * o2


def merge_fwd(os_, ls, ts):
    BH, S, dh = os_[0].shape

    def body(o0, o1, o2, l0, l1, l2, out):
        out[0] = _merge(o0[0], o1[0], o2[0], l0[0], l1[0], l2[0]).astype(out.dtype)

    ospec = pl.BlockSpec((1, ts, dh), lambda b, i: (b, i, 0))
    lspec = pl.BlockSpec((1, ts, 1), lambda b, i: (b, i, 0))
    return pl.pallas_call(
        body, name="merge_fwd", grid=(BH, S // ts), in_specs=[ospec] * 3 + [lspec] * 3, out_specs=ospec,
        out_shape=jax.ShapeDtypeStruct((BH, S, dh), BF16),
        compiler_params=_cparams(("arbitrary", "arbitrary")),
    )(*os_, *ls)


def merge_bwd(os_, ls, do, ts):
    BH, S, dh = os_[0].shape

    def body(o0, o1, o2, l0, l1, l2, d_ref, do0, do1, do2, dl0, dl1, dl2):
        _, vjp = jax.vjp(_merge, o0[0], o1[0], o2[0], l0[0], l1[0], l2[0])
        g = vjp(d_ref[0])
        for r, val in zip((do0, do1, do2, dl0, dl1, dl2), g):
            r[0] = val

    ospec = pl.BlockSpec((1, ts, dh), lambda b, i: (b, i, 0))
    lspec = pl.BlockSpec((1, ts, 1), lambda b, i: (b, i, 0))
    osh = jax.ShapeDtypeStruct((BH, S, dh), F32)
    lsh = jax.ShapeDtypeStruct((BH, S, 1), F32)
    return pl.pallas_call(
        body, name="merge_bwd", grid=(BH, S // ts), in_specs=[ospec] * 3 + [lspec] * 3 + [ospec],
        out_specs=[ospec] * 3 + [lspec] * 3, out_shape=[osh] * 3 + [lsh] * 3,
        compiler_params=_cparams(("arbitrary", "arbitrary")),
    )(*os_, *ls, do)


def _split3(x):
    hi = x.astype(BF16)
    r = x - hi.astype(F32)
    mid = r.astype(BF16)
    lo = (r - mid.astype(F32)).astype(BF16)
    return hi, mid, lo


def bias_bucket_grad(onehot, dbias):
    NB, Q = onehot.shape
    H = dbias.shape[0]

    def body(oh_ref, d_ref, o_ref):
        oh = oh_ref[...]
        acc = jnp.zeros((NB, H), F32)
        for part in _split3(d_ref[...]):
            acc = acc + _dg(oh, part, 1, 1, False)
        o_ref[...] = acc

    full = lambda s: pl.BlockSpec(s, lambda: (0,) * len(s))
    return pl.pallas_call(
        body, name="bias_bucket_grad", in_specs=[full((NB, Q)), full((H, Q))], out_specs=full((NB, H)),
        out_shape=jax.ShapeDtypeStruct((NB, H), F32), compiler_params=_cparams(),
    )(onehot, dbias)


def _shift_rows(x, k, S):
    t = lax.broadcasted_iota(jnp.int32, x.shape, 0)
    if k >= 0:
        return jnp.where(t >= k, pltpu.roll(x, k, 0), 0.0)
    return jnp.where(t < S + k, pltpu.roll(x, S + k, 0), 0.0)


def conv_fwd(x, w, ncol, cb):
    B, S, _ = x.shape

    def body(x_ref, w_ref, o_ref):
        xv = x_ref[0]
        wv = w_ref[...]
        acc = xv * wv[B_CONV - 1:B_CONV]
        for k in range(1, B_CONV):
            acc = acc + _shift_rows(xv, k, S) * wv[B_CONV - 1 - k:B_CONV - k]
        o_ref[0] = acc

    blk = pl.BlockSpec((1, S, cb), lambda b, j: (b, 0, j))
    return pl.pallas_call(
        body, name="conv_fwd", grid=(B, ncol // cb),
        in_specs=[blk, pl.BlockSpec((B_CONV, cb), lambda b, j: (0, j))], out_specs=blk,
        out_shape=jax.ShapeDtypeStruct((B, S, ncol), F32),
        compiler_params=_cparams(("arbitrary", "arbitrary")),
    )(x, w)


def conv_bwd(x, w, dc, col0, cb):
    B, S, C = dc.shape
    j0 = col0 // cb

    def body(x_ref, w_ref, dc_ref, dx_ref, dw_ref):
        xv = x_ref[0]
        wv = w_ref[...]
        d = dc_ref[0]
        acc = d * wv[B_CONV - 1:B_CONV]
        rows = [jnp.sum(d * xv, axis=0, keepdims=True)]
        for k in range(1, B_CONV):
            acc = acc + _shift_rows(d, -k, S) * wv[B_CONV - 1 - k:B_CONV - k]
            rows.append(jnp.sum(d * _shift_rows(xv, k, S), axis=0, keepdims=True))
        dx_ref[0] = acc.astype(dx_ref.dtype)

        @pl.when(pl.program_id(1) == 0)
        def _():
            dw_ref[...] = jnp.zeros_like(dw_ref)

        dw_ref[...] += jnp.concatenate(rows[::-1], axis=0)

    return pl.pallas_call(
        body, name="conv_bwd", grid=(C // cb, B),
        in_specs=[pl.BlockSpec((1, S, cb), lambda j, b: (b, 0, j + j0)),
                  pl.BlockSpec((B_CONV, cb), lambda j, b: (0, j)),
                  pl.BlockSpec((1, S, cb), lambda j, b: (b, 0, j))],
        out_specs=[pl.BlockSpec((1, S, cb), lambda j, b: (b, 0, j)), pl.BlockSpec((B_CONV, cb), lambda j, b: (0, j))],
        out_shape=[jax.ShapeDtypeStruct((B, S, C), BF16), jax.ShapeDtypeStruct((B_CONV, C), F32)],
        compiler_params=_cparams(("arbitrary", "arbitrary")),
    )(x, w, dc)


def _neumann_inverse(low):
    C = low.shape[0]
    r = lax.broadcasted_iota(jnp.int32, (C, C), 0)
    c = lax.broadcasted_iota(jnp.int32, (C, C), 1)
    x = jnp.where(r == c, 1.0, 0.0) - low
    p = hdot_nn(low, low)
    n = 2
    while n < C:
        x = x + hdot_nn(x, p)
        n *= 2
        if n < C:
            p = hdot_nn(p, p)
    return x


def _gdn_chunk(cq, ck, cv, z, ab, prm, nw, state, h):
    C = cq.shape[0]
    lane = lax.broadcasted_iota(jnp.int32, ab.shape, 1)
    a = jnp.sum(jnp.where(lane == h, ab, 0.0), axis=1, keepdims=True)
    b = jnp.sum(jnp.where(lane == h + B_HEADS, ab, 0.0), axis=1, keepdims=True)
    a_log, dtb = prm[:, 0:1], prm[:, 1:2]
    q = _silu(cq)
    q = q * lax.rsqrt(jnp.sum(q * q, axis=-1, keepdims=True) + NORM_EPS) * (B_HEAD_DIM ** -0.5)
    k = _silu(ck)
    k = k * lax.rsqrt(jnp.sum(k * k, axis=-1, keepdims=True) + NORM_EPS)
    v = _silu(cv)
    beta = _sigmoid(b)
    g = -jnp.exp(a_log) * _softplus(a + dtb)

    r = lax.broadcasted_iota(jnp.int32, (C, C), 0)
    c = lax.broadcasted_iota(jnp.int32, (C, C), 1)
    tril = r >= c
    gb = jnp.broadcast_to(g, (C, C))
    gci = hdot_nn(jnp.where(tril, 1.0, 0.0), gb)
    gcj = hdot_tn(gb, jnp.where(r <= c, 1.0, 0.0))
    decay = jnp.where(tril, jnp.exp(jnp.where(tril, gci - gcj, 0.0)), 0.0)
    gc = gci[:, 0:1]
    gl = gci[C - 1:C, 0:1]
    egc = jnp.exp(gc)

    kb = k * beta
    low = jnp.where(r > c, bdot_nt(kb, k) * decay, 0.0)
    tinv = _neumann_inverse(low)
    u = hdot_nn(tinv, v * beta)
    w = hdot_nn(tinv, kb * egc)
    attn = bdot_nt(q, k) * decay
    v_new = u - bdot_nn(w, state)
    o = bdot_nn(q * egc, state) + bdot_nn(attn, v_new)
    new_state = state * jnp.exp(gl) + bdot_tn(k * jnp.exp(gl - gc), v_new)
    y = _rms(o, nw) * _silu(z)
    return y, new_state


def gdn_fwd(c, proj, prm, nw):
    B, S, _ = c.shape
    H, dh, C = B_HEADS, B_HEAD_DIM, B_CHUNK
    N = S // C

    def body(cq_ref, ck_ref, cv_ref, z_ref, ab_ref, prm_ref, nw_ref, y_ref, st_ref, state):
        h = pl.program_id(1)
        state[...] = jnp.zeros_like(state)
        prmv = prm_ref[0]
        nwv = nw_ref[...]

        def step(n, carry):
            rows = pl.ds(pl.multiple_of(n * C, C), C)
            s_in = state[...]
            st_ref[0, n] = s_in
            y, s_out = _gdn_chunk(cq_ref[0, rows, :], ck_ref[0, rows, :], cv_ref[0, rows, :], z_ref[0, rows, :],
                                  ab_ref[0, rows, :], prmv, nwv, s_in, h)
            y_ref[0, rows, :] = y.astype(y_ref.dtype)
            state[...] = s_out
            return carry

        lax.fori_loop(0, N, step, 0)

    def col(off):
        return pl.BlockSpec((1, S, dh), lambda b, h: (b, 0, h + off))

    return pl.pallas_call(
        body, name="gdn_fwd", grid=(B, H),
        in_specs=[col(0), col(H), col(2 * H), col(3 * H), pl.BlockSpec((1, S, B_AB_PAD), lambda b, h: (b, 0, 4 * H)),
                  pl.BlockSpec((1, 1, LANES), lambda b, h: (h, 0, 0)), pl.BlockSpec((1, dh), lambda b, h: (0, 0))],
        out_specs=[col(0), pl.BlockSpec((1, N, dh, dh), lambda b, h: (b * H + h, 0, 0, 0))],
        out_shape=[jax.ShapeDtypeStruct((B, S, H * dh), BF16), jax.ShapeDtypeStruct((B * H, N, dh, dh), F32)],
        scratch_shapes=[pltpu.VMEM((dh, dh), F32)],
        compiler_params=_cparams(("arbitrary", "arbitrary")),
    )(c, c, c, proj, proj, prm, nw)


def gdn_bwd(c, proj, prm, nw, states, dy):
    B, S, _ = c.shape
    H, dh, C = B_HEADS, B_HEAD_DIM, B_CHUNK
    N = S // C

    def body(cq_ref, ck_ref, cv_ref, z_ref, ab_ref, prm_ref, nw_ref, st_ref, dy_ref,
             dq_ref, dk_ref, dv_ref, dz_ref, dab_ref, dprm_ref, dnw_ref, dstate):
        b = pl.program_id(0)
        h = pl.program_id(1)
        dstate[...] = jnp.zeros_like(dstate)

        @pl.when((b == 0) & (h == 0))
        def _():
            dprm_ref[...] = jnp.zeros_like(dprm_ref)
            dnw_ref[...] = jnp.zeros_like(dnw_ref)

        @pl.when(h == 0)
        def _():
            dab_ref[...] = jnp.zeros_like(dab_ref)

        prmv = prm_ref[h]
        nwv = nw_ref[...]

        def step(t, carry):
            n = N - 1 - t
            rows = pl.ds(pl.multiple_of(n * C, C), C)
            _, vjp = jax.vjp(functools.partial(_gdn_chunk, h=h), cq_ref[0, rows, :], ck_ref[0, rows, :],
                             cv_ref[0, rows, :], z_ref[0, rows, :], ab_ref[0, rows, :], prmv, nwv, st_ref[0, n])
            dcq, dck, dcv, dz, dab, dprm, dnw, ds = vjp((dy_ref[0, rows, :].astype(F32), dstate[...]))
            dq_ref[0, rows, :] = dcq
            dk_ref[0, rows, :] = dck
            dv_ref[0, rows, :] = dcv
            dz_ref[0, rows, :] = dz.astype(dz_ref.dtype)
            dab_ref[0, rows, :] += dab
            dprm_ref[h] += dprm
            dnw_ref[...] += dnw
            dstate[...] = ds
            return carry

        lax.fori_loop(0, N, step, 0)

    def col(off):
        return pl.BlockSpec((1, S, dh), lambda b, h: (b, 0, h + off))

    abspec = pl.BlockSpec((1, S, B_AB_PAD), lambda b, h: (b, 0, 4 * H))
    ab0 = pl.BlockSpec((1, S, B_AB_PAD), lambda b, h: (b, 0, 0))
    prm_full = pl.BlockSpec((H, 1, LANES), lambda b, h: (0, 0, 0))
    nwspec = pl.BlockSpec((1, dh), lambda b, h: (0, 0))
    wsh = jax.ShapeDtypeStruct((B, S, H * dh), F32)
    return pl.pallas_call(
        body, name="gdn_bwd", grid=(B, H),
        in_specs=[col(0), col(H), col(2 * H), col(3 * H), abspec, prm_full, nwspec,
                  pl.BlockSpec((1, N, dh, dh), lambda b, h: (b * H + h, 0, 0, 0)), col(0)],
        out_specs=[col(0), col(0), col(0), col(0), ab0, prm_full, nwspec],
        out_shape=[wsh, wsh, wsh, jax.ShapeDtypeStruct((B, S, H * dh), BF16),
                   jax.ShapeDtypeStruct((B, S, B_AB_PAD), F32), jax.ShapeDtypeStruct((H, 1, LANES), F32),
                   jax.ShapeDtypeStruct((1, dh), F32)],
        scratch_shapes=[pltpu.VMEM((dh, dh), F32)],
        compiler_params=_cparams(("arbitrary", "arbitrary")),
    )(c, c, c, proj, proj, prm, nw, states, dy)


def _my_place():
    return lax.axis_index("x"), lax.axis_index("y"), lax.axis_index("c")


HBM_SPEC = pl.BlockSpec(memory_space=pltpu.HBM)


def allgather_blocks(buf):
    R, W = buf.shape

    def body(x_ref, out_ref, send_sems, recv_sems, local_sem):
        x, y, c = _my_place()
        me, sibling = (x, y, c), (x, y, 1 - c)
        chips = [(1 - x, y), (x, 1 - y), (1 - x, 1 - y)]

        def slot(px, py, pc):
            return out_ref.at[4 * px + 2 * py + pc]

        def copy(k, block, to, src=None):
            return pltpu.make_async_remote_copy(
                src_ref=slot(*block) if src is None else src, dst_ref=slot(*block),
                send_sem=send_sems.at[k], recv_sem=recv_sems.at[k], device_id=to, device_id_type=MESH)

        mine = pltpu.make_async_copy(x_ref, slot(*me), local_sem)
        mine.start()
        first = [copy(0, me, sibling, src=x_ref)]
        first += [copy(1 + j, me, (*chip, c), src=x_ref) for j, chip in enumerate(chips)]
        for cp in first:
            cp.start()
        passed = [copy(4 + j, (*chip, c), sibling) for j, chip in enumerate(chips)]
        for j, chip in enumerate(chips):
            copy(1 + j, (*chip, c), me).wait_recv()
            passed[j].start()
        copy(0, sibling, me).wait_recv()
        for j, chip in enumerate(chips):
            copy(4 + j, (*chip, 1 - c), me).wait_recv()
        for cp in first + passed:
            cp.wait_send()
        mine.wait()

    return pl.pallas_call(
        body, name="allgather_blocks", out_shape=jax.ShapeDtypeStruct((N_DEV, R, W), buf.dtype),
        in_specs=[HBM_SPEC], out_specs=HBM_SPEC,
        scratch_shapes=[pltpu.SemaphoreType.DMA((7,)), pltpu.SemaphoreType.DMA((7,)), pltpu.SemaphoreType.DMA],
    )(buf)


def exchange_sibling(g):
    _, R, W = g.shape

    def body(g_ref, out_ref, send_sems, recv_sems):
        x, y, c = _my_place()
        copies = []
        for k in range(4):
            copies.append(pltpu.make_async_remote_copy(
                src_ref=g_ref.at[2 * k + (1 - c)], dst_ref=out_ref.at[k], send_sem=send_sems.at[k],
                recv_sem=recv_sems.at[k], device_id=(x, y, 1 - c), device_id_type=MESH))
        for cp in copies:
            cp.start()
        for cp in copies:
            cp.wait()

    return pl.pallas_call(
        body, name="exchange_sibling", out_shape=jax.ShapeDtypeStruct((4, R, W), g.dtype),
        in_specs=[HBM_SPEC], out_specs=HBM_SPEC,
        scratch_shapes=[pltpu.SemaphoreType.DMA((4,)), pltpu.SemaphoreType.DMA((4,))],
    )(g)


def add_sibling(g, got, c_idx):
    _, R, W = g.shape
    tr = PACK_ROWS

    def body(c_ref, g_ref, r_ref, o_ref):
        o_ref[...] = (g_ref[...].astype(F32) + r_ref[...].astype(F32)).astype(o_ref.dtype)

    grid_spec = pltpu.PrefetchScalarGridSpec(
        num_scalar_prefetch=1, grid=(4, R // tr),
        in_specs=[pl.BlockSpec((1, tr, W), lambda k, i, c: (2 * k + c[0], i, 0)),
                  pl.BlockSpec((1, tr, W), lambda k, i, c: (k, i, 0))],
        out_specs=pl.BlockSpec((1, tr, W), lambda k, i, c: (k, i, 0)))
    return pl.pallas_call(
        body, name="add_sibling", grid_spec=grid_spec, out_shape=jax.ShapeDtypeStruct((4, R, W), RS_DTYPE),
        compiler_params=_cparams(("arbitrary", "arbitrary")),
    )(c_idx, g, got)


def exchange_chips(p):
    _, R, W = p.shape

    def body(p_ref, out_ref, send_sems, recv_sems, local_sem):
        x, y, c = _my_place()
        mychip = 2 * x + y
        chips = [(1 - x, y), (x, 1 - y), (1 - x, 1 - y)]
        mine = pltpu.make_async_copy(p_ref.at[mychip], out_ref.at[mychip], local_sem)
        mine.start()
        copies = []
        for j, (px, py) in enumerate(chips):
            copies.append(pltpu.make_async_remote_copy(
                src_ref=p_ref.at[2 * px + py], dst_ref=out_ref.at[mychip], send_sem=send_sems.at[j],
                recv_sem=recv_sems.at[j], device_id=(px, py, c), device_id_type=MESH))
        for cp in copies:
            cp.start()
        for j, (px, py) in enumerate(chips):
            pltpu.make_async_remote_copy(
                src_ref=p_ref.at[mychip], dst_ref=out_ref.at[2 * px + py], send_sem=send_sems.at[j],
                recv_sem=recv_sems.at[j], device_id=(px, py, c), device_id_type=MESH).wait_recv()
        for cp in copies:
            cp.wait_send()
        mine.wait()

    return pl.pallas_call(
        body, name="exchange_chips", out_shape=jax.ShapeDtypeStruct((4, R, W), p.dtype),
        in_specs=[HBM_SPEC], out_specs=HBM_SPEC,
        scratch_shapes=[pltpu.SemaphoreType.DMA((3,)), pltpu.SemaphoreType.DMA((3,)), pltpu.SemaphoreType.DMA],
    )(p)


def allgather_small(s):
    R, W = s.shape

    def body(s_ref, out_ref, send_sems, recv_sems):
        x, y, c = _my_place()
        me = 4 * x + 2 * y + c
        out_ref[me] = s_ref[...]
        copies = []
        for j in range(1, N_DEV):
            peer = (x ^ (j >> 2), y ^ ((j >> 1) & 1), c ^ (j & 1))
            copies.append(pltpu.make_async_remote_copy(
                src_ref=s_ref, dst_ref=out_ref.at[me], send_sem=send_sems.at[j - 1], recv_sem=recv_sems.at[j - 1],
                device_id=peer, device_id_type=MESH))
        for cp in copies:
            cp.start()
        for j in range(1, N_DEV):
            px, py, pc = x ^ (j >> 2), y ^ ((j >> 1) & 1), c ^ (j & 1)
            pltpu.make_async_remote_copy(
                src_ref=s_ref, dst_ref=out_ref.at[4 * px + 2 * py + pc], send_sem=send_sems.at[j - 1],
                recv_sem=recv_sems.at[j - 1], device_id=(px, py, pc), device_id_type=MESH).wait_recv()
        for cp in copies:
            cp.wait_send()

    vm = pl.BlockSpec(memory_space=pltpu.VMEM)
    return pl.pallas_call(
        body, name="allgather_small", out_shape=jax.ShapeDtypeStruct((N_DEV, R, W), s.dtype),
        in_specs=[vm], out_specs=vm,
        scratch_shapes=[pltpu.SemaphoreType.DMA((7,)), pltpu.SemaphoreType.DMA((7,))],
    )(s)


def _adamw(w, g, m, v):
    m = ADAM_B1 * m + (1.0 - ADAM_B1) * g
    v = ADAM_B2 * v + (1.0 - ADAM_B2) * jnp.square(g)
    m_hat = m / (1.0 - ADAM_B1 ** ADAM_STEP)
    v_hat = v / (1.0 - ADAM_B2 ** ADAM_STEP)
    delta = -ADAM_LR * (m_hat / (jnp.sqrt(v_hat) + ADAM_EPS) + ADAM_WD * w)
    return delta, m, v


def sum_adamw(parts, w, m, v, tr):
    P, R, W = parts.shape

    def body(p_ref, w_ref, m_ref, v_ref, g_ref, d_ref, nm_ref, nv_ref):
        g = p_ref[0].astype(F32)
        for k in range(1, P):
            g = g + p_ref[k].astype(F32)
        d, nm, nv = _adamw(w_ref[...], g, m_ref[...], v_ref[...])
        g_ref[...] = g
        d_ref[...] = d
        nm_ref[...] = nm
        nv_ref[...] = nv

    row = pl.BlockSpec((tr, W), lambda i: (i, 0))
    sh = jax.ShapeDtypeStruct((R, W), F32)
    return pl.pallas_call(
        body, name="sum_adamw", grid=(R // tr,),
        in_specs=[pl.BlockSpec((P, tr, W), lambda i: (0, i, 0)), row, row, row],
        out_specs=[row] * 4, out_shape=[sh] * 4, compiler_params=_cparams(("arbitrary",)),
    )(parts, w, m, v)


def _pack(arrs, dtype, row_mult):
    flat = jnp.concatenate([a.astype(dtype).reshape(-1) for a in arrs])
    n = flat.shape[0]
    rows = -(-n // LANES)
    rows = -(-rows // row_mult) * row_mult
    return jnp.pad(flat, (0, rows * LANES - n)).reshape(rows, LANES)


def _pack_blocks(arrs, axes, row_mult):
    segs = []
    for a, ax in zip(arrs, axes):
        sh = a.shape
        t = a.reshape(sh[:ax] + (N_DEV, sh[ax] // N_DEV) + sh[ax + 1:])
        segs.append(jnp.moveaxis(t, ax, 0).reshape(N_DEV, -1))
    flat = jnp.concatenate(segs, axis=1)
    n = flat.shape[1]
    rows = -(-n // LANES)
    rows = -(-rows // row_mult) * row_mult
    return jnp.pad(flat, ((0, 0), (0, rows * LANES - n))).reshape(N_DEV, rows, LANES)


def _unpack(buf, shapes):
    flat = buf.reshape(-1)
    out, off = [], 0
    for sh in shapes:
        n = math.prod(sh)
        out.append(flat[off:off + n].reshape(sh))
        off += n
    return out


def _unpack_gathered(buf, shard_shapes, axes):
    flat = buf.reshape(N_DEV, -1)
    out, off = [], 0
    for sh, ax in zip(shard_shapes, axes):
        n = math.prod(sh)
        t = jnp.moveaxis(flat[:, off:off + n].reshape((N_DEV,) + tuple(sh)), 0, ax)
        out.append(t.reshape(sh[:ax] + (N_DEV * sh[ax],) + sh[ax + 1:]))
        off += n
    return out


def _ffn_fwd(x, g, wg, wu, wd, tm):
    h = rmsnorm_fwd(x, g, tm)
    gate, up, act = ffn_up(h, wg, wu)
    x_new = mm_nn(act, wd, res=x, res_scale=MACARON_WEIGHT, name="ffn_down")
    return x_new, (x, h, gate, up, act)


def _ffn_bwd(dx, saved, g, wg, wu, wd, tm):
    x, h, gate, up, act = saved
    dout = scale_cast(dx, MACARON_WEIGHT, tm)
    dgate, dup = ffn_bwd_act(dout, wd, gate, up)
    dwd = mm_tn(act, dout, name="ffn_dwd")
    dwg = mm_tn(h, dgate, name="ffn_dwg")
    dwu = mm_tn(h, dup, name="ffn_dwu")
    dh = mm_nt([(dgate, wg), (dup, wu)], name="ffn_dh")
    dx_in, dg = rmsnorm_bwd(x, g, dh, dx, tm)
    return dx_in, dg, dwg, dwu, dwd


def _to_strided(t, B, S, d):
    L = S // d
    t = t.reshape(B, L, d, A_HEADS, A_HEAD_DIM)
    return jnp.transpose(t, (0, 2, 3, 1, 4)).reshape(B * d, A_HEADS, L, A_HEAD_DIM)


def _from_strided(t, B, S, d):
    L = S // d
    w = t.shape[-1]
    t = t.reshape(B, d, A_HEADS, L, w)
    return jnp.transpose(t, (0, 2, 3, 1, 4)).reshape(B * A_HEADS, S, w)


def _strided_from_heads(t, B, S, d):
    L = S // d
    w = t.shape[-1]
    t = t.reshape(B, A_HEADS, L, d, w)
    return jnp.transpose(t, (0, 3, 1, 2, 4)).reshape(B * d, A_HEADS, L, w)


def _strided_to_cols(t, B, S, d):
    L = S // d
    t = t.reshape(B, d, A_HEADS, L, A_HEAD_DIM)
    return jnp.transpose(t, (0, 3, 1, 2, 4)).reshape(B * S, A_GROUP_WIDTH)


def _mixer_a_fwd(x, g, w_in, w_out, rel_bias, B, S, tm):
    h = rmsnorm_fwd(x, g, tm)
    proj = mm_nn(h, w_in, name="a_proj")
    gw = A_GROUP_WIDTH
    qkv, biases, outs, lses = [], [], [], []
    for gi, (window, d) in enumerate(A_PATTERNS):
        base = gi * 3 * gw
        q, k, v = (_to_strided(proj[:, base + t * gw: base + (t + 1) * gw], B, S, d) for t in range(3))
        nb = (S // d) // A_BLOCK
        idx = _bucket_index(A_BLOCK, nb, d)
        bias = jnp.transpose(rel_bias[:, gi * A_HEADS:(gi + 1) * A_HEADS][idx], (2, 0, 1))
        o, lse = attn_fwd(q, k, v, bias, window // d)
        qkv.append((q, k, v))
        biases.append(bias)
        outs.append(_from_strided(o, B, S, d))
        lses.append(_from_strided(lse, B, S, d))
    ts = _pick(S, (512, 256, 128))
    om = merge_fwd(outs, lses, ts)
    om2 = jnp.transpose(om.reshape(B, A_HEADS, S, A_HEAD_DIM), (0, 2, 1, 3)).reshape(B * S, gw)
    x_new = mm_nn(om2, w_out, res=x, name="a_out")
    return x_new, (x, h, qkv, biases, outs, lses, om2)


def _mixer_a_bwd(dx, saved, g, w_in, w_out, B, S, tm):
    x, h, qkv, biases, outs, lses, om2 = saved
    gw = A_GROUP_WIDTH
    dxb = scale_cast(dx, 1.0, tm)
    dw_out = mm_tn(om2, dxb, name="a_dwout")
    dom = mm_nt([(dxb, w_out)], name="a_dom")
    dom = jnp.transpose(dom.reshape(B, S, A_HEADS, A_HEAD_DIM), (0, 2, 1, 3)).reshape(B * A_HEADS, S, A_HEAD_DIM)
    ts = _pick(S, (512, 256, 128))
    g6 = merge_bwd(outs, lses, dom, ts)
    dparts, dbias_cols = [], []
    for gi, (window, d) in enumerate(A_PATTERNS):
        q, k, v = qkv[gi]
        do = _strided_from_heads(g6[gi], B, S, d)
        dl = _strided_from_heads(g6[3 + gi], B, S, d)
        dq, dk, dv, dbias = attn_bwd(q, k, v, biases[gi], do, dl, window // d)
        dparts += [_strided_to_cols(t, B, S, d) for t in (dq, dk, dv)]
        nb = (S // d) // A_BLOCK
        idx = _bucket_index(A_BLOCK, nb, d).reshape(-1)
        onehot = (idx[None, :] == jnp.arange(NUM_BUCKETS)[:, None]).astype(BF16)
        dbias_cols.append(bias_bucket_grad(onehot, dbias.reshape(A_HEADS, -1)))
    d_rel_bias = jnp.concatenate(dbias_cols, axis=1)
    dproj = jnp.concatenate(dparts, axis=1).astype(BF16)
    dw_in = mm_tn(h, dproj, name="a_dwin")
    dh = mm_nt([(dproj, w_in)], name="a_dh")
    dx_in, dg = rmsnorm_bwd(x, g, dh, dx, tm)
    return dx_in, dg, dw_in, dw_out, d_rel_bias


def _mixer_b_fwd(x, g, w_in_p, conv_w, prm, nw, w_out, B, S, tm):
    h = rmsnorm_fwd(x, g, tm)
    proj = mm_nn(h, w_in_p, name="b_proj").reshape(B, S, -1)
    c = conv_fwd(proj, conv_w, 3 * B_WIDTH, 512)
    y, states = gdn_fwd(c, proj, prm, nw)
    y2 = y.reshape(B * S, B_WIDTH)
    x_new = mm_nn(y2, w_out, res=x, name="b_out")
    return x_new, (x, h, proj, c, states, y2)


def _mixer_b_bwd(dx, saved, g, w_in_p, conv_w, prm, nw, w_out, B, S, tm):
    x, h, proj, c, states, y2 = saved
    W = B_WIDTH
    dxb = scale_cast(dx, 1.0, tm)
    dw_out = mm_tn(y2, dxb, name="b_dwout")
    dy = mm_nt([(dxb, w_out)], out_dtype=BF16, name="b_dy").reshape(B, S, W)
    dcq, dck, dcv, dz, dab, dprm, dnw = gdn_bwd(c, proj, prm, nw, states, dy)
    dxs, dws = [], []
    for t, dc in enumerate((dcq, dck, dcv)):
        dxp, dwp = conv_bwd(proj, conv_w[:, t * W:(t + 1) * W], dc, t * W, 512)
        dxs.append(dxp)
        dws.append(dwp)
    dconv_w = jnp.concatenate(dws, axis=1)
    dproj = jnp.concatenate(dxs + [dz, dab.astype(BF16)], axis=2).reshape(B * S, -1)
    dw_in_p = mm_tn(h, dproj, name="b_dwin")
    dh = mm_nt([(dproj, w_in_p)], name="b_dh")
    dx_in, dg = rmsnorm_bwd(x, g, dh, dx, tm)
    return dx_in, dg, dw_in_p, dconv_w, dprm, dnw, dw_out


BIG_NAMES = ("ffn_w_gate", "ffn_w_up", "ffn_w_down", "a_w_in", "a_w_out", "b_w_in", "b_w_out")
BIG_AXES = (3, 3, 2, 2, 2, 2, 1)
MID_NAMES = ("norm_g", "b_conv_w")
MID_AXES = (2, 2)
SMALL_NAMES = ("rel_bias", "b_a_log", "b_dt_bias", "b_norm_w", "final_g")


def kernel(x, norm_g, ffn_w_gate, ffn_w_up, ffn_w_down, rel_bias, a_w_in, a_w_out, b_w_in, b_conv_w, b_a_log, b_dt_bias, b_norm_w, b_w_out, final_g, loss_target, m_norm_g, m_ffn_w_gate, m_ffn_w_up, m_ffn_w_down, m_rel_bias, m_a_w_in, m_a_w_out, m_b_w_in, m_b_conv_w, m_b_a_log, m_b_dt_bias, m_b_norm_w, m_b_w_out, m_final_g, v_norm_g, v_ffn_w_gate, v_ffn_w_up, v_ffn_w_down, v_rel_bias, v_a_w_in, v_a_w_out, v_b_w_in, v_b_conv_w, v_b_a_log, v_b_dt_bias, v_b_norm_w, v_b_w_out, v_final_g):
    W = dict(norm_g=norm_g, ffn_w_gate=ffn_w_gate, ffn_w_up=ffn_w_up, ffn_w_down=ffn_w_down, rel_bias=rel_bias,
             a_w_in=a_w_in, a_w_out=a_w_out, b_w_in=b_w_in, b_conv_w=b_conv_w, b_a_log=b_a_log,
             b_dt_bias=b_dt_bias, b_norm_w=b_norm_w, b_w_out=b_w_out, final_g=final_g)
    Mo = dict(norm_g=m_norm_g, ffn_w_gate=m_ffn_w_gate, ffn_w_up=m_ffn_w_up, ffn_w_down=m_ffn_w_down,
              rel_bias=m_rel_bias, a_w_in=m_a_w_in, a_w_out=m_a_w_out, b_w_in=m_b_w_in, b_conv_w=m_b_conv_w,
              b_a_log=m_b_a_log, b_dt_bias=m_b_dt_bias, b_norm_w=m_b_norm_w, b_w_out=m_b_w_out, final_g=m_final_g)
    Vo = dict(norm_g=v_norm_g, ffn_w_gate=v_ffn_w_gate, ffn_w_up=v_ffn_w_up, ffn_w_down=v_ffn_w_down,
              rel_bias=v_rel_bias, a_w_in=v_a_w_in, a_w_out=v_a_w_out, b_w_in=v_b_w_in, b_conv_w=v_b_conv_w,
              b_a_log=v_b_a_log, b_dt_bias=v_b_dt_bias, b_norm_w=v_b_norm_w, b_w_out=v_b_w_out, final_g=v_final_g)

    B, S, D = x.shape
    M = B * S
    tm = _pick(M, (512, 256, 128))
    c_idx = lax.axis_index("c").astype(jnp.int32).reshape(1)

    big_shards = [W[n] for n in BIG_NAMES]
    mid_shards = [W[n] for n in MID_NAMES]
    big = _unpack_gathered(allgather_blocks(_pack(big_shards, BF16, 16)), [a.shape for a in big_shards], BIG_AXES)
    mid = _unpack_gathered(allgather_blocks(_pack(mid_shards, F32, 8)), [a.shape for a in mid_shards], MID_AXES)
    wg_all, wu_all, wd_all, a_in_all, a_out_all, b_in_all, b_out_all = big
    norm_all, conv_all = mid
    b_in_pad = jnp.pad(b_in_all, ((0, 0), (0, 0), (0, B_AB_PAD - 2 * B_HEADS)))
    prm_all = jnp.zeros((b_a_log.shape[0], B_HEADS, 1, LANES), F32)
    prm_all = prm_all.at[:, :, 0, 0].set(b_a_log).at[:, :, 0, 1].set(b_dt_bias)

    xs = x.reshape(M, D)
    saved = []
    for i in range(DEPTH):
        j = i // 2
        xs, s1 = _ffn_fwd(xs, norm_all[i, 0][None], wg_all[i, 0], wu_all[i, 0], wd_all[i, 0], tm)
        if i % 2 == 0:
            xs, s2 = _mixer_a_fwd(xs, norm_all[i, 1][None], a_in_all[j], a_out_all[j], rel_bias, B, S, tm)
        else:
            xs, s2 = _mixer_b_fwd(xs, norm_all[i, 1][None], b_in_pad[j], conv_all[j], prm_all[j],
                                  b_norm_w[j][None], b_out_all[j], B, S, tm)
        xs, s3 = _ffn_fwd(xs, norm_all[i, 2][None], wg_all[i, 1], wu_all[i, 1], wd_all[i, 1], tm)
        saved.append((s1, s2, s3))

    dx, d_final_g, loss_row = loss_head(xs, final_g[None], loss_target.reshape(M, D), tm)

    zero = lambda a: jnp.zeros(a.shape, F32)
    d_norm = [[None] * 3 for _ in range(DEPTH)]
    d_wg = [[None] * 2 for _ in range(DEPTH)]
    d_wu = [[None] * 2 for _ in range(DEPTH)]
    d_wd = [[None] * 2 for _ in range(DEPTH)]
    d_a_in, d_a_out, d_b_in, d_b_out, d_conv = [None] * 2, [None] * 2, [None] * 2, [None] * 2, [None] * 2
    d_prm, d_nw = [None] * 2, [None] * 2
    d_rel = zero(rel_bias)
    for i in reversed(range(DEPTH)):
        j = i // 2
        s1, s2, s3 = saved[i]
        dx, d_norm[i][2], d_wg[i][1], d_wu[i][1], d_wd[i][1] = _ffn_bwd(
            dx, s3, norm_all[i, 2][None], wg_all[i, 1], wu_all[i, 1], wd_all[i, 1], tm)
        if i % 2 == 0:
            dx, d_norm[i][1], d_a_in[j], d_a_out[j], drb = _mixer_a_bwd(
                dx, s2, norm_all[i, 1][None], a_in_all[j], a_out_all[j], B, S, tm)
            d_rel = d_rel + drb
        else:
            dx, d_norm[i][1], dbp, d_conv[j], d_prm[j], d_nw[j], d_b_out[j] = _mixer_b_bwd(
                dx, s2, norm_all[i, 1][None], b_in_pad[j], conv_all[j], prm_all[j], b_norm_w[j][None],
                b_out_all[j], B, S, tm)
            d_b_in[j] = dbp[:, :b_in_all.shape[2]]
        dx, d_norm[i][0], d_wg[i][0], d_wu[i][0], d_wd[i][0] = _ffn_bwd(
            dx, s1, norm_all[i, 0][None], wg_all[i, 0], wu_all[i, 0], wd_all[i, 0], tm)
    grad_x = dx.reshape(B, S, D)

    stack2 = lambda rows: jnp.stack([jnp.stack(r) for r in rows])
    full_grads = [stack2(d_wg), stack2(d_wu), stack2(d_wd), jnp.stack(d_a_in), jnp.stack(d_a_out),
                  jnp.stack(d_b_in), jnp.stack(d_b_out),
                  jnp.stack([jnp.concatenate(r, axis=0) for r in d_norm]), jnp.stack(d_conv)]
    names = BIG_NAMES + MID_NAMES
    axes = BIG_AXES + MID_AXES

    gblocks = _pack_blocks(full_grads, axes, PACK_ROWS)
    got = exchange_sibling(gblocks)
    chip_part = add_sibling(gblocks, got, c_idx)
    parts = exchange_chips(chip_part)
    shard_shapes = [W[n].shape for n in names]
    pk = lambda d: _pack([d[n] for n in names], F32, PACK_ROWS)
    g_p, dl_p, nm_p, nv_p = sum_adamw(parts, pk(W), pk(Mo), pk(Vo), PACK_ROWS)
    out_g = dict(zip(names, _unpack(g_p, shard_shapes)))
    out_d = dict(zip(names, _unpack(dl_p, shard_shapes)))
    out_m = dict(zip(names, _unpack(nm_p, shard_shapes)))
    out_v = dict(zip(names, _unpack(nv_p, shard_shapes)))

    d_alog = jnp.stack([p[:, 0, 0] for p in d_prm])
    d_dtb = jnp.stack([p[:, 0, 1] for p in d_prm])
    small_g = [d_rel, d_alog, d_dtb, jnp.concatenate(d_nw, axis=0), d_final_g[0], loss_row[0, :1]]
    small_shapes = [W[n].shape for n in SMALL_NAMES]
    sg = allgather_small(_pack(small_g, F32, 8))
    pks = lambda d: _pack([d[n] for n in SMALL_NAMES] + [jnp.zeros((1,), F32)], F32, 8)
    sg_p, sd_p, sm_p, sv_p = sum_adamw(sg, pks(W), pks(Mo), pks(Vo), sg.shape[1])
    loss = sg_p.reshape(-1)[sum(math.prod(s) for s in small_shapes)]
    for nm_, src, dst in ((0, sg_p, out_g), (1, sd_p, out_d), (2, sm_p, out_m), (3, sv_p, out_v)):
        dst.update(zip(SMALL_NAMES, _unpack(src, small_shapes)))

    order = ("norm_g", "ffn_w_gate", "ffn_w_up", "ffn_w_down", "rel_bias", "a_w_in", "a_w_out", "b_w_in",
             "b_conv_w", "b_a_log", "b_dt_bias", "b_norm_w", "b_w_out", "final_g")
    return (loss, grad_x, *[out_g[n] for n in order], *[out_d[n] for n in order],
            *[out_m[n] for n in order], *[out_v[n] for n in order])
```

```python
import functools
import math

import jax
import jax.numpy as jnp
from jax import lax
from jax.experimental import pallas as pl
from jax.experimental.pallas import tpu as pltpu

F32 = jnp.float32
BF16 = jnp.bfloat16

N_DEV = 8
NORM_EPS = 1e-6
MACARON_WEIGHT = 0.5
DEPTH = 4

A_PATTERNS = ((128, 1), (512, 4), (2048, 16))
A_HEADS = 8
A_HEAD_DIM = 64
A_GROUP_WIDTH = A_HEADS * A_HEAD_DIM
A_BLOCK = 128
NEG_INF = -1e30
NUM_BUCKETS = 32
MAX_DISTANCE = 2048

B_HEADS = 8
B_HEAD_DIM = 128
B_WIDTH = B_HEADS * B_HEAD_DIM
B_CONV = 4
B_CHUNK = 128
B_AB_PAD = 128

ADAM_LR = 0.001
ADAM_B1 = 0.9
ADAM_B2 = 0.999
ADAM_EPS = 1e-08
ADAM_WD = 0.01
ADAM_STEP = 10

LANES = 128
VMEM_LIMIT = 48 * 1024 * 1024
RS_DTYPE = BF16

MESH = pl.DeviceIdType.MESH


def _cparams(sem=None):
    return pltpu.CompilerParams(dimension_semantics=sem, vmem_limit_bytes=VMEM_LIMIT)


def _sigmoid(x):
    return 0.5 * (jnp.tanh(0.5 * x) + 1.0)


def _silu(x):
    return x * _sigmoid(x)


def _softplus(x):
    return jnp.maximum(x, 0.0) + jnp.log1p(jnp.exp(-jnp.abs(x)))


def _rms(x, g):
    return x * lax.rsqrt(jnp.mean(x * x, axis=-1, keepdims=True) + NORM_EPS) * g


def _dg(a, b, ca, cb, hp):
    dims = (((ca,), (cb,)), ((), ()))
    if hp:
        return lax.dot_general(a.astype(F32), b.astype(F32), dims, precision=lax.Precision.HIGH,
                               preferred_element_type=F32)
    return lax.dot_general(a.astype(BF16), b.astype(BF16), dims, preferred_element_type=F32)


def _make_dots(hp):
    @jax.custom_vjp
    def nn(a, b):
        return _dg(a, b, 1, 0, hp)

    def nn_f(a, b):
        return nn(a, b), (a, b)

    def nn_b(r, ct):
        a, b = r
        return _dg(ct, b, 1, 1, hp), _dg(a, ct, 0, 0, hp)

    nn.defvjp(nn_f, nn_b)

    @jax.custom_vjp
    def nt(a, b):
        return _dg(a, b, 1, 1, hp)

    def nt_f(a, b):
        return nt(a, b), (a, b)

    def nt_b(r, ct):
        a, b = r
        return _dg(ct, b, 1, 0, hp), _dg(ct, a, 0, 0, hp)

    nt.defvjp(nt_f, nt_b)

    @jax.custom_vjp
    def tn(a, b):
        return _dg(a, b, 0, 0, hp)

    def tn_f(a, b):
        return tn(a, b), (a, b)

    def tn_b(r, ct):
        a, b = r
        return _dg(b, ct, 1, 1, hp), _dg(a, ct, 1, 0, hp)

    tn.defvjp(tn_f, tn_b)
    return nn, nt, tn


bdot_nn, bdot_nt, bdot_tn = _make_dots(False)
hdot_nn, hdot_nt, hdot_tn = _make_dots(True)


def rmsnorm_fwd(x, g, tm):
    M, D = x.shape

    def body(x_ref, g_ref, h_ref):
        h_ref[...] = _rms(x_ref[...], g_ref[...]).astype(h_ref.dtype)

    return pl.pallas_call(
        body, name="rmsnorm_fwd", grid=(M // tm,),
        in_specs=[pl.BlockSpec((tm, D), lambda i: (i, 0)), pl.BlockSpec((1, D), lambda i: (0, 0))],
        out_specs=pl.BlockSpec((tm, D), lambda i: (i, 0)),
        out_shape=jax.ShapeDtypeStruct((M, D), BF16),
        compiler_params=_cparams(("arbitrary",)),
    )(x, g)


def rmsnorm_bwd(x, g, dh, dres, tm):
    M, D = x.shape

    def body(x_ref, g_ref, dh_ref, dres_ref, dx_ref, dg_ref):
        _, vjp = jax.vjp(_rms, x_ref[...], g_ref[...])
        dx, dg = vjp(dh_ref[...])
        dx_ref[...] = dres_ref[...] + dx

        @pl.when(pl.program_id(0) == 0)
        def _():
            dg_ref[...] = jnp.zeros_like(dg_ref)

        dg_ref[...] += dg

    row = pl.BlockSpec((tm, D), lambda i: (i, 0))
    vec = pl.BlockSpec((1, D), lambda i: (0, 0))
    return pl.pallas_call(
        body, name="rmsnorm_bwd", grid=(M // tm,),
        in_specs=[row, vec, row, row], out_specs=[row, vec],
        out_shape=[jax.ShapeDtypeStruct((M, D), F32), jax.ShapeDtypeStruct((1, D), F32)],
        compiler_params=_cparams(("arbitrary",)),
    )(x, g, dh, dres)


def loss_head(x, g, target, tm):
    M, D = x.shape

    def body(x_ref, g_ref, t_ref, dx_ref, dg_ref, loss_ref):
        t = t_ref[...]

        def f(xv, gv):
            err = _rms(xv, gv) - t
            return 0.5 * jnp.sum(jnp.mean(err * err, axis=-1, keepdims=True), axis=0, keepdims=True)

        val, vjp = jax.vjp(f, x_ref[...], g_ref[...])
        dx, dg = vjp(jnp.ones((1, 1), F32))
        dx_ref[...] = dx

        @pl.when(pl.program_id(0) == 0)
        def _():
            dg_ref[...] = jnp.zeros_like(dg_ref)
            loss_ref[...] = jnp.zeros_like(loss_ref)

        dg_ref[...] += dg
        loss_ref[...] += jnp.broadcast_to(val, loss_ref.shape)

    row = pl.BlockSpec((tm, D), lambda i: (i, 0))
    vec = pl.BlockSpec((1, D), lambda i: (0, 0))
    return pl.pallas_call(
        body, name="loss_head", grid=(M // tm,),
        in_specs=[row, vec, row],
        out_specs=[row, vec, pl.BlockSpec((1, LANES), lambda i: (0, 0))],
        out_shape=[jax.ShapeDtypeStruct((M, D), F32), jax.ShapeDtypeStruct((1, D), F32),
                   jax.ShapeDtypeStruct((1, LANES), F32)],
        compiler_params=_cparams(("arbitrary",)),
    )(x, g, target)


def _pick(n, cands):
    for c in cands:
        if n % c == 0:
            return c
    return n


def mm_nn(a, b, *, out_dtype=F32, res=None, res_scale=1.0, name="mm_nn"):
    M, K = a.shape
    N = b.shape[1]
    tm = _pick(M, (512, 256, 128))
    tn = _pick(N, (1408, 1536, 1024, 512, 256, 128))

    def body(*refs):
        if res is None:
            a_ref, b_ref, o_ref = refs
        else:
            a_ref, b_ref, r_ref, o_ref = refs
        acc = _dg(a_ref[...], b_ref[...], 1, 0, False)
        if res is not None:
            acc = r_ref[...] + res_scale * acc
        o_ref[...] = acc.astype(o_ref.dtype)

    in_specs = [pl.BlockSpec((tm, K), lambda j, i: (i, 0)), pl.BlockSpec((K, tn), lambda j, i: (0, j))]
    args = [a, b]
    if res is not None:
        in_specs.append(pl.BlockSpec((tm, tn), lambda j, i: (i, j)))
        args.append(res)
    return pl.pallas_call(
        body, name=name, grid=(N // tn, M // tm), in_specs=in_specs,
        out_specs=pl.BlockSpec((tm, tn), lambda j, i: (i, j)),
        out_shape=jax.ShapeDtypeStruct((M, N), out_dtype),
        compiler_params=_cparams(("arbitrary", "arbitrary")),
    )(*args)


def mm_nt(pairs, *, out_dtype=F32, name="mm_nt"):
    M = pairs[0][0].shape[0]
    N = pairs[0][1].shape[0]
    tm = _pick(M, (512, 256, 128))
    tn = _pick(N, (1024, 512, 256, 128))
    np_ = len(pairs)

    def body(*refs):
        o_ref = refs[-1]
        acc = None
        for p in range(np_):
            t = _dg(refs[2 * p][...], refs[2 * p + 1][...], 1, 1, False)
            acc = t if acc is None else acc + t
        o_ref[...] = acc.astype(o_ref.dtype)

    in_specs, args = [], []
    for a, b in pairs:
        K = a.shape[1]
        in_specs += [pl.BlockSpec((tm, K), lambda j, i: (i, 0)), pl.BlockSpec((tn, K), lambda j, i: (j, 0))]
        args += [a, b]
    return pl.pallas_call(
        body, name=name, grid=(N // tn, M // tm), in_specs=in_specs,
        out_specs=pl.BlockSpec((tm, tn), lambda j, i: (i, j)),
        out_shape=jax.ShapeDtypeStruct((M, N), out_dtype),
        compiler_params=_cparams(("arbitrary", "arbitrary")),
    )(*args)


def mm_tn(a, b, *, name="mm_tn"):
    M, K = a.shape
    N = b.shape[1]
    tm = _pick(M, (512, 256, 128))
    tk = _pick(K, (1408, 1536, 1024, 512, 256, 128))
    tn = _pick(N, (1408, 1536, 1024, 1056, 512, 256, 128))

    def body(a_ref, b_ref, o_ref):
        @pl.when(pl.program_id(2) == 0)
        def _():
            o_ref[...] = jnp.zeros_like(o_ref)

        o_ref[...] += _dg(a_ref[...], b_ref[...], 0, 0, False)

    return pl.pallas_call(
        body, name=name, grid=(K // tk, N // tn, M // tm),
        in_specs=[pl.BlockSpec((tm, tk), lambda k, j, m: (m, k)), pl.BlockSpec((tm, tn), lambda k, j, m: (m, j))],
        out_specs=pl.BlockSpec((tk, tn), lambda k, j, m: (k, j)),
        out_shape=jax.ShapeDtypeStruct((K, N), F32),
        compiler_params=_cparams(("arbitrary", "arbitrary", "arbitrary")),
    )(a, b)


def ffn_up(h, wg, wu):
    M, K = h.shape
    N = wg.shape[1]
    tm = _pick(M, (512, 256, 128))
    tn = _pick(N, (1408, 1536, 1024, 512, 256, 128))

    def body(h_ref, wg_ref, wu_ref, g_ref, u_ref, a_ref):
        hv = h_ref[...]
        g = _dg(hv, wg_ref[...], 1, 0, False)
        u = _dg(hv, wu_ref[...], 1, 0, False)
        g_ref[...] = g
        u_ref[...] = u
        a_ref[...] = (_silu(g) * u).astype(a_ref.dtype)

    wspec = pl.BlockSpec((K, tn), lambda j, i: (0, j))
    ospec = pl.BlockSpec((tm, tn), lambda j, i: (i, j))
    return pl.pallas_call(
        body, name="ffn_up", grid=(N // tn, M // tm),
        in_specs=[pl.BlockSpec((tm, K), lambda j, i: (i, 0)), wspec, wspec],
        out_specs=[ospec, ospec, ospec],
        out_shape=[jax.ShapeDtypeStruct((M, N), F32), jax.ShapeDtypeStruct((M, N), F32),
                   jax.ShapeDtypeStruct((M, N), BF16)],
        compiler_params=_cparams(("arbitrary", "arbitrary")),
    )(h, wg, wu)


def ffn_bwd_act(dout, wd, gate, up):
    M, D = dout.shape
    N = wd.shape[0]
    tm = _pick(M, (512, 256, 128))
    tn = _pick(N, (1408, 1536, 1024, 512, 256, 128))

    def body(d_ref, w_ref, g_ref, u_ref, dg_ref, du_ref):
        dact = _dg(d_ref[...], w_ref[...], 1, 1, False)
        _, vjp = jax.vjp(lambda g, u: _silu(g) * u, g_ref[...], u_ref[...])
        dg, du = vjp(dact)
        dg_ref[...] = dg.astype(dg_ref.dtype)
        du_ref[...] = du.astype(du_ref.dtype)

    ospec = pl.BlockSpec((tm, tn), lambda j, i: (i, j))
    return pl.pallas_call(
        body, name="ffn_bwd_act", grid=(N // tn, M // tm),
        in_specs=[pl.BlockSpec((tm, D), lambda j, i: (i, 0)), pl.BlockSpec((tn, D), lambda j, i: (j, 0)), ospec, ospec],
        out_specs=[ospec, ospec],
        out_shape=[jax.ShapeDtypeStruct((M, N), BF16), jax.ShapeDtypeStruct((M, N), BF16)],
        compiler_params=_cparams(("arbitrary", "arbitrary")),
    )(dout, wd, gate, up)


def scale_cast(x, scale, tm):
    M, D = x.shape

    def body(x_ref, o_ref):
        o_ref[...] = (scale * x_ref[...]).astype(o_ref.dtype)

    row = pl.BlockSpec((tm, D), lambda i: (i, 0))
    return pl.pallas_call(
        body, name="scale_cast", grid=(M // tm,), in_specs=[row], out_specs=row,
        out_shape=jax.ShapeDtypeStruct((M, D), BF16), compiler_params=_cparams(("arbitrary",)),
    )(x)


def _t5_bucket(distance):
    max_exact = NUM_BUCKETS // 2
    n = distance.astype(jnp.float32)
    large = max_exact + (jnp.log(jnp.maximum(n, 1.0) / max_exact)
                         / math.log(MAX_DISTANCE / max_exact) * (NUM_BUCKETS - max_exact))
    large = jnp.minimum(large.astype(jnp.int32), NUM_BUCKETS - 1)
    return jnp.where(distance < max_exact, distance, large)


def _attn_geometry(blk, nb):
    nk = 2 * blk if nb > 1 else blk
    off = blk if nb > 1 else 0
    return nk, off


def _bucket_index(blk, nb, dilation):
    nk, off = _attn_geometry(blk, nb)
    rel = jnp.arange(blk)[:, None] + off - jnp.arange(nk)[None, :]
    return _t5_bucket(jnp.maximum(rel, 0) * dilation)


def _attn_block(q, kk, vv, bias, valid, scale):
    s = bdot_nt(q, kk) * scale + bias
    s = jnp.where(valid, s, NEG_INF)
    m = lax.stop_gradient(jnp.max(s, axis=-1, keepdims=True))
    p = jnp.exp(s - m)
    den = jnp.sum(p, axis=-1, keepdims=True)
    o = bdot_nn(p, vv) / den
    return o, m + jnp.log(den)


def _attn_valid(blk, nk, off, steps, first):
    q_loc = lax.broadcasted_iota(jnp.int32, (blk, nk), 0)
    k_loc = lax.broadcasted_iota(jnp.int32, (blk, nk), 1)
    rel = q_loc + off - k_loc
    valid = (rel >= 0) & (rel <= steps)
    if off:
        valid = valid & ((k_loc >= off) | jnp.logical_not(first))
    return valid


def attn_fwd(q, k, v, bias, steps):
    BD, H, L, dh = q.shape
    blk = A_BLOCK
    nb = L // blk
    nk, off = _attn_geometry(blk, nb)
    scale = A_HEAD_DIM ** -0.5

    def body(*refs):
        if nb > 1:
            q_ref, kp_ref, kc_ref, vp_ref, vc_ref, b_ref, o_ref, l_ref = refs
        else:
            q_ref, kc_ref, vc_ref, b_ref, o_ref, l_ref = refs
        valid = _attn_valid(blk, nk, off, steps, pl.program_id(1) == 0)
        for h in range(H):
            if nb > 1:
                kk = jnp.concatenate([kp_ref[0, h], kc_ref[0, h]], axis=0)
                vv = jnp.concatenate([vp_ref[0, h], vc_ref[0, h]], axis=0)
            else:
                kk, vv = kc_ref[0, h], vc_ref[0, h]
            o, lse = _attn_block(q_ref[0, h], kk, vv, b_ref[h], valid, scale)
            o_ref[0, h] = o
            l_ref[0, h] = lse

    cur = pl.BlockSpec((1, H, blk, dh), lambda b, i: (b, 0, i, 0))
    prev = pl.BlockSpec((1, H, blk, dh), lambda b, i: (b, 0, jnp.maximum(i - 1, 0), 0))
    bspec = pl.BlockSpec((H, blk, nk), lambda b, i: (0, 0, 0))
    if nb > 1:
        in_specs, args = [cur, prev, cur, prev, cur, bspec], [q, k, k, v, v, bias]
    else:
        in_specs, args = [cur, cur, cur, bspec], [q, k, v, bias]
    return pl.pallas_call(
        body, name="attn_fwd", grid=(BD, nb), in_specs=in_specs,
        out_specs=[cur, pl.BlockSpec((1, H, blk, 1), lambda b, i: (b, 0, i, 0))],
        out_shape=[jax.ShapeDtypeStruct((BD, H, L, dh), F32), jax.ShapeDtypeStruct((BD, H, L, 1), F32)],
        compiler_params=_cparams(("arbitrary", "arbitrary")),
    )(*args)


def attn_bwd(q, k, v, bias, do, dlse, steps):
    BD, H, L, dh = q.shape
    blk = A_BLOCK
    nb = L // blk
    nk, off = _attn_geometry(blk, nb)
    scale = A_HEAD_DIM ** -0.5
    nsteps = nb + 1 if nb > 1 else 1

    def body(*refs):
        if nb > 1:
            (q_ref, kp_ref, kc_ref, vp_ref, vc_ref, b_ref, do_ref, dl_ref,
             dq_ref, dk_ref, dv_ref, db_ref, ck, cv) = refs
        else:
            q_ref, kc_ref, vc_ref, b_ref, do_ref, dl_ref, dq_ref, dk_ref, dv_ref, db_ref = refs
        b = pl.program_id(0)
        i = pl.program_id(1)

        @pl.when((b == 0) & (i == 0))
        def _():
            db_ref[...] = jnp.zeros_like(db_ref)

        if nb > 1:
            @pl.when(i == 0)
            def _():
                ck[...] = jnp.zeros_like(ck)
                cv[...] = jnp.zeros_like(cv)

        def work():
            valid = _attn_valid(blk, nk, off, steps, i == 0)
            for h in range(H):
                if nb > 1:
                    kk = jnp.concatenate([kp_ref[0, h], kc_ref[0, h]], axis=0)
                    vv = jnp.concatenate([vp_ref[0, h], vc_ref[0, h]], axis=0)
                else:
                    kk, vv = kc_ref[0, h], vc_ref[0, h]
                _, vjp = jax.vjp(lambda a, c, d, e: _attn_block(a, c, d, e, valid, scale),
                                 q_ref[0, h], kk, vv, b_ref[h])
                dq, dkk, dvv, dbias = vjp((do_ref[0, h], dl_ref[0, h]))
                dq_ref[0, h] = dq
                db_ref[h] += dbias
                if nb > 1:
                    dk_ref[0, h] = ck[h] + dkk[:blk]
                    dv_ref[0, h] = cv[h] + dvv[:blk]
                    ck[h] = dkk[blk:]
                    cv[h] = dvv[blk:]
                else:
                    dk_ref[0, h] = dkk
                    dv_ref[0, h] = dvv

        if nb > 1:
            pl.when(i < nb)(work)

            @pl.when(i == nb)
            def _():
                dk_ref[0] = ck[...]
                dv_ref[0] = cv[...]
        else:
            work()

    last = nb - 1
    cur = pl.BlockSpec((1, H, blk, dh), lambda b, i: (b, 0, jnp.minimum(i, last), 0))
    prev = pl.BlockSpec((1, H, blk, dh), lambda b, i: (b, 0, jnp.clip(i - 1, 0, last), 0))
    col = pl.BlockSpec((1, H, blk, 1), lambda b, i: (b, 0, jnp.minimum(i, last), 0))
    bspec = pl.BlockSpec((H, blk, nk), lambda b, i: (0, 0, 0))
    if nb > 1:
        in_specs, args = [cur, prev, cur, prev, cur, bspec, cur, col], [q, k, k, v, v, bias, do, dlse]
        out_specs = [cur, prev, prev, bspec]
        scratch = [pltpu.VMEM((H, blk, dh), F32), pltpu.VMEM((H, blk, dh), F32)]
    else:
        in_specs, args = [cur, cur, cur, bspec, cur, col], [q, k, v, bias, do, dlse]
        out_specs = [cur, cur, cur, bspec]
        scratch = []
    qs = jax.ShapeDtypeStruct((BD, H, L, dh), F32)
    return pl.pallas_call(
        body, name="attn_bwd", grid=(BD, nsteps), in_specs=in_specs, out_specs=out_specs,
        out_shape=[qs, qs, qs, jax.ShapeDtypeStruct((H, blk, nk), F32)],
        scratch_shapes=scratch,
        compiler_params=_cparams(("arbitrary", "arbitrary")),
    )(*args)


def _merge(o0, o1, o2, l0, l1, l2):
    m = lax.stop_gradient(jnp.maximum(jnp.maximum(l0, l1), l2))
    e0, e1, e2 = jnp.exp(l0 - m), jnp.exp(l1 - m), jnp.exp(l2 - m)
    inv = 1.0 / (e0 + e1 + e2)
    return (e0 * inv) * o0 + (e1 * inv) * o1 + (e2 * inv) * o2


def merge_fwd(os_, ls, ts):
    BH, S, dh = os_[0].shape

    def body(o0, o1, o2, l0, l1, l2, out):
        out[0] = _merge(o0[0], o1[0], o2[0], l0[0], l1[0], l2[0]).astype(out.dtype)

    ospec = pl.BlockSpec((1, ts, dh), lambda b, i: (b, i, 0))
    lspec = pl.BlockSpec((1, ts, 1), lambda b, i: (b, i, 0))
    return pl.pallas_call(
        body, name="merge_fwd", grid=(BH, S // ts), in_specs=[ospec] * 3 + [lspec] * 3, out_specs=ospec,
        out_shape=jax.ShapeDtypeStruct((BH, S, dh), BF16),
        compiler_params=_cparams(("arbitrary", "arbitrary")),
    )(*os_, *ls)


def merge_bwd(os_, ls, do, ts):
    BH, S, dh = os_[0].shape

    def body(o0, o1, o2, l0, l1, l2, d_ref, do0, do1, do2, dl0, dl1, dl2):
        _, vjp = jax.vjp(_merge, o0[0], o1[0], o2[0], l0[0], l1[0], l2[0])
        g = vjp(d_ref[0])
        for r, val in zip((do0, do1, do2, dl0, dl1, dl2), g):
            r[0] = val

    ospec = pl.BlockSpec((1, ts, dh), lambda b, i: (b, i, 0))
    lspec = pl.BlockSpec((1, ts, 1), lambda b, i: (b, i, 0))
    osh = jax.ShapeDtypeStruct((BH, S, dh), F32)
    lsh = jax.ShapeDtypeStruct((BH, S, 1), F32)
    return pl.pallas_call(
        body, name="merge_bwd", grid=(BH, S // ts), in_specs=[ospec] * 3 + [lspec] * 3 + [ospec],
        out_specs=[ospec] * 3 + [lspec] * 3, out_shape=[osh] * 3 + [lsh] * 3,
        compiler_params=_cparams(("arbitrary", "arbitrary")),
    )(*os_, *ls, do)


def _split3(x):
    hi = x.astype(BF16)
    r = x - hi.astype(F32)
    mid = r.astype(BF16)
    lo = (r - mid.astype(F32)).astype(BF16)
    return hi, mid, lo


def bias_table(onehot, rb):
    NB, Q = onehot.shape
    H = rb.shape[1]

    def body(oh_ref, rb_ref, o_ref):
        oh = oh_ref[...]
        acc = jnp.zeros((H, Q), F32)
        for part in _split3(rb_ref[...]):
            acc = acc + _dg(part, oh, 0, 0, False)
        o_ref[...] = acc

    full = lambda s: pl.BlockSpec(s, lambda: (0,) * len(s))
    return pl.pallas_call(
        body, name="bias_table", in_specs=[full((NB, Q)), full((NB, H))], out_specs=full((H, Q)),
        out_shape=jax.ShapeDtypeStruct((H, Q), F32), compiler_params=_cparams(),
    )(onehot, rb)


def bias_bucket_grad(onehot, dbias):
    NB, Q = onehot.shape
    H = dbias.shape[0]

    def body(oh_ref, d_ref, o_ref):
        oh = oh_ref[...]
        acc = jnp.zeros((NB, H), F32)
        for part in _split3(d_ref[...]):
            acc = acc + _dg(oh, part, 1, 1, False)
        o_ref[...] = acc

    full = lambda s: pl.BlockSpec(s, lambda: (0,) * len(s))
    return pl.pallas_call(
        body, name="bias_bucket_grad", in_specs=[full((NB, Q)), full((H, Q))], out_specs=full((NB, H)),
        out_shape=jax.ShapeDtypeStruct((NB, H), F32), compiler_params=_cparams(),
    )(onehot, dbias)


def _shift_rows(x, k, S):
    t = lax.broadcasted_iota(jnp.int32, x.shape, 0)
    if k >= 0:
        return jnp.where(t >= k, pltpu.roll(x, k, 0), 0.0)
    return jnp.where(t < S + k, pltpu.roll(x, S + k, 0), 0.0)


def conv_fwd(x, w, ncol, cb):
    B, S, _ = x.shape

    def body(x_ref, w_ref, o_ref):
        xv = x_ref[0]
        wv = w_ref[...]
        acc = xv * wv[B_CONV - 1:B_CONV]
        for k in range(1, B_CONV):
            acc = acc + _shift_rows(xv, k, S) * wv[B_CONV - 1 - k:B_CONV - k]
        o_ref[0] = acc

    blk = pl.BlockSpec((1, S, cb), lambda b, j: (b, 0, j))
    return pl.pallas_call(
        body, name="conv_fwd", grid=(B, ncol // cb),
        in_specs=[blk, pl.BlockSpec((B_CONV, cb), lambda b, j: (0, j))], out_specs=blk,
        out_shape=jax.ShapeDtypeStruct((B, S, ncol), F32),
        compiler_params=_cparams(("arbitrary", "arbitrary")),
    )(x, w)


def conv_bwd(x, w, dc, col0, cb):
    B, S, C = dc.shape
    j0 = col0 // cb

    def body(x_ref, w_ref, dc_ref, dx_ref, dw_ref):
        xv = x_ref[0]
        wv = w_ref[...]
        d = dc_ref[0]
        acc = d * wv[B_CONV - 1:B_CONV]
        rows = [jnp.sum(d * xv, axis=0, keepdims=True)]
        for k in range(1, B_CONV):
            acc = acc + _shift_rows(d, -k, S) * wv[B_CONV - 1 - k:B_CONV - k]
            rows.append(jnp.sum(d * _shift_rows(xv, k, S), axis=0, keepdims=True))
        dx_ref[0] = acc.astype(dx_ref.dtype)

        @pl.when(pl.program_id(1) == 0)
        def _():
            dw_ref[...] = jnp.zeros_like(dw_ref)

        dw_ref[...] += jnp.concatenate(rows[::-1], axis=0)

    return pl.pallas_call(
        body, name="conv_bwd", grid=(C // cb, B),
        in_specs=[pl.BlockSpec((1, S, cb), lambda j, b: (b, 0, j + j0)),
                  pl.BlockSpec((B_CONV, cb), lambda j, b: (0, j)),
                  pl.BlockSpec((1, S, cb), lambda j, b: (b, 0, j))],
        out_specs=[pl.BlockSpec((1, S, cb), lambda j, b: (b, 0, j)), pl.BlockSpec((B_CONV, cb), lambda j, b: (0, j))],
        out_shape=[jax.ShapeDtypeStruct((B, S, C), BF16), jax.ShapeDtypeStruct((B_CONV, C), F32)],
        compiler_params=_cparams(("arbitrary", "arbitrary")),
    )(x, w, dc)


def _neumann_inverse(low):
    C = low.shape[0]
    r = lax.broadcasted_iota(jnp.int32, (C, C), 0)
    c = lax.broadcasted_iota(jnp.int32, (C, C), 1)
    x = jnp.where(r == c, 1.0, 0.0) - low
    p = hdot_nn(low, low)
    n = 2
    while n < C:
        x = x + hdot_nn(x, p)
        n *= 2
        if n < C:
            p = hdot_nn(p, p)
    return x


def _gdn_chunk(cq, ck, cv, z, ab, prm, nw, state, h):
    C = cq.shape[0]
    lane = lax.broadcasted_iota(jnp.int32, ab.shape, 1)
    a = jnp.sum(jnp.where(lane == h, ab, 0.0), axis=1, keepdims=True)
    b = jnp.sum(jnp.where(lane == h + B_HEADS, ab, 0.0), axis=1, keepdims=True)
    a_log, dtb = prm[:, 0:1], prm[:, 1:2]
    q = _silu(cq)
    q = q * lax.rsqrt(jnp.sum(q * q, axis=-1, keepdims=True) + NORM_EPS) * (B_HEAD_DIM ** -0.5)
    k = _silu(ck)
    k = k * lax.rsqrt(jnp.sum(k * k, axis=-1, keepdims=True) + NORM_EPS)
    v = _silu(cv)
    beta = _sigmoid(b)
    g = -jnp.exp(a_log) * _softplus(a + dtb)

    r = lax.broadcasted_iota(jnp.int32, (C, C), 0)
    c = lax.broadcasted_iota(jnp.int32, (C, C), 1)
    tril = r >= c
    gb = jnp.broadcast_to(g, (C, C))
    gci = hdot_nn(jnp.where(tril, 1.0, 0.0), gb)
    gcj = hdot_tn(gb, jnp.where(r <= c, 1.0, 0.0))
    decay = jnp.where(tril, jnp.exp(jnp.where(tril, gci - gcj, 0.0)), 0.0)
    gc = gci[:, 0:1]
    gl = gci[C - 1:C, 0:1]
    egc = jnp.exp(gc)

    kb = k * beta
    low = jnp.where(r > c, bdot_nt(kb, k) * decay, 0.0)
    tinv = _neumann_inverse(low)
    u = bdot_nn(tinv, v * beta)
    w = bdot_nn(tinv, kb * egc)
    attn = bdot_nt(q, k) * decay
    v_new = u - bdot_nn(w, state)
    o = bdot_nn(q * egc, state) + bdot_nn(attn, v_new)
    new_state = state * jnp.exp(gl) + bdot_tn(k * jnp.exp(gl - gc), v_new)
    y = _rms(o, nw) * _silu(z)
    return y, new_state


def gdn_fwd(c, proj, prm, nw):
    B, S, _ = c.shape
    H, dh, C = B_HEADS, B_HEAD_DIM, B_CHUNK
    N = S // C

    def body(cq_ref, ck_ref, cv_ref, z_ref, ab_ref, prm_ref, nw_ref, y_ref, st_ref, state):
        h = pl.program_id(1)
        state[...] = jnp.zeros_like(state)
        prmv = prm_ref[0]
        nwv = nw_ref[...]

        def step(n, carry):
            rows = pl.ds(pl.multiple_of(n * C, C), C)
            s_in = state[...]
            st_ref[0, n] = s_in
            y, s_out = _gdn_chunk(cq_ref[0, rows, :], ck_ref[0, rows, :], cv_ref[0, rows, :], z_ref[0, rows, :],
                                  ab_ref[0, rows, :], prmv, nwv, s_in, h)
            y_ref[0, rows, :] = y.astype(y_ref.dtype)
            state[...] = s_out
            return carry

        lax.fori_loop(0, N, step, 0, unroll=2)

    def col(off):
        return pl.BlockSpec((1, S, dh), lambda b, h: (b, 0, h + off))

    return pl.pallas_call(
        body, name="gdn_fwd", grid=(B, H),
        in_specs=[col(0), col(H), col(2 * H), col(3 * H), pl.BlockSpec((1, S, B_AB_PAD), lambda b, h: (b, 0, 4 * H)),
                  pl.BlockSpec((1, 1, LANES), lambda b, h: (h, 0, 0)), pl.BlockSpec((1, dh), lambda b, h: (0, 0))],
        out_specs=[col(0), pl.BlockSpec((1, N, dh, dh), lambda b, h: (b * H + h, 0, 0, 0))],
        out_shape=[jax.ShapeDtypeStruct((B, S, H * dh), BF16), jax.ShapeDtypeStruct((B * H, N, dh, dh), F32)],
        scratch_shapes=[pltpu.VMEM((dh, dh), F32)],
        compiler_params=_cparams(("arbitrary", "arbitrary")),
    )(c, c, c, proj, proj, prm, nw)


def gdn_bwd(c, proj, prm, nw, states, dy):
    B, S, _ = c.shape
    H, dh, C = B_HEADS, B_HEAD_DIM, B_CHUNK
    N = S // C

    def body(cq_ref, ck_ref, cv_ref, z_ref, ab_ref, prm_ref, nw_ref, st_ref, dy_ref,
             dq_ref, dk_ref, dv_ref, dz_ref, dab_ref, dprm_ref, dnw_ref, dstate):
        b = pl.program_id(0)
        h = pl.program_id(1)
        dstate[...] = jnp.zeros_like(dstate)

        @pl.when((b == 0) & (h == 0))
        def _():
            dprm_ref[...] = jnp.zeros_like(dprm_ref)
            dnw_ref[...] = jnp.zeros_like(dnw_ref)

        @pl.when(h == 0)
        def _():
            dab_ref[...] = jnp.zeros_like(dab_ref)

        prmv = prm_ref[h]
        nwv = nw_ref[...]

        def step(t, carry):
            n = N - 1 - t
            rows = pl.ds(pl.multiple_of(n * C, C), C)
            _, vjp = jax.vjp(functools.partial(_gdn_chunk, h=h), cq_ref[0, rows, :], ck_ref[0, rows, :],
                             cv_ref[0, rows, :], z_ref[0, rows, :], ab_ref[0, rows, :], prmv, nwv, st_ref[0, n])
            dcq, dck, dcv, dz, dab, dprm, dnw, ds = vjp((dy_ref[0, rows, :].astype(F32), dstate[...]))
            dq_ref[0, rows, :] = dcq
            dk_ref[0, rows, :] = dck
            dv_ref[0, rows, :] = dcv
            dz_ref[0, rows, :] = dz.astype(dz_ref.dtype)
            dab_ref[0, rows, :] += dab
            dprm_ref[h] += dprm
            dnw_ref[...] += dnw
            dstate[...] = ds
            return carry

        lax.fori_loop(0, N, step, 0, unroll=2)

    def col(off):
        return pl.BlockSpec((1, S, dh), lambda b, h: (b, 0, h + off))

    abspec = pl.BlockSpec((1, S, B_AB_PAD), lambda b, h: (b, 0, 4 * H))
    ab0 = pl.BlockSpec((1, S, B_AB_PAD), lambda b, h: (b, 0, 0))
    prm_full = pl.BlockSpec((H, 1, LANES), lambda b, h: (0, 0, 0))
    nwspec = pl.BlockSpec((1, dh), lambda b, h: (0, 0))
    wsh = jax.ShapeDtypeStruct((B, S, H * dh), F32)
    return pl.pallas_call(
        body, name="gdn_bwd", grid=(B, H),
        in_specs=[col(0), col(H), col(2 * H), col(3 * H), abspec, prm_full, nwspec,
                  pl.BlockSpec((1, N, dh, dh), lambda b, h: (b * H + h, 0, 0, 0)), col(0)],
        out_specs=[col(0), col(0), col(0), col(0), ab0, prm_full, nwspec],
        out_shape=[wsh, wsh, wsh, jax.ShapeDtypeStruct((B, S, H * dh), BF16),
                   jax.ShapeDtypeStruct((B, S, B_AB_PAD), F32), jax.ShapeDtypeStruct((H, 1, LANES), F32),
                   jax.ShapeDtypeStruct((1, dh), F32)],
        scratch_shapes=[pltpu.VMEM((dh, dh), F32)],
        compiler_params=_cparams(("arbitrary", "arbitrary")),
    )(c, c, c, proj, proj, prm, nw, states, dy)


def _my_place():
    return lax.axis_index("x"), lax.axis_index("y"), lax.axis_index("c")


HBM_SPEC = pl.BlockSpec(memory_space=pltpu.HBM)


def allgather_shards(shards):
    T = len(shards)

    def body(*refs):
        x_refs, out_refs = refs[:T], refs[T:2 * T]
        send_sems, recv_sems, local_sems = refs[2 * T:]
        x, y, c = _my_place()
        me, sibling = (x, y, c), (x, y, 1 - c)
        chips = [(1 - x, y), (x, 1 - y), (1 - x, 1 - y)]

        def copy(t, k, block, to, src=None):
            px, py, pc = block
            slot = out_refs[t].at[4 * px + 2 * py + pc]
            return pltpu.make_async_remote_copy(
                src_ref=slot if src is None else src, dst_ref=slot, send_sem=send_sems.at[7 * t + k],
                recv_sem=recv_sems.at[7 * t + k], device_id=to, device_id_type=MESH)

        started, mine = [], []
        for t in range(T):
            mine.append(pltpu.make_async_copy(x_refs[t], out_refs[t].at[4 * x + 2 * y + c], local_sems.at[t]))
            mine[-1].start()
            first = [copy(t, 0, me, sibling, src=x_refs[t])]
            first += [copy(t, 1 + j, me, (*chip, c), src=x_refs[t]) for j, chip in enumerate(chips)]
            for cp in first:
                cp.start()
            started += first
        for j, chip in enumerate(chips):
            for t in range(T):
                copy(t, 1 + j, (*chip, c), me).wait_recv()
                fwd = copy(t, 4 + j, (*chip, c), sibling)
                fwd.start()
                started.append(fwd)
        for t in range(T):
            copy(t, 0, sibling, me).wait_recv()
            for j, chip in enumerate(chips):
                copy(t, 4 + j, (*chip, 1 - c), me).wait_recv()
        for cp in started:
            cp.wait_send()
        for cp in mine:
            cp.wait()

    return pl.pallas_call(
        body, name="allgather_shards",
        out_shape=[jax.ShapeDtypeStruct((N_DEV,) + s.shape, s.dtype) for s in shards],
        in_specs=[HBM_SPEC] * T, out_specs=[HBM_SPEC] * T,
        scratch_shapes=[pltpu.SemaphoreType.DMA((7 * T,)), pltpu.SemaphoreType.DMA((7 * T,)),
                        pltpu.SemaphoreType.DMA((T,))],
    )(*shards)


def exchange_sibling(gs):
    T = len(gs)

    def body(*refs):
        g_refs, out_refs = refs[:T], refs[T:2 * T]
        send_sems, recv_sems = refs[2 * T:]
        x, y, c = _my_place()
        copies = []
        for t in range(T):
            for k in range(4):
                copies.append(pltpu.make_async_remote_copy(
                    src_ref=g_refs[t].at[2 * k + (1 - c)], dst_ref=out_refs[t].at[k],
                    send_sem=send_sems.at[4 * t + k], recv_sem=recv_sems.at[4 * t + k],
                    device_id=(x, y, 1 - c), device_id_type=MESH))
        for cp in copies:
            cp.start()
        for cp in copies:
            cp.wait()

    return pl.pallas_call(
        body, name="exchange_sibling",
        out_shape=[jax.ShapeDtypeStruct((4,) + g.shape[1:], g.dtype) for g in gs],
        in_specs=[HBM_SPEC] * T, out_specs=[HBM_SPEC] * T,
        scratch_shapes=[pltpu.SemaphoreType.DMA((4 * T,)), pltpu.SemaphoreType.DMA((4 * T,))],
    )(*gs)


def _row_tile(rows, row_bytes, budget):
    if rows * row_bytes <= budget:
        return rows
    best = None
    for t in range(16, rows, 16):
        if rows % t == 0 and t * row_bytes <= budget:
            best = t
    return best if best is not None else rows


def add_sibling(g, got, c_idx):
    _, R, L = g.shape
    tr = _row_tile(R, L * 4, 2 << 20)

    def body(c_ref, g_ref, r_ref, o_ref):
        o_ref[...] = (g_ref[...].astype(F32) + r_ref[...].astype(F32)).astype(o_ref.dtype)

    grid_spec = pltpu.PrefetchScalarGridSpec(
        num_scalar_prefetch=1, grid=(4, R // tr),
        in_specs=[pl.BlockSpec((1, tr, L), lambda k, i, c: (2 * k + c[0], i, 0)),
                  pl.BlockSpec((1, tr, L), lambda k, i, c: (k, i, 0))],
        out_specs=pl.BlockSpec((1, tr, L), lambda k, i, c: (k, i, 0)))
    return pl.pallas_call(
        body, name="add_sibling", grid_spec=grid_spec, out_shape=jax.ShapeDtypeStruct((4, R, L), RS_DTYPE),
        compiler_params=_cparams(("arbitrary", "arbitrary")),
    )(c_idx, g, got)


def exchange_chips(ps):
    T = len(ps)

    def body(*refs):
        p_refs, out_refs = refs[:T], refs[T:2 * T]
        send_sems, recv_sems, local_sems = refs[2 * T:]
        x, y, c = _my_place()
        mychip = 2 * x + y
        chips = [(1 - x, y), (x, 1 - y), (1 - x, 1 - y)]
        mine, copies = [], []
        for t in range(T):
            mine.append(pltpu.make_async_copy(p_refs[t].at[mychip], out_refs[t].at[mychip], local_sems.at[t]))
            mine[-1].start()
            for j, (px, py) in enumerate(chips):
                copies.append(pltpu.make_async_remote_copy(
                    src_ref=p_refs[t].at[2 * px + py], dst_ref=out_refs[t].at[mychip],
                    send_sem=send_sems.at[3 * t + j], recv_sem=recv_sems.at[3 * t + j],
                    device_id=(px, py, c), device_id_type=MESH))
                copies[-1].start()
        for t in range(T):
            for j, (px, py) in enumerate(chips):
                pltpu.make_async_remote_copy(
                    src_ref=p_refs[t].at[mychip], dst_ref=out_refs[t].at[2 * px + py],
                    send_sem=send_sems.at[3 * t + j], recv_sem=recv_sems.at[3 * t + j],
                    device_id=(px, py, c), device_id_type=MESH).wait_recv()
        for cp in copies:
            cp.wait_send()
        for cp in mine:
            cp.wait()

    return pl.pallas_call(
        body, name="exchange_chips", out_shape=[jax.ShapeDtypeStruct(p.shape, p.dtype) for p in ps],
        in_specs=[HBM_SPEC] * T, out_specs=[HBM_SPEC] * T,
        scratch_shapes=[pltpu.SemaphoreType.DMA((3 * T,)), pltpu.SemaphoreType.DMA((3 * T,)),
                        pltpu.SemaphoreType.DMA((T,))],
    )(*ps)


def allgather_small(s):
    R, W = s.shape

    def body(s_ref, out_ref, send_sems, recv_sems):
        x, y, c = _my_place()
        me = 4 * x + 2 * y + c
        out_ref[me] = s_ref[...]
        copies = []
        for j in range(1, N_DEV):
            peer = (x ^ (j >> 2), y ^ ((j >> 1) & 1), c ^ (j & 1))
            copies.append(pltpu.make_async_remote_copy(
                src_ref=s_ref, dst_ref=out_ref.at[me], send_sem=send_sems.at[j - 1], recv_sem=recv_sems.at[j - 1],
                device_id=peer, device_id_type=MESH))
        for cp in copies:
            cp.start()
        for j in range(1, N_DEV):
            px, py, pc = x ^ (j >> 2), y ^ ((j >> 1) & 1), c ^ (j & 1)
            pltpu.make_async_remote_copy(
                src_ref=s_ref, dst_ref=out_ref.at[4 * px + 2 * py + pc], send_sem=send_sems.at[j - 1],
                recv_sem=recv_sems.at[j - 1], device_id=(px, py, pc), device_id_type=MESH).wait_recv()
        for cp in copies:
            cp.wait_send()

    vm = pl.BlockSpec(memory_space=pltpu.VMEM)
    return pl.pallas_call(
        body, name="allgather_small", out_shape=jax.ShapeDtypeStruct((N_DEV, R, W), s.dtype),
        in_specs=[vm], out_specs=vm,
        scratch_shapes=[pltpu.SemaphoreType.DMA((7,)), pltpu.SemaphoreType.DMA((7,))],
    )(s)


def _adamw(w, g, m, v):
    m = ADAM_B1 * m + (1.0 - ADAM_B1) * g
    v = ADAM_B2 * v + (1.0 - ADAM_B2) * jnp.square(g)
    m_hat = m / (1.0 - ADAM_B1 ** ADAM_STEP)
    v_hat = v / (1.0 - ADAM_B2 ** ADAM_STEP)
    delta = -ADAM_LR * (m_hat / (jnp.sqrt(v_hat) + ADAM_EPS) + ADAM_WD * w)
    return delta, m, v


def sum_adamw(parts, w, m, v):
    P, G, Rp, Lp = parts.shape
    _, R, L = w.shape
    tr = _row_tile(R, P * Lp * parts.dtype.itemsize + 7 * L * 4, 5 << 20)

    def body(p_ref, w_ref, m_ref, v_ref, g_ref, d_ref, nm_ref, nv_ref):
        g = p_ref[0, 0].astype(F32)
        for k in range(1, P):
            g = g + p_ref[k, 0].astype(F32)
        g = g[:, :L]
        d, nm, nv = _adamw(w_ref[0], g, m_ref[0], v_ref[0])
        g_ref[0] = g
        d_ref[0] = d
        nm_ref[0] = nm
        nv_ref[0] = nv

    row = pl.BlockSpec((1, tr, L), lambda gi, i: (gi, i, 0))
    sh = jax.ShapeDtypeStruct((G, R, L), F32)
    return pl.pallas_call(
        body, name="sum_adamw", grid=(G, R // tr),
        in_specs=[pl.BlockSpec((P, 1, tr, Lp), lambda gi, i: (0, gi, i, 0)), row, row, row],
        out_specs=[row] * 4, out_shape=[sh] * 4, compiler_params=_cparams(("arbitrary", "arbitrary")),
    )(parts, w, m, v)


def _unblock(t, axis):
    sh = t.shape[1:]
    t = jnp.moveaxis(t, 0, axis)
    return t.reshape(sh[:axis] + (N_DEV * sh[axis],) + sh[axis + 1:])


def _block(a, axis):
    sh = a.shape
    t = a.reshape(sh[:axis] + (N_DEV, sh[axis] // N_DEV) + sh[axis + 1:])
    return jnp.moveaxis(t, axis, 0)


def _rows3(t):
    return t.reshape((t.shape[0], -1, t.shape[-1]))


def _pad_to(a, axis, n):
    pad = [(0, 0)] * a.ndim
    pad[axis] = (0, n - a.shape[axis])
    return jnp.pad(a, pad)


def _lane_pad(n):
    return -(-n // LANES) * LANES


def _pack(arrs, row_mult):
    flat = jnp.concatenate([a.astype(F32).reshape(-1) for a in arrs])
    n = flat.shape[0]
    rows = -(-n // LANES)
    rows = -(-rows // row_mult) * row_mult
    return jnp.pad(flat, (0, rows * LANES - n)).reshape(rows, LANES)


def _unpack(buf, shapes):
    flat = buf.reshape(-1)
    out, off = [], 0
    for sh in shapes:
        n = math.prod(sh)
        out.append(flat[off:off + n].reshape(sh))
        off += n
    return out


def _ffn_fwd(x, g, wg, wu, wd, tm):
    h = rmsnorm_fwd(x, g, tm)
    gate, up, act = ffn_up(h, wg, wu)
    x_new = mm_nn(act, wd, res=x, res_scale=MACARON_WEIGHT, name="ffn_down")
    return x_new, (x, h, gate, up, act)


def _ffn_bwd(dx, saved, g, wg, wu, wd, tm):
    x, h, gate, up, act = saved
    dout = scale_cast(dx, MACARON_WEIGHT, tm)
    dgate, dup = ffn_bwd_act(dout, wd, gate, up)
    dwd = mm_tn(act, dout, name="ffn_dwd")
    dwg = mm_tn(h, dgate, name="ffn_dwg")
    dwu = mm_tn(h, dup, name="ffn_dwu")
    dh = mm_nt([(dgate, wg), (dup, wu)], name="ffn_dh")
    dx_in, dg = rmsnorm_bwd(x, g, dh, dx, tm)
    return dx_in, dg, dwg, dwu, dwd


def _to_strided(t, B, S, d):
    L = S // d
    t = t.reshape(B, L, d, A_HEADS, A_HEAD_DIM)
    return jnp.transpose(t, (0, 2, 3, 1, 4)).reshape(B * d, A_HEADS, L, A_HEAD_DIM)


def _from_strided(t, B, S, d):
    L = S // d
    w = t.shape[-1]
    t = t.reshape(B, d, A_HEADS, L, w)
    return jnp.transpose(t, (0, 2, 3, 1, 4)).reshape(B * A_HEADS, S, w)


def _strided_from_heads(t, B, S, d):
    L = S // d
    w = t.shape[-1]
    t = t.reshape(B, A_HEADS, L, d, w)
    return jnp.transpose(t, (0, 3, 1, 2, 4)).reshape(B * d, A_HEADS, L, w)


def _strided_to_cols(t, B, S, d):
    L = S // d
    t = t.reshape(B, d, A_HEADS, L, A_HEAD_DIM)
    return jnp.transpose(t, (0, 3, 1, 2, 4)).reshape(B * S, A_GROUP_WIDTH)


def _bucket_onehot(nb, d):
    idx = _bucket_index(A_BLOCK, nb, d)
    onehot = (idx.reshape(-1)[None, :] == jnp.arange(NUM_BUCKETS)[:, None]).astype(BF16)
    return onehot, idx.shape


def _mixer_a_fwd(x, g, w_in, w_out, rel_bias, B, S, tm):
    h = rmsnorm_fwd(x, g, tm)
    proj = mm_nn(h, w_in, name="a_proj")
    gw = A_GROUP_WIDTH
    qkv, biases, outs, lses = [], [], [], []
    for gi, (window, d) in enumerate(A_PATTERNS):
        base = gi * 3 * gw
        q, k, v = (_to_strided(proj[:, base + t * gw: base + (t + 1) * gw], B, S, d) for t in range(3))
        nb = (S // d) // A_BLOCK
        onehot, ishape = _bucket_onehot(nb, d)
        bias = bias_table(onehot, rel_bias[:, gi * A_HEADS:(gi + 1) * A_HEADS]).reshape((A_HEADS,) + ishape)
        o, lse = attn_fwd(q, k, v, bias, window // d)
        qkv.append((q, k, v))
        biases.append(bias)
        outs.append(_from_strided(o, B, S, d))
        lses.append(_from_strided(lse, B, S, d))
    ts = _pick(S, (512, 256, 128))
    om = merge_fwd(outs, lses, ts)
    om2 = jnp.transpose(om.reshape(B, A_HEADS, S, A_HEAD_DIM), (0, 2, 1, 3)).reshape(B * S, gw)
    x_new = mm_nn(om2, w_out, res=x, name="a_out")
    return x_new, (x, h, qkv, biases, outs, lses, om2)


def _mixer_a_bwd(dx, saved, g, w_in, w_out, B, S, tm):
    x, h, qkv, biases, outs, lses, om2 = saved
    gw = A_GROUP_WIDTH
    dxb = scale_cast(dx, 1.0, tm)
    dw_out = mm_tn(om2, dxb, name="a_dwout")
    dom = mm_nt([(dxb, w_out)], name="a_dom")
    dom = jnp.transpose(dom.reshape(B, S, A_HEADS, A_HEAD_DIM), (0, 2, 1, 3)).reshape(B * A_HEADS, S, A_HEAD_DIM)
    ts = _pick(S, (512, 256, 128))
    g6 = merge_bwd(outs, lses, dom, ts)
    dparts, dbias_cols = [], []
    for gi, (window, d) in enumerate(A_PATTERNS):
        q, k, v = qkv[gi]
        do = _strided_from_heads(g6[gi], B, S, d)
        dl = _strided_from_heads(g6[3 + gi], B, S, d)
        dq, dk, dv, dbias = attn_bwd(q, k, v, biases[gi], do, dl, window // d)
        dparts += [_strided_to_cols(t, B, S, d) for t in (dq, dk, dv)]
        onehot, _ = _bucket_onehot((S // d) // A_BLOCK, d)
        dbias_cols.append(bias_bucket_grad(onehot, dbias.reshape(A_HEADS, -1)))
    d_rel_bias = jnp.concatenate(dbias_cols, axis=1)
    dproj = jnp.concatenate(dparts, axis=1).astype(BF16)
    dw_in = mm_tn(h, dproj, name="a_dwin")
    dh = mm_nt([(dproj, w_in)], name="a_dh")
    dx_in, dg = rmsnorm_bwd(x, g, dh, dx, tm)
    return dx_in, dg, dw_in, dw_out, d_rel_bias


def _mixer_b_fwd(x, g, w_in_p, conv_w, prm, nw, w_out, B, S, tm):
    h = rmsnorm_fwd(x, g, tm)
    proj = mm_nn(h, w_in_p, name="b_proj").reshape(B, S, -1)
    c = conv_fwd(proj, conv_w, 3 * B_WIDTH, 512)
    y, states = gdn_fwd(c, proj, prm, nw)
    y2 = y.reshape(B * S, B_WIDTH)
    x_new = mm_nn(y2, w_out, res=x, name="b_out")
    return x_new, (x, h, proj, c, states, y2)


def _mixer_b_bwd(dx, saved, g, w_in_p, conv_w, prm, nw, w_out, B, S, tm):
    x, h, proj, c, states, y2 = saved
    W = B_WIDTH
    dxb = scale_cast(dx, 1.0, tm)
    dw_out = mm_tn(y2, dxb, name="b_dwout")
    dy = mm_nt([(dxb, w_out)], out_dtype=BF16, name="b_dy").reshape(B, S, W)
    dcq, dck, dcv, dz, dab, dprm, dnw = gdn_bwd(c, proj, prm, nw, states, dy)
    dxs, dws = [], []
    for t, dc in enumerate((dcq, dck, dcv)):
        dxp, dwp = conv_bwd(proj, conv_w[:, t * W:(t + 1) * W], dc, t * W, 512)
        dxs.append(dxp)
        dws.append(dwp)
    dconv_w = jnp.concatenate(dws, axis=1)
    dproj = jnp.concatenate(dxs + [dz, dab.astype(BF16)], axis=2).reshape(B * S, -1)
    dw_in_p = mm_tn(h, dproj, name="b_dwin")
    dh = mm_nt([(dproj, w_in_p)], name="b_dh")
    dx_in, dg = rmsnorm_bwd(x, g, dh, dx, tm)
    return dx_in, dg, dw_in_p, dconv_w, dprm, dnw, dw_out


SHARD_NAMES = ("ffn_w_gate", "ffn_w_up", "ffn_w_down", "a_w_in", "a_w_out", "b_w_in", "b_w_out", "norm_g", "b_conv_w")
SHARD_AXES = (3, 3, 2, 2, 2, 2, 1, 2, 2)
SMALL_NAMES = ("rel_bias", "b_a_log", "b_dt_bias", "b_norm_w", "final_g")


def kernel(x, norm_g, ffn_w_gate, ffn_w_up, ffn_w_down, rel_bias, a_w_in, a_w_out, b_w_in, b_conv_w, b_a_log, b_dt_bias, b_norm_w, b_w_out, final_g, loss_target, m_norm_g, m_ffn_w_gate, m_ffn_w_up, m_ffn_w_down, m_rel_bias, m_a_w_in, m_a_w_out, m_b_w_in, m_b_conv_w, m_b_a_log, m_b_dt_bias, m_b_norm_w, m_b_w_out, m_final_g, v_norm_g, v_ffn_w_gate, v_ffn_w_up, v_ffn_w_down, v_rel_bias, v_a_w_in, v_a_w_out, v_b_w_in, v_b_conv_w, v_b_a_log, v_b_dt_bias, v_b_norm_w, v_b_w_out, v_final_g):
    W = dict(norm_g=norm_g, ffn_w_gate=ffn_w_gate, ffn_w_up=ffn_w_up, ffn_w_down=ffn_w_down, rel_bias=rel_bias,
             a_w_in=a_w_in, a_w_out=a_w_out, b_w_in=b_w_in, b_conv_w=b_conv_w, b_a_log=b_a_log,
             b_dt_bias=b_dt_bias, b_norm_w=b_norm_w, b_w_out=b_w_out, final_g=final_g)
    Mo = dict(norm_g=m_norm_g, ffn_w_gate=m_ffn_w_gate, ffn_w_up=m_ffn_w_up, ffn_w_down=m_ffn_w_down,
              rel_bias=m_rel_bias, a_w_in=m_a_w_in, a_w_out=m_a_w_out, b_w_in=m_b_w_in, b_conv_w=m_b_conv_w,
              b_a_log=m_b_a_log, b_dt_bias=m_b_dt_bias, b_norm_w=m_b_norm_w, b_w_out=m_b_w_out, final_g=m_final_g)
    Vo = dict(norm_g=v_norm_g, ffn_w_gate=v_ffn_w_gate, ffn_w_up=v_ffn_w_up, ffn_w_down=v_ffn_w_down,
              rel_bias=v_rel_bias, a_w_in=v_a_w_in, a_w_out=v_a_w_out, b_w_in=v_b_w_in, b_conv_w=v_b_conv_w,
              b_a_log=v_b_a_log, b_dt_bias=v_b_dt_bias, b_norm_w=v_b_norm_w, b_w_out=v_b_w_out, final_g=v_final_g)

    B, S, D = x.shape
    M = B * S
    tm = _pick(M, (512, 256, 128))
    c_idx = lax.axis_index("c").astype(jnp.int32).reshape(1)
    fp = _lane_pad(ffn_w_gate.shape[3])

    shards = [_pad_to(ffn_w_gate.astype(BF16), 3, fp), _pad_to(ffn_w_up.astype(BF16), 3, fp),
              _pad_to(ffn_w_down.astype(BF16), 2, fp), a_w_in.astype(BF16), a_w_out.astype(BF16),
              b_w_in.astype(BF16), b_w_out.astype(BF16), norm_g, b_conv_w]
    gathered = allgather_shards(shards)
    wg_all, wu_all, wd_all, a_in_all, a_out_all, b_in_all, b_out_all, norm_all, conv_all = (
        _unblock(t, ax) for t, ax in zip(gathered, SHARD_AXES))
    b_in_pad = _pad_to(b_in_all, 2, b_in_all.shape[2] + B_AB_PAD - 2 * B_HEADS)
    prm_all = jnp.zeros((b_a_log.shape[0], B_HEADS, 1, LANES), F32)
    prm_all = prm_all.at[:, :, 0, 0].set(b_a_log).at[:, :, 0, 1].set(b_dt_bias)

    xs = x.reshape(M, D)
    saved = []
    for i in range(DEPTH):
        j = i // 2
        xs, s1 = _ffn_fwd(xs, norm_all[i, 0][None], wg_all[i, 0], wu_all[i, 0], wd_all[i, 0], tm)
        if i % 2 == 0:
            xs, s2 = _mixer_a_fwd(xs, norm_all[i, 1][None], a_in_all[j], a_out_all[j], rel_bias, B, S, tm)
        else:
            xs, s2 = _mixer_b_fwd(xs, norm_all[i, 1][None], b_in_pad[j], conv_all[j], prm_all[j],
                                  b_norm_w[j][None], b_out_all[j], B, S, tm)
        xs, s3 = _ffn_fwd(xs, norm_all[i, 2][None], wg_all[i, 1], wu_all[i, 1], wd_all[i, 1], tm)
        saved.append((s1, s2, s3))

    dx, d_final_g, loss_row = loss_head(xs, final_g[None], loss_target.reshape(M, D), tm)

    d_norm = [[None] * 3 for _ in range(DEPTH)]
    d_wg = [[None] * 2 for _ in range(DEPTH)]
    d_wu = [[None] * 2 for _ in range(DEPTH)]
    d_wd = [[None] * 2 for _ in range(DEPTH)]
    d_a_in, d_a_out, d_b_in, d_b_out, d_conv = [None] * 2, [None] * 2, [None] * 2, [None] * 2, [None] * 2
    d_prm, d_nw = [None] * 2, [None] * 2
    d_rel = jnp.zeros(rel_bias.shape, F32)
    for i in reversed(range(DEPTH)):
        j = i // 2
        s1, s2, s3 = saved[i]
        dx, d_norm[i][2], d_wg[i][1], d_wu[i][1], d_wd[i][1] = _ffn_bwd(
            dx, s3, norm_all[i, 2][None], wg_all[i, 1], wu_all[i, 1], wd_all[i, 1], tm)
        if i % 2 == 0:
            dx, d_norm[i][1], d_a_in[j], d_a_out[j], drb = _mixer_a_bwd(
                dx, s2, norm_all[i, 1][None], a_in_all[j], a_out_all[j], B, S, tm)
            d_rel = d_rel + drb
        else:
            dx, d_norm[i][1], dbp, d_conv[j], d_prm[j], d_nw[j], d_b_out[j] = _mixer_b_bwd(
                dx, s2, norm_all[i, 1][None], b_in_pad[j], conv_all[j], prm_all[j], b_norm_w[j][None],
                b_out_all[j], B, S, tm)
            d_b_in[j] = dbp[:, :b_in_all.shape[2]]
        dx, d_norm[i][0], d_wg[i][0], d_wu[i][0], d_wd[i][0] = _ffn_bwd(
            dx, s1, norm_all[i, 0][None], wg_all[i, 0], wu_all[i, 0], wd_all[i, 0], tm)
    grad_x = dx.reshape(B, S, D)

    stack2 = lambda rows: jnp.stack([jnp.stack(r) for r in rows])
    full_grads = [stack2(d_wg), stack2(d_wu), stack2(d_wd), jnp.stack(d_a_in), jnp.stack(d_a_out),
                  jnp.stack(d_b_in), jnp.stack(d_b_out),
                  jnp.stack([jnp.concatenate(r, axis=0) for r in d_norm]), jnp.stack(d_conv)]

    gblocks = [_rows3(_block(g, ax).astype(RS_DTYPE)) for g, ax in zip(full_grads, SHARD_AXES)]
    got = exchange_sibling(gblocks)
    chip_parts = [add_sibling(g, r, c_idx) for g, r in zip(gblocks, got)]
    parts = exchange_chips(chip_parts)
    out_g, out_d, out_m, out_v = {}, {}, {}, {}
    for n, p in zip(SHARD_NAMES, parts):
        w = W[n]
        w3 = w.reshape((-1,) + w.shape[-2:])
        G = w3.shape[0]
        p4 = p.reshape((4, G, p.shape[1] // G, p.shape[2]))
        res = sum_adamw(p4, w3, Mo[n].reshape(w3.shape), Vo[n].reshape(w3.shape))
        for dst, r in zip((out_g, out_d, out_m, out_v), res):
            dst[n] = r.reshape(w.shape)

    d_alog = jnp.stack([p[:, 0, 0] for p in d_prm])
    d_dtb = jnp.stack([p[:, 0, 1] for p in d_prm])
    small_g = [d_rel, d_alog, d_dtb, jnp.concatenate(d_nw, axis=0), d_final_g[0], loss_row[0, :1]]
    small_shapes = [W[n].shape for n in SMALL_NAMES]
    sg = allgather_small(_pack(small_g, 8))
    pks = lambda d: _pack([d[n] for n in SMALL_NAMES] + [jnp.zeros((1,), F32)], 8)[None]
    res = sum_adamw(sg[:, None], pks(W), pks(Mo), pks(Vo))
    loss = res[0].reshape(-1)[sum(math.prod(s) for s in small_shapes)]
    for dst, r in zip((out_g, out_d, out_m, out_v), res):
        dst.update(zip(SMALL_NAMES, _unpack(r, small_shapes)))

    order = ("norm_g", "ffn_w_gate", "ffn_w_up", "ffn_w_down", "rel_bias", "a_w_in", "a_w_out", "b_w_in",
             "b_conv_w", "b_a_log", "b_dt_bias", "b_norm_w", "b_w_out", "final_g")
    return (loss, grad_x, *[out_g[n] for n in order], *[out_d[n] for n in order],
            *[out_m[n] for n in order], *[out_v[n] for n in order])
```

```python
import functools
import math

import jax
import jax.numpy as jnp
from jax import lax
from jax.experimental import pallas as pl
from jax.experimental.pallas import tpu as pltpu

F32 = jnp.float32
BF16 = jnp.bfloat16

N_DEV = 8
NORM_EPS = 1e-6
MACARON_WEIGHT = 0.5
DEPTH = 4

A_PATTERNS = ((128, 1), (512, 4), (2048, 16))
A_HEADS = 8
A_HEAD_DIM = 64
A_GROUP_WIDTH = A_HEADS * A_HEAD_DIM
A_BLOCK = 128
NEG_INF = -1e30
NUM_BUCKETS = 32
MAX_DISTANCE = 2048

B_HEADS = 8
B_HEAD_DIM = 128
B_WIDTH = B_HEADS * B_HEAD_DIM
B_CONV = 4
B_CHUNK = 128
B_AB_PAD = 128

ADAM_LR = 0.001
ADAM_B1 = 0.9
ADAM_B2 = 0.999
ADAM_EPS = 1e-08
ADAM_WD = 0.01
ADAM_STEP = 10

LANES = 128
VMEM_LIMIT = 48 * 1024 * 1024
VMEM_LIMIT_BIG = 60 * 1024 * 1024
RS_DTYPE = BF16

MESH = pl.DeviceIdType.MESH


def _cparams(sem=None, vmem=VMEM_LIMIT):
    return pltpu.CompilerParams(dimension_semantics=sem, vmem_limit_bytes=vmem)


def _sigmoid(x):
    return 0.5 * (jnp.tanh(0.5 * x) + 1.0)


def _silu(x):
    return x * _sigmoid(x)


def _softplus(x):
    return jnp.maximum(x, 0.0) + jnp.log1p(jnp.exp(-jnp.abs(x)))


def _rms(x, g):
    return x * lax.rsqrt(jnp.mean(x * x, axis=-1, keepdims=True) + NORM_EPS) * g


def _dg(a, b, ca, cb, hp):
    dims = (((ca,), (cb,)), ((), ()))
    if hp:
        return lax.dot_general(a.astype(F32), b.astype(F32), dims, precision=lax.Precision.HIGH,
                               preferred_element_type=F32)
    return lax.dot_general(a.astype(BF16), b.astype(BF16), dims, preferred_element_type=F32)


def _make_dots(hp):
    @jax.custom_vjp
    def nn(a, b):
        return _dg(a, b, 1, 0, hp)

    def nn_f(a, b):
        return nn(a, b), (a, b)

    def nn_b(r, ct):
        a, b = r
        return _dg(ct, b, 1, 1, hp), _dg(a, ct, 0, 0, hp)

    nn.defvjp(nn_f, nn_b)

    @jax.custom_vjp
    def nt(a, b):
        return _dg(a, b, 1, 1, hp)

    def nt_f(a, b):
        return nt(a, b), (a, b)

    def nt_b(r, ct):
        a, b = r
        return _dg(ct, b, 1, 0, hp), _dg(ct, a, 0, 0, hp)

    nt.defvjp(nt_f, nt_b)

    @jax.custom_vjp
    def tn(a, b):
        return _dg(a, b, 0, 0, hp)

    def tn_f(a, b):
        return tn(a, b), (a, b)

    def tn_b(r, ct):
        a, b = r
        return _dg(b, ct, 1, 1, hp), _dg(a, ct, 1, 0, hp)

    tn.defvjp(tn_f, tn_b)
    return nn, nt, tn


bdot_nn, bdot_nt, bdot_tn = _make_dots(False)


def rmsnorm_fwd(x, g, tm):
    M, D = x.shape

    def body(x_ref, g_ref, h_ref):
        h_ref[...] = _rms(x_ref[...], g_ref[...]).astype(h_ref.dtype)

    return pl.pallas_call(
        body, name="rmsnorm_fwd", grid=(M // tm,),
        in_specs=[pl.BlockSpec((tm, D), lambda i: (i, 0)), pl.BlockSpec((1, D), lambda i: (0, 0))],
        out_specs=pl.BlockSpec((tm, D), lambda i: (i, 0)),
        out_shape=jax.ShapeDtypeStruct((M, D), BF16),
        compiler_params=_cparams(("arbitrary",)),
    )(x, g)


def rmsnorm_bwd(x, g, dh, dres, tm):
    M, D = x.shape

    def body(x_ref, g_ref, dh_ref, dres_ref, dx_ref, dg_ref):
        _, vjp = jax.vjp(_rms, x_ref[...], g_ref[...])
        dx, dg = vjp(dh_ref[...])
        dx_ref[...] = dres_ref[...] + dx

        @pl.when(pl.program_id(0) == 0)
        def _():
            dg_ref[...] = jnp.zeros_like(dg_ref)

        dg_ref[...] += dg

    row = pl.BlockSpec((tm, D), lambda i: (i, 0))
    vec = pl.BlockSpec((1, D), lambda i: (0, 0))
    return pl.pallas_call(
        body, name="rmsnorm_bwd", grid=(M // tm,),
        in_specs=[row, vec, row, row], out_specs=[row, vec],
        out_shape=[jax.ShapeDtypeStruct((M, D), F32), jax.ShapeDtypeStruct((1, D), F32)],
        compiler_params=_cparams(("arbitrary",)),
    )(x, g, dh, dres)


def loss_head(x, g, target, tm):
    M, D = x.shape

    def body(x_ref, g_ref, t_ref, dx_ref, dg_ref, loss_ref):
        t = t_ref[...]

        def f(xv, gv):
            err = _rms(xv, gv) - t
            return 0.5 * jnp.sum(jnp.mean(err * err, axis=-1, keepdims=True), axis=0, keepdims=True)

        val, vjp = jax.vjp(f, x_ref[...], g_ref[...])
        dx, dg = vjp(jnp.ones((1, 1), F32))
        dx_ref[...] = dx

        @pl.when(pl.program_id(0) == 0)
        def _():
            dg_ref[...] = jnp.zeros_like(dg_ref)
            loss_ref[...] = jnp.zeros_like(loss_ref)

        dg_ref[...] += dg
        loss_ref[...] += jnp.broadcast_to(val, loss_ref.shape)

    row = pl.BlockSpec((tm, D), lambda i: (i, 0))
    vec = pl.BlockSpec((1, D), lambda i: (0, 0))
    return pl.pallas_call(
        body, name="loss_head", grid=(M // tm,),
        in_specs=[row, vec, row],
        out_specs=[row, vec, pl.BlockSpec((1, LANES), lambda i: (0, 0))],
        out_shape=[jax.ShapeDtypeStruct((M, D), F32), jax.ShapeDtypeStruct((1, D), F32),
                   jax.ShapeDtypeStruct((1, LANES), F32)],
        compiler_params=_cparams(("arbitrary",)),
    )(x, g, target)


def _pick(n, cands):
    for c in cands:
        if n % c == 0:
            return c
    return n


def mm_nn(a, b, *, out_dtype=F32, res=None, res_scale=1.0, name="mm_nn"):
    M, K = a.shape
    N = b.shape[1]
    tm = _pick(M, (512, 256, 128))
    tn = _pick(N, (1408, 1536, 1024, 512, 256, 128))

    def body(*refs):
        if res is None:
            a_ref, b_ref, o_ref = refs
        else:
            a_ref, b_ref, r_ref, o_ref = refs
        acc = _dg(a_ref[...], b_ref[...], 1, 0, False)
        if res is not None:
            acc = r_ref[...] + res_scale * acc
        o_ref[...] = acc.astype(o_ref.dtype)

    in_specs = [pl.BlockSpec((tm, K), lambda j, i: (i, 0)), pl.BlockSpec((K, tn), lambda j, i: (0, j))]
    args = [a, b]
    if res is not None:
        in_specs.append(pl.BlockSpec((tm, tn), lambda j, i: (i, j)))
        args.append(res)
    return pl.pallas_call(
        body, name=name, grid=(N // tn, M // tm), in_specs=in_specs,
        out_specs=pl.BlockSpec((tm, tn), lambda j, i: (i, j)),
        out_shape=jax.ShapeDtypeStruct((M, N), out_dtype),
        compiler_params=_cparams(("arbitrary", "arbitrary")),
    )(*args)


def mm_nt(pairs, *, out_dtype=F32, name="mm_nt"):
    M = pairs[0][0].shape[0]
    N = pairs[0][1].shape[0]
    tm = _pick(M, (512, 256, 128))
    tn = _pick(N, (1024, 512, 256, 128))
    np_ = len(pairs)

    def body(*refs):
        o_ref = refs[-1]
        acc = None
        for p in range(np_):
            t = _dg(refs[2 * p][...], refs[2 * p + 1][...], 1, 1, False)
            acc = t if acc is None else acc + t
        o_ref[...] = acc.astype(o_ref.dtype)

    in_specs, args = [], []
    for a, b in pairs:
        K = a.shape[1]
        in_specs += [pl.BlockSpec((tm, K), lambda j, i: (i, 0)), pl.BlockSpec((tn, K), lambda j, i: (j, 0))]
        args += [a, b]
    return pl.pallas_call(
        body, name=name, grid=(N // tn, M // tm), in_specs=in_specs,
        out_specs=pl.BlockSpec((tm, tn), lambda j, i: (i, j)),
        out_shape=jax.ShapeDtypeStruct((M, N), out_dtype),
        compiler_params=_cparams(("arbitrary", "arbitrary")),
    )(*args)


def mm_tn(a, b, *, name="mm_tn"):
    M, K = a.shape
    N = b.shape[1]
    tm = _pick(M, (512, 256, 128))
    tk = _pick(K, (1408, 1536, 1024, 512, 256, 128))
    tn = _pick(N, (1408, 1536, 1024, 1056, 512, 256, 128))

    def body(a_ref, b_ref, o_ref):
        @pl.when(pl.program_id(2) == 0)
        def _():
            o_ref[...] = jnp.zeros_like(o_ref)

        o_ref[...] += _dg(a_ref[...], b_ref[...], 0, 0, False)

    return pl.pallas_call(
        body, name=name, grid=(K // tk, N // tn, M // tm),
        in_specs=[pl.BlockSpec((tm, tk), lambda k, j, m: (m, k)), pl.BlockSpec((tm, tn), lambda k, j, m: (m, j))],
        out_specs=pl.BlockSpec((tk, tn), lambda k, j, m: (k, j)),
        out_shape=jax.ShapeDtypeStruct((K, N), F32),
        compiler_params=_cparams(("arbitrary", "arbitrary", "arbitrary")),
    )(a, b)


def ffn_up(h, wg, wu):
    M, K = h.shape
    N = wg.shape[1]
    tm = _pick(M, (512, 256, 128))
    tn = _pick(N, (1408, 1536, 1024, 512, 256, 128))

    def body(h_ref, wg_ref, wu_ref, g_ref, u_ref, a_ref):
        hv = h_ref[...]
        g = _dg(hv, wg_ref[...], 1, 0, False)
        u = _dg(hv, wu_ref[...], 1, 0, False)
        g_ref[...] = g
        u_ref[...] = u
        a_ref[...] = (_silu(g) * u).astype(a_ref.dtype)

    wspec = pl.BlockSpec((K, tn), lambda j, i: (0, j))
    ospec = pl.BlockSpec((tm, tn), lambda j, i: (i, j))
    return pl.pallas_call(
        body, name="ffn_up", grid=(N // tn, M // tm),
        in_specs=[pl.BlockSpec((tm, K), lambda j, i: (i, 0)), wspec, wspec],
        out_specs=[ospec, ospec, ospec],
        out_shape=[jax.ShapeDtypeStruct((M, N), F32), jax.ShapeDtypeStruct((M, N), F32),
                   jax.ShapeDtypeStruct((M, N), BF16)],
        compiler_params=_cparams(("arbitrary", "arbitrary")),
    )(h, wg, wu)


def ffn_bwd_act(dout, wd, gate, up):
    M, D = dout.shape
    N = wd.shape[0]
    tm = _pick(M, (512, 256, 128))
    tn = _pick(N, (1408, 1536, 1024, 512, 256, 128))

    def body(d_ref, w_ref, g_ref, u_ref, dg_ref, du_ref):
        dact = _dg(d_ref[...], w_ref[...], 1, 1, False)
        _, vjp = jax.vjp(lambda g, u: _silu(g) * u, g_ref[...], u_ref[...])
        dg, du = vjp(dact)
        dg_ref[...] = dg.astype(dg_ref.dtype)
        du_ref[...] = du.astype(du_ref.dtype)

    ospec = pl.BlockSpec((tm, tn), lambda j, i: (i, j))
    return pl.pallas_call(
        body, name="ffn_bwd_act", grid=(N // tn, M // tm),
        in_specs=[pl.BlockSpec((tm, D), lambda j, i: (i, 0)), pl.BlockSpec((tn, D), lambda j, i: (j, 0)), ospec, ospec],
        out_specs=[ospec, ospec],
        out_shape=[jax.ShapeDtypeStruct((M, N), BF16), jax.ShapeDtypeStruct((M, N), BF16)],
        compiler_params=_cparams(("arbitrary", "arbitrary")),
    )(dout, wd, gate, up)


def scale_cast(x, scale, tm):
    M, D = x.shape

    def body(x_ref, o_ref):
        o_ref[...] = (scale * x_ref[...]).astype(o_ref.dtype)

    row = pl.BlockSpec((tm, D), lambda i: (i, 0))
    return pl.pallas_call(
        body, name="scale_cast", grid=(M // tm,), in_specs=[row], out_specs=row,
        out_shape=jax.ShapeDtypeStruct((M, D), BF16), compiler_params=_cparams(("arbitrary",)),
    )(x)


def _t5_bucket(distance):
    max_exact = NUM_BUCKETS // 2
    n = distance.astype(jnp.float32)
    large = max_exact + (jnp.log(jnp.maximum(n, 1.0) / max_exact)
                         / math.log(MAX_DISTANCE / max_exact) * (NUM_BUCKETS - max_exact))
    large = jnp.minimum(large.astype(jnp.int32), NUM_BUCKETS - 1)
    return jnp.where(distance < max_exact, distance, large)


def _attn_geometry(blk, nb):
    nk = 2 * blk if nb > 1 else blk
    off = blk if nb > 1 else 0
    return nk, off


def _bucket_index(blk, nb, dilation):
    nk, off = _attn_geometry(blk, nb)
    rel = jnp.arange(blk)[:, None] + off - jnp.arange(nk)[None, :]
    return _t5_bucket(jnp.maximum(rel, 0) * dilation)


def _attn_block(q, kk, vv, bias, valid, scale):
    s = bdot_nt(q, kk) * scale + bias
    s = jnp.where(valid, s, NEG_INF)
    m = lax.stop_gradient(jnp.max(s, axis=-1, keepdims=True))
    p = jnp.exp(s - m)
    den = jnp.sum(p, axis=-1, keepdims=True)
    o = bdot_nn(p, vv) / den
    return o, m + jnp.log(den)


def _attn_valid(blk, nk, off, steps, first):
    q_loc = lax.broadcasted_iota(jnp.int32, (blk, nk), 0)
    k_loc = lax.broadcasted_iota(jnp.int32, (blk, nk), 1)
    rel = q_loc + off - k_loc
    valid = (rel >= 0) & (rel <= steps)
    if off:
        valid = valid & ((k_loc >= off) | jnp.logical_not(first))
    return valid


def attn_fwd(q, k, v, bias, steps):
    BD, H, L, dh = q.shape
    blk = A_BLOCK
    nb = L // blk
    nk, off = _attn_geometry(blk, nb)
    scale = A_HEAD_DIM ** -0.5

    def body(*refs):
        if nb > 1:
            q_ref, kp_ref, kc_ref, vp_ref, vc_ref, b_ref, o_ref, l_ref = refs
        else:
            q_ref, kc_ref, vc_ref, b_ref, o_ref, l_ref = refs
        valid = _attn_valid(blk, nk, off, steps, pl.program_id(1) == 0)
        for h in range(H):
            if nb > 1:
                kk = jnp.concatenate([kp_ref[0, h], kc_ref[0, h]], axis=0)
                vv = jnp.concatenate([vp_ref[0, h], vc_ref[0, h]], axis=0)
            else:
                kk, vv = kc_ref[0, h], vc_ref[0, h]
            o, lse = _attn_block(q_ref[0, h], kk, vv, b_ref[h], valid, scale)
            o_ref[0, h] = o
            l_ref[0, h] = lse

    cur = pl.BlockSpec((1, H, blk, dh), lambda b, i: (b, 0, i, 0))
    prev = pl.BlockSpec((1, H, blk, dh), lambda b, i: (b, 0, jnp.maximum(i - 1, 0), 0))
    bspec = pl.BlockSpec((H, blk, nk), lambda b, i: (0, 0, 0))
    if nb > 1:
        in_specs, args = [cur, prev, cur, prev, cur, bspec], [q, k, k, v, v, bias]
    else:
        in_specs, args = [cur, cur, cur, bspec], [q, k, v, bias]
    return pl.pallas_call(
        body, name="attn_fwd", grid=(BD, nb), in_specs=in_specs,
        out_specs=[cur, pl.BlockSpec((1, H, blk, 1), lambda b, i: (b, 0, i, 0))],
        out_shape=[jax.ShapeDtypeStruct((BD, H, L, dh), F32), jax.ShapeDtypeStruct((BD, H, L, 1), F32)],
        compiler_params=_cparams(("arbitrary", "arbitrary")),
    )(*args)


def attn_bwd(q, k, v, bias, do, dlse, steps):
    BD, H, L, dh = q.shape
    blk = A_BLOCK
    nb = L // blk
    nk, off = _attn_geometry(blk, nb)
    scale = A_HEAD_DIM ** -0.5
    nsteps = nb + 1 if nb > 1 else 1

    def body(*refs):
        if nb > 1:
            (q_ref, kp_ref, kc_ref, vp_ref, vc_ref, b_ref, do_ref, dl_ref,
             dq_ref, dk_ref, dv_ref, db_ref, ck, cv) = refs
        else:
            q_ref, kc_ref, vc_ref, b_ref, do_ref, dl_ref, dq_ref, dk_ref, dv_ref, db_ref = refs
        b = pl.program_id(0)
        i = pl.program_id(1)

        @pl.when((b == 0) & (i == 0))
        def _():
            db_ref[...] = jnp.zeros_like(db_ref)

        if nb > 1:
            @pl.when(i == 0)
            def _():
                ck[...] = jnp.zeros_like(ck)
                cv[...] = jnp.zeros_like(cv)

        def work():
            valid = _attn_valid(blk, nk, off, steps, i == 0)
            for h in range(H):
                if nb > 1:
                    kk = jnp.concatenate([kp_ref[0, h], kc_ref[0, h]], axis=0)
                    vv = jnp.concatenate([vp_ref[0, h], vc_ref[0, h]], axis=0)
                else:
                    kk, vv = kc_ref[0, h], vc_ref[0, h]
                _, vjp = jax.vjp(lambda a, c, d, e: _attn_block(a, c, d, e, valid, scale),
                                 q_ref[0, h], kk, vv, b_ref[h])
                dq, dkk, dvv, dbias = vjp((do_ref[0, h], dl_ref[0, h]))
                dq_ref[0, h] = dq
                db_ref[h] += dbias
                if nb > 1:
                    dk_ref[0, h] = ck[h] + dkk[:blk]
                    dv_ref[0, h] = cv[h] + dvv[:blk]
                    ck[h] = dkk[blk:]
                    cv[h] = dvv[blk:]
                else:
                    dk_ref[0, h] = dkk
                    dv_ref[0, h] = dvv

        if nb > 1:
            pl.when(i < nb)(work)

            @pl.when(i == nb)
            def _():
                dk_ref[0] = ck[...]
                dv_ref[0] = cv[...]
        else:
            work()

    last = nb - 1
    cur = pl.BlockSpec((1, H, blk, dh), lambda b, i: (b, 0, jnp.minimum(i, last), 0))
    prev = pl.BlockSpec((1, H, blk, dh), lambda b, i: (b, 0, jnp.clip(i - 1, 0, last), 0))
    col = pl.BlockSpec((1, H, blk, 1), lambda b, i: (b, 0, jnp.minimum(i, last), 0))
    bspec = pl.BlockSpec((H, blk, nk), lambda b, i: (0, 0, 0))
    if nb > 1:
        in_specs, args = [cur, prev, cur, prev, cur, bspec, cur, col], [q, k, k, v, v, bias, do, dlse]
        out_specs = [cur, prev, prev, bspec]
        scratch = [pltpu.VMEM((H, blk, dh), F32), pltpu.VMEM((H, blk, dh), F32)]
    else:
        in_specs, args = [cur, cur, cur, bspec, cur, col], [q, k, v, bias, do, dlse]
        out_specs = [cur, cur, cur, bspec]
        scratch = []
    qs = jax.ShapeDtypeStruct((BD, H, L, dh), F32)
    return pl.pallas_call(
        body, name="attn_bwd", grid=(BD, nsteps), in_specs=in_specs, out_specs=out_specs,
        out_shape=[qs, qs, qs, jax.ShapeDtypeStruct((H, blk, nk), F32)],
        scratch_shapes=scratch,
        compiler_params=_cparams(("arbitrary", "arbitrary")),
    )(*args)


def _merge(o0, o1, o2, l0, l1, l2):
    m = lax.stop_gradient(jnp.maximum(jnp.maximum(l0, l1), l2))
    e0, e1, e2 = jnp.exp(l0 - m), jnp.exp(l1 - m), jnp.exp(l2 - m)
    inv = 1.0 / (e0 + e1 + e2)
    return (e0 * inv) * o0 + (e1 * inv) * o1 + (e2 * inv) * o2


def merge_fwd(os_, ls, ts):
    BH, S, dh = os_[0].shape

    def body(o0, o1, o2, l0, l1, l2, out):
        out[0] = _merge(o0[0], o1[0], o2[0], l0[0], l1[0], l2[0]).astype(out.dtype)

    ospec = pl.BlockSpec((1, ts, dh), lambda b, i: (b, i, 0))
    lspec = pl.BlockSpec((1, ts, 1), lambda b, i: (b, i, 0))
    return pl.pallas_call(
        body, name="merge_fwd", grid=(BH, S // ts), in_specs=[ospec] * 3 + [lspec] * 3, out_specs=ospec,
        out_shape=jax.ShapeDtypeStruct((BH, S, dh), BF16),
        compiler_params=_cparams(("arbitrary", "arbitrary")),
    )(*os_, *ls)


def merge_bwd(os_, ls, do, ts):
    BH, S, dh = os_[0].shape

    def body(o0, o1, o2, l0, l1, l2, d_ref, do0, do1, do2, dl0, dl1, dl2):
        _, vjp = jax.vjp(_merge, o0[0], o1[0], o2[0], l0[0], l1[0], l2[0])
        g = vjp(d_ref[0])
        for r, val in zip((do0, do1, do2, dl0, dl1, dl2), g):
            r[0] = val

    ospec = pl.BlockSpec((1, ts, dh), lambda b, i: (b, i, 0))
    lspec = pl.BlockSpec((1, ts, 1), lambda b, i: (b, i, 0))
    osh = jax.ShapeDtypeStruct((BH, S, dh), F32)
    lsh = jax.ShapeDtypeStruct((BH, S, 1), F32)
    return pl.pallas_call(
        body, name="merge_bwd", grid=(BH, S // ts), in_specs=[ospec] * 3 + [lspec] * 3 + [ospec],
        out_specs=[ospec] * 3 + [lspec] * 3, out_shape=[osh] * 3 + [lsh] * 3,
        compiler_params=_cparams(("arbitrary", "arbitrary")),
    )(*os_, *ls, do)


def _split3(x):
    hi = x.astype(BF16)
    r = x - hi.astype(F32)
    mid = r.astype(BF16)
    lo = (r - mid.astype(F32)).astype(BF16)
    return hi, mid, lo


def bias_table(onehot, rb):
    NB, Q = onehot.shape
    H = rb.shape[1]

    def body(oh_ref, rb_ref, o_ref):
        oh = oh_ref[...]
        acc = jnp.zeros((H, Q), F32)
        for part in _split3(rb_ref[...]):
            acc = acc + _dg(part, oh, 0, 0, False)
        o_ref[...] = acc

    full = lambda s: pl.BlockSpec(s, lambda: (0,) * len(s))
    return pl.pallas_call(
        body, name="bias_table", in_specs=[full((NB, Q)), full((NB, H))], out_specs=full((H, Q)),
        out_shape=jax.ShapeDtypeStruct((H, Q), F32), compiler_params=_cparams(),
    )(onehot, rb)


def bias_bucket_grad(onehot, dbias):
    NB, Q = onehot.shape
    H = dbias.shape[0]

    def body(oh_ref, d_ref, o_ref):
        oh = oh_ref[...]
        acc = jnp.zeros((NB, H), F32)
        for part in _split3(d_ref[...]):
            acc = acc + _dg(oh, part, 1, 1, False)
        o_ref[...] = acc

    full = lambda s: pl.BlockSpec(s, lambda: (0,) * len(s))
    return pl.pallas_call(
        body, name="bias_bucket_grad", in_specs=[full((NB, Q)), full((H, Q))], out_specs=full((NB, H)),
        out_shape=jax.ShapeDtypeStruct((NB, H), F32), compiler_params=_cparams(),
    )(onehot, dbias)


def _shift_rows(x, k, S):
    t = lax.broadcasted_iota(jnp.int32, x.shape, 0)
    if k >= 0:
        return jnp.where(t >= k, pltpu.roll(x, k, 0), 0.0)
    return jnp.where(t < S + k, pltpu.roll(x, S + k, 0), 0.0)


def conv_fwd(x, w, ncol, cb):
    B, S, _ = x.shape

    def body(x_ref, w_ref, o_ref):
        xv = x_ref[0]
        wv = w_ref[...]
        acc = xv * wv[B_CONV - 1:B_CONV]
        for k in range(1, B_CONV):
            acc = acc + _shift_rows(xv, k, S) * wv[B_CONV - 1 - k:B_CONV - k]
        o_ref[0] = acc

    blk = pl.BlockSpec((1, S, cb), lambda b, j: (b, 0, j))
    return pl.pallas_call(
        body, name="conv_fwd", grid=(B, ncol // cb),
        in_specs=[blk, pl.BlockSpec((B_CONV, cb), lambda b, j: (0, j))], out_specs=blk,
        out_shape=jax.ShapeDtypeStruct((B, S, ncol), F32),
        compiler_params=_cparams(("arbitrary", "arbitrary")),
    )(x, w)


def conv_bwd(x, w, dc, col0, cb):
    B, S, C = dc.shape
    j0 = col0 // cb

    def body(x_ref, w_ref, dc_ref, dx_ref, dw_ref):
        xv = x_ref[0]
        wv = w_ref[...]
        d = dc_ref[0]
        acc = d * wv[B_CONV - 1:B_CONV]
        rows = [jnp.sum(d * xv, axis=0, keepdims=True)]
        for k in range(1, B_CONV):
            acc = acc + _shift_rows(d, -k, S) * wv[B_CONV - 1 - k:B_CONV - k]
            rows.append(jnp.sum(d * _shift_rows(xv, k, S), axis=0, keepdims=True))
        dx_ref[0] = acc.astype(dx_ref.dtype)

        @pl.when(pl.program_id(1) == 0)
        def _():
            dw_ref[...] = jnp.zeros_like(dw_ref)

        dw_ref[...] += jnp.concatenate(rows[::-1], axis=0)

    return pl.pallas_call(
        body, name="conv_bwd", grid=(C // cb, B),
        in_specs=[pl.BlockSpec((1, S, cb), lambda j, b: (b, 0, j + j0)),
                  pl.BlockSpec((B_CONV, cb), lambda j, b: (0, j)),
                  pl.BlockSpec((1, S, cb), lambda j, b: (b, 0, j))],
        out_specs=[pl.BlockSpec((1, S, cb), lambda j, b: (b, 0, j)), pl.BlockSpec((B_CONV, cb), lambda j, b: (0, j))],
        out_shape=[jax.ShapeDtypeStruct((B, S, C), BF16), jax.ShapeDtypeStruct((B_CONV, C), F32)],
        compiler_params=_cparams(("arbitrary", "arbitrary")),
    )(x, w, dc)


def _lane_half(shape):
    return lax.broadcasted_iota(jnp.int32, shape, len(shape) - 1) < LANES


def _bd(xw):
    first = _lane_half(xw.shape)
    return jnp.concatenate([jnp.where(first, xw, 0.0), jnp.where(first, 0.0, xw)], axis=0)


def _unbd(y):
    r = y.shape[0] // 2
    return jnp.where(_lane_half((r, y.shape[1])), y[:r], y[r:])


def _rs(xw):
    return jnp.concatenate([xw[:, :LANES], xw[:, LANES:]], axis=0)


def _unrs(y):
    r = y.shape[0] // 2
    return jnp.concatenate([y[:r], y[r:]], axis=1)


def _make_pair_dots(hp):
    @jax.custom_vjp
    def nn(xw, pw):
        return _dg(xw, _bd(pw), 1, 0, hp)

    def nn_f(xw, pw):
        return nn(xw, pw), (xw, pw)

    def nn_b(r, ct):
        xw, pw = r
        return _dg(ct, _bd(pw), 1, 1, hp), _unbd(_dg(xw, ct, 0, 0, hp))

    nn.defvjp(nn_f, nn_b)

    @jax.custom_vjp
    def nt(xw, yw):
        return _dg(xw, _bd(yw), 1, 1, hp)

    def nt_f(xw, yw):
        return nt(xw, yw), (xw, yw)

    def nt_b(r, ct):
        xw, yw = r
        return _dg(ct, _bd(yw), 1, 0, hp), _unbd(_dg(ct, xw, 0, 0, hp))

    nt.defvjp(nt_f, nt_b)

    @jax.custom_vjp
    def tn(xw, yw):
        return _dg(_rs(xw), _bd(yw), 0, 0, hp)

    def tn_f(xw, yw):
        return tn(xw, yw), (xw, yw)

    def tn_b(r, ct):
        xw, yw = r
        return _unrs(_dg(_bd(yw), ct, 1, 1, hp)), _unbd(_dg(_rs(xw), ct, 1, 0, hp))

    tn.defvjp(tn_f, tn_b)
    return nn, nt, tn


pdot_nn, pdot_nt, pdot_tn = _make_pair_dots(False)
phdot_nn, phdot_nt, phdot_tn = _make_pair_dots(True)


@jax.custom_vjp
def _tril_cumsum(gw):
    C = gw.shape[0]
    r = lax.broadcasted_iota(jnp.int32, (C, C), 0)
    c = lax.broadcasted_iota(jnp.int32, (C, C), 1)
    return _dg(jnp.where(r >= c, 1.0, 0.0), gw, 1, 0, True)


def _tril_cumsum_f(gw):
    return _tril_cumsum(gw), gw.shape[0]


def _tril_cumsum_b(C, ct):
    r = lax.broadcasted_iota(jnp.int32, (C, C), 0)
    c = lax.broadcasted_iota(jnp.int32, (C, C), 1)
    return (_dg(jnp.where(r <= c, 1.0, 0.0), ct, 1, 0, True),)


_tril_cumsum.defvjp(_tril_cumsum_f, _tril_cumsum_b)


def _pair(fa, fb, shape):
    return jnp.where(_lane_half(shape), fa, fb)


def _half_sums(xw):
    sa = jnp.sum(xw[:, :LANES], axis=-1, keepdims=True)
    sb = jnp.sum(xw[:, LANES:], axis=-1, keepdims=True)
    return _pair(sa, sb, xw.shape)


def _neumann_inverse_pair(low):
    C = low.shape[0]
    r = lax.broadcasted_iota(jnp.int32, low.shape, 0)
    c = lax.broadcasted_iota(jnp.int32, low.shape, 1) % LANES
    x = jnp.where(r == c, 1.0, 0.0) - low
    p = phdot_nn(low, low)
    n = 2
    while n < C:
        x = x + phdot_nn(x, p)
        n *= 2
        if n < C:
            p = phdot_nn(p, p)
    return x


@jax.custom_vjp
def _saved_inverse(low, tinv):
    return tinv


def _saved_inverse_f(low, tinv):
    return tinv, tinv


def _saved_inverse_b(tinv, ct):
    return -phdot_nt(phdot_tn(tinv, ct), tinv), jnp.zeros_like(tinv)


_saved_inverse.defvjp(_saved_inverse_f, _saved_inverse_b)


def _gdn_chunk_pair(cq, ck, cv, z, ab, prm_a, prm_b, nw, state, h0, tinv_saved=None):
    C, W2 = cq.shape
    lane = lax.broadcasted_iota(jnp.int32, ab.shape, 1)
    col = lambda j: jnp.sum(jnp.where(lane == j, ab, 0.0), axis=1, keepdims=True)
    a = _pair(col(h0), col(h0 + 1), (C, W2))
    b = _pair(col(h0 + B_HEADS), col(h0 + 1 + B_HEADS), (C, W2))
    a_log = _pair(prm_a[:, 0:1], prm_b[:, 0:1], (1, W2))
    dtb = _pair(prm_a[:, 1:2], prm_b[:, 1:2], (1, W2))
    nw2 = jnp.concatenate([nw, nw], axis=1)
    q = _silu(cq)
    q = q * lax.rsqrt(_half_sums(q * q) + NORM_EPS) * (B_HEAD_DIM ** -0.5)
    k = _silu(ck)
    k = k * lax.rsqrt(_half_sums(k * k) + NORM_EPS)
    v = _silu(cv)
    beta = _sigmoid(b)
    g = -jnp.exp(a_log) * _softplus(a + dtb)

    r = lax.broadcasted_iota(jnp.int32, (C, W2), 0)
    c = lax.broadcasted_iota(jnp.int32, (C, W2), 1) % LANES
    tril = r >= c
    gc = _tril_cumsum(g)
    gcj = jnp.concatenate([gc[:, :LANES].T, gc[:, LANES:].T], axis=1)
    decay = jnp.where(tril, jnp.exp(jnp.where(tril, gc - gcj, 0.0)), 0.0)
    gl = gc[C - 1:C, :]
    egc = jnp.exp(gc)

    kb = k * beta
    kk = pdot_nt(jnp.concatenate([kb, q], axis=0), k)
    low = jnp.where(r > c, kk[:C] * decay, 0.0)
    attn = kk[C:] * decay
    tinv = _neumann_inverse_pair(low) if tinv_saved is None else _saved_inverse(low, tinv_saved)
    u = pdot_nn(tinv, v * beta)
    w = pdot_nn(tinv, kb * egc)
    ws = pdot_nn(jnp.concatenate([w, q * egc], axis=0), state)
    v_new = u - ws[:C]
    o = ws[C:] + pdot_nn(attn, v_new)
    new_state = state * jnp.exp(gl) + pdot_tn(k * jnp.exp(gl - gc), v_new)
    y = o * lax.rsqrt(_half_sums(o * o) * (1.0 / B_HEAD_DIM) + NORM_EPS) * nw2 * _silu(z)
    if tinv_saved is None:
        return y, new_state, tinv
    return y, new_state


def gdn_fwd(c, proj, prm, nw):
    B, S, _ = c.shape
    H, dh, C = B_HEADS, B_HEAD_DIM, B_CHUNK
    N, HP, W2 = S // C, H // 2, 2 * dh

    def body(cq_ref, ck_ref, cv_ref, z_ref, ab_ref, prm_ref, nw_ref, y_ref, st_ref, ti_ref, state):
        h0 = pl.program_id(1) * 2
        state[...] = jnp.zeros_like(state)
        nwv = nw_ref[...]

        def step(n, carry):
            rows = pl.ds(pl.multiple_of(n * C, C), C)
            s_in = state[...]
            st_ref[0, n] = s_in
            y, s_out, tinv = _gdn_chunk_pair(cq_ref[0, rows, :], ck_ref[0, rows, :], cv_ref[0, rows, :],
                                             z_ref[0, rows, :], ab_ref[0, rows, :], prm_ref[h0], prm_ref[h0 + 1],
                                             nwv, s_in, h0)
            ti_ref[0, n] = tinv
            y_ref[0, rows, :] = y.astype(y_ref.dtype)
            state[...] = s_out
            return carry

        lax.fori_loop(0, N, step, 0)

    def col(off):
        return pl.BlockSpec((1, S, W2), lambda b, h: (b, 0, h + off // 2))

    return pl.pallas_call(
        body, name="gdn_fwd", grid=(B, HP),
        in_specs=[col(0), col(H), col(2 * H), col(3 * H), pl.BlockSpec((1, S, B_AB_PAD), lambda b, h: (b, 0, 4 * H)),
                  pl.BlockSpec((H, 1, LANES), lambda b, h: (0, 0, 0)), pl.BlockSpec((1, dh), lambda b, h: (0, 0))],
        out_specs=[col(0), pl.BlockSpec((1, N, dh, W2), lambda b, h: (b * HP + h, 0, 0, 0)),
                   pl.BlockSpec((1, N, C, W2), lambda b, h: (b * HP + h, 0, 0, 0))],
        out_shape=[jax.ShapeDtypeStruct((B, S, H * dh), BF16), jax.ShapeDtypeStruct((B * HP, N, dh, W2), F32),
                   jax.ShapeDtypeStruct((B * HP, N, C, W2), F32)],
        scratch_shapes=[pltpu.VMEM((dh, W2), F32)],
        compiler_params=_cparams(("arbitrary", "arbitrary")),
    )(c, c, c, proj, proj, prm, nw)


def gdn_bwd(c, proj, prm, nw, states, tinvs, dy):
    B, S, _ = c.shape
    H, dh, C = B_HEADS, B_HEAD_DIM, B_CHUNK
    N, HP, W2 = S // C, H // 2, 2 * dh

    def body(cq_ref, ck_ref, cv_ref, z_ref, ab_ref, prm_ref, nw_ref, st_ref, ti_ref, dy_ref,
             dq_ref, dk_ref, dv_ref, dz_ref, dab_ref, dprm_ref, dnw_ref, dstate):
        b = pl.program_id(0)
        hp = pl.program_id(1)
        h0 = hp * 2
        dstate[...] = jnp.zeros_like(dstate)

        @pl.when((b == 0) & (hp == 0))
        def _():
            dprm_ref[...] = jnp.zeros_like(dprm_ref)
            dnw_ref[...] = jnp.zeros_like(dnw_ref)

        @pl.when(hp == 0)
        def _():
            dab_ref[...] = jnp.zeros_like(dab_ref)

        nwv = nw_ref[...]

        def step(t, carry):
            n = N - 1 - t
            rows = pl.ds(pl.multiple_of(n * C, C), C)
            _, vjp = jax.vjp(functools.partial(_gdn_chunk_pair, h0=h0, tinv_saved=ti_ref[0, n]), cq_ref[0, rows, :],
                             ck_ref[0, rows, :], cv_ref[0, rows, :], z_ref[0, rows, :], ab_ref[0, rows, :],
                             prm_ref[h0], prm_ref[h0 + 1], nwv, st_ref[0, n])
            dcq, dck, dcv, dz, dab, dpa, dpb, dnw, ds = vjp((dy_ref[0, rows, :].astype(F32), dstate[...]))
            dq_ref[0, rows, :] = dcq
            dk_ref[0, rows, :] = dck
            dv_ref[0, rows, :] = dcv
            dz_ref[0, rows, :] = dz.astype(dz_ref.dtype)
            dab_ref[0, rows, :] += dab
            dprm_ref[h0] += dpa
            dprm_ref[h0 + 1] += dpb
            dnw_ref[...] += dnw
            dstate[...] = ds
            return carry

        lax.fori_loop(0, N, step, 0)

    def col(off):
        return pl.BlockSpec((1, S, W2), lambda b, h: (b, 0, h + off // 2))

    abspec = pl.BlockSpec((1, S, B_AB_PAD), lambda b, h: (b, 0, 4 * H))
    ab0 = pl.BlockSpec((1, S, B_AB_PAD), lambda b, h: (b, 0, 0))
    prm_full = pl.BlockSpec((H, 1, LANES), lambda b, h: (0, 0, 0))
    nwspec = pl.BlockSpec((1, dh), lambda b, h: (0, 0))
    wsh = jax.ShapeDtypeStruct((B, S, H * dh), F32)
    return pl.pallas_call(
        body, name="gdn_bwd", grid=(B, HP),
        in_specs=[col(0), col(H), col(2 * H), col(3 * H), abspec, prm_full, nwspec,
                  pl.BlockSpec((1, N, dh, W2), lambda b, h: (b * HP + h, 0, 0, 0)),
                  pl.BlockSpec((1, N, C, W2), lambda b, h: (b * HP + h, 0, 0, 0)), col(0)],
        out_specs=[col(0), col(0), col(0), col(0), ab0, prm_full, nwspec],
        out_shape=[wsh, wsh, wsh, jax.ShapeDtypeStruct((B, S, H * dh), BF16),
                   jax.ShapeDtypeStruct((B, S, B_AB_PAD), F32), jax.ShapeDtypeStruct((H, 1, LANES), F32),
                   jax.ShapeDtypeStruct((1, dh), F32)],
        scratch_shapes=[pltpu.VMEM((dh, W2), F32)],
        compiler_params=_cparams(("arbitrary", "arbitrary"), VMEM_LIMIT_BIG),
    )(c, c, c, proj, proj, prm, nw, states, tinvs, dy)


def _my_place():
    return lax.axis_index("x"), lax.axis_index("y"), lax.axis_index("c")


HBM_SPEC = pl.BlockSpec(memory_space=pltpu.HBM)


def allgather_shards(shards):
    T = len(shards)

    def body(*refs):
        x_refs, out_refs = refs[:T], refs[T:2 * T]
        send_sems, recv_sems, local_sems = refs[2 * T:]
        x, y, c = _my_place()
        me, sibling = (x, y, c), (x, y, 1 - c)
        chips = [(1 - x, y), (x, 1 - y), (1 - x, 1 - y)]

        def copy(t, k, block, to, src=None):
            px, py, pc = block
            slot = out_refs[t].at[4 * px + 2 * py + pc]
            return pltpu.make_async_remote_copy(
                src_ref=slot if src is None else src, dst_ref=slot, send_sem=send_sems.at[7 * t + k],
                recv_sem=recv_sems.at[7 * t + k], device_id=to, device_id_type=MESH)

        started, mine = [], []
        for t in range(T):
            mine.append(pltpu.make_async_copy(x_refs[t], out_refs[t].at[4 * x + 2 * y + c], local_sems.at[t]))
            mine[-1].start()
            first = [copy(t, 0, me, sibling, src=x_refs[t])]
            first += [copy(t, 1 + j, me, (*chip, c), src=x_refs[t]) for j, chip in enumerate(chips)]
            for cp in first:
                cp.start()
            started += first
        for j, chip in enumerate(chips):
            for t in range(T):
                copy(t, 1 + j, (*chip, c), me).wait_recv()
                fwd = copy(t, 4 + j, (*chip, c), sibling)
                fwd.start()
                started.append(fwd)
        for t in range(T):
            copy(t, 0, sibling, me).wait_recv()
            for j, chip in enumerate(chips):
                copy(t, 4 + j, (*chip, 1 - c), me).wait_recv()
        for cp in started:
            cp.wait_send()
        for cp in mine:
            cp.wait()

    return pl.pallas_call(
        body, name="allgather_shards",
        out_shape=[jax.ShapeDtypeStruct((N_DEV,) + s.shape, s.dtype) for s in shards],
        in_specs=[HBM_SPEC] * T, out_specs=[HBM_SPEC] * T,
        scratch_shapes=[pltpu.SemaphoreType.DMA((7 * T,)), pltpu.SemaphoreType.DMA((7 * T,)),
                        pltpu.SemaphoreType.DMA((T,))],
    )(*shards)


def exchange_sibling(gs):
    T = len(gs)

    def body(*refs):
        g_refs, out_refs = refs[:T], refs[T:2 * T]
        send_sems, recv_sems = refs[2 * T:]
        x, y, c = _my_place()
        copies = []
        for t in range(T):
            for k in range(4):
                copies.append(pltpu.make_async_remote_copy(
                    src_ref=g_refs[t].at[2 * k + (1 - c)], dst_ref=out_refs[t].at[k],
                    send_sem=send_sems.at[4 * t + k], recv_sem=recv_sems.at[4 * t + k],
                    device_id=(x, y, 1 - c), device_id_type=MESH))
        for cp in copies:
            cp.start()
        for cp in copies:
            cp.wait()

    return pl.pallas_call(
        body, name="exchange_sibling",
        out_shape=[jax.ShapeDtypeStruct((4,) + g.shape[1:], g.dtype) for g in gs],
        in_specs=[HBM_SPEC] * T, out_specs=[HBM_SPEC] * T,
        scratch_shapes=[pltpu.SemaphoreType.DMA((4 * T,)), pltpu.SemaphoreType.DMA((4 * T,))],
    )(*gs)


def _row_tile(rows, row_bytes, budget):
    if rows * row_bytes <= budget:
        return rows
    best = None
    for t in range(16, rows, 16):
        if rows % t == 0 and t * row_bytes <= budget:
            best = t
    return best if best is not None else rows


def add_sibling(g, got, c_idx):
    _, R, L = g.shape
    tr = _row_tile(R, L * 4, 2 << 20)

    def body(c_ref, g_ref, r_ref, o_ref):
        o_ref[...] = (g_ref[...].astype(F32) + r_ref[...].astype(F32)).astype(o_ref.dtype)

    grid_spec = pltpu.PrefetchScalarGridSpec(
        num_scalar_prefetch=1, grid=(4, R // tr),
        in_specs=[pl.BlockSpec((1, tr, L), lambda k, i, c: (2 * k + c[0], i, 0)),
                  pl.BlockSpec((1, tr, L), lambda k, i, c: (k, i, 0))],
        out_specs=pl.BlockSpec((1, tr, L), lambda k, i, c: (k, i, 0)))
    return pl.pallas_call(
        body, name="add_sibling", grid_spec=grid_spec, out_shape=jax.ShapeDtypeStruct((4, R, L), RS_DTYPE),
        compiler_params=_cparams(("arbitrary", "arbitrary")),
    )(c_idx, g, got)


def exchange_chips(ps):
    T = len(ps)

    def body(*refs):
        p_refs, out_refs = refs[:T], refs[T:2 * T]
        send_sems, recv_sems, local_sems = refs[2 * T:]
        x, y, c = _my_place()
        mychip = 2 * x + y
        chips = [(1 - x, y), (x, 1 - y), (1 - x, 1 - y)]
        mine, copies = [], []
        for t in range(T):
            mine.append(pltpu.make_async_copy(p_refs[t].at[mychip], out_refs[t].at[mychip], local_sems.at[t]))
            mine[-1].start()
            for j, (px, py) in enumerate(chips):
                copies.append(pltpu.make_async_remote_copy(
                    src_ref=p_refs[t].at[2 * px + py], dst_ref=out_refs[t].at[mychip],
                    send_sem=send_sems.at[3 * t + j], recv_sem=recv_sems.at[3 * t + j],
                    device_id=(px, py, c), device_id_type=MESH))
                copies[-1].start()
        for t in range(T):
            for j, (px, py) in enumerate(chips):
                pltpu.make_async_remote_copy(
                    src_ref=p_refs[t].at[mychip], dst_ref=out_refs[t].at[2 * px + py],
                    send_sem=send_sems.at[3 * t + j], recv_sem=recv_sems.at[3 * t + j],
                    device_id=(px, py, c), device_id_type=MESH).wait_recv()
        for cp in copies:
            cp.wait_send()
        for cp in mine:
            cp.wait()

    return pl.pallas_call(
        body, name="exchange_chips", out_shape=[jax.ShapeDtypeStruct(p.shape, p.dtype) for p in ps],
        in_specs=[HBM_SPEC] * T, out_specs=[HBM_SPEC] * T,
        scratch_shapes=[pltpu.SemaphoreType.DMA((3 * T,)), pltpu.SemaphoreType.DMA((3 * T,)),
                        pltpu.SemaphoreType.DMA((T,))],
    )(*ps)


def allgather_small(s):
    R, W = s.shape

    def body(s_ref, out_ref, send_sems, recv_sems):
        x, y, c = _my_place()
        me = 4 * x + 2 * y + c
        out_ref[me] = s_ref[...]
        copies = []
        for j in range(1, N_DEV):
            peer = (x ^ (j >> 2), y ^ ((j >> 1) & 1), c ^ (j & 1))
            copies.append(pltpu.make_async_remote_copy(
                src_ref=s_ref, dst_ref=out_ref.at[me], send_sem=send_sems.at[j - 1], recv_sem=recv_sems.at[j - 1],
                device_id=peer, device_id_type=MESH))
        for cp in copies:
            cp.start()
        for j in range(1, N_DEV):
            px, py, pc = x ^ (j >> 2), y ^ ((j >> 1) & 1), c ^ (j & 1)
            pltpu.make_async_remote_copy(
                src_ref=s_ref, dst_ref=out_ref.at[4 * px + 2 * py + pc], send_sem=send_sems.at[j - 1],
                recv_sem=recv_sems.at[j - 1], device_id=(px, py, pc), device_id_type=MESH).wait_recv()
        for cp in copies:
            cp.wait_send()

    vm = pl.BlockSpec(memory_space=pltpu.VMEM)
    return pl.pallas_call(
        body, name="allgather_small", out_shape=jax.ShapeDtypeStruct((N_DEV, R, W), s.dtype),
        in_specs=[vm], out_specs=vm,
        scratch_shapes=[pltpu.SemaphoreType.DMA((7,)), pltpu.SemaphoreType.DMA((7,))],
    )(s)


def _adamw(w, g, m, v):
    m = ADAM_B1 * m + (1.0 - ADAM_B1) * g
    v = ADAM_B2 * v + (1.0 - ADAM_B2) * jnp.square(g)
    m_hat = m / (1.0 - ADAM_B1 ** ADAM_STEP)
    v_hat = v / (1.0 - ADAM_B2 ** ADAM_STEP)
    delta = -ADAM_LR * (m_hat / (jnp.sqrt(v_hat) + ADAM_EPS) + ADAM_WD * w)
    return delta, m, v


def sum_adamw(parts, w, m, v):
    P, G, Rp, Lp = parts.shape
    _, R, L = w.shape
    tr = _row_tile(R, P * Lp * parts.dtype.itemsize + 7 * L * 4, 5 << 20)

    def body(p_ref, w_ref, m_ref, v_ref, g_ref, d_ref, nm_ref, nv_ref):
        g = p_ref[0, 0].astype(F32)
        for k in range(1, P):
            g = g + p_ref[k, 0].astype(F32)
        g = g[:, :L]
        d, nm, nv = _adamw(w_ref[0], g, m_ref[0], v_ref[0])
        g_ref[0] = g
        d_ref[0] = d
        nm_ref[0] = nm
        nv_ref[0] = nv

    row = pl.BlockSpec((1, tr, L), lambda gi, i: (gi, i, 0))
    sh = jax.ShapeDtypeStruct((G, R, L), F32)
    return pl.pallas_call(
        body, name="sum_adamw", grid=(G, R // tr),
        in_specs=[pl.BlockSpec((P, 1, tr, Lp), lambda gi, i: (0, gi, i, 0)), row, row, row],
        out_specs=[row] * 4, out_shape=[sh] * 4, compiler_params=_cparams(("arbitrary", "arbitrary")),
    )(parts, w, m, v)


def _unblock(t, axis):
    sh = t.shape[1:]
    t = jnp.moveaxis(t, 0, axis)
    return t.reshape(sh[:axis] + (N_DEV * sh[axis],) + sh[axis + 1:])


def _block(a, axis):
    sh = a.shape
    t = a.reshape(sh[:axis] + (N_DEV, sh[axis] // N_DEV) + sh[axis + 1:])
    return jnp.moveaxis(t, axis, 0)


def _rows3(t):
    return t.reshape((t.shape[0], -1, t.shape[-1]))


def _pad_to(a, axis, n):
    pad = [(0, 0)] * a.ndim
    pad[axis] = (0, n - a.shape[axis])
    return jnp.pad(a, pad)


def _lane_pad(n):
    return -(-n // LANES) * LANES


def _pack(arrs, row_mult):
    flat = jnp.concatenate([a.astype(F32).reshape(-1) for a in arrs])
    n = flat.shape[0]
    rows = -(-n // LANES)
    rows = -(-rows // row_mult) * row_mult
    return jnp.pad(flat, (0, rows * LANES - n)).reshape(rows, LANES)


def _unpack(buf, shapes):
    flat = buf.reshape(-1)
    out, off = [], 0
    for sh in shapes:
        n = math.prod(sh)
        out.append(flat[off:off + n].reshape(sh))
        off += n
    return out


def _ffn_fwd(x, g, wg, wu, wd, tm):
    h = rmsnorm_fwd(x, g, tm)
    gate, up, act = ffn_up(h, wg, wu)
    x_new = mm_nn(act, wd, res=x, res_scale=MACARON_WEIGHT, name="ffn_down")
    return x_new, (x, h, gate, up, act)


def _ffn_bwd(dx, saved, g, wg, wu, wd, tm):
    x, h, gate, up, act = saved
    dout = scale_cast(dx, MACARON_WEIGHT, tm)
    dgate, dup = ffn_bwd_act(dout, wd, gate, up)
    dwd = mm_tn(act, dout, name="ffn_dwd")
    dwg = mm_tn(h, dgate, name="ffn_dwg")
    dwu = mm_tn(h, dup, name="ffn_dwu")
    dh = mm_nt([(dgate, wg), (dup, wu)], name="ffn_dh")
    dx_in, dg = rmsnorm_bwd(x, g, dh, dx, tm)
    return dx_in, dg, dwg, dwu, dwd


def _to_strided(t, B, S, d):
    L = S // d
    t = t.reshape(B, L, d, A_HEADS, A_HEAD_DIM)
    return jnp.transpose(t, (0, 2, 3, 1, 4)).reshape(B * d, A_HEADS, L, A_HEAD_DIM)


def _from_strided(t, B, S, d):
    L = S // d
    w = t.shape[-1]
    t = t.reshape(B, d, A_HEADS, L, w)
    return jnp.transpose(t, (0, 2, 3, 1, 4)).reshape(B * A_HEADS, S, w)


def _strided_from_heads(t, B, S, d):
    L = S // d
    w = t.shape[-1]
    t = t.reshape(B, A_HEADS, L, d, w)
    return jnp.transpose(t, (0, 3, 1, 2, 4)).reshape(B * d, A_HEADS, L, w)


def _strided_to_cols(t, B, S, d):
    L = S // d
    t = t.reshape(B, d, A_HEADS, L, A_HEAD_DIM)
    return jnp.transpose(t, (0, 3, 1, 2, 4)).reshape(B * S, A_GROUP_WIDTH)


def _bucket_onehot(nb, d):
    idx = _bucket_index(A_BLOCK, nb, d)
    onehot = (idx.reshape(-1)[None, :] == jnp.arange(NUM_BUCKETS)[:, None]).astype(BF16)
    return onehot, idx.shape


def _mixer_a_fwd(x, g, w_in, w_out, rel_bias, B, S, tm):
    h = rmsnorm_fwd(x, g, tm)
    proj = mm_nn(h, w_in, name="a_proj")
    gw = A_GROUP_WIDTH
    qkv, biases, outs, lses = [], [], [], []
    for gi, (window, d) in enumerate(A_PATTERNS):
        base = gi * 3 * gw
        q, k, v = (_to_strided(proj[:, base + t * gw: base + (t + 1) * gw], B, S, d) for t in range(3))
        nb = (S // d) // A_BLOCK
        onehot, ishape = _bucket_onehot(nb, d)
        bias = bias_table(onehot, rel_bias[:, gi * A_HEADS:(gi + 1) * A_HEADS]).reshape((A_HEADS,) + ishape)
        o, lse = attn_fwd(q, k, v, bias, window // d)
        qkv.append((q, k, v))
        biases.append(bias)
        outs.append(_from_strided(o, B, S, d))
        lses.append(_from_strided(lse, B, S, d))
    ts = _pick(S, (512, 256, 128))
    om = merge_fwd(outs, lses, ts)
    om2 = jnp.transpose(om.reshape(B, A_HEADS, S, A_HEAD_DIM), (0, 2, 1, 3)).reshape(B * S, gw)
    x_new = mm_nn(om2, w_out, res=x, name="a_out")
    return x_new, (x, h, qkv, biases, outs, lses, om2)


def _mixer_a_bwd(dx, saved, g, w_in, w_out, B, S, tm):
    x, h, qkv, biases, outs, lses, om2 = saved
    gw = A_GROUP_WIDTH
    dxb = scale_cast(dx, 1.0, tm)
    dw_out = mm_tn(om2, dxb, name="a_dwout")
    dom = mm_nt([(dxb, w_out)], name="a_dom")
    dom = jnp.transpose(dom.reshape(B, S, A_HEADS, A_HEAD_DIM), (0, 2, 1, 3)).reshape(B * A_HEADS, S, A_HEAD_DIM)
    ts = _pick(S, (512, 256, 128))
    g6 = merge_bwd(outs, lses, dom, ts)
    dparts, dbias_cols = [], []
    for gi, (window, d) in enumerate(A_PATTERNS):
        q, k, v = qkv[gi]
        do = _strided_from_heads(g6[gi], B, S, d)
        dl = _strided_from_heads(g6[3 + gi], B, S, d)
        dq, dk, dv, dbias = attn_bwd(q, k, v, biases[gi], do, dl, window // d)
        dparts += [_strided_to_cols(t, B, S, d) for t in (dq, dk, dv)]
        onehot, _ = _bucket_onehot((S // d) // A_BLOCK, d)
        dbias_cols.append(bias_bucket_grad(onehot, dbias.reshape(A_HEADS, -1)))
    d_rel_bias = jnp.concatenate(dbias_cols, axis=1)
    dproj = jnp.concatenate(dparts, axis=1).astype(BF16)
    dw_in = mm_tn(h, dproj, name="a_dwin")
    dh = mm_nt([(dproj, w_in)], name="a_dh")
    dx_in, dg = rmsnorm_bwd(x, g, dh, dx, tm)
    return dx_in, dg, dw_in, dw_out, d_rel_bias


def _mixer_b_fwd(x, g, w_in_p, conv_w, prm, nw, w_out, B, S, tm):
    h = rmsnorm_fwd(x, g, tm)
    proj = mm_nn(h, w_in_p, name="b_proj").reshape(B, S, -1)
    c = conv_fwd(proj, conv_w, 3 * B_WIDTH, 512)
    y, states, tinvs = gdn_fwd(c, proj, prm, nw)
    y2 = y.reshape(B * S, B_WIDTH)
    x_new = mm_nn(y2, w_out, res=x, name="b_out")
    return x_new, (x, h, proj, c, states, tinvs, y2)


def _mixer_b_bwd(dx, saved, g, w_in_p, conv_w, prm, nw, w_out, B, S, tm):
    x, h, proj, c, states, tinvs, y2 = saved
    W = B_WIDTH
    dxb = scale_cast(dx, 1.0, tm)
    dw_out = mm_tn(y2, dxb, name="b_dwout")
    dy = mm_nt([(dxb, w_out)], out_dtype=BF16, name="b_dy").reshape(B, S, W)
    dcq, dck, dcv, dz, dab, dprm, dnw = gdn_bwd(c, proj, prm, nw, states, tinvs, dy)
    dxs, dws = [], []
    for t, dc in enumerate((dcq, dck, dcv)):
        dxp, dwp = conv_bwd(proj, conv_w[:, t * W:(t + 1) * W], dc, t * W, 512)
        dxs.append(dxp)
        dws.append(dwp)
    dconv_w = jnp.concatenate(dws, axis=1)
    dproj = jnp.concatenate(dxs + [dz, dab.astype(BF16)], axis=2).reshape(B * S, -1)
    dw_in_p = mm_tn(h, dproj, name="b_dwin")
    dh = mm_nt([(dproj, w_in_p)], name="b_dh")
    dx_in, dg = rmsnorm_bwd(x, g, dh, dx, tm)
    return dx_in, dg, dw_in_p, dconv_w, dprm, dnw, dw_out


SHARD_NAMES = ("ffn_w_gate", "ffn_w_up", "ffn_w_down", "a_w_in", "a_w_out", "b_w_in", "b_w_out", "norm_g", "b_conv_w")
SHARD_AXES = (3, 3, 2, 2, 2, 2, 1, 2, 2)
SMALL_NAMES = ("rel_bias", "b_a_log", "b_dt_bias", "b_norm_w", "final_g")


def kernel(x, norm_g, ffn_w_gate, ffn_w_up, ffn_w_down, rel_bias, a_w_in, a_w_out, b_w_in, b_conv_w, b_a_log, b_dt_bias, b_norm_w, b_w_out, final_g, loss_target, m_norm_g, m_ffn_w_gate, m_ffn_w_up, m_ffn_w_down, m_rel_bias, m_a_w_in, m_a_w_out, m_b_w_in, m_b_conv_w, m_b_a_log, m_b_dt_bias, m_b_norm_w, m_b_w_out, m_final_g, v_norm_g, v_ffn_w_gate, v_ffn_w_up, v_ffn_w_down, v_rel_bias, v_a_w_in, v_a_w_out, v_b_w_in, v_b_conv_w, v_b_a_log, v_b_dt_bias, v_b_norm_w, v_b_w_out, v_final_g):
    W = dict(norm_g=norm_g, ffn_w_gate=ffn_w_gate, ffn_w_up=ffn_w_up, ffn_w_down=ffn_w_down, rel_bias=rel_bias,
             a_w_in=a_w_in, a_w_out=a_w_out, b_w_in=b_w_in, b_conv_w=b_conv_w, b_a_log=b_a_log,
             b_dt_bias=b_dt_bias, b_norm_w=b_norm_w, b_w_out=b_w_out, final_g=final_g)
    Mo = dict(norm_g=m_norm_g, ffn_w_gate=m_ffn_w_gate, ffn_w_up=m_ffn_w_up, ffn_w_down=m_ffn_w_down,
              rel_bias=m_rel_bias, a_w_in=m_a_w_in, a_w_out=m_a_w_out, b_w_in=m_b_w_in, b_conv_w=m_b_conv_w,
              b_a_log=m_b_a_log, b_dt_bias=m_b_dt_bias, b_norm_w=m_b_norm_w, b_w_out=m_b_w_out, final_g=m_final_g)
    Vo = dict(norm_g=v_norm_g, ffn_w_gate=v_ffn_w_gate, ffn_w_up=v_ffn_w_up, ffn_w_down=v_ffn_w_down,
              rel_bias=v_rel_bias, a_w_in=v_a_w_in, a_w_out=v_a_w_out, b_w_in=v_b_w_in, b_conv_w=v_b_conv_w,
              b_a_log=v_b_a_log, b_dt_bias=v_b_dt_bias, b_norm_w=v_b_norm_w, b_w_out=v_b_w_out, final_g=v_final_g)

    B, S, D = x.shape
    M = B * S
    tm = _pick(M, (512, 256, 128))
    c_idx = lax.axis_index("c").astype(jnp.int32).reshape(1)
    fp = _lane_pad(ffn_w_gate.shape[3])

    shards = [_pad_to(ffn_w_gate.astype(BF16), 3, fp), _pad_to(ffn_w_up.astype(BF16), 3, fp),
              _pad_to(ffn_w_down.astype(BF16), 2, fp), a_w_in.astype(BF16), a_w_out.astype(BF16),
              b_w_in.astype(BF16), b_w_out.astype(BF16), norm_g, b_conv_w]
    gathered = allgather_shards(shards)
    wg_all, wu_all, wd_all, a_in_all, a_out_all, b_in_all, b_out_all, norm_all, conv_all = (
        _unblock(t, ax) for t, ax in zip(gathered, SHARD_AXES))
    b_in_pad = _pad_to(b_in_all, 2, b_in_all.shape[2] + B_AB_PAD - 2 * B_HEADS)
    prm_all = _pad_to(jnp.stack([b_a_log, b_dt_bias], axis=-1)[:, :, None, :], 3, LANES)

    xs = x.reshape(M, D)
    saved = []
    for i in range(DEPTH):
        j = i // 2
        xs, s1 = _ffn_fwd(xs, norm_all[i, 0][None], wg_all[i, 0], wu_all[i, 0], wd_all[i, 0], tm)
        if i % 2 == 0:
            xs, s2 = _mixer_a_fwd(xs, norm_all[i, 1][None], a_in_all[j], a_out_all[j], rel_bias, B, S, tm)
        else:
            xs, s2 = _mixer_b_fwd(xs, norm_all[i, 1][None], b_in_pad[j], conv_all[j], prm_all[j],
                                  b_norm_w[j][None], b_out_all[j], B, S, tm)
        xs, s3 = _ffn_fwd(xs, norm_all[i, 2][None], wg_all[i, 1], wu_all[i, 1], wd_all[i, 1], tm)
        saved.append((s1, s2, s3))

    dx, d_final_g, loss_row = loss_head(xs, final_g[None], loss_target.reshape(M, D), tm)

    d_norm = [[None] * 3 for _ in range(DEPTH)]
    d_wg = [[None] * 2 for _ in range(DEPTH)]
    d_wu = [[None] * 2 for _ in range(DEPTH)]
    d_wd = [[None] * 2 for _ in range(DEPTH)]
    d_a_in, d_a_out, d_b_in, d_b_out, d_conv = [None] * 2, [None] * 2, [None] * 2, [None] * 2, [None] * 2
    d_prm, d_nw = [None] * 2, [None] * 2
    d_rel = jnp.zeros(rel_bias.shape, F32)
    for i in reversed(range(DEPTH)):
        j = i // 2
        s1, s2, s3 = saved[i]
        dx, d_norm[i][2], d_wg[i][1], d_wu[i][1], d_wd[i][1] = _ffn_bwd(
            dx, s3, norm_all[i, 2][None], wg_all[i, 1], wu_all[i, 1], wd_all[i, 1], tm)
        if i % 2 == 0:
            dx, d_norm[i][1], d_a_in[j], d_a_out[j], drb = _mixer_a_bwd(
                dx, s2, norm_all[i, 1][None], a_in_all[j], a_out_all[j], B, S, tm)
            d_rel = d_rel + drb
        else:
            dx, d_norm[i][1], dbp, d_conv[j], d_prm[j], d_nw[j], d_b_out[j] = _mixer_b_bwd(
                dx, s2, norm_all[i, 1][None], b_in_pad[j], conv_all[j], prm_all[j], b_norm_w[j][None],
                b_out_all[j], B, S, tm)
            d_b_in[j] = dbp[:, :b_in_all.shape[2]]
        dx, d_norm[i][0], d_wg[i][0], d_wu[i][0], d_wd[i][0] = _ffn_bwd(
            dx, s1, norm_all[i, 0][None], wg_all[i, 0], wu_all[i, 0], wd_all[i, 0], tm)
    grad_x = dx.reshape(B, S, D)

    stack2 = lambda rows: jnp.stack([jnp.stack(r) for r in rows])
    full_grads = [stack2(d_wg), stack2(d_wu), stack2(d_wd), jnp.stack(d_a_in), jnp.stack(d_a_out),
                  jnp.stack(d_b_in), jnp.stack(d_b_out),
                  jnp.stack([jnp.concatenate(r, axis=0) for r in d_norm]), jnp.stack(d_conv)]

    gblocks = [_rows3(_block(g, ax).astype(RS_DTYPE)) for g, ax in zip(full_grads, SHARD_AXES)]
    got = exchange_sibling(gblocks)
    chip_parts = [add_sibling(g, r, c_idx) for g, r in zip(gblocks, got)]
    parts = exchange_chips(chip_parts)
    out_g, out_d, out_m, out_v = {}, {}, {}, {}
    for n, p in zip(SHARD_NAMES, parts):
        w = W[n]
        w3 = w.reshape((-1,) + w.shape[-2:])
        G = w3.shape[0]
        p4 = p.reshape((4, G, p.shape[1] // G, p.shape[2]))
        res = sum_adamw(p4, w3, Mo[n].reshape(w3.shape), Vo[n].reshape(w3.shape))
        for dst, r in zip((out_g, out_d, out_m, out_v), res):
            dst[n] = r.reshape(w.shape)

    d_alog = jnp.stack([p[:, 0, 0] for p in d_prm])
    d_dtb = jnp.stack([p[:, 0, 1] for p in d_prm])
    small_g = [d_rel, d_alog, d_dtb, jnp.concatenate(d_nw, axis=0), d_final_g[0], loss_row[0, :1]]
    small_shapes = [W[n].shape for n in SMALL_NAMES]
    sg = allgather_small(_pack(small_g, 8))
    pks = lambda d: _pack([d[n] for n in SMALL_NAMES] + [jnp.zeros((1,), F32)], 8)[None]
    res = sum_adamw(sg[:, None], pks(W), pks(Mo), pks(Vo))
    loss = res[0].reshape(-1)[sum(math.prod(s) for s in small_shapes)]
    for dst, r in zip((out_g, out_d, out_m, out_v), res):
        dst.update(zip(SMALL_NAMES, _unpack(r, small_shapes)))

    order = ("norm_g", "ffn_w_gate", "ffn_w_up", "ffn_w_down", "rel_bias", "a_w_in", "a_w_out", "b_w_in",
             "b_conv_w", "b_a_log", "b_dt_bias", "b_norm_w", "b_w_out", "final_g")
    return (loss, grad_x, *[out_g[n] for n in order], *[out_d[n] for n in order],
            *[out_m[n] for n in order], *[out_v[n] for n in order])
```

```python
import functools
import math

import jax
import jax.numpy as jnp
from jax import lax
from jax.experimental import pallas as pl
from jax.experimental.pallas import tpu as pltpu

F32 = jnp.float32
BF16 = jnp.bfloat16

N_DEV = 8
NORM_EPS = 1e-6
MACARON_WEIGHT = 0.5
DEPTH = 4

A_PATTERNS = ((128, 1), (512, 4), (2048, 16))
A_HEADS = 8
A_HEAD_DIM = 64
A_GROUP_WIDTH = A_HEADS * A_HEAD_DIM
A_BLOCK = 128
NEG_INF = -1e30
NUM_BUCKETS = 32
MAX_DISTANCE = 2048

B_HEADS = 8
B_HEAD_DIM = 128
B_WIDTH = B_HEADS * B_HEAD_DIM
B_CONV = 4
B_CHUNK = 128
B_AB_PAD = 128

ADAM_LR = 0.001
ADAM_B1 = 0.9
ADAM_B2 = 0.999
ADAM_EPS = 1e-08
ADAM_WD = 0.01
ADAM_STEP = 10

LANES = 128
VMEM_LIMIT = 48 * 1024 * 1024
VMEM_LIMIT_BIG = 60 * 1024 * 1024
RS_DTYPE = BF16

MESH = pl.DeviceIdType.MESH


def _cparams(sem=None, vmem=VMEM_LIMIT):
    return pltpu.CompilerParams(dimension_semantics=sem, vmem_limit_bytes=vmem)


def _sigmoid(x):
    return 0.5 * (jnp.tanh(0.5 * x) + 1.0)


def _silu(x):
    return x * _sigmoid(x)


def _softplus(x):
    return jnp.maximum(x, 0.0) + jnp.log1p(jnp.exp(-jnp.abs(x)))


def _rms(x, g):
    return x * lax.rsqrt(jnp.mean(x * x, axis=-1, keepdims=True) + NORM_EPS) * g


def _dg(a, b, ca, cb, hp):
    dims = (((ca,), (cb,)), ((), ()))
    if hp:
        return lax.dot_general(a.astype(F32), b.astype(F32), dims, precision=lax.Precision.HIGH,
                               preferred_element_type=F32)
    return lax.dot_general(a.astype(BF16), b.astype(BF16), dims, preferred_element_type=F32)


def _make_dots(hp):
    @jax.custom_vjp
    def nn(a, b):
        return _dg(a, b, 1, 0, hp)

    def nn_f(a, b):
        return nn(a, b), (a, b)

    def nn_b(r, ct):
        a, b = r
        return _dg(ct, b, 1, 1, hp), _dg(a, ct, 0, 0, hp)

    nn.defvjp(nn_f, nn_b)

    @jax.custom_vjp
    def nt(a, b):
        return _dg(a, b, 1, 1, hp)

    def nt_f(a, b):
        return nt(a, b), (a, b)

    def nt_b(r, ct):
        a, b = r
        return _dg(ct, b, 1, 0, hp), _dg(ct, a, 0, 0, hp)

    nt.defvjp(nt_f, nt_b)

    @jax.custom_vjp
    def tn(a, b):
        return _dg(a, b, 0, 0, hp)

    def tn_f(a, b):
        return tn(a, b), (a, b)

    def tn_b(r, ct):
        a, b = r
        return _dg(b, ct, 1, 1, hp), _dg(a, ct, 1, 0, hp)

    tn.defvjp(tn_f, tn_b)
    return nn, nt, tn


bdot_nn, bdot_nt, bdot_tn = _make_dots(False)


def rmsnorm_fwd(x, g, tm):
    M, D = x.shape

    def body(x_ref, g_ref, h_ref):
        h_ref[...] = _rms(x_ref[...], g_ref[...]).astype(h_ref.dtype)

    return pl.pallas_call(
        body, name="rmsnorm_fwd", grid=(M // tm,),
        in_specs=[pl.BlockSpec((tm, D), lambda i: (i, 0)), pl.BlockSpec((1, D), lambda i: (0, 0))],
        out_specs=pl.BlockSpec((tm, D), lambda i: (i, 0)),
        out_shape=jax.ShapeDtypeStruct((M, D), BF16),
        compiler_params=_cparams(("arbitrary",)),
    )(x, g)


def rmsnorm_bwd(x, g, dh, dres, tm):
    M, D = x.shape

    def body(x_ref, g_ref, dh_ref, dres_ref, dx_ref, dg_ref):
        _, vjp = jax.vjp(_rms, x_ref[...], g_ref[...])
        dx, dg = vjp(dh_ref[...])
        dx_ref[...] = dres_ref[...] + dx

        @pl.when(pl.program_id(0) == 0)
        def _():
            dg_ref[...] = jnp.zeros_like(dg_ref)

        dg_ref[...] += dg

    row = pl.BlockSpec((tm, D), lambda i: (i, 0))
    vec = pl.BlockSpec((1, D), lambda i: (0, 0))
    return pl.pallas_call(
        body, name="rmsnorm_bwd", grid=(M // tm,),
        in_specs=[row, vec, row, row], out_specs=[row, vec],
        out_shape=[jax.ShapeDtypeStruct((M, D), F32), jax.ShapeDtypeStruct((1, D), F32)],
        compiler_params=_cparams(("arbitrary",)),
    )(x, g, dh, dres)


def loss_head(x, g, target, tm):
    M, D = x.shape

    def body(x_ref, g_ref, t_ref, dx_ref, dg_ref, loss_ref):
        t = t_ref[...]

        def f(xv, gv):
            err = _rms(xv, gv) - t
            return 0.5 * jnp.sum(jnp.mean(err * err, axis=-1, keepdims=True), axis=0, keepdims=True)

        val, vjp = jax.vjp(f, x_ref[...], g_ref[...])
        dx, dg = vjp(jnp.ones((1, 1), F32))
        dx_ref[...] = dx

        @pl.when(pl.program_id(0) == 0)
        def _():
            dg_ref[...] = jnp.zeros_like(dg_ref)
            loss_ref[...] = jnp.zeros_like(loss_ref)

        dg_ref[...] += dg
        loss_ref[...] += jnp.broadcast_to(val, loss_ref.shape)

    row = pl.BlockSpec((tm, D), lambda i: (i, 0))
    vec = pl.BlockSpec((1, D), lambda i: (0, 0))
    return pl.pallas_call(
        body, name="loss_head", grid=(M // tm,),
        in_specs=[row, vec, row],
        out_specs=[row, vec, pl.BlockSpec((1, LANES), lambda i: (0, 0))],
        out_shape=[jax.ShapeDtypeStruct((M, D), F32), jax.ShapeDtypeStruct((1, D), F32),
                   jax.ShapeDtypeStruct((1, LANES), F32)],
        compiler_params=_cparams(("arbitrary",)),
    )(x, g, target)


def _pick(n, cands):
    for c in cands:
        if n % c == 0:
            return c
    return n


def mm_nn(a, b, *, out_dtype=F32, res=None, res_scale=1.0, name="mm_nn"):
    M, K = a.shape
    N = b.shape[1]
    tm = _pick(M, (512, 256, 128))
    tn = _pick(N, (1408, 1536, 1024, 512, 256, 128))

    def body(*refs):
        if res is None:
            a_ref, b_ref, o_ref = refs
        else:
            a_ref, b_ref, r_ref, o_ref = refs
        acc = _dg(a_ref[...], b_ref[...], 1, 0, False)
        if res is not None:
            acc = r_ref[...] + res_scale * acc
        o_ref[...] = acc.astype(o_ref.dtype)

    in_specs = [pl.BlockSpec((tm, K), lambda j, i: (i, 0)), pl.BlockSpec((K, tn), lambda j, i: (0, j))]
    args = [a, b]
    if res is not None:
        in_specs.append(pl.BlockSpec((tm, tn), lambda j, i: (i, j)))
        args.append(res)
    return pl.pallas_call(
        body, name=name, grid=(N // tn, M // tm), in_specs=in_specs,
        out_specs=pl.BlockSpec((tm, tn), lambda j, i: (i, j)),
        out_shape=jax.ShapeDtypeStruct((M, N), out_dtype),
        compiler_params=_cparams(("arbitrary", "arbitrary")),
    )(*args)


def mm_nt(pairs, *, out_dtype=F32, name="mm_nt"):
    M = pairs[0][0].shape[0]
    N = pairs[0][1].shape[0]
    tm = _pick(M, (512, 256, 128))
    tn = _pick(N, (1024, 512, 256, 128))
    np_ = len(pairs)

    def body(*refs):
        o_ref = refs[-1]
        acc = None
        for p in range(np_):
            t = _dg(refs[2 * p][...], refs[2 * p + 1][...], 1, 1, False)
            acc = t if acc is None else acc + t
        o_ref[...] = acc.astype(o_ref.dtype)

    in_specs, args = [], []
    for a, b in pairs:
        K = a.shape[1]
        in_specs += [pl.BlockSpec((tm, K), lambda j, i: (i, 0)), pl.BlockSpec((tn, K), lambda j, i: (j, 0))]
        args += [a, b]
    return pl.pallas_call(
        body, name=name, grid=(N // tn, M // tm), in_specs=in_specs,
        out_specs=pl.BlockSpec((tm, tn), lambda j, i: (i, j)),
        out_shape=jax.ShapeDtypeStruct((M, N), out_dtype),
        compiler_params=_cparams(("arbitrary", "arbitrary")),
    )(*args)


def mm_tn(a, b, *, name="mm_tn"):
    M, K = a.shape
    N = b.shape[1]
    tm = _pick(M, (512, 256, 128))
    tk = _pick(K, (1408, 1536, 1024, 512, 256, 128))
    tn = _pick(N, (1408, 1536, 1024, 1056, 512, 256, 128))

    def body(a_ref, b_ref, o_ref):
        @pl.when(pl.program_id(2) == 0)
        def _():
            o_ref[...] = jnp.zeros_like(o_ref)

        o_ref[...] += _dg(a_ref[...], b_ref[...], 0, 0, False)

    return pl.pallas_call(
        body, name=name, grid=(K // tk, N // tn, M // tm),
        in_specs=[pl.BlockSpec((tm, tk), lambda k, j, m: (m, k)), pl.BlockSpec((tm, tn), lambda k, j, m: (m, j))],
        out_specs=pl.BlockSpec((tk, tn), lambda k, j, m: (k, j)),
        out_shape=jax.ShapeDtypeStruct((K, N), F32),
        compiler_params=_cparams(("arbitrary", "arbitrary", "arbitrary")),
    )(a, b)


def ffn_up(h, wg, wu):
    M, K = h.shape
    N = wg.shape[1]
    tm = _pick(M, (512, 256, 128))
    tn = _pick(N, (1408, 1536, 1024, 512, 256, 128))

    def body(h_ref, wg_ref, wu_ref, g_ref, u_ref, a_ref):
        hv = h_ref[...]
        g = _dg(hv, wg_ref[...], 1, 0, False)
        u = _dg(hv, wu_ref[...], 1, 0, False)
        g_ref[...] = g
        u_ref[...] = u
        a_ref[...] = (_silu(g) * u).astype(a_ref.dtype)

    wspec = pl.BlockSpec((K, tn), lambda j, i: (0, j))
    ospec = pl.BlockSpec((tm, tn), lambda j, i: (i, j))
    return pl.pallas_call(
        body, name="ffn_up", grid=(N // tn, M // tm),
        in_specs=[pl.BlockSpec((tm, K), lambda j, i: (i, 0)), wspec, wspec],
        out_specs=[ospec, ospec, ospec],
        out_shape=[jax.ShapeDtypeStruct((M, N), F32), jax.ShapeDtypeStruct((M, N), F32),
                   jax.ShapeDtypeStruct((M, N), BF16)],
        compiler_params=_cparams(("arbitrary", "arbitrary")),
    )(h, wg, wu)


def ffn_bwd_act(dout, wd, gate, up):
    M, D = dout.shape
    N = wd.shape[0]
    tm = _pick(M, (512, 256, 128))
    tn = _pick(N, (1408, 1536, 1024, 512, 256, 128))

    def body(d_ref, w_ref, g_ref, u_ref, dg_ref, du_ref):
        dact = _dg(d_ref[...], w_ref[...], 1, 1, False)
        _, vjp = jax.vjp(lambda g, u: _silu(g) * u, g_ref[...], u_ref[...])
        dg, du = vjp(dact)
        dg_ref[...] = dg.astype(dg_ref.dtype)
        du_ref[...] = du.astype(du_ref.dtype)

    ospec = pl.BlockSpec((tm, tn), lambda j, i: (i, j))
    return pl.pallas_call(
        body, name="ffn_bwd_act", grid=(N // tn, M // tm),
        in_specs=[pl.BlockSpec((tm, D), lambda j, i: (i, 0)), pl.BlockSpec((tn, D), lambda j, i: (j, 0)), ospec, ospec],
        out_specs=[ospec, ospec],
        out_shape=[jax.ShapeDtypeStruct((M, N), BF16), jax.ShapeDtypeStruct((M, N), BF16)],
        compiler_params=_cparams(("arbitrary", "arbitrary")),
    )(dout, wd, gate, up)


def scale_cast(x, scale, tm):
    M, D = x.shape

    def body(x_ref, o_ref):
        o_ref[...] = (scale * x_ref[...]).astype(o_ref.dtype)

    row = pl.BlockSpec((tm, D), lambda i: (i, 0))
    return pl.pallas_call(
        body, name="scale_cast", grid=(M // tm,), in_specs=[row], out_specs=row,
        out_shape=jax.ShapeDtypeStruct((M, D), BF16), compiler_params=_cparams(("arbitrary",)),
    )(x)


def _t5_bucket(distance):
    max_exact = NUM_BUCKETS // 2
    n = distance.astype(jnp.float32)
    large = max_exact + (jnp.log(jnp.maximum(n, 1.0) / max_exact)
                         / math.log(MAX_DISTANCE / max_exact) * (NUM_BUCKETS - max_exact))
    large = jnp.minimum(large.astype(jnp.int32), NUM_BUCKETS - 1)
    return jnp.where(distance < max_exact, distance, large)


def _attn_geometry(blk, nb):
    nk = 2 * blk if nb > 1 else blk
    off = blk if nb > 1 else 0
    return nk, off


def _bucket_index(blk, nb, dilation):
    nk, off = _attn_geometry(blk, nb)
    rel = jnp.arange(blk)[:, None] + off - jnp.arange(nk)[None, :]
    return _t5_bucket(jnp.maximum(rel, 0) * dilation)


def _attn_block_pair(q2, kk2, vv2, bias2, valid2, scale):
    nk = kk2.shape[0]
    blk = q2.shape[0]
    s = pdot_nt(q2, kk2) * scale + bias2
    s = jnp.where(valid2, s, NEG_INF)
    ma = jnp.max(s[:, :nk], axis=-1, keepdims=True)
    mb = jnp.max(s[:, nk:], axis=-1, keepdims=True)
    m = lax.stop_gradient(jnp.where(_lane_half(s.shape), ma, mb))
    p = jnp.exp(s - m)
    da = jnp.sum(p[:, :nk], axis=-1, keepdims=True)
    db = jnp.sum(p[:, nk:], axis=-1, keepdims=True)
    first = _lane_half((blk, q2.shape[1]))
    o = pdot_nn(p, vv2) / jnp.where(first, da, db)
    lse = jnp.where(first, lax.stop_gradient(ma) + jnp.log(da), lax.stop_gradient(mb) + jnp.log(db))
    return o, lse


def _attn_valid(blk, nk, off, steps, first):
    q_loc = lax.broadcasted_iota(jnp.int32, (blk, 2 * nk), 0)
    k_loc = lax.broadcasted_iota(jnp.int32, (blk, 2 * nk), 1) % nk
    rel = q_loc + off - k_loc
    valid = (rel >= 0) & (rel <= steps)
    if off:
        valid = valid & ((k_loc >= off) | jnp.logical_not(first))
    return valid


def _sub(ref, r, pp, blk, d):
    if d == 1:
        return ref.at[0, :, pl.ds(pp * LANES, LANES)]
    return ref.at[0, pl.ds(r, blk, stride=d), :]


def _pairs_per_step(d):
    return 1 if d > 1 else A_GROUP_WIDTH // LANES


def _for_units(d, fn):
    for r in range(d):
        for pp in range(_pairs_per_step(d)):
            fn(r, pp)


def attn_fwd(proj, bias2, gi, d, steps):
    B, S, _ = proj.shape
    blk = A_BLOCK
    T = blk * d
    nb = S // T
    nk, off = _attn_geometry(blk, nb)
    scale = A_HEAD_DIM ** -0.5
    PP = _pairs_per_step(d)
    HP = A_GROUP_WIDTH // LANES // PP
    Wb = PP * LANES

    def body(*refs):
        if nb > 1:
            q_ref, kp_ref, kc_ref, vp_ref, vc_ref, b_ref, o_ref, l_ref = refs
        else:
            q_ref, kc_ref, vc_ref, b_ref, o_ref, l_ref = refs
        valid = _attn_valid(blk, nk, off, steps, pl.program_id(2) == 0)

        def one(r, pp):
            sub = lambda ref: _sub(ref, r, pp, blk, d)
            if nb > 1:
                kk = jnp.concatenate([sub(kp_ref)[...], sub(kc_ref)[...]], axis=0)
                vv = jnp.concatenate([sub(vp_ref)[...], sub(vc_ref)[...]], axis=0)
            else:
                kk, vv = sub(kc_ref)[...], sub(vc_ref)[...]
            o, lse = _attn_block_pair(sub(q_ref)[...], kk, vv, b_ref[pp], valid, scale)
            sub(o_ref)[...] = o
            sub(l_ref)[...] = lse

        _for_units(d, one)

    def cur(t):
        return pl.BlockSpec((1, T, Wb), lambda b, h, i: (b, i, (gi * 3 + t) * HP + h))

    def prev(t):
        return pl.BlockSpec((1, T, Wb), lambda b, h, i: (b, jnp.maximum(i - 1, 0), (gi * 3 + t) * HP + h))

    bspec = pl.BlockSpec((PP, blk, 2 * nk), lambda b, h, i: (h, 0, 0))
    out = pl.BlockSpec((1, T, Wb), lambda b, h, i: (b, i, h))
    if nb > 1:
        in_specs, args = [cur(0), prev(1), cur(1), prev(2), cur(2), bspec], [proj] * 5 + [bias2]
    else:
        in_specs, args = [cur(0), cur(1), cur(2), bspec], [proj] * 3 + [bias2]
    osh = jax.ShapeDtypeStruct((B, S, A_GROUP_WIDTH), F32)
    return pl.pallas_call(
        body, name="attn_fwd", grid=(B, HP, nb), in_specs=in_specs, out_specs=[out, out], out_shape=[osh, osh],
        compiler_params=_cparams(("arbitrary", "arbitrary", "arbitrary")),
    )(*args)


def attn_bwd(proj, bias2, do, dlse, gi, d, steps):
    B, S, _ = proj.shape
    blk = A_BLOCK
    T = blk * d
    nb = S // T
    nk, off = _attn_geometry(blk, nb)
    scale = A_HEAD_DIM ** -0.5
    PP = _pairs_per_step(d)
    HP = A_GROUP_WIDTH // LANES // PP
    Wb = PP * LANES
    nsteps = nb + 1 if nb > 1 else 1

    def body(*refs):
        if nb > 1:
            (q_ref, kp_ref, kc_ref, vp_ref, vc_ref, b_ref, do_ref, dl_ref,
             dq_ref, dk_ref, dv_ref, db_ref, ck, cv) = refs
        else:
            q_ref, kc_ref, vc_ref, b_ref, do_ref, dl_ref, dq_ref, dk_ref, dv_ref, db_ref = refs
        b = pl.program_id(0)
        h = pl.program_id(1)
        i = pl.program_id(2)

        @pl.when((b == 0) & (h == 0) & (i == 0))
        def _():
            db_ref[...] = jnp.zeros_like(db_ref)

        if nb > 1:
            @pl.when(i == 0)
            def _():
                ck[...] = jnp.zeros_like(ck)
                cv[...] = jnp.zeros_like(cv)

        def work():
            valid = _attn_valid(blk, nk, off, steps, i == 0)

            def one(r, pp):
                sub = lambda ref: _sub(ref, r, pp, blk, d)
                if nb > 1:
                    kk = jnp.concatenate([sub(kp_ref)[...], sub(kc_ref)[...]], axis=0)
                    vv = jnp.concatenate([sub(vp_ref)[...], sub(vc_ref)[...]], axis=0)
                else:
                    kk, vv = sub(kc_ref)[...], sub(vc_ref)[...]
                _, vjp = jax.vjp(lambda a, c, e, f: _attn_block_pair(a, c, e, f, valid, scale),
                                 sub(q_ref)[...], kk, vv, b_ref[h * PP + pp])
                dq, dkk, dvv, dbias = vjp((sub(do_ref)[...], sub(dl_ref)[...]))
                sub(dq_ref)[...] = dq
                db_ref[h * PP + pp] += dbias
                if nb > 1:
                    idx = (slice(None), pl.ds(pp * LANES, LANES)) if d == 1 else (pl.ds(r, blk, stride=d), slice(None))
                    sub(dk_ref)[...] = ck[idx] + dkk[:blk]
                    sub(dv_ref)[...] = cv[idx] + dvv[:blk]
                    ck[idx] = dkk[blk:]
                    cv[idx] = dvv[blk:]
                else:
                    sub(dk_ref)[...] = dkk
                    sub(dv_ref)[...] = dvv

            _for_units(d, one)

        if nb > 1:
            pl.when(i < nb)(work)

            @pl.when(i == nb)
            def _():
                dk_ref[0] = ck[...]
                dv_ref[0] = cv[...]
        else:
            work()

    last = nb - 1

    def cur(t):
        return pl.BlockSpec((1, T, Wb), lambda b, h, i: (b, jnp.minimum(i, last), (gi * 3 + t) * HP + h))

    def prev(t):
        return pl.BlockSpec((1, T, Wb), lambda b, h, i: (b, jnp.clip(i - 1, 0, last), (gi * 3 + t) * HP + h))

    ocur = pl.BlockSpec((1, T, Wb), lambda b, h, i: (b, jnp.minimum(i, last), h))
    oprev = pl.BlockSpec((1, T, Wb), lambda b, h, i: (b, jnp.clip(i - 1, 0, last), h))
    bspec = pl.BlockSpec((HP * PP, blk, 2 * nk), lambda b, h, i: (0, 0, 0))
    if nb > 1:
        in_specs, args = [cur(0), prev(1), cur(1), prev(2), cur(2), bspec, ocur, ocur], [proj] * 5 + [bias2, do, dlse]
        out_specs = [ocur, oprev, oprev, bspec]
        scratch = [pltpu.VMEM((T, Wb), F32), pltpu.VMEM((T, Wb), F32)]
    else:
        in_specs, args = [cur(0), cur(1), cur(2), bspec, ocur, ocur], [proj] * 3 + [bias2, do, dlse]
        out_specs = [ocur, ocur, ocur, bspec]
        scratch = []
    osh = jax.ShapeDtypeStruct((B, S, A_GROUP_WIDTH), F32)
    return pl.pallas_call(
        body, name="attn_bwd", grid=(B, HP, nsteps), in_specs=in_specs, out_specs=out_specs,
        out_shape=[osh, osh, osh, jax.ShapeDtypeStruct((HP * PP, blk, 2 * nk), F32)],
        scratch_shapes=scratch,
        compiler_params=_cparams(("arbitrary", "arbitrary", "arbitrary")),
    )(*args)


def _merge(o0, o1, o2, l0, l1, l2):
    m = lax.stop_gradient(jnp.maximum(jnp.maximum(l0, l1), l2))
    e0, e1, e2 = jnp.exp(l0 - m), jnp.exp(l1 - m), jnp.exp(l2 - m)
    inv = 1.0 / (e0 + e1 + e2)
    return (e0 * inv) * o0 + (e1 * inv) * o1 + (e2 * inv) * o2


def merge_fwd(os_, ls, tm):
    M, W = os_[0].shape

    def body(o0, o1, o2, l0, l1, l2, out):
        out[...] = _merge(o0[...], o1[...], o2[...], l0[...], l1[...], l2[...]).astype(out.dtype)

    row = pl.BlockSpec((tm, W), lambda i: (i, 0))
    return pl.pallas_call(
        body, name="merge_fwd", grid=(M // tm,), in_specs=[row] * 6, out_specs=row,
        out_shape=jax.ShapeDtypeStruct((M, W), BF16), compiler_params=_cparams(("arbitrary",)),
    )(*os_, *ls)


def merge_bwd(os_, ls, do, tm):
    M, W = os_[0].shape

    def body(o0, o1, o2, l0, l1, l2, d_ref, do0, do1, do2, dl0, dl1, dl2):
        _, vjp = jax.vjp(_merge, o0[...], o1[...], o2[...], l0[...], l1[...], l2[...])
        for r, val in zip((do0, do1, do2, dl0, dl1, dl2), vjp(d_ref[...])):
            r[...] = val

    row = pl.BlockSpec((tm, W), lambda i: (i, 0))
    sh = jax.ShapeDtypeStruct((M, W), F32)
    return pl.pallas_call(
        body, name="merge_bwd", grid=(M // tm,), in_specs=[row] * 7, out_specs=[row] * 6, out_shape=[sh] * 6,
        compiler_params=_cparams(("arbitrary",)),
    )(*os_, *ls, do)


def _split3(x):
    hi = x.astype(BF16)
    r = x - hi.astype(F32)
    mid = r.astype(BF16)
    lo = (r - mid.astype(F32)).astype(BF16)
    return hi, mid, lo


def bias_table(onehot, rb):
    NB, Q = onehot.shape
    H = rb.shape[1]

    def body(oh_ref, rb_ref, o_ref):
        oh = oh_ref[...]
        acc = jnp.zeros((H, Q), F32)
        for part in _split3(rb_ref[...]):
            acc = acc + _dg(part, oh, 0, 0, False)
        o_ref[...] = acc

    full = lambda s: pl.BlockSpec(s, lambda: (0,) * len(s))
    return pl.pallas_call(
        body, name="bias_table", in_specs=[full((NB, Q)), full((NB, H))], out_specs=full((H, Q)),
        out_shape=jax.ShapeDtypeStruct((H, Q), F32), compiler_params=_cparams(),
    )(onehot, rb)


def bias_bucket_grad(onehot, dbias):
    NB, Q = onehot.shape
    H = dbias.shape[0]

    def body(oh_ref, d_ref, o_ref):
        oh = oh_ref[...]
        acc = jnp.zeros((NB, H), F32)
        for part in _split3(d_ref[...]):
            acc = acc + _dg(oh, part, 1, 1, False)
        o_ref[...] = acc

    full = lambda s: pl.BlockSpec(s, lambda: (0,) * len(s))
    return pl.pallas_call(
        body, name="bias_bucket_grad", in_specs=[full((NB, Q)), full((H, Q))], out_specs=full((NB, H)),
        out_shape=jax.ShapeDtypeStruct((NB, H), F32), compiler_params=_cparams(),
    )(onehot, dbias)


def _shift_rows(x, k, S):
    t = lax.broadcasted_iota(jnp.int32, x.shape, 0)
    if k >= 0:
        return jnp.where(t >= k, pltpu.roll(x, k, 0), 0.0)
    return jnp.where(t < S + k, pltpu.roll(x, S + k, 0), 0.0)


def conv_fwd(x, w, ncol, cb):
    B, S, _ = x.shape

    def body(x_ref, w_ref, o_ref):
        xv = x_ref[0]
        wv = w_ref[...]
        acc = xv * wv[B_CONV - 1:B_CONV]
        for k in range(1, B_CONV):
            acc = acc + _shift_rows(xv, k, S) * wv[B_CONV - 1 - k:B_CONV - k]
        o_ref[0] = acc

    blk = pl.BlockSpec((1, S, cb), lambda b, j: (b, 0, j))
    return pl.pallas_call(
        body, name="conv_fwd", grid=(B, ncol // cb),
        in_specs=[blk, pl.BlockSpec((B_CONV, cb), lambda b, j: (0, j))], out_specs=blk,
        out_shape=jax.ShapeDtypeStruct((B, S, ncol), F32),
        compiler_params=_cparams(("arbitrary", "arbitrary")),
    )(x, w)


def conv_bwd(x, w, dc, col0, cb):
    B, S, C = dc.shape
    j0 = col0 // cb

    def body(x_ref, w_ref, dc_ref, dx_ref, dw_ref):
        xv = x_ref[0]
        wv = w_ref[...]
        d = dc_ref[0]
        acc = d * wv[B_CONV - 1:B_CONV]
        rows = [jnp.sum(d * xv, axis=0, keepdims=True)]
        for k in range(1, B_CONV):
            acc = acc + _shift_rows(d, -k, S) * wv[B_CONV - 1 - k:B_CONV - k]
            rows.append(jnp.sum(d * _shift_rows(xv, k, S), axis=0, keepdims=True))
        dx_ref[0] = acc.astype(dx_ref.dtype)

        @pl.when(pl.program_id(1) == 0)
        def _():
            dw_ref[...] = jnp.zeros_like(dw_ref)

        dw_ref[...] += jnp.concatenate(rows[::-1], axis=0)

    return pl.pallas_call(
        body, name="conv_bwd", grid=(C // cb, B),
        in_specs=[pl.BlockSpec((1, S, cb), lambda j, b: (b, 0, j + j0)),
                  pl.BlockSpec((B_CONV, cb), lambda j, b: (0, j)),
                  pl.BlockSpec((1, S, cb), lambda j, b: (b, 0, j))],
        out_specs=[pl.BlockSpec((1, S, cb), lambda j, b: (b, 0, j)), pl.BlockSpec((B_CONV, cb), lambda j, b: (0, j))],
        out_shape=[jax.ShapeDtypeStruct((B, S, C), BF16), jax.ShapeDtypeStruct((B_CONV, C), F32)],
        compiler_params=_cparams(("arbitrary", "arbitrary")),
    )(x, w, dc)


def _lane_half(shape):
    return lax.broadcasted_iota(jnp.int32, shape, len(shape) - 1) < shape[-1] // 2


def _bd(xw):
    first = _lane_half(xw.shape)
    return jnp.concatenate([jnp.where(first, xw, 0.0), jnp.where(first, 0.0, xw)], axis=0)


def _unbd(y):
    r = y.shape[0] // 2
    return jnp.where(_lane_half((r, y.shape[1])), y[:r], y[r:])


def _rs(xw):
    return jnp.concatenate([xw[:, :LANES], xw[:, LANES:]], axis=0)


def _unrs(y):
    r = y.shape[0] // 2
    return jnp.concatenate([y[:r], y[r:]], axis=1)


def _make_pair_dots(hp):
    @jax.custom_vjp
    def nn(xw, pw):
        return _dg(xw, _bd(pw), 1, 0, hp)

    def nn_f(xw, pw):
        return nn(xw, pw), (xw, pw)

    def nn_b(r, ct):
        xw, pw = r
        return _dg(ct, _bd(pw), 1, 1, hp), _unbd(_dg(xw, ct, 0, 0, hp))

    nn.defvjp(nn_f, nn_b)

    @jax.custom_vjp
    def nt(xw, yw):
        return _dg(xw, _bd(yw), 1, 1, hp)

    def nt_f(xw, yw):
        return nt(xw, yw), (xw, yw)

    def nt_b(r, ct):
        xw, yw = r
        return _dg(ct, _bd(yw), 1, 0, hp), _unbd(_dg(ct, xw, 0, 0, hp))

    nt.defvjp(nt_f, nt_b)

    @jax.custom_vjp
    def tn(xw, yw):
        return _dg(_rs(xw), _bd(yw), 0, 0, hp)

    def tn_f(xw, yw):
        return tn(xw, yw), (xw, yw)

    def tn_b(r, ct):
        xw, yw = r
        return _unrs(_dg(_bd(yw), ct, 1, 1, hp)), _unbd(_dg(_rs(xw), ct, 1, 0, hp))

    tn.defvjp(tn_f, tn_b)
    return nn, nt, tn


pdot_nn, pdot_nt, pdot_tn = _make_pair_dots(False)
phdot_nn, phdot_nt, phdot_tn = _make_pair_dots(True)


@jax.custom_vjp
def _tril_cumsum(gw):
    C = gw.shape[0]
    r = lax.broadcasted_iota(jnp.int32, (C, C), 0)
    c = lax.broadcasted_iota(jnp.int32, (C, C), 1)
    return _dg(jnp.where(r >= c, 1.0, 0.0), gw, 1, 0, True)


def _tril_cumsum_f(gw):
    return _tril_cumsum(gw), gw.shape[0]


def _tril_cumsum_b(C, ct):
    r = lax.broadcasted_iota(jnp.int32, (C, C), 0)
    c = lax.broadcasted_iota(jnp.int32, (C, C), 1)
    return (_dg(jnp.where(r <= c, 1.0, 0.0), ct, 1, 0, True),)


_tril_cumsum.defvjp(_tril_cumsum_f, _tril_cumsum_b)


def _pair(fa, fb, shape):
    return jnp.where(_lane_half(shape), fa, fb)


def _half_sums(xw):
    sa = jnp.sum(xw[:, :LANES], axis=-1, keepdims=True)
    sb = jnp.sum(xw[:, LANES:], axis=-1, keepdims=True)
    return _pair(sa, sb, xw.shape)


def _neumann_inverse_pair(low):
    C = low.shape[0]
    r = lax.broadcasted_iota(jnp.int32, low.shape, 0)
    c = lax.broadcasted_iota(jnp.int32, low.shape, 1) % LANES
    x = jnp.where(r == c, 1.0, 0.0) - low
    p = phdot_nn(low, low)
    n = 2
    while n < C:
        x = x + phdot_nn(x, p)
        n *= 2
        if n < C:
            p = phdot_nn(p, p)
    return x


@jax.custom_vjp
def _saved_inverse(low, tinv):
    return tinv


def _saved_inverse_f(low, tinv):
    return tinv, tinv


def _saved_inverse_b(tinv, ct):
    return -phdot_nt(phdot_tn(tinv, ct), tinv), jnp.zeros_like(tinv)


_saved_inverse.defvjp(_saved_inverse_f, _saved_inverse_b)


def _gdn_chunk_pair(cq, ck, cv, z, ab, prm_a, prm_b, nw, state, h0, tinv_saved=None):
    C, W2 = cq.shape
    lane = lax.broadcasted_iota(jnp.int32, ab.shape, 1)
    col = lambda j: jnp.sum(jnp.where(lane == j, ab, 0.0), axis=1, keepdims=True)
    a = _pair(col(h0), col(h0 + 1), (C, W2))
    b = _pair(col(h0 + B_HEADS), col(h0 + 1 + B_HEADS), (C, W2))
    a_log = _pair(prm_a[:, 0:1], prm_b[:, 0:1], (1, W2))
    dtb = _pair(prm_a[:, 1:2], prm_b[:, 1:2], (1, W2))
    nw2 = jnp.concatenate([nw, nw], axis=1)
    q = _silu(cq)
    q = q * lax.rsqrt(_half_sums(q * q) + NORM_EPS) * (B_HEAD_DIM ** -0.5)
    k = _silu(ck)
    k = k * lax.rsqrt(_half_sums(k * k) + NORM_EPS)
    v = _silu(cv)
    beta = _sigmoid(b)
    g = -jnp.exp(a_log) * _softplus(a + dtb)

    r = lax.broadcasted_iota(jnp.int32, (C, W2), 0)
    c = lax.broadcasted_iota(jnp.int32, (C, W2), 1) % LANES
    tril = r >= c
    gc = _tril_cumsum(g)
    gcj = jnp.concatenate([gc[:, :LANES].T, gc[:, LANES:].T], axis=1)
    decay = jnp.where(tril, jnp.exp(jnp.where(tril, gc - gcj, 0.0)), 0.0)
    gl = gc[C - 1:C, :]
    egc = jnp.exp(gc)

    kb = k * beta
    kk = pdot_nt(jnp.concatenate([kb, q], axis=0), k)
    low = jnp.where(r > c, kk[:C] * decay, 0.0)
    attn = kk[C:] * decay
    tinv = _neumann_inverse_pair(low) if tinv_saved is None else _saved_inverse(low, tinv_saved)
    u = pdot_nn(tinv, v * beta)
    w = pdot_nn(tinv, kb * egc)
    ws = pdot_nn(jnp.concatenate([w, q * egc], axis=0), state)
    v_new = u - ws[:C]
    o = ws[C:] + pdot_nn(attn, v_new)
    new_state = state * jnp.exp(gl) + pdot_tn(k * jnp.exp(gl - gc), v_new)
    y = o * lax.rsqrt(_half_sums(o * o) * (1.0 / B_HEAD_DIM) + NORM_EPS) * nw2 * _silu(z)
    if tinv_saved is None:
        return y, new_state, tinv
    return y, new_state


def gdn_fwd(c, proj, prm, nw):
    B, S, _ = c.shape
    H, dh, C = B_HEADS, B_HEAD_DIM, B_CHUNK
    N, HP, W2 = S // C, H // 2, 2 * dh

    def body(cq_ref, ck_ref, cv_ref, z_ref, ab_ref, prm_ref, nw_ref, y_ref, st_ref, ti_ref, state):
        h0 = pl.program_id(1) * 2
        state[...] = jnp.zeros_like(state)
        nwv = nw_ref[...]

        def step(n, carry):
            rows = pl.ds(pl.multiple_of(n * C, C), C)
            s_in = state[...]
            st_ref[0, n] = s_in
            y, s_out, tinv = _gdn_chunk_pair(cq_ref[0, rows, :], ck_ref[0, rows, :], cv_ref[0, rows, :],
                                             z_ref[0, rows, :], ab_ref[0, rows, :], prm_ref[h0], prm_ref[h0 + 1],
                                             nwv, s_in, h0)
            ti_ref[0, n] = tinv
            y_ref[0, rows, :] = y.astype(y_ref.dtype)
            state[...] = s_out
            return carry

        lax.fori_loop(0, N, step, 0)

    def col(off):
        return pl.BlockSpec((1, S, W2), lambda b, h: (b, 0, h + off // 2))

    return pl.pallas_call(
        body, name="gdn_fwd", grid=(B, HP),
        in_specs=[col(0), col(H), col(2 * H), col(3 * H), pl.BlockSpec((1, S, B_AB_PAD), lambda b, h: (b, 0, 4 * H)),
                  pl.BlockSpec((H, 1, LANES), lambda b, h: (0, 0, 0)), pl.BlockSpec((1, dh), lambda b, h: (0, 0))],
        out_specs=[col(0), pl.BlockSpec((1, N, dh, W2), lambda b, h: (b * HP + h, 0, 0, 0)),
                   pl.BlockSpec((1, N, C, W2), lambda b, h: (b * HP + h, 0, 0, 0))],
        out_shape=[jax.ShapeDtypeStruct((B, S, H * dh), BF16), jax.ShapeDtypeStruct((B * HP, N, dh, W2), F32),
                   jax.ShapeDtypeStruct((B * HP, N, C, W2), F32)],
        scratch_shapes=[pltpu.VMEM((dh, W2), F32)],
        compiler_params=_cparams(("arbitrary", "arbitrary")),
    )(c, c, c, proj, proj, prm, nw)


def gdn_bwd(c, proj, prm, nw, states, tinvs, dy):
    B, S, _ = c.shape
    H, dh, C = B_HEADS, B_HEAD_DIM, B_CHUNK
    N, HP, W2 = S // C, H // 2, 2 * dh

    def body(cq_ref, ck_ref, cv_ref, z_ref, ab_ref, prm_ref, nw_ref, st_ref, ti_ref, dy_ref,
             dq_ref, dk_ref, dv_ref, dz_ref, dab_ref, dprm_ref, dnw_ref, dstate):
        b = pl.program_id(0)
        hp = pl.program_id(1)
        h0 = hp * 2
        dstate[...] = jnp.zeros_like(dstate)

        @pl.when((b == 0) & (hp == 0))
        def _():
            dprm_ref[...] = jnp.zeros_like(dprm_ref)
            dnw_ref[...] = jnp.zeros_like(dnw_ref)

        @pl.when(hp == 0)
        def _():
            dab_ref[...] = jnp.zeros_like(dab_ref)

        nwv = nw_ref[...]

        def step(t, carry):
            n = N - 1 - t
            rows = pl.ds(pl.multiple_of(n * C, C), C)
            _, vjp = jax.vjp(functools.partial(_gdn_chunk_pair, h0=h0, tinv_saved=ti_ref[0, n]), cq_ref[0, rows, :],
                             ck_ref[0, rows, :], cv_ref[0, rows, :], z_ref[0, rows, :], ab_ref[0, rows, :],
                             prm_ref[h0], prm_ref[h0 + 1], nwv, st_ref[0, n])
            dcq, dck, dcv, dz, dab, dpa, dpb, dnw, ds = vjp((dy_ref[0, rows, :].astype(F32), dstate[...]))
            dq_ref[0, rows, :] = dcq
            dk_ref[0, rows, :] = dck
            dv_ref[0, rows, :] = dcv
            dz_ref[0, rows, :] = dz.astype(dz_ref.dtype)
            dab_ref[0, rows, :] += dab
            dprm_ref[h0] += dpa
            dprm_ref[h0 + 1] += dpb
            dnw_ref[...] += dnw
            dstate[...] = ds
            return carry

        lax.fori_loop(0, N, step, 0)

    def col(off):
        return pl.BlockSpec((1, S, W2), lambda b, h: (b, 0, h + off // 2))

    abspec = pl.BlockSpec((1, S, B_AB_PAD), lambda b, h: (b, 0, 4 * H))
    ab0 = pl.BlockSpec((1, S, B_AB_PAD), lambda b, h: (b, 0, 0))
    prm_full = pl.BlockSpec((H, 1, LANES), lambda b, h: (0, 0, 0))
    nwspec = pl.BlockSpec((1, dh), lambda b, h: (0, 0))
    wsh = jax.ShapeDtypeStruct((B, S, H * dh), F32)
    return pl.pallas_call(
        body, name="gdn_bwd", grid=(B, HP),
        in_specs=[col(0), col(H), col(2 * H), col(3 * H), abspec, prm_full, nwspec,
                  pl.BlockSpec((1, N, dh, W2), lambda b, h: (b * HP + h, 0, 0, 0)),
                  pl.BlockSpec((1, N, C, W2), lambda b, h: (b * HP + h, 0, 0, 0)), col(0)],
        out_specs=[col(0), col(0), col(0), col(0), ab0, prm_full, nwspec],
        out_shape=[wsh, wsh, wsh, jax.ShapeDtypeStruct((B, S, H * dh), BF16),
                   jax.ShapeDtypeStruct((B, S, B_AB_PAD), F32), jax.ShapeDtypeStruct((H, 1, LANES), F32),
                   jax.ShapeDtypeStruct((1, dh), F32)],
        scratch_shapes=[pltpu.VMEM((dh, W2), F32)],
        compiler_params=_cparams(("arbitrary", "arbitrary"), VMEM_LIMIT_BIG),
    )(c, c, c, proj, proj, prm, nw, states, tinvs, dy)


def _my_place():
    return lax.axis_index("x"), lax.axis_index("y"), lax.axis_index("c")


HBM_SPEC = pl.BlockSpec(memory_space=pltpu.HBM)


def allgather_shards(shards):
    T = len(shards)

    def body(*refs):
        x_refs, out_refs = refs[:T], refs[T:2 * T]
        send_sems, recv_sems, local_sems = refs[2 * T:]
        x, y, c = _my_place()
        me, sibling = (x, y, c), (x, y, 1 - c)
        chips = [(1 - x, y), (x, 1 - y), (1 - x, 1 - y)]

        def copy(t, k, block, to, src=None):
            px, py, pc = block
            slot = out_refs[t].at[4 * px + 2 * py + pc]
            return pltpu.make_async_remote_copy(
                src_ref=slot if src is None else src, dst_ref=slot, send_sem=send_sems.at[7 * t + k],
                recv_sem=recv_sems.at[7 * t + k], device_id=to, device_id_type=MESH)

        started, mine = [], []
        for t in range(T):
            mine.append(pltpu.make_async_copy(x_refs[t], out_refs[t].at[4 * x + 2 * y + c], local_sems.at[t]))
            mine[-1].start()
            first = [copy(t, 0, me, sibling, src=x_refs[t])]
            first += [copy(t, 1 + j, me, (*chip, c), src=x_refs[t]) for j, chip in enumerate(chips)]
            for cp in first:
                cp.start()
            started += first
        for j, chip in enumerate(chips):
            for t in range(T):
                copy(t, 1 + j, (*chip, c), me).wait_recv()
                fwd = copy(t, 4 + j, (*chip, c), sibling)
                fwd.start()
                started.append(fwd)
        for t in range(T):
            copy(t, 0, sibling, me).wait_recv()
            for j, chip in enumerate(chips):
                copy(t, 4 + j, (*chip, 1 - c), me).wait_recv()
        for cp in started:
            cp.wait_send()
        for cp in mine:
            cp.wait()

    return pl.pallas_call(
        body, name="allgather_shards",
        out_shape=[jax.ShapeDtypeStruct((N_DEV,) + s.shape, s.dtype) for s in shards],
        in_specs=[HBM_SPEC] * T, out_specs=[HBM_SPEC] * T,
        scratch_shapes=[pltpu.SemaphoreType.DMA((7 * T,)), pltpu.SemaphoreType.DMA((7 * T,)),
                        pltpu.SemaphoreType.DMA((T,))],
    )(*shards)


def exchange_sibling(gs):
    T = len(gs)

    def body(*refs):
        g_refs, out_refs = refs[:T], refs[T:2 * T]
        send_sems, recv_sems = refs[2 * T:]
        x, y, c = _my_place()
        copies = []
        for t in range(T):
            for k in range(4):
                copies.append(pltpu.make_async_remote_copy(
                    src_ref=g_refs[t].at[2 * k + (1 - c)], dst_ref=out_refs[t].at[k],
                    send_sem=send_sems.at[4 * t + k], recv_sem=recv_sems.at[4 * t + k],
                    device_id=(x, y, 1 - c), device_id_type=MESH))
        for cp in copies:
            cp.start()
        for cp in copies:
            cp.wait()

    return pl.pallas_call(
        body, name="exchange_sibling",
        out_shape=[jax.ShapeDtypeStruct((4,) + g.shape[1:], g.dtype) for g in gs],
        in_specs=[HBM_SPEC] * T, out_specs=[HBM_SPEC] * T,
        scratch_shapes=[pltpu.SemaphoreType.DMA((4 * T,)), pltpu.SemaphoreType.DMA((4 * T,))],
    )(*gs)


def _row_tile(rows, row_bytes, budget):
    if rows * row_bytes <= budget:
        return rows
    best = None
    for t in range(16, rows, 16):
        if rows % t == 0 and t * row_bytes <= budget:
            best = t
    return best if best is not None else rows


def add_sibling(g, got, c_idx):
    _, R, L = g.shape
    tr = _row_tile(R, L * 4, 2 << 20)

    def body(c_ref, g_ref, r_ref, o_ref):
        o_ref[...] = (g_ref[...].astype(F32) + r_ref[...].astype(F32)).astype(o_ref.dtype)

    grid_spec = pltpu.PrefetchScalarGridSpec(
        num_scalar_prefetch=1, grid=(4, R // tr),
        in_specs=[pl.BlockSpec((1, tr, L), lambda k, i, c: (2 * k + c[0], i, 0)),
                  pl.BlockSpec((1, tr, L), lambda k, i, c: (k, i, 0))],
        out_specs=pl.BlockSpec((1, tr, L), lambda k, i, c: (k, i, 0)))
    return pl.pallas_call(
        body, name="add_sibling", grid_spec=grid_spec, out_shape=jax.ShapeDtypeStruct((4, R, L), RS_DTYPE),
        compiler_params=_cparams(("arbitrary", "arbitrary")),
    )(c_idx, g, got)


def exchange_chips(ps):
    T = len(ps)

    def body(*refs):
        p_refs, out_refs = refs[:T], refs[T:2 * T]
        send_sems, recv_sems, local_sems = refs[2 * T:]
        x, y, c = _my_place()
        mychip = 2 * x + y
        chips = [(1 - x, y), (x, 1 - y), (1 - x, 1 - y)]
        mine, copies = [], []
        for t in range(T):
            mine.append(pltpu.make_async_copy(p_refs[t].at[mychip], out_refs[t].at[mychip], local_sems.at[t]))
            mine[-1].start()
            for j, (px, py) in enumerate(chips):
                copies.append(pltpu.make_async_remote_copy(
                    src_ref=p_refs[t].at[2 * px + py], dst_ref=out_refs[t].at[mychip],
                    send_sem=send_sems.at[3 * t + j], recv_sem=recv_sems.at[3 * t + j],
                    device_id=(px, py, c), device_id_type=MESH))
                copies[-1].start()
        for t in range(T):
            for j, (px, py) in enumerate(chips):
                pltpu.make_async_remote_copy(
                    src_ref=p_refs[t].at[mychip], dst_ref=out_refs[t].at[2 * px + py],
                    send_sem=send_sems.at[3 * t + j], recv_sem=recv_sems.at[3 * t + j],
                    device_id=(px, py, c), device_id_type=MESH).wait_recv()
        for cp in copies:
            cp.wait_send()
        for cp in mine:
            cp.wait()

    return pl.pallas_call(
        body, name="exchange_chips", out_shape=[jax.ShapeDtypeStruct(p.shape, p.dtype) for p in ps],
        in_specs=[HBM_SPEC] * T, out_specs=[HBM_SPEC] * T,
        scratch_shapes=[pltpu.SemaphoreType.DMA((3 * T,)), pltpu.SemaphoreType.DMA((3 * T,)),
                        pltpu.SemaphoreType.DMA((T,))],
    )(*ps)


def allgather_small(s):
    R, W = s.shape

    def body(s_ref, out_ref, send_sems, recv_sems):
        x, y, c = _my_place()
        me = 4 * x + 2 * y + c
        out_ref[me] = s_ref[...]
        copies = []
        for j in range(1, N_DEV):
            peer = (x ^ (j >> 2), y ^ ((j >> 1) & 1), c ^ (j & 1))
            copies.append(pltpu.make_async_remote_copy(
                src_ref=s_ref, dst_ref=out_ref.at[me], send_sem=send_sems.at[j - 1], recv_sem=recv_sems.at[j - 1],
                device_id=peer, device_id_type=MESH))
        for cp in copies:
            cp.start()
        for j in range(1, N_DEV):
            px, py, pc = x ^ (j >> 2), y ^ ((j >> 1) & 1), c ^ (j & 1)
            pltpu.make_async_remote_copy(
                src_ref=s_ref, dst_ref=out_ref.at[4 * px + 2 * py + pc], send_sem=send_sems.at[j - 1],
                recv_sem=recv_sems.at[j - 1], device_id=(px, py, pc), device_id_type=MESH).wait_recv()
        for cp in copies:
            cp.wait_send()

    vm = pl.BlockSpec(memory_space=pltpu.VMEM)
    return pl.pallas_call(
        body, name="allgather_small", out_shape=jax.ShapeDtypeStruct((N_DEV, R, W), s.dtype),
        in_specs=[vm], out_specs=vm,
        scratch_shapes=[pltpu.SemaphoreType.DMA((7,)), pltpu.SemaphoreType.DMA((7,))],
    )(s)


def _adamw(w, g, m, v):
    m = ADAM_B1 * m + (1.0 - ADAM_B1) * g
    v = ADAM_B2 * v + (1.0 - ADAM_B2) * jnp.square(g)
    m_hat = m / (1.0 - ADAM_B1 ** ADAM_STEP)
    v_hat = v / (1.0 - ADAM_B2 ** ADAM_STEP)
    delta = -ADAM_LR * (m_hat / (jnp.sqrt(v_hat) + ADAM_EPS) + ADAM_WD * w)
    return delta, m, v


def sum_adamw(parts, w, m, v):
    P, G, Rp, Lp = parts.shape
    _, R, L = w.shape
    tr = _row_tile(R, P * Lp * parts.dtype.itemsize + 7 * L * 4, 5 << 20)

    def body(p_ref, w_ref, m_ref, v_ref, g_ref, d_ref, nm_ref, nv_ref):
        g = p_ref[0, 0].astype(F32)
        for k in range(1, P):
            g = g + p_ref[k, 0].astype(F32)
        g = g[:, :L]
        d, nm, nv = _adamw(w_ref[0], g, m_ref[0], v_ref[0])
        g_ref[0] = g
        d_ref[0] = d
        nm_ref[0] = nm
        nv_ref[0] = nv

    row = pl.BlockSpec((1, tr, L), lambda gi, i: (gi, i, 0))
    sh = jax.ShapeDtypeStruct((G, R, L), F32)
    return pl.pallas_call(
        body, name="sum_adamw", grid=(G, R // tr),
        in_specs=[pl.BlockSpec((P, 1, tr, Lp), lambda gi, i: (0, gi, i, 0)), row, row, row],
        out_specs=[row] * 4, out_shape=[sh] * 4, compiler_params=_cparams(("arbitrary", "arbitrary")),
    )(parts, w, m, v)


def _unblock(t, axis):
    sh = t.shape[1:]
    t = jnp.moveaxis(t, 0, axis)
    return t.reshape(sh[:axis] + (N_DEV * sh[axis],) + sh[axis + 1:])


def _block(a, axis):
    sh = a.shape
    t = a.reshape(sh[:axis] + (N_DEV, sh[axis] // N_DEV) + sh[axis + 1:])
    return jnp.moveaxis(t, axis, 0)


def _rows3(t):
    return t.reshape((t.shape[0], -1, t.shape[-1]))


def _pad_to(a, axis, n):
    pad = [(0, 0)] * a.ndim
    pad[axis] = (0, n - a.shape[axis])
    return jnp.pad(a, pad)


def _lane_pad(n):
    return -(-n // LANES) * LANES


def _pack(arrs, row_mult):
    flat = jnp.concatenate([a.astype(F32).reshape(-1) for a in arrs])
    n = flat.shape[0]
    rows = -(-n // LANES)
    rows = -(-rows // row_mult) * row_mult
    return jnp.pad(flat, (0, rows * LANES - n)).reshape(rows, LANES)


def _unpack(buf, shapes):
    flat = buf.reshape(-1)
    out, off = [], 0
    for sh in shapes:
        n = math.prod(sh)
        out.append(flat[off:off + n].reshape(sh))
        off += n
    return out


def _ffn_fwd(x, g, wg, wu, wd, tm):
    h = rmsnorm_fwd(x, g, tm)
    gate, up, act = ffn_up(h, wg, wu)
    x_new = mm_nn(act, wd, res=x, res_scale=MACARON_WEIGHT, name="ffn_down")
    return x_new, (x, h, gate, up, act)


def _ffn_bwd(dx, saved, g, wg, wu, wd, tm):
    x, h, gate, up, act = saved
    dout = scale_cast(dx, MACARON_WEIGHT, tm)
    dgate, dup = ffn_bwd_act(dout, wd, gate, up)
    dwd = mm_tn(act, dout, name="ffn_dwd")
    dwg = mm_tn(h, dgate, name="ffn_dwg")
    dwu = mm_tn(h, dup, name="ffn_dwu")
    dh = mm_nt([(dgate, wg), (dup, wu)], name="ffn_dh")
    dx_in, dg = rmsnorm_bwd(x, g, dh, dx, tm)
    return dx_in, dg, dwg, dwu, dwd


def _bucket_onehot(nb, d):
    idx = _bucket_index(A_BLOCK, nb, d)
    onehot = (idx.reshape(-1)[None, :] == jnp.arange(NUM_BUCKETS)[:, None]).astype(BF16)
    return onehot, idx.shape


def _pair_tables(t, blk, nk):
    t = t.reshape(A_HEADS // 2, 2, blk, nk)
    return jnp.transpose(t, (0, 2, 1, 3)).reshape(A_HEADS // 2, blk, 2 * nk)


def _unpair_tables(t, blk, nk):
    t = t.reshape(A_HEADS // 2, blk, 2, nk)
    return jnp.transpose(t, (0, 2, 1, 3)).reshape(A_HEADS, blk * nk)


def _mixer_a_fwd(x, g, w_in, w_out, rel_bias, B, S, tm):
    h = rmsnorm_fwd(x, g, tm)
    proj = mm_nn(h, w_in, name="a_proj").reshape(B, S, -1)
    biases, outs, lses = [], [], []
    for gi, (window, d) in enumerate(A_PATTERNS):
        nb = (S // d) // A_BLOCK
        onehot, (blk, nk) = _bucket_onehot(nb, d)
        bias2 = _pair_tables(bias_table(onehot, rel_bias[:, gi * A_HEADS:(gi + 1) * A_HEADS]), blk, nk)
        o, lse = attn_fwd(proj, bias2, gi, d, window // d)
        biases.append(bias2)
        outs.append(o.reshape(B * S, -1))
        lses.append(lse.reshape(B * S, -1))
    om = merge_fwd(outs, lses, tm)
    x_new = mm_nn(om, w_out, res=x, name="a_out")
    return x_new, (x, h, proj, biases, outs, lses, om)


def _mixer_a_bwd(dx, saved, g, w_in, w_out, B, S, tm):
    x, h, proj, biases, outs, lses, om = saved
    dxb = scale_cast(dx, 1.0, tm)
    dw_out = mm_tn(om, dxb, name="a_dwout")
    dom = mm_nt([(dxb, w_out)], name="a_dom")
    g6 = merge_bwd(outs, lses, dom, tm)
    dparts, dbias_cols = [], []
    for gi, (window, d) in enumerate(A_PATTERNS):
        do = g6[gi].reshape(B, S, -1)
        dl = g6[3 + gi].reshape(B, S, -1)
        dq, dk, dv, dbias2 = attn_bwd(proj, biases[gi], do, dl, gi, d, window // d)
        dparts += [dq, dk, dv]
        onehot, (blk, nk) = _bucket_onehot((S // d) // A_BLOCK, d)
        dbias_cols.append(bias_bucket_grad(onehot, _unpair_tables(dbias2, blk, nk)))
    d_rel_bias = jnp.concatenate(dbias_cols, axis=1)
    dproj = jnp.concatenate(dparts, axis=2).astype(BF16).reshape(B * S, -1)
    dw_in = mm_tn(h, dproj, name="a_dwin")
    dh = mm_nt([(dproj, w_in)], name="a_dh")
    dx_in, dg = rmsnorm_bwd(x, g, dh, dx, tm)
    return dx_in, dg, dw_in, dw_out, d_rel_bias


def _mixer_b_fwd(x, g, w_in_p, conv_w, prm, nw, w_out, B, S, tm):
    h = rmsnorm_fwd(x, g, tm)
    proj = mm_nn(h, w_in_p, name="b_proj").reshape(B, S, -1)
    c = conv_fwd(proj, conv_w, 3 * B_WIDTH, 512)
    y, states, tinvs = gdn_fwd(c, proj, prm, nw)
    y2 = y.reshape(B * S, B_WIDTH)
    x_new = mm_nn(y2, w_out, res=x, name="b_out")
    return x_new, (x, h, proj, c, states, tinvs, y2)


def _mixer_b_bwd(dx, saved, g, w_in_p, conv_w, prm, nw, w_out, B, S, tm):
    x, h, proj, c, states, tinvs, y2 = saved
    W = B_WIDTH
    dxb = scale_cast(dx, 1.0, tm)
    dw_out = mm_tn(y2, dxb, name="b_dwout")
    dy = mm_nt([(dxb, w_out)], out_dtype=BF16, name="b_dy").reshape(B, S, W)
    dcq, dck, dcv, dz, dab, dprm, dnw = gdn_bwd(c, proj, prm, nw, states, tinvs, dy)
    dxs, dws = [], []
    for t, dc in enumerate((dcq, dck, dcv)):
        dxp, dwp = conv_bwd(proj, conv_w[:, t * W:(t + 1) * W], dc, t * W, 512)
        dxs.append(dxp)
        dws.append(dwp)
    dconv_w = jnp.concatenate(dws, axis=1)
    dproj = jnp.concatenate(dxs + [dz, dab.astype(BF16)], axis=2).reshape(B * S, -1)
    dw_in_p = mm_tn(h, dproj, name="b_dwin")
    dh = mm_nt([(dproj, w_in_p)], name="b_dh")
    dx_in, dg = rmsnorm_bwd(x, g, dh, dx, tm)
    return dx_in, dg, dw_in_p, dconv_w, dprm, dnw, dw_out


SHARD_NAMES = ("ffn_w_gate", "ffn_w_up", "ffn_w_down", "a_w_in", "a_w_out", "b_w_in", "b_w_out", "norm_g", "b_conv_w")
SHARD_AXES = (3, 3, 2, 2, 2, 2, 1, 2, 2)
SMALL_NAMES = ("rel_bias", "b_a_log", "b_dt_bias", "b_norm_w", "final_g")


def kernel(x, norm_g, ffn_w_gate, ffn_w_up, ffn_w_down, rel_bias, a_w_in, a_w_out, b_w_in, b_conv_w, b_a_log, b_dt_bias, b_norm_w, b_w_out, final_g, loss_target, m_norm_g, m_ffn_w_gate, m_ffn_w_up, m_ffn_w_down, m_rel_bias, m_a_w_in, m_a_w_out, m_b_w_in, m_b_conv_w, m_b_a_log, m_b_dt_bias, m_b_norm_w, m_b_w_out, m_final_g, v_norm_g, v_ffn_w_gate, v_ffn_w_up, v_ffn_w_down, v_rel_bias, v_a_w_in, v_a_w_out, v_b_w_in, v_b_conv_w, v_b_a_log, v_b_dt_bias, v_b_norm_w, v_b_w_out, v_final_g):
    W = dict(norm_g=norm_g, ffn_w_gate=ffn_w_gate, ffn_w_up=ffn_w_up, ffn_w_down=ffn_w_down, rel_bias=rel_bias,
             a_w_in=a_w_in, a_w_out=a_w_out, b_w_in=b_w_in, b_conv_w=b_conv_w, b_a_log=b_a_log,
             b_dt_bias=b_dt_bias, b_norm_w=b_norm_w, b_w_out=b_w_out, final_g=final_g)
    Mo = dict(norm_g=m_norm_g, ffn_w_gate=m_ffn_w_gate, ffn_w_up=m_ffn_w_up, ffn_w_down=m_ffn_w_down,
              rel_bias=m_rel_bias, a_w_in=m_a_w_in, a_w_out=m_a_w_out, b_w_in=m_b_w_in, b_conv_w=m_b_conv_w,
              b_a_log=m_b_a_log, b_dt_bias=m_b_dt_bias, b_norm_w=m_b_norm_w, b_w_out=m_b_w_out, final_g=m_final_g)
    Vo = dict(norm_g=v_norm_g, ffn_w_gate=v_ffn_w_gate, ffn_w_up=v_ffn_w_up, ffn_w_down=v_ffn_w_down,
              rel_bias=v_rel_bias, a_w_in=v_a_w_in, a_w_out=v_a_w_out, b_w_in=v_b_w_in, b_conv_w=v_b_conv_w,
              b_a_log=v_b_a_log, b_dt_bias=v_b_dt_bias, b_norm_w=v_b_norm_w, b_w_out=v_b_w_out, final_g=v_final_g)

    B, S, D = x.shape
    M = B * S
    tm = _pick(M, (512, 256, 128))
    c_idx = lax.axis_index("c").astype(jnp.int32).reshape(1)
    fp = _lane_pad(ffn_w_gate.shape[3])

    shards = [_pad_to(ffn_w_gate.astype(BF16), 3, fp), _pad_to(ffn_w_up.astype(BF16), 3, fp),
              _pad_to(ffn_w_down.astype(BF16), 2, fp), a_w_in.astype(BF16), a_w_out.astype(BF16),
              b_w_in.astype(BF16), b_w_out.astype(BF16), norm_g, b_conv_w]
    gathered = allgather_shards(shards)
    wg_all, wu_all, wd_all, a_in_all, a_out_all, b_in_all, b_out_all, norm_all, conv_all = (
        _unblock(t, ax) for t, ax in zip(gathered, SHARD_AXES))
    b_in_pad = _pad_to(b_in_all, 2, b_in_all.shape[2] + B_AB_PAD - 2 * B_HEADS)
    prm_all = _pad_to(jnp.stack([b_a_log, b_dt_bias], axis=-1)[:, :, None, :], 3, LANES)

    xs = x.reshape(M, D)
    saved = []
    for i in range(DEPTH):
        j = i // 2
        xs, s1 = _ffn_fwd(xs, norm_all[i, 0][None], wg_all[i, 0], wu_all[i, 0], wd_all[i, 0], tm)
        if i % 2 == 0:
            xs, s2 = _mixer_a_fwd(xs, norm_all[i, 1][None], a_in_all[j], a_out_all[j], rel_bias, B, S, tm)
        else:
            xs, s2 = _mixer_b_fwd(xs, norm_all[i, 1][None], b_in_pad[j], conv_all[j], prm_all[j],
                                  b_norm_w[j][None], b_out_all[j], B, S, tm)
        xs, s3 = _ffn_fwd(xs, norm_all[i, 2][None], wg_all[i, 1], wu_all[i, 1], wd_all[i, 1], tm)
        saved.append((s1, s2, s3))

    dx, d_final_g, loss_row = loss_head(xs, final_g[None], loss_target.reshape(M, D), tm)

    d_norm = [[None] * 3 for _ in range(DEPTH)]
    d_wg = [[None] * 2 for _ in range(DEPTH)]
    d_wu = [[None] * 2 for _ in range(DEPTH)]
    d_wd = [[None] * 2 for _ in range(DEPTH)]
    d_a_in, d_a_out, d_b_in, d_b_out, d_conv = [None] * 2, [None] * 2, [None] * 2, [None] * 2, [None] * 2
    d_prm, d_nw = [None] * 2, [None] * 2
    d_rel = jnp.zeros(rel_bias.shape, F32)
    for i in reversed(range(DEPTH)):
        j = i // 2
        s1, s2, s3 = saved[i]
        dx, d_norm[i][2], d_wg[i][1], d_wu[i][1], d_wd[i][1] = _ffn_bwd(
            dx, s3, norm_all[i, 2][None], wg_all[i, 1], wu_all[i, 1], wd_all[i, 1], tm)
        if i % 2 == 0:
            dx, d_norm[i][1], d_a_in[j], d_a_out[j], drb = _mixer_a_bwd(
                dx, s2, norm_all[i, 1][None], a_in_all[j], a_out_all[j], B, S, tm)
            d_rel = d_rel + drb
        else:
            dx, d_norm[i][1], dbp, d_conv[j], d_prm[j], d_nw[j], d_b_out[j] = _mixer_b_bwd(
                dx, s2, norm_all[i, 1][None], b_in_pad[j], conv_all[j], prm_all[j], b_norm_w[j][None],
                b_out_all[j], B, S, tm)
            d_b_in[j] = dbp[:, :b_in_all.shape[2]]
        dx, d_norm[i][0], d_wg[i][0], d_wu[i][0], d_wd[i][0] = _ffn_bwd(
            dx, s1, norm_all[i, 0][None], wg_all[i, 0], wu_all[i, 0], wd_all[i, 0], tm)
    grad_x = dx.reshape(B, S, D)

    stack2 = lambda rows: jnp.stack([jnp.stack(r) for r in rows])
    full_grads = [stack2(d_wg), stack2(d_wu), stack2(d_wd), jnp.stack(d_a_in), jnp.stack(d_a_out),
                  jnp.stack(d_b_in), jnp.stack(d_b_out),
                  jnp.stack([jnp.concatenate(r, axis=0) for r in d_norm]), jnp.stack(d_conv)]

    gblocks = [_rows3(_block(g, ax).astype(RS_DTYPE)) for g, ax in zip(full_grads, SHARD_AXES)]
    got = exchange_sibling(gblocks)
    chip_parts = [add_sibling(g, r, c_idx) for g, r in zip(gblocks, got)]
    parts = exchange_chips(chip_parts)
    out_g, out_d, out_m, out_v = {}, {}, {}, {}
    for n, p in zip(SHARD_NAMES, parts):
        w = W[n]
        w3 = w.reshape((-1,) + w.shape[-2:])
        G = w3.shape[0]
        p4 = p.reshape((4, G, p.shape[1] // G, p.shape[2]))
        res = sum_adamw(p4, w3, Mo[n].reshape(w3.shape), Vo[n].reshape(w3.shape))
        for dst, r in zip((out_g, out_d, out_m, out_v), res):
            dst[n] = r.reshape(w.shape)

    d_alog = jnp.stack([p[:, 0, 0] for p in d_prm])
    d_dtb = jnp.stack([p[:, 0, 1] for p in d_prm])
    small_g = [d_rel, d_alog, d_dtb, jnp.concatenate(d_nw, axis=0), d_final_g[0], loss_row[0, :1]]
    small_shapes = [W[n].shape for n in SMALL_NAMES]
    sg = allgather_small(_pack(small_g, 8))
    pks = lambda d: _pack([d[n] for n in SMALL_NAMES] + [jnp.zeros((1,), F32)], 8)[None]
    res = sum_adamw(sg[:, None], pks(W), pks(Mo), pks(Vo))
    loss = res[0].reshape(-1)[sum(math.prod(s) for s in small_shapes)]
    for dst, r in zip((out_g, out_d, out_m, out_v), res):
        dst.update(zip(SMALL_NAMES, _unpack(r, small_shapes)))

    order = ("norm_g", "ffn_w_gate", "ffn_w_up", "ffn_w_down", "rel_bias", "a_w_in", "a_w_out", "b_w_in",
             "b_conv_w", "b_a_log", "b_dt_bias", "b_norm_w", "b_w_out", "final_g")
    return (loss, grad_x, *[out_g[n] for n in order], *[out_d[n] for n in order],
            *[out_m[n] for n in order], *[out_v[n] for n in order])
```

```python
import functools
import math

import jax
import jax.numpy as jnp
from jax import lax
from jax.experimental import pallas as pl
from jax.experimental.pallas import tpu as pltpu

F32 = jnp.float32
BF16 = jnp.bfloat16

N_DEV = 8
NORM_EPS = 1e-6
MACARON_WEIGHT = 0.5
DEPTH = 4

A_PATTERNS = ((128, 1), (512, 4), (2048, 16))
A_HEADS = 8
A_HEAD_DIM = 64
A_GROUP_WIDTH = A_HEADS * A_HEAD_DIM
A_BLOCK = 128
NEG_INF = -1e30
NUM_BUCKETS = 32
MAX_DISTANCE = 2048

B_HEADS = 8
B_HEAD_DIM = 128
B_WIDTH = B_HEADS * B_HEAD_DIM
B_CONV = 4
B_CHUNK = 128
B_AB_PAD = 128

ADAM_LR = 0.001
ADAM_B1 = 0.9
ADAM_B2 = 0.999
ADAM_EPS = 1e-08
ADAM_WD = 0.01
ADAM_STEP = 10

LANES = 128
VMEM_LIMIT = 48 * 1024 * 1024
VMEM_LIMIT_BIG = 60 * 1024 * 1024
RS_DTYPE = BF16

MESH = pl.DeviceIdType.MESH


def _cparams(sem=None, vmem=VMEM_LIMIT):
    return pltpu.CompilerParams(dimension_semantics=sem, vmem_limit_bytes=vmem)


def _sigmoid(x):
    return 0.5 * (jnp.tanh(0.5 * x) + 1.0)


def _silu(x):
    return x * _sigmoid(x)


def _softplus(x):
    return jnp.maximum(x, 0.0) + jnp.log1p(jnp.exp(-jnp.abs(x)))


def _rms(x, g):
    return x * lax.rsqrt(jnp.mean(x * x, axis=-1, keepdims=True) + NORM_EPS) * g


def _dg(a, b, ca, cb, hp):
    dims = (((ca,), (cb,)), ((), ()))
    if hp:
        return lax.dot_general(a.astype(F32), b.astype(F32), dims, precision=lax.Precision.HIGH,
                               preferred_element_type=F32)
    return lax.dot_general(a.astype(BF16), b.astype(BF16), dims, preferred_element_type=F32)


def _make_dots(hp):
    @jax.custom_vjp
    def nn(a, b):
        return _dg(a, b, 1, 0, hp)

    def nn_f(a, b):
        return nn(a, b), (a, b)

    def nn_b(r, ct):
        a, b = r
        return _dg(ct, b, 1, 1, hp), _dg(a, ct, 0, 0, hp)

    nn.defvjp(nn_f, nn_b)

    @jax.custom_vjp
    def nt(a, b):
        return _dg(a, b, 1, 1, hp)

    def nt_f(a, b):
        return nt(a, b), (a, b)

    def nt_b(r, ct):
        a, b = r
        return _dg(ct, b, 1, 0, hp), _dg(ct, a, 0, 0, hp)

    nt.defvjp(nt_f, nt_b)

    @jax.custom_vjp
    def tn(a, b):
        return _dg(a, b, 0, 0, hp)

    def tn_f(a, b):
        return tn(a, b), (a, b)

    def tn_b(r, ct):
        a, b = r
        return _dg(b, ct, 1, 1, hp), _dg(a, ct, 1, 0, hp)

    tn.defvjp(tn_f, tn_b)
    return nn, nt, tn


bdot_nn, bdot_nt, bdot_tn = _make_dots(False)


def rmsnorm_fwd(x, g, tm):
    M, D = x.shape

    def body(x_ref, g_ref, h_ref):
        h_ref[...] = _rms(x_ref[...], g_ref[...]).astype(h_ref.dtype)

    return pl.pallas_call(
        body, name="rmsnorm_fwd", grid=(M // tm,),
        in_specs=[pl.BlockSpec((tm, D), lambda i: (i, 0)), pl.BlockSpec((1, D), lambda i: (0, 0))],
        out_specs=pl.BlockSpec((tm, D), lambda i: (i, 0)),
        out_shape=jax.ShapeDtypeStruct((M, D), BF16),
        compiler_params=_cparams(("arbitrary",)),
    )(x, g)


def rmsnorm_bwd(x, g, dh, dres, tm):
    M, D = x.shape

    def body(x_ref, g_ref, dh_ref, dres_ref, dx_ref, dg_ref):
        _, vjp = jax.vjp(_rms, x_ref[...], g_ref[...])
        dx, dg = vjp(dh_ref[...])
        dx_ref[...] = dres_ref[...] + dx

        @pl.when(pl.program_id(0) == 0)
        def _():
            dg_ref[...] = jnp.zeros_like(dg_ref)

        dg_ref[...] += dg

    row = pl.BlockSpec((tm, D), lambda i: (i, 0))
    vec = pl.BlockSpec((1, D), lambda i: (0, 0))
    return pl.pallas_call(
        body, name="rmsnorm_bwd", grid=(M // tm,),
        in_specs=[row, vec, row, row], out_specs=[row, vec],
        out_shape=[jax.ShapeDtypeStruct((M, D), F32), jax.ShapeDtypeStruct((1, D), F32)],
        compiler_params=_cparams(("arbitrary",)),
    )(x, g, dh, dres)


def loss_head(x, g, target, tm):
    M, D = x.shape

    def body(x_ref, g_ref, t_ref, dx_ref, dg_ref, loss_ref):
        t = t_ref[...]

        def f(xv, gv):
            err = _rms(xv, gv) - t
            return 0.5 * jnp.sum(jnp.mean(err * err, axis=-1, keepdims=True), axis=0, keepdims=True)

        val, vjp = jax.vjp(f, x_ref[...], g_ref[...])
        dx, dg = vjp(jnp.ones((1, 1), F32))
        dx_ref[...] = dx

        @pl.when(pl.program_id(0) == 0)
        def _():
            dg_ref[...] = jnp.zeros_like(dg_ref)
            loss_ref[...] = jnp.zeros_like(loss_ref)

        dg_ref[...] += dg
        loss_ref[...] += jnp.broadcast_to(val, loss_ref.shape)

    row = pl.BlockSpec((tm, D), lambda i: (i, 0))
    vec = pl.BlockSpec((1, D), lambda i: (0, 0))
    return pl.pallas_call(
        body, name="loss_head", grid=(M // tm,),
        in_specs=[row, vec, row],
        out_specs=[row, vec, pl.BlockSpec((1, LANES), lambda i: (0, 0))],
        out_shape=[jax.ShapeDtypeStruct((M, D), F32), jax.ShapeDtypeStruct((1, D), F32),
                   jax.ShapeDtypeStruct((1, LANES), F32)],
        compiler_params=_cparams(("arbitrary",)),
    )(x, g, target)


def _pick(n, cands):
    for c in cands:
        if n % c == 0:
            return c
    return n


def mm_nn(a, b, *, out_dtype=F32, res=None, res_scale=1.0, name="mm_nn"):
    M, K = a.shape
    N = b.shape[1]
    tm = _pick(M, (512, 256, 128))
    tn = _pick(N, (1408, 1536, 1024, 512, 256, 128))

    def body(*refs):
        if res is None:
            a_ref, b_ref, o_ref = refs
        else:
            a_ref, b_ref, r_ref, o_ref = refs
        acc = _dg(a_ref[...], b_ref[...], 1, 0, False)
        if res is not None:
            acc = r_ref[...] + res_scale * acc
        o_ref[...] = acc.astype(o_ref.dtype)

    in_specs = [pl.BlockSpec((tm, K), lambda j, i: (i, 0)), pl.BlockSpec((K, tn), lambda j, i: (0, j))]
    args = [a, b]
    if res is not None:
        in_specs.append(pl.BlockSpec((tm, tn), lambda j, i: (i, j)))
        args.append(res)
    return pl.pallas_call(
        body, name=name, grid=(N // tn, M // tm), in_specs=in_specs,
        out_specs=pl.BlockSpec((tm, tn), lambda j, i: (i, j)),
        out_shape=jax.ShapeDtypeStruct((M, N), out_dtype),
        compiler_params=_cparams(("arbitrary", "arbitrary")),
    )(*args)


def mm_nt(pairs, *, out_dtype=F32, name="mm_nt"):
    M = pairs[0][0].shape[0]
    N = pairs[0][1].shape[0]
    tm = _pick(M, (512, 256, 128))
    tn = _pick(N, (1024, 512, 256, 128))
    np_ = len(pairs)

    def body(*refs):
        o_ref = refs[-1]
        acc = None
        for p in range(np_):
            t = _dg(refs[2 * p][...], refs[2 * p + 1][...], 1, 1, False)
            acc = t if acc is None else acc + t
        o_ref[...] = acc.astype(o_ref.dtype)

    in_specs, args = [], []
    for a, b in pairs:
        K = a.shape[1]
        in_specs += [pl.BlockSpec((tm, K), lambda j, i: (i, 0)), pl.BlockSpec((tn, K), lambda j, i: (j, 0))]
        args += [a, b]
    return pl.pallas_call(
        body, name=name, grid=(N // tn, M // tm), in_specs=in_specs,
        out_specs=pl.BlockSpec((tm, tn), lambda j, i: (i, j)),
        out_shape=jax.ShapeDtypeStruct((M, N), out_dtype),
        compiler_params=_cparams(("arbitrary", "arbitrary")),
    )(*args)


def mm_tn(a, b, *, name="mm_tn"):
    M, K = a.shape
    N = b.shape[1]
    tm = _pick(M, (512, 256, 128))
    tk = _pick(K, (1408, 1536, 1024, 512, 256, 128))
    tn = _pick(N, (1408, 1536, 1024, 1056, 512, 256, 128))

    def body(a_ref, b_ref, o_ref):
        @pl.when(pl.program_id(2) == 0)
        def _():
            o_ref[...] = jnp.zeros_like(o_ref)

        o_ref[...] += _dg(a_ref[...], b_ref[...], 0, 0, False)

    return pl.pallas_call(
        body, name=name, grid=(K // tk, N // tn, M // tm),
        in_specs=[pl.BlockSpec((tm, tk), lambda k, j, m: (m, k)), pl.BlockSpec((tm, tn), lambda k, j, m: (m, j))],
        out_specs=pl.BlockSpec((tk, tn), lambda k, j, m: (k, j)),
        out_shape=jax.ShapeDtypeStruct((K, N), F32),
        compiler_params=_cparams(("arbitrary", "arbitrary", "arbitrary")),
    )(a, b)


def ffn_up(h, wg, wu, gather=()):
    M, K = h.shape
    N = wg.shape[1]
    tm = _pick(M, (512, 256, 128))
    tn = _pick(N, (1408, 1536, 1024, 512, 256, 128))
    T = len(gather)
    ni = M // tm
    nsteps = (N // tn) * ni
    fwd_step = (3 * nsteps) // 4

    def body(*refs):
        h_ref, wg_ref, wu_ref = refs[:3]
        g_ref, u_ref, a_ref = refs[3 + T:6 + T]
        if T:
            start, forward, finish = _gather_phases(refs[3:3 + T], refs[6 + T:6 + 2 * T], *refs[6 + 2 * T:])
            step = pl.program_id(0) * ni + pl.program_id(1)
            pl.when(step == 0)(start)
        hv = h_ref[...]
        g = _dg(hv, wg_ref[...], 1, 0, False)
        u = _dg(hv, wu_ref[...], 1, 0, False)
        g_ref[...] = g
        u_ref[...] = u
        a_ref[...] = (_silu(g) * u).astype(a_ref.dtype)
        if T:
            pl.when(step == fwd_step)(forward)
            pl.when(step == nsteps - 1)(finish)

    wspec = pl.BlockSpec((K, tn), lambda j, i: (0, j))
    ospec = pl.BlockSpec((tm, tn), lambda j, i: (i, j))
    return pl.pallas_call(
        body, name="ffn_up_gather" if T else "ffn_up", grid=(N // tn, ni),
        in_specs=[pl.BlockSpec((tm, K), lambda j, i: (i, 0)), wspec, wspec] + [HBM_SPEC] * T,
        out_specs=[ospec, ospec, ospec] + [HBM_SPEC] * T,
        out_shape=[jax.ShapeDtypeStruct((M, N), F32), jax.ShapeDtypeStruct((M, N), F32),
                   jax.ShapeDtypeStruct((M, N), BF16)]
        + [jax.ShapeDtypeStruct((N_DEV,) + s.shape, s.dtype) for s in gather],
        scratch_shapes=_gather_scratch(T) if T else [],
        compiler_params=_cparams(("arbitrary", "arbitrary")),
    )(h, wg, wu, *gather)


def ffn_bwd_act(dout, wd, gate, up):
    M, D = dout.shape
    N = wd.shape[0]
    tm = _pick(M, (512, 256, 128))
    tn = _pick(N, (1408, 1536, 1024, 512, 256, 128))

    def body(d_ref, w_ref, g_ref, u_ref, dg_ref, du_ref):
        dact = _dg(d_ref[...], w_ref[...], 1, 1, False)
        _, vjp = jax.vjp(lambda g, u: _silu(g) * u, g_ref[...], u_ref[...])
        dg, du = vjp(dact)
        dg_ref[...] = dg.astype(dg_ref.dtype)
        du_ref[...] = du.astype(du_ref.dtype)

    ospec = pl.BlockSpec((tm, tn), lambda j, i: (i, j))
    return pl.pallas_call(
        body, name="ffn_bwd_act", grid=(N // tn, M // tm),
        in_specs=[pl.BlockSpec((tm, D), lambda j, i: (i, 0)), pl.BlockSpec((tn, D), lambda j, i: (j, 0)), ospec, ospec],
        out_specs=[ospec, ospec],
        out_shape=[jax.ShapeDtypeStruct((M, N), BF16), jax.ShapeDtypeStruct((M, N), BF16)],
        compiler_params=_cparams(("arbitrary", "arbitrary")),
    )(dout, wd, gate, up)


def scale_cast(x, scale, tm):
    M, D = x.shape

    def body(x_ref, o_ref):
        o_ref[...] = (scale * x_ref[...]).astype(o_ref.dtype)

    row = pl.BlockSpec((tm, D), lambda i: (i, 0))
    return pl.pallas_call(
        body, name="scale_cast", grid=(M // tm,), in_specs=[row], out_specs=row,
        out_shape=jax.ShapeDtypeStruct((M, D), BF16), compiler_params=_cparams(("arbitrary",)),
    )(x)


def _t5_bucket(distance):
    max_exact = NUM_BUCKETS // 2
    n = distance.astype(jnp.float32)
    large = max_exact + (jnp.log(jnp.maximum(n, 1.0) / max_exact)
                         / math.log(MAX_DISTANCE / max_exact) * (NUM_BUCKETS - max_exact))
    large = jnp.minimum(large.astype(jnp.int32), NUM_BUCKETS - 1)
    return jnp.where(distance < max_exact, distance, large)


def _attn_geometry(blk, nb):
    nk = 2 * blk if nb > 1 else blk
    off = blk if nb > 1 else 0
    return nk, off


def _bucket_index(blk, nb, dilation):
    nk, off = _attn_geometry(blk, nb)
    rel = jnp.arange(blk)[:, None] + off - jnp.arange(nk)[None, :]
    return _t5_bucket(jnp.maximum(rel, 0) * dilation)


def _attn_block_pair(q2, kk2, vv2, bias2, valid2, scale):
    nk = kk2.shape[0]
    blk = q2.shape[0]
    s = pdot_nt(q2, kk2) * scale + bias2
    s = jnp.where(valid2, s, NEG_INF)
    ma = jnp.max(s[:, :nk], axis=-1, keepdims=True)
    mb = jnp.max(s[:, nk:], axis=-1, keepdims=True)
    m = lax.stop_gradient(jnp.where(_lane_half(s.shape), ma, mb))
    p = jnp.exp(s - m)
    da = jnp.sum(p[:, :nk], axis=-1, keepdims=True)
    db = jnp.sum(p[:, nk:], axis=-1, keepdims=True)
    first = _lane_half((blk, q2.shape[1]))
    o = pdot_nn(p, vv2) / jnp.where(first, da, db)
    lse = jnp.where(first, lax.stop_gradient(ma) + jnp.log(da), lax.stop_gradient(mb) + jnp.log(db))
    return o, lse


def _attn_valid(blk, nk, off, steps, first):
    q_loc = lax.broadcasted_iota(jnp.int32, (blk, 2 * nk), 0)
    k_loc = lax.broadcasted_iota(jnp.int32, (blk, 2 * nk), 1) % nk
    rel = q_loc + off - k_loc
    valid = (rel >= 0) & (rel <= steps)
    if off:
        valid = valid & ((k_loc >= off) | jnp.logical_not(first))
    return valid


def _sub(ref, r, pp, blk, d):
    if d == 1:
        return ref.at[0, :, pl.ds(pp * LANES, LANES)]
    return ref.at[0, pl.ds(r, blk, stride=d), :]


def _pairs_per_step(d):
    return 1 if d > 1 else A_GROUP_WIDTH // LANES


def _for_units(d, fn):
    for r in range(d):
        for pp in range(_pairs_per_step(d)):
            fn(r, pp)


def attn_fwd(proj, bias2, gi, d, steps):
    B, S, _ = proj.shape
    blk = A_BLOCK
    T = blk * d
    nb = S // T
    nk, off = _attn_geometry(blk, nb)
    scale = A_HEAD_DIM ** -0.5
    PP = _pairs_per_step(d)
    HP = A_GROUP_WIDTH // LANES // PP
    Wb = PP * LANES

    def body(*refs):
        if nb > 1:
            q_ref, kp_ref, kc_ref, vp_ref, vc_ref, b_ref, o_ref, l_ref = refs
        else:
            q_ref, kc_ref, vc_ref, b_ref, o_ref, l_ref = refs
        valid = _attn_valid(blk, nk, off, steps, pl.program_id(2) == 0)

        def one(r, pp):
            sub = lambda ref: _sub(ref, r, pp, blk, d)
            if nb > 1:
                kk = jnp.concatenate([sub(kp_ref)[...], sub(kc_ref)[...]], axis=0)
                vv = jnp.concatenate([sub(vp_ref)[...], sub(vc_ref)[...]], axis=0)
            else:
                kk, vv = sub(kc_ref)[...], sub(vc_ref)[...]
            o, lse = _attn_block_pair(sub(q_ref)[...], kk, vv, b_ref[pp], valid, scale)
            sub(o_ref)[...] = o
            sub(l_ref)[...] = lse

        _for_units(d, one)

    def cur(t):
        return pl.BlockSpec((1, T, Wb), lambda b, h, i: (b, i, (gi * 3 + t) * HP + h))

    def prev(t):
        return pl.BlockSpec((1, T, Wb), lambda b, h, i: (b, jnp.maximum(i - 1, 0), (gi * 3 + t) * HP + h))

    bspec = pl.BlockSpec((PP, blk, 2 * nk), lambda b, h, i: (h, 0, 0))
    out = pl.BlockSpec((1, T, Wb), lambda b, h, i: (b, i, h))
    if nb > 1:
        in_specs, args = [cur(0), prev(1), cur(1), prev(2), cur(2), bspec], [proj] * 5 + [bias2]
    else:
        in_specs, args = [cur(0), cur(1), cur(2), bspec], [proj] * 3 + [bias2]
    osh = jax.ShapeDtypeStruct((B, S, A_GROUP_WIDTH), F32)
    return pl.pallas_call(
        body, name="attn_fwd", grid=(B, HP, nb), in_specs=in_specs, out_specs=[out, out], out_shape=[osh, osh],
        compiler_params=_cparams(("arbitrary", "arbitrary", "arbitrary")),
    )(*args)


def attn_bwd(proj, bias2, do, dlse, gi, d, steps):
    B, S, _ = proj.shape
    blk = A_BLOCK
    T = blk * d
    nb = S // T
    nk, off = _attn_geometry(blk, nb)
    scale = A_HEAD_DIM ** -0.5
    PP = _pairs_per_step(d)
    HP = A_GROUP_WIDTH // LANES // PP
    Wb = PP * LANES
    nsteps = nb + 1 if nb > 1 else 1

    def body(*refs):
        if nb > 1:
            (q_ref, kp_ref, kc_ref, vp_ref, vc_ref, b_ref, do_ref, dl_ref,
             dq_ref, dk_ref, dv_ref, db_ref, ck, cv) = refs
        else:
            q_ref, kc_ref, vc_ref, b_ref, do_ref, dl_ref, dq_ref, dk_ref, dv_ref, db_ref = refs
        b = pl.program_id(0)
        h = pl.program_id(1)
        i = pl.program_id(2)

        @pl.when((b == 0) & (h == 0) & (i == 0))
        def _():
            db_ref[...] = jnp.zeros_like(db_ref)

        if nb > 1:
            @pl.when(i == 0)
            def _():
                ck[...] = jnp.zeros_like(ck)
                cv[...] = jnp.zeros_like(cv)

        def work():
            valid = _attn_valid(blk, nk, off, steps, i == 0)

            def one(r, pp):
                sub = lambda ref: _sub(ref, r, pp, blk, d)
                if nb > 1:
                    kk = jnp.concatenate([sub(kp_ref)[...], sub(kc_ref)[...]], axis=0)
                    vv = jnp.concatenate([sub(vp_ref)[...], sub(vc_ref)[...]], axis=0)
                else:
                    kk, vv = sub(kc_ref)[...], sub(vc_ref)[...]
                _, vjp = jax.vjp(lambda a, c, e, f: _attn_block_pair(a, c, e, f, valid, scale),
                                 sub(q_ref)[...], kk, vv, b_ref[h * PP + pp])
                dq, dkk, dvv, dbias = vjp((sub(do_ref)[...], sub(dl_ref)[...]))
                sub(dq_ref)[...] = dq
                db_ref[h * PP + pp] += dbias
                if nb > 1:
                    idx = (slice(None), pl.ds(pp * LANES, LANES)) if d == 1 else (pl.ds(r, blk, stride=d), slice(None))
                    sub(dk_ref)[...] = ck[idx] + dkk[:blk]
                    sub(dv_ref)[...] = cv[idx] + dvv[:blk]
                    ck[idx] = dkk[blk:]
                    cv[idx] = dvv[blk:]
                else:
                    sub(dk_ref)[...] = dkk
                    sub(dv_ref)[...] = dvv

            _for_units(d, one)

        if nb > 1:
            pl.when(i < nb)(work)

            @pl.when(i == nb)
            def _():
                dk_ref[0] = ck[...]
                dv_ref[0] = cv[...]
        else:
            work()

    last = nb - 1

    def cur(t):
        return pl.BlockSpec((1, T, Wb), lambda b, h, i: (b, jnp.minimum(i, last), (gi * 3 + t) * HP + h))

    def prev(t):
        return pl.BlockSpec((1, T, Wb), lambda b, h, i: (b, jnp.clip(i - 1, 0, last), (gi * 3 + t) * HP + h))

    ocur = pl.BlockSpec((1, T, Wb), lambda b, h, i: (b, jnp.minimum(i, last), h))
    oprev = pl.BlockSpec((1, T, Wb), lambda b, h, i: (b, jnp.clip(i - 1, 0, last), h))
    bspec = pl.BlockSpec((HP * PP, blk, 2 * nk), lambda b, h, i: (0, 0, 0))
    if nb > 1:
        in_specs, args = [cur(0), prev(1), cur(1), prev(2), cur(2), bspec, ocur, ocur], [proj] * 5 + [bias2, do, dlse]
        out_specs = [ocur, oprev, oprev, bspec]
        scratch = [pltpu.VMEM((T, Wb), F32), pltpu.VMEM((T, Wb), F32)]
    else:
        in_specs, args = [cur(0), cur(1), cur(2), bspec, ocur, ocur], [proj] * 3 + [bias2, do, dlse]
        out_specs = [ocur, ocur, ocur, bspec]
        scratch = []
    osh = jax.ShapeDtypeStruct((B, S, A_GROUP_WIDTH), F32)
    return pl.pallas_call(
        body, name="attn_bwd", grid=(B, HP, nsteps), in_specs=in_specs, out_specs=out_specs,
        out_shape=[osh, osh, osh, jax.ShapeDtypeStruct((HP * PP, blk, 2 * nk), F32)],
        scratch_shapes=scratch,
        compiler_params=_cparams(("arbitrary", "arbitrary", "arbitrary")),
    )(*args)


def _merge(o0, o1, o2, l0, l1, l2):
    m = lax.stop_gradient(jnp.maximum(jnp.maximum(l0, l1), l2))
    e0, e1, e2 = jnp.exp(l0 - m), jnp.exp(l1 - m), jnp.exp(l2 - m)
    inv = 1.0 / (e0 + e1 + e2)
    return (e0 * inv) * o0 + (e1 * inv) * o1 + (e2 * inv) * o2


def merge_fwd(os_, ls, tm):
    M, W = os_[0].shape

    def body(o0, o1, o2, l0, l1, l2, out):
        out[...] = _merge(o0[...], o1[...], o2[...], l0[...], l1[...], l2[...]).astype(out.dtype)

    row = pl.BlockSpec((tm, W), lambda i: (i, 0))
    return pl.pallas_call(
        body, name="merge_fwd", grid=(M // tm,), in_specs=[row] * 6, out_specs=row,
        out_shape=jax.ShapeDtypeStruct((M, W), BF16), compiler_params=_cparams(("arbitrary",)),
    )(*os_, *ls)


def merge_bwd(os_, ls, do, tm):
    M, W = os_[0].shape

    def body(o0, o1, o2, l0, l1, l2, d_ref, do0, do1, do2, dl0, dl1, dl2):
        _, vjp = jax.vjp(_merge, o0[...], o1[...], o2[...], l0[...], l1[...], l2[...])
        for r, val in zip((do0, do1, do2, dl0, dl1, dl2), vjp(d_ref[...])):
            r[...] = val

    row = pl.BlockSpec((tm, W), lambda i: (i, 0))
    sh = jax.ShapeDtypeStruct((M, W), F32)
    return pl.pallas_call(
        body, name="merge_bwd", grid=(M // tm,), in_specs=[row] * 7, out_specs=[row] * 6, out_shape=[sh] * 6,
        compiler_params=_cparams(("arbitrary",)),
    )(*os_, *ls, do)


def _split3(x):
    hi = x.astype(BF16)
    r = x - hi.astype(F32)
    mid = r.astype(BF16)
    lo = (r - mid.astype(F32)).astype(BF16)
    return hi, mid, lo


def bias_table(onehot, rb):
    NB, Q = onehot.shape
    H = rb.shape[1]

    def body(oh_ref, rb_ref, o_ref):
        oh = oh_ref[...]
        acc = jnp.zeros((H, Q), F32)
        for part in _split3(rb_ref[...]):
            acc = acc + _dg(part, oh, 0, 0, False)
        o_ref[...] = acc

    full = lambda s: pl.BlockSpec(s, lambda: (0,) * len(s))
    return pl.pallas_call(
        body, name="bias_table", in_specs=[full((NB, Q)), full((NB, H))], out_specs=full((H, Q)),
        out_shape=jax.ShapeDtypeStruct((H, Q), F32), compiler_params=_cparams(),
    )(onehot, rb)


def bias_bucket_grad(onehot, dbias):
    NB, Q = onehot.shape
    H = dbias.shape[0]

    def body(oh_ref, d_ref, o_ref):
        oh = oh_ref[...]
        acc = jnp.zeros((NB, H), F32)
        for part in _split3(d_ref[...]):
            acc = acc + _dg(oh, part, 1, 1, False)
        o_ref[...] = acc

    full = lambda s: pl.BlockSpec(s, lambda: (0,) * len(s))
    return pl.pallas_call(
        body, name="bias_bucket_grad", in_specs=[full((NB, Q)), full((H, Q))], out_specs=full((NB, H)),
        out_shape=jax.ShapeDtypeStruct((NB, H), F32), compiler_params=_cparams(),
    )(onehot, dbias)


def _shift_rows(x, k, S):
    t = lax.broadcasted_iota(jnp.int32, x.shape, 0)
    if k >= 0:
        return jnp.where(t >= k, pltpu.roll(x, k, 0), 0.0)
    return jnp.where(t < S + k, pltpu.roll(x, S + k, 0), 0.0)


def conv_fwd(x, w, ncol, cb):
    B, S, _ = x.shape

    def body(x_ref, w_ref, o_ref):
        xv = x_ref[0]
        wv = w_ref[...]
        acc = xv * wv[B_CONV - 1:B_CONV]
        for k in range(1, B_CONV):
            acc = acc + _shift_rows(xv, k, S) * wv[B_CONV - 1 - k:B_CONV - k]
        o_ref[0] = acc

    blk = pl.BlockSpec((1, S, cb), lambda b, j: (b, 0, j))
    return pl.pallas_call(
        body, name="conv_fwd", grid=(B, ncol // cb),
        in_specs=[blk, pl.BlockSpec((B_CONV, cb), lambda b, j: (0, j))], out_specs=blk,
        out_shape=jax.ShapeDtypeStruct((B, S, ncol), F32),
        compiler_params=_cparams(("arbitrary", "arbitrary")),
    )(x, w)


def conv_bwd(x, w, dc, col0, cb):
    B, S, C = dc.shape
    j0 = col0 // cb

    def body(x_ref, w_ref, dc_ref, dx_ref, dw_ref):
        xv = x_ref[0]
        wv = w_ref[...]
        d = dc_ref[0]
        acc = d * wv[B_CONV - 1:B_CONV]
        rows = [jnp.sum(d * xv, axis=0, keepdims=True)]
        for k in range(1, B_CONV):
            acc = acc + _shift_rows(d, -k, S) * wv[B_CONV - 1 - k:B_CONV - k]
            rows.append(jnp.sum(d * _shift_rows(xv, k, S), axis=0, keepdims=True))
        dx_ref[0] = acc.astype(dx_ref.dtype)

        @pl.when(pl.program_id(1) == 0)
        def _():
            dw_ref[...] = jnp.zeros_like(dw_ref)

        dw_ref[...] += jnp.concatenate(rows[::-1], axis=0)

    return pl.pallas_call(
        body, name="conv_bwd", grid=(C // cb, B),
        in_specs=[pl.BlockSpec((1, S, cb), lambda j, b: (b, 0, j + j0)),
                  pl.BlockSpec((B_CONV, cb), lambda j, b: (0, j)),
                  pl.BlockSpec((1, S, cb), lambda j, b: (b, 0, j))],
        out_specs=[pl.BlockSpec((1, S, cb), lambda j, b: (b, 0, j)), pl.BlockSpec((B_CONV, cb), lambda j, b: (0, j))],
        out_shape=[jax.ShapeDtypeStruct((B, S, C), BF16), jax.ShapeDtypeStruct((B_CONV, C), F32)],
        compiler_params=_cparams(("arbitrary", "arbitrary")),
    )(x, w, dc)


def _lane_half(shape):
    return lax.broadcasted_iota(jnp.int32, shape, len(shape) - 1) < shape[-1] // 2


def _bd(xw):
    first = _lane_half(xw.shape)
    return jnp.concatenate([jnp.where(first, xw, 0.0), jnp.where(first, 0.0, xw)], axis=0)


def _unbd(y):
    r = y.shape[0] // 2
    return jnp.where(_lane_half((r, y.shape[1])), y[:r], y[r:])


def _rs(xw):
    return jnp.concatenate([xw[:, :LANES], xw[:, LANES:]], axis=0)


def _unrs(y):
    r = y.shape[0] // 2
    return jnp.concatenate([y[:r], y[r:]], axis=1)


def _make_pair_dots(hp):
    @jax.custom_vjp
    def nn(xw, pw):
        return _dg(xw, _bd(pw), 1, 0, hp)

    def nn_f(xw, pw):
        return nn(xw, pw), (xw, pw)

    def nn_b(r, ct):
        xw, pw = r
        return _dg(ct, _bd(pw), 1, 1, hp), _unbd(_dg(xw, ct, 0, 0, hp))

    nn.defvjp(nn_f, nn_b)

    @jax.custom_vjp
    def nt(xw, yw):
        return _dg(xw, _bd(yw), 1, 1, hp)

    def nt_f(xw, yw):
        return nt(xw, yw), (xw, yw)

    def nt_b(r, ct):
        xw, yw = r
        return _dg(ct, _bd(yw), 1, 0, hp), _unbd(_dg(ct, xw, 0, 0, hp))

    nt.defvjp(nt_f, nt_b)

    @jax.custom_vjp
    def tn(xw, yw):
        return _dg(_rs(xw), _bd(yw), 0, 0, hp)

    def tn_f(xw, yw):
        return tn(xw, yw), (xw, yw)

    def tn_b(r, ct):
        xw, yw = r
        return _unrs(_dg(_bd(yw), ct, 1, 1, hp)), _unbd(_dg(_rs(xw), ct, 1, 0, hp))

    tn.defvjp(tn_f, tn_b)
    return nn, nt, tn


pdot_nn, pdot_nt, pdot_tn = _make_pair_dots(False)
phdot_nn, phdot_nt, phdot_tn = _make_pair_dots(True)


@jax.custom_vjp
def _tril_cumsum(gw):
    C = gw.shape[0]
    r = lax.broadcasted_iota(jnp.int32, (C, C), 0)
    c = lax.broadcasted_iota(jnp.int32, (C, C), 1)
    return _dg(jnp.where(r >= c, 1.0, 0.0), gw, 1, 0, True)


def _tril_cumsum_f(gw):
    return _tril_cumsum(gw), gw.shape[0]


def _tril_cumsum_b(C, ct):
    r = lax.broadcasted_iota(jnp.int32, (C, C), 0)
    c = lax.broadcasted_iota(jnp.int32, (C, C), 1)
    return (_dg(jnp.where(r <= c, 1.0, 0.0), ct, 1, 0, True),)


_tril_cumsum.defvjp(_tril_cumsum_f, _tril_cumsum_b)


def _pair(fa, fb, shape):
    return jnp.where(_lane_half(shape), fa, fb)


def _half_sums(xw):
    sa = jnp.sum(xw[:, :LANES], axis=-1, keepdims=True)
    sb = jnp.sum(xw[:, LANES:], axis=-1, keepdims=True)
    return _pair(sa, sb, xw.shape)


def _neumann_inverse_pair(low):
    C = low.shape[0]
    r = lax.broadcasted_iota(jnp.int32, low.shape, 0)
    c = lax.broadcasted_iota(jnp.int32, low.shape, 1) % LANES
    x = jnp.where(r == c, 1.0, 0.0) - low
    p = phdot_nn(low, low)
    n = 2
    while n < C:
        x = x + phdot_nn(x, p)
        n *= 2
        if n < C:
            p = phdot_nn(p, p)
    return x


@jax.custom_vjp
def _saved_inverse(low, tinv):
    return tinv


def _saved_inverse_f(low, tinv):
    return tinv, tinv


def _saved_inverse_b(tinv, ct):
    return -phdot_nt(phdot_tn(tinv, ct), tinv), jnp.zeros_like(tinv)


_saved_inverse.defvjp(_saved_inverse_f, _saved_inverse_b)


def _gdn_chunk_pair(cq, ck, cv, z, ab, prm_a, prm_b, nw, state, h0, tinv_saved=None):
    C, W2 = cq.shape
    lane = lax.broadcasted_iota(jnp.int32, ab.shape, 1)
    col = lambda j: jnp.sum(jnp.where(lane == j, ab, 0.0), axis=1, keepdims=True)
    a = _pair(col(h0), col(h0 + 1), (C, W2))
    b = _pair(col(h0 + B_HEADS), col(h0 + 1 + B_HEADS), (C, W2))
    a_log = _pair(prm_a[:, 0:1], prm_b[:, 0:1], (1, W2))
    dtb = _pair(prm_a[:, 1:2], prm_b[:, 1:2], (1, W2))
    nw2 = jnp.concatenate([nw, nw], axis=1)
    q = _silu(cq)
    q = q * lax.rsqrt(_half_sums(q * q) + NORM_EPS) * (B_HEAD_DIM ** -0.5)
    k = _silu(ck)
    k = k * lax.rsqrt(_half_sums(k * k) + NORM_EPS)
    v = _silu(cv)
    beta = _sigmoid(b)
    g = -jnp.exp(a_log) * _softplus(a + dtb)

    r = lax.broadcasted_iota(jnp.int32, (C, W2), 0)
    c = lax.broadcasted_iota(jnp.int32, (C, W2), 1) % LANES
    tril = r >= c
    gc = _tril_cumsum(g)
    gcj = jnp.concatenate([gc[:, :LANES].T, gc[:, LANES:].T], axis=1)
    decay = jnp.where(tril, jnp.exp(jnp.where(tril, gc - gcj, 0.0)), 0.0)
    gl = gc[C - 1:C, :]
    egc = jnp.exp(gc)

    kb = k * beta
    kk = pdot_nt(jnp.concatenate([kb, q], axis=0), k)
    low = jnp.where(r > c, kk[:C] * decay, 0.0)
    attn = kk[C:] * decay
    tinv = _neumann_inverse_pair(low) if tinv_saved is None else _saved_inverse(low, tinv_saved)
    u = pdot_nn(tinv, v * beta)
    w = pdot_nn(tinv, kb * egc)
    ws = pdot_nn(jnp.concatenate([w, q * egc], axis=0), state)
    v_new = u - ws[:C]
    o = ws[C:] + pdot_nn(attn, v_new)
    new_state = state * jnp.exp(gl) + pdot_tn(k * jnp.exp(gl - gc), v_new)
    y = o * lax.rsqrt(_half_sums(o * o) * (1.0 / B_HEAD_DIM) + NORM_EPS) * nw2 * _silu(z)
    if tinv_saved is None:
        return y, new_state, tinv
    return y, new_state


def gdn_fwd(c, proj, prm, nw):
    B, S, _ = c.shape
    H, dh, C = B_HEADS, B_HEAD_DIM, B_CHUNK
    N, HP, W2 = S // C, H // 2, 2 * dh

    def body(cq_ref, ck_ref, cv_ref, z_ref, ab_ref, prm_ref, nw_ref, y_ref, st_ref, ti_ref, state):
        h0 = pl.program_id(1) * 2
        state[...] = jnp.zeros_like(state)
        nwv = nw_ref[...]

        def step(n, carry):
            rows = pl.ds(pl.multiple_of(n * C, C), C)
            s_in = state[...]
            st_ref[0, n] = s_in
            y, s_out, tinv = _gdn_chunk_pair(cq_ref[0, rows, :], ck_ref[0, rows, :], cv_ref[0, rows, :],
                                             z_ref[0, rows, :], ab_ref[0, rows, :], prm_ref[h0], prm_ref[h0 + 1],
                                             nwv, s_in, h0)
            ti_ref[0, n] = tinv
            y_ref[0, rows, :] = y.astype(y_ref.dtype)
            state[...] = s_out
            return carry

        lax.fori_loop(0, N, step, 0)

    def col(off):
        return pl.BlockSpec((1, S, W2), lambda b, h: (b, 0, h + off // 2))

    return pl.pallas_call(
        body, name="gdn_fwd", grid=(B, HP),
        in_specs=[col(0), col(H), col(2 * H), col(3 * H), pl.BlockSpec((1, S, B_AB_PAD), lambda b, h: (b, 0, 4 * H)),
                  pl.BlockSpec((H, 1, LANES), lambda b, h: (0, 0, 0)), pl.BlockSpec((1, dh), lambda b, h: (0, 0))],
        out_specs=[col(0), pl.BlockSpec((1, N, dh, W2), lambda b, h: (b * HP + h, 0, 0, 0)),
                   pl.BlockSpec((1, N, C, W2), lambda b, h: (b * HP + h, 0, 0, 0))],
        out_shape=[jax.ShapeDtypeStruct((B, S, H * dh), BF16), jax.ShapeDtypeStruct((B * HP, N, dh, W2), F32),
                   jax.ShapeDtypeStruct((B * HP, N, C, W2), F32)],
        scratch_shapes=[pltpu.VMEM((dh, W2), F32)],
        compiler_params=_cparams(("arbitrary", "arbitrary")),
    )(c, c, c, proj, proj, prm, nw)


def gdn_bwd(c, proj, prm, nw, states, tinvs, dy):
    B, S, _ = c.shape
    H, dh, C = B_HEADS, B_HEAD_DIM, B_CHUNK
    N, HP, W2 = S // C, H // 2, 2 * dh

    def body(cq_ref, ck_ref, cv_ref, z_ref, ab_ref, prm_ref, nw_ref, st_ref, ti_ref, dy_ref,
             dq_ref, dk_ref, dv_ref, dz_ref, dab_ref, dprm_ref, dnw_ref, dstate):
        b = pl.program_id(0)
        hp = pl.program_id(1)
        h0 = hp * 2
        dstate[...] = jnp.zeros_like(dstate)

        @pl.when((b == 0) & (hp == 0))
        def _():
            dprm_ref[...] = jnp.zeros_like(dprm_ref)
            dnw_ref[...] = jnp.zeros_like(dnw_ref)

        @pl.when(hp == 0)
        def _():
            dab_ref[...] = jnp.zeros_like(dab_ref)

        nwv = nw_ref[...]

        def step(t, carry):
            n = N - 1 - t
            rows = pl.ds(pl.multiple_of(n * C, C), C)
            _, vjp = jax.vjp(functools.partial(_gdn_chunk_pair, h0=h0, tinv_saved=ti_ref[0, n]), cq_ref[0, rows, :],
                             ck_ref[0, rows, :], cv_ref[0, rows, :], z_ref[0, rows, :], ab_ref[0, rows, :],
                             prm_ref[h0], prm_ref[h0 + 1], nwv, st_ref[0, n])
            dcq, dck, dcv, dz, dab, dpa, dpb, dnw, ds = vjp((dy_ref[0, rows, :].astype(F32), dstate[...]))
            dq_ref[0, rows, :] = dcq
            dk_ref[0, rows, :] = dck
            dv_ref[0, rows, :] = dcv
            dz_ref[0, rows, :] = dz.astype(dz_ref.dtype)
            dab_ref[0, rows, :] += dab
            dprm_ref[h0] += dpa
            dprm_ref[h0 + 1] += dpb
            dnw_ref[...] += dnw
            dstate[...] = ds
            return carry

        lax.fori_loop(0, N, step, 0)

    def col(off):
        return pl.BlockSpec((1, S, W2), lambda b, h: (b, 0, h + off // 2))

    abspec = pl.BlockSpec((1, S, B_AB_PAD), lambda b, h: (b, 0, 4 * H))
    ab0 = pl.BlockSpec((1, S, B_AB_PAD), lambda b, h: (b, 0, 0))
    prm_full = pl.BlockSpec((H, 1, LANES), lambda b, h: (0, 0, 0))
    nwspec = pl.BlockSpec((1, dh), lambda b, h: (0, 0))
    wsh = jax.ShapeDtypeStruct((B, S, H * dh), F32)
    return pl.pallas_call(
        body, name="gdn_bwd", grid=(B, HP),
        in_specs=[col(0), col(H), col(2 * H), col(3 * H), abspec, prm_full, nwspec,
                  pl.BlockSpec((1, N, dh, W2), lambda b, h: (b * HP + h, 0, 0, 0)),
                  pl.BlockSpec((1, N, C, W2), lambda b, h: (b * HP + h, 0, 0, 0)), col(0)],
        out_specs=[col(0), col(0), col(0), col(0), ab0, prm_full, nwspec],
        out_shape=[wsh, wsh, wsh, jax.ShapeDtypeStruct((B, S, H * dh), BF16),
                   jax.ShapeDtypeStruct((B, S, B_AB_PAD), F32), jax.ShapeDtypeStruct((H, 1, LANES), F32),
                   jax.ShapeDtypeStruct((1, dh), F32)],
        scratch_shapes=[pltpu.VMEM((dh, W2), F32)],
        compiler_params=_cparams(("arbitrary", "arbitrary"), VMEM_LIMIT_BIG),
    )(c, c, c, proj, proj, prm, nw, states, tinvs, dy)


def _my_place():
    return lax.axis_index("x"), lax.axis_index("y"), lax.axis_index("c")


HBM_SPEC = pl.BlockSpec(memory_space=pltpu.HBM)


def _gather_phases(x_refs, out_refs, send_sems, recv_sems, local_sems):
    T = len(x_refs)
    x, y, c = _my_place()
    me, sibling = (x, y, c), (x, y, 1 - c)
    chips = [(1 - x, y), (x, 1 - y), (1 - x, 1 - y)]

    def copy(t, k, block, to, src=None):
        px, py, pc = block
        slot = out_refs[t].at[4 * px + 2 * py + pc]
        return pltpu.make_async_remote_copy(
            src_ref=slot if src is None else src, dst_ref=slot, send_sem=send_sems.at[7 * t + k],
            recv_sem=recv_sems.at[7 * t + k], device_id=to, device_id_type=MESH)

    def local(t):
        return pltpu.make_async_copy(x_refs[t], out_refs[t].at[4 * x + 2 * y + c], local_sems.at[t])

    def first(t):
        return [copy(t, 0, me, sibling, src=x_refs[t])] + [
            copy(t, 1 + j, me, (*chip, c), src=x_refs[t]) for j, chip in enumerate(chips)]

    def start():
        for t in range(T):
            local(t).start()
            for cp in first(t):
                cp.start()

    def forward():
        for j, chip in enumerate(chips):
            for t in range(T):
                copy(t, 1 + j, (*chip, c), me).wait_recv()
                copy(t, 4 + j, (*chip, c), sibling).start()

    def finish():
        for t in range(T):
            copy(t, 0, sibling, me).wait_recv()
            for j, chip in enumerate(chips):
                copy(t, 4 + j, (*chip, 1 - c), me).wait_recv()
        for t in range(T):
            for cp in first(t):
                cp.wait_send()
            for j, chip in enumerate(chips):
                copy(t, 4 + j, (*chip, c), sibling).wait_send()
            local(t).wait()

    return start, forward, finish


def _gather_scratch(T):
    return [pltpu.SemaphoreType.DMA((7 * T,)), pltpu.SemaphoreType.DMA((7 * T,)), pltpu.SemaphoreType.DMA((T,))]


def allgather_shards(shards):
    T = len(shards)

    def body(*refs):
        start, forward, finish = _gather_phases(refs[:T], refs[T:2 * T], *refs[2 * T:])
        start()
        forward()
        finish()

    return pl.pallas_call(
        body, name="allgather_shards",
        out_shape=[jax.ShapeDtypeStruct((N_DEV,) + s.shape, s.dtype) for s in shards],
        in_specs=[HBM_SPEC] * T, out_specs=[HBM_SPEC] * T, scratch_shapes=_gather_scratch(T),
    )(*shards)


def exchange_sibling(gs):
    T = len(gs)

    def body(*refs):
        g_refs, out_refs = refs[:T], refs[T:2 * T]
        send_sems, recv_sems = refs[2 * T:]
        x, y, c = _my_place()
        copies = []
        for t in range(T):
            for k in range(4):
                copies.append(pltpu.make_async_remote_copy(
                    src_ref=g_refs[t].at[2 * k + (1 - c)], dst_ref=out_refs[t].at[k],
                    send_sem=send_sems.at[4 * t + k], recv_sem=recv_sems.at[4 * t + k],
                    device_id=(x, y, 1 - c), device_id_type=MESH))
        for cp in copies:
            cp.start()
        for cp in copies:
            cp.wait()

    return pl.pallas_call(
        body, name="exchange_sibling",
        out_shape=[jax.ShapeDtypeStruct((4,) + g.shape[1:], g.dtype) for g in gs],
        in_specs=[HBM_SPEC] * T, out_specs=[HBM_SPEC] * T,
        scratch_shapes=[pltpu.SemaphoreType.DMA((4 * T,)), pltpu.SemaphoreType.DMA((4 * T,))],
    )(*gs)


def _row_tile(rows, row_bytes, budget):
    if rows * row_bytes <= budget:
        return rows
    best = None
    for t in range(16, rows, 16):
        if rows % t == 0 and t * row_bytes <= budget:
            best = t
    return best if best is not None else rows


def add_sibling(g, got, c_idx):
    _, R, L = g.shape
    tr = _row_tile(R, L * 4, 2 << 20)

    def body(c_ref, g_ref, r_ref, o_ref):
        o_ref[...] = (g_ref[...].astype(F32) + r_ref[...].astype(F32)).astype(o_ref.dtype)

    grid_spec = pltpu.PrefetchScalarGridSpec(
        num_scalar_prefetch=1, grid=(4, R // tr),
        in_specs=[pl.BlockSpec((1, tr, L), lambda k, i, c: (2 * k + c[0], i, 0)),
                  pl.BlockSpec((1, tr, L), lambda k, i, c: (k, i, 0))],
        out_specs=pl.BlockSpec((1, tr, L), lambda k, i, c: (k, i, 0)))
    return pl.pallas_call(
        body, name="add_sibling", grid_spec=grid_spec, out_shape=jax.ShapeDtypeStruct((4, R, L), RS_DTYPE),
        compiler_params=_cparams(("arbitrary", "arbitrary")),
    )(c_idx, g, got)


def exchange_chips(ps):
    T = len(ps)

    def body(*refs):
        p_refs, out_refs = refs[:T], refs[T:2 * T]
        send_sems, recv_sems, local_sems = refs[2 * T:]
        x, y, c = _my_place()
        mychip = 2 * x + y
        chips = [(1 - x, y), (x, 1 - y), (1 - x, 1 - y)]
        mine, copies = [], []
        for t in range(T):
            mine.append(pltpu.make_async_copy(p_refs[t].at[mychip], out_refs[t].at[mychip], local_sems.at[t]))
            mine[-1].start()
            for j, (px, py) in enumerate(chips):
                copies.append(pltpu.make_async_remote_copy(
                    src_ref=p_refs[t].at[2 * px + py], dst_ref=out_refs[t].at[mychip],
                    send_sem=send_sems.at[3 * t + j], recv_sem=recv_sems.at[3 * t + j],
                    device_id=(px, py, c), device_id_type=MESH))
                copies[-1].start()
        for t in range(T):
            for j, (px, py) in enumerate(chips):
                pltpu.make_async_remote_copy(
                    src_ref=p_refs[t].at[mychip], dst_ref=out_refs[t].at[2 * px + py],
                    send_sem=send_sems.at[3 * t + j], recv_sem=recv_sems.at[3 * t + j],
                    device_id=(px, py, c), device_id_type=MESH).wait_recv()
        for cp in copies:
            cp.wait_send()
        for cp in mine:
            cp.wait()

    return pl.pallas_call(
        body, name="exchange_chips", out_shape=[jax.ShapeDtypeStruct(p.shape, p.dtype) for p in ps],
        in_specs=[HBM_SPEC] * T, out_specs=[HBM_SPEC] * T,
        scratch_shapes=[pltpu.SemaphoreType.DMA((3 * T,)), pltpu.SemaphoreType.DMA((3 * T,)),
                        pltpu.SemaphoreType.DMA((T,))],
    )(*ps)


def allgather_small(s):
    R, W = s.shape

    def body(s_ref, out_ref, send_sems, recv_sems):
        x, y, c = _my_place()
        me = 4 * x + 2 * y + c
        out_ref[me] = s_ref[...]
        copies = []
        for j in range(1, N_DEV):
            peer = (x ^ (j >> 2), y ^ ((j >> 1) & 1), c ^ (j & 1))
            copies.append(pltpu.make_async_remote_copy(
                src_ref=s_ref, dst_ref=out_ref.at[me], send_sem=send_sems.at[j - 1], recv_sem=recv_sems.at[j - 1],
                device_id=peer, device_id_type=MESH))
        for cp in copies:
            cp.start()
        for j in range(1, N_DEV):
            px, py, pc = x ^ (j >> 2), y ^ ((j >> 1) & 1), c ^ (j & 1)
            pltpu.make_async_remote_copy(
                src_ref=s_ref, dst_ref=out_ref.at[4 * px + 2 * py + pc], send_sem=send_sems.at[j - 1],
                recv_sem=recv_sems.at[j - 1], device_id=(px, py, pc), device_id_type=MESH).wait_recv()
        for cp in copies:
            cp.wait_send()

    vm = pl.BlockSpec(memory_space=pltpu.VMEM)
    return pl.pallas_call(
        body, name="allgather_small", out_shape=jax.ShapeDtypeStruct((N_DEV, R, W), s.dtype),
        in_specs=[vm], out_specs=vm,
        scratch_shapes=[pltpu.SemaphoreType.DMA((7,)), pltpu.SemaphoreType.DMA((7,))],
    )(s)


def _adamw(w, g, m, v):
    m = ADAM_B1 * m + (1.0 - ADAM_B1) * g
    v = ADAM_B2 * v + (1.0 - ADAM_B2) * jnp.square(g)
    m_hat = m / (1.0 - ADAM_B1 ** ADAM_STEP)
    v_hat = v / (1.0 - ADAM_B2 ** ADAM_STEP)
    delta = -ADAM_LR * (m_hat / (jnp.sqrt(v_hat) + ADAM_EPS) + ADAM_WD * w)
    return delta, m, v


def sum_adamw(parts, w, m, v):
    P, G, Rp, Lp = parts.shape
    _, R, L = w.shape
    tr = _row_tile(R, P * Lp * parts.dtype.itemsize + 7 * L * 4, 5 << 20)

    def body(p_ref, w_ref, m_ref, v_ref, g_ref, d_ref, nm_ref, nv_ref):
        g = p_ref[0, 0].astype(F32)
        for k in range(1, P):
            g = g + p_ref[k, 0].astype(F32)
        g = g[:, :L]
        d, nm, nv = _adamw(w_ref[0], g, m_ref[0], v_ref[0])
        g_ref[0] = g
        d_ref[0] = d
        nm_ref[0] = nm
        nv_ref[0] = nv

    row = pl.BlockSpec((1, tr, L), lambda gi, i: (gi, i, 0))
    sh = jax.ShapeDtypeStruct((G, R, L), F32)
    return pl.pallas_call(
        body, name="sum_adamw", grid=(G, R // tr),
        in_specs=[pl.BlockSpec((P, 1, tr, Lp), lambda gi, i: (0, gi, i, 0)), row, row, row],
        out_specs=[row] * 4, out_shape=[sh] * 4, compiler_params=_cparams(("arbitrary", "arbitrary")),
    )(parts, w, m, v)


def _unblock(t, axis):
    sh = t.shape[1:]
    t = jnp.moveaxis(t, 0, axis)
    return t.reshape(sh[:axis] + (N_DEV * sh[axis],) + sh[axis + 1:])


def _block(a, axis):
    sh = a.shape
    t = a.reshape(sh[:axis] + (N_DEV, sh[axis] // N_DEV) + sh[axis + 1:])
    return jnp.moveaxis(t, axis, 0)


def _rows3(t):
    return t.reshape((t.shape[0], -1, t.shape[-1]))


def _pad_to(a, axis, n):
    pad = [(0, 0)] * a.ndim
    pad[axis] = (0, n - a.shape[axis])
    return jnp.pad(a, pad)


def _lane_pad(n):
    return -(-n // LANES) * LANES


def _pack(arrs, row_mult):
    flat = jnp.concatenate([a.astype(F32).reshape(-1) for a in arrs])
    n = flat.shape[0]
    rows = -(-n // LANES)
    rows = -(-rows // row_mult) * row_mult
    return jnp.pad(flat, (0, rows * LANES - n)).reshape(rows, LANES)


def _unpack(buf, shapes):
    flat = buf.reshape(-1)
    out, off = [], 0
    for sh in shapes:
        n = math.prod(sh)
        out.append(flat[off:off + n].reshape(sh))
        off += n
    return out


def _ffn_fwd(x, g, wg, wu, wd, tm, gather=()):
    h = rmsnorm_fwd(x, g, tm)
    gate, up, act, *gathered = ffn_up(h, wg, wu, gather)
    x_new = mm_nn(act, wd, res=x, res_scale=MACARON_WEIGHT, name="ffn_down")
    return x_new, (x, h, gate, up, act), gathered


def _ffn_bwd(dx, saved, g, wg, wu, wd, tm):
    x, h, gate, up, act = saved
    dout = scale_cast(dx, MACARON_WEIGHT, tm)
    dgate, dup = ffn_bwd_act(dout, wd, gate, up)
    dwd = mm_tn(act, dout, name="ffn_dwd")
    dwg = mm_tn(h, dgate, name="ffn_dwg")
    dwu = mm_tn(h, dup, name="ffn_dwu")
    dh = mm_nt([(dgate, wg), (dup, wu)], name="ffn_dh")
    dx_in, dg = rmsnorm_bwd(x, g, dh, dx, tm)
    return dx_in, dg, dwg, dwu, dwd


def _bucket_onehot(nb, d):
    idx = _bucket_index(A_BLOCK, nb, d)
    onehot = (idx.reshape(-1)[None, :] == jnp.arange(NUM_BUCKETS)[:, None]).astype(BF16)
    return onehot, idx.shape


def _pair_tables(t, blk, nk):
    t = t.reshape(A_HEADS // 2, 2, blk, nk)
    return jnp.transpose(t, (0, 2, 1, 3)).reshape(A_HEADS // 2, blk, 2 * nk)


def _unpair_tables(t, blk, nk):
    t = t.reshape(A_HEADS // 2, blk, 2, nk)
    return jnp.transpose(t, (0, 2, 1, 3)).reshape(A_HEADS, blk * nk)


def _mixer_a_fwd(x, g, w_in, w_out, rel_bias, B, S, tm):
    h = rmsnorm_fwd(x, g, tm)
    proj = mm_nn(h, w_in, name="a_proj").reshape(B, S, -1)
    biases, outs, lses = [], [], []
    for gi, (window, d) in enumerate(A_PATTERNS):
        nb = (S // d) // A_BLOCK
        onehot, (blk, nk) = _bucket_onehot(nb, d)
        bias2 = _pair_tables(bias_table(onehot, rel_bias[:, gi * A_HEADS:(gi + 1) * A_HEADS]), blk, nk)
        o, lse = attn_fwd(proj, bias2, gi, d, window // d)
        biases.append(bias2)
        outs.append(o.reshape(B * S, -1))
        lses.append(lse.reshape(B * S, -1))
    om = merge_fwd(outs, lses, tm)
    x_new = mm_nn(om, w_out, res=x, name="a_out")
    return x_new, (x, h, proj, biases, outs, lses, om)


def _mixer_a_bwd(dx, saved, g, w_in, w_out, B, S, tm):
    x, h, proj, biases, outs, lses, om = saved
    dxb = scale_cast(dx, 1.0, tm)
    dw_out = mm_tn(om, dxb, name="a_dwout")
    dom = mm_nt([(dxb, w_out)], name="a_dom")
    g6 = merge_bwd(outs, lses, dom, tm)
    dparts, dbias_cols = [], []
    for gi, (window, d) in enumerate(A_PATTERNS):
        do = g6[gi].reshape(B, S, -1)
        dl = g6[3 + gi].reshape(B, S, -1)
        dq, dk, dv, dbias2 = attn_bwd(proj, biases[gi], do, dl, gi, d, window // d)
        dparts += [dq, dk, dv]
        onehot, (blk, nk) = _bucket_onehot((S // d) // A_BLOCK, d)
        dbias_cols.append(bias_bucket_grad(onehot, _unpair_tables(dbias2, blk, nk)))
    d_rel_bias = jnp.concatenate(dbias_cols, axis=1)
    dproj = jnp.concatenate(dparts, axis=2).astype(BF16).reshape(B * S, -1)
    dw_in = mm_tn(h, dproj, name="a_dwin")
    dh = mm_nt([(dproj, w_in)], name="a_dh")
    dx_in, dg = rmsnorm_bwd(x, g, dh, dx, tm)
    return dx_in, dg, dw_in, dw_out, d_rel_bias


def _mixer_b_fwd(x, g, w_in_p, conv_w, prm, nw, w_out, B, S, tm):
    h = rmsnorm_fwd(x, g, tm)
    proj = mm_nn(h, w_in_p, name="b_proj").reshape(B, S, -1)
    c = conv_fwd(proj, conv_w, 3 * B_WIDTH, 512)
    y, states, tinvs = gdn_fwd(c, proj, prm, nw)
    y2 = y.reshape(B * S, B_WIDTH)
    x_new = mm_nn(y2, w_out, res=x, name="b_out")
    return x_new, (x, h, proj, c, states, tinvs, y2)


def _mixer_b_bwd(dx, saved, g, w_in_p, conv_w, prm, nw, w_out, B, S, tm):
    x, h, proj, c, states, tinvs, y2 = saved
    W = B_WIDTH
    dxb = scale_cast(dx, 1.0, tm)
    dw_out = mm_tn(y2, dxb, name="b_dwout")
    dy = mm_nt([(dxb, w_out)], out_dtype=BF16, name="b_dy").reshape(B, S, W)
    dcq, dck, dcv, dz, dab, dprm, dnw = gdn_bwd(c, proj, prm, nw, states, tinvs, dy)
    dxs, dws = [], []
    for t, dc in enumerate((dcq, dck, dcv)):
        dxp, dwp = conv_bwd(proj, conv_w[:, t * W:(t + 1) * W], dc, t * W, 512)
        dxs.append(dxp)
        dws.append(dwp)
    dconv_w = jnp.concatenate(dws, axis=1)
    dproj = jnp.concatenate(dxs + [dz, dab.astype(BF16)], axis=2).reshape(B * S, -1)
    dw_in_p = mm_tn(h, dproj, name="b_dwin")
    dh = mm_nt([(dproj, w_in_p)], name="b_dh")
    dx_in, dg = rmsnorm_bwd(x, g, dh, dx, tm)
    return dx_in, dg, dw_in_p, dconv_w, dprm, dnw, dw_out


SHARD_NAMES = ("ffn_w_gate", "ffn_w_up", "ffn_w_down", "a_w_in", "a_w_out", "b_w_in", "b_w_out", "norm_g", "b_conv_w")
SHARD_AXES = (3, 3, 2, 2, 2, 2, 1, 2, 2)
SMALL_NAMES = ("rel_bias", "b_a_log", "b_dt_bias", "b_norm_w", "final_g")


def kernel(x, norm_g, ffn_w_gate, ffn_w_up, ffn_w_down, rel_bias, a_w_in, a_w_out, b_w_in, b_conv_w, b_a_log, b_dt_bias, b_norm_w, b_w_out, final_g, loss_target, m_norm_g, m_ffn_w_gate, m_ffn_w_up, m_ffn_w_down, m_rel_bias, m_a_w_in, m_a_w_out, m_b_w_in, m_b_conv_w, m_b_a_log, m_b_dt_bias, m_b_norm_w, m_b_w_out, m_final_g, v_norm_g, v_ffn_w_gate, v_ffn_w_up, v_ffn_w_down, v_rel_bias, v_a_w_in, v_a_w_out, v_b_w_in, v_b_conv_w, v_b_a_log, v_b_dt_bias, v_b_norm_w, v_b_w_out, v_final_g):
    W = dict(norm_g=norm_g, ffn_w_gate=ffn_w_gate, ffn_w_up=ffn_w_up, ffn_w_down=ffn_w_down, rel_bias=rel_bias,
             a_w_in=a_w_in, a_w_out=a_w_out, b_w_in=b_w_in, b_conv_w=b_conv_w, b_a_log=b_a_log,
             b_dt_bias=b_dt_bias, b_norm_w=b_norm_w, b_w_out=b_w_out, final_g=final_g)
    Mo = dict(norm_g=m_norm_g, ffn_w_gate=m_ffn_w_gate, ffn_w_up=m_ffn_w_up, ffn_w_down=m_ffn_w_down,
              rel_bias=m_rel_bias, a_w_in=m_a_w_in, a_w_out=m_a_w_out, b_w_in=m_b_w_in, b_conv_w=m_b_conv_w,
              b_a_log=m_b_a_log, b_dt_bias=m_b_dt_bias, b_norm_w=m_b_norm_w, b_w_out=m_b_w_out, final_g=m_final_g)
    Vo = dict(norm_g=v_norm_g, ffn_w_gate=v_ffn_w_gate, ffn_w_up=v_ffn_w_up, ffn_w_down=v_ffn_w_down,
              rel_bias=v_rel_bias, a_w_in=v_a_w_in, a_w_out=v_a_w_out, b_w_in=v_b_w_in, b_conv_w=v_b_conv_w,
              b_a_log=v_b_a_log, b_dt_bias=v_b_dt_bias, b_norm_w=v_b_norm_w, b_w_out=v_b_w_out, final_g=v_final_g)

    B, S, D = x.shape
    M = B * S
    tm = _pick(M, (512, 256, 128))
    c_idx = lax.axis_index("c").astype(jnp.int32).reshape(1)
    fp = _lane_pad(ffn_w_gate.shape[3])

    wg_s, wu_s = (_pad_to(t.astype(BF16), 3, fp) for t in (ffn_w_gate, ffn_w_up))
    wd_s = _pad_to(ffn_w_down.astype(BF16), 2, fp)
    a_in_s, a_out_s, b_in_s, b_out_s = (t.astype(BF16) for t in (a_w_in, a_w_out, b_w_in, b_w_out))
    n_b_in = b_w_in.shape[2] * N_DEV

    def ffn_shards(i, k):
        return [wg_s[i, k], wu_s[i, k], wd_s[i, k]]

    def ffn_full(t):
        return _unblock(t[0], 1), _unblock(t[1], 1), _unblock(t[2], 0)

    def mixer_shards(i):
        return [a_in_s[i // 2], a_out_s[i // 2]] if i % 2 == 0 else [b_in_s[i // 2], b_out_s[i // 2]]

    def mixer_full(i, t):
        if i % 2 == 0:
            return _unblock(t[0], 1), _unblock(t[1], 1)
        return _pad_to(_unblock(t[0], 1), 1, n_b_in + B_AB_PAD - 2 * B_HEADS), _unblock(t[1], 0)

    first = allgather_shards(ffn_shards(0, 0) + [norm_g, b_conv_w])
    w_ffn = ffn_full(first[:3])
    norm_all, conv_all = _unblock(first[3], 2), _unblock(first[4], 2)
    prm_all = _pad_to(jnp.stack([b_a_log, b_dt_bias], axis=-1)[:, :, None, :], 3, LANES)

    xs = x.reshape(M, D)
    saved, weights = [], []
    for i in range(DEPTH):
        j = i // 2
        w1 = w_ffn
        xs, s1, got = _ffn_fwd(xs, norm_all[i, 0][None], *w1, tm, gather=mixer_shards(i) + ffn_shards(i, 1))
        wm, w2 = mixer_full(i, got[:2]), ffn_full(got[2:])
        if i % 2 == 0:
            xs, s2 = _mixer_a_fwd(xs, norm_all[i, 1][None], *wm, rel_bias, B, S, tm)
        else:
            xs, s2 = _mixer_b_fwd(xs, norm_all[i, 1][None], wm[0], conv_all[j], prm_all[j],
                                  b_norm_w[j][None], wm[1], B, S, tm)
        nxt = ffn_shards(i + 1, 0) if i + 1 < DEPTH else []
        xs, s3, got = _ffn_fwd(xs, norm_all[i, 2][None], *w2, tm, gather=nxt)
        if nxt:
            w_ffn = ffn_full(got)
        saved.append((s1, s2, s3))
        weights.append((w1, wm, w2))

    dx, d_final_g, loss_row = loss_head(xs, final_g[None], loss_target.reshape(M, D), tm)

    d_norm = [[None] * 3 for _ in range(DEPTH)]
    d_wg = [[None] * 2 for _ in range(DEPTH)]
    d_wu = [[None] * 2 for _ in range(DEPTH)]
    d_wd = [[None] * 2 for _ in range(DEPTH)]
    d_a_in, d_a_out, d_b_in, d_b_out, d_conv = [None] * 2, [None] * 2, [None] * 2, [None] * 2, [None] * 2
    d_prm, d_nw = [None] * 2, [None] * 2
    d_rel = jnp.zeros(rel_bias.shape, F32)
    for i in reversed(range(DEPTH)):
        j = i // 2
        s1, s2, s3 = saved[i]
        w1, wm, w2 = weights[i]
        dx, d_norm[i][2], d_wg[i][1], d_wu[i][1], d_wd[i][1] = _ffn_bwd(dx, s3, norm_all[i, 2][None], *w2, tm)
        if i % 2 == 0:
            dx, d_norm[i][1], d_a_in[j], d_a_out[j], drb = _mixer_a_bwd(
                dx, s2, norm_all[i, 1][None], *wm, B, S, tm)
            d_rel = d_rel + drb
        else:
            dx, d_norm[i][1], dbp, d_conv[j], d_prm[j], d_nw[j], d_b_out[j] = _mixer_b_bwd(
                dx, s2, norm_all[i, 1][None], wm[0], conv_all[j], prm_all[j], b_norm_w[j][None], wm[1], B, S, tm)
            d_b_in[j] = dbp[:, :n_b_in]
        dx, d_norm[i][0], d_wg[i][0], d_wu[i][0], d_wd[i][0] = _ffn_bwd(dx, s1, norm_all[i, 0][None], *w1, tm)
    grad_x = dx.reshape(B, S, D)

    stack2 = lambda rows: jnp.stack([jnp.stack(r) for r in rows])
    full_grads = [stack2(d_wg), stack2(d_wu), stack2(d_wd), jnp.stack(d_a_in), jnp.stack(d_a_out),
                  jnp.stack(d_b_in), jnp.stack(d_b_out),
                  jnp.stack([jnp.concatenate(r, axis=0) for r in d_norm]), jnp.stack(d_conv)]

    gblocks = [_rows3(_block(g, ax).astype(RS_DTYPE)) for g, ax in zip(full_grads, SHARD_AXES)]
    got = exchange_sibling(gblocks)
    chip_parts = [add_sibling(g, r, c_idx) for g, r in zip(gblocks, got)]
    parts = exchange_chips(chip_parts)
    out_g, out_d, out_m, out_v = {}, {}, {}, {}
    for n, p in zip(SHARD_NAMES, parts):
        w = W[n]
        w3 = w.reshape((-1,) + w.shape[-2:])
        G = w3.shape[0]
        p4 = p.reshape((4, G, p.shape[1] // G, p.shape[2]))
        res = sum_adamw(p4, w3, Mo[n].reshape(w3.shape), Vo[n].reshape(w3.shape))
        for dst, r in zip((out_g, out_d, out_m, out_v), res):
            dst[n] = r.reshape(w.shape)

    d_alog = jnp.stack([p[:, 0, 0] for p in d_prm])
    d_dtb = jnp.stack([p[:, 0, 1] for p in d_prm])
    small_g = [d_rel, d_alog, d_dtb, jnp.concatenate(d_nw, axis=0), d_final_g[0], loss_row[0, :1]]
    small_shapes = [W[n].shape for n in SMALL_NAMES]
    sg = allgather_small(_pack(small_g, 8))
    pks = lambda d: _pack([d[n] for n in SMALL_NAMES] + [jnp.zeros((1,), F32)], 8)[None]
    res = sum_adamw(sg[:, None], pks(W), pks(Mo), pks(Vo))
    loss = res[0].reshape(-1)[sum(math.prod(s) for s in small_shapes)]
    for dst, r in zip((out_g, out_d, out_m, out_v), res):
        dst.update(zip(SMALL_NAMES, _unpack(r, small_shapes)))

    order = ("norm_g", "ffn_w_gate", "ffn_w_up", "ffn_w_down", "rel_bias", "a_w_in", "a_w_out", "b_w_in",
             "b_conv_w", "b_a_log", "b_dt_bias", "b_norm_w", "b_w_out", "final_g")
    return (loss, grad_x, *[out_g[n] for n in order], *[out_d[n] for n in order],
            *[out_m[n] for n in order], *[out_v[n] for n in order])
```

```python
import functools
import math

import jax
import jax.numpy as jnp
from jax import lax
from jax.experimental import pallas as pl
from jax.experimental.pallas import tpu as pltpu

F32 = jnp.float32
BF16 = jnp.bfloat16

N_DEV = 8
NORM_EPS = 1e-6
MACARON_WEIGHT = 0.5
DEPTH = 4

A_PATTERNS = ((128, 1), (512, 4), (2048, 16))
A_HEADS = 8
A_HEAD_DIM = 64
A_GROUP_WIDTH = A_HEADS * A_HEAD_DIM
A_BLOCK = 128
NEG_INF = -1e30
NUM_BUCKETS = 32
MAX_DISTANCE = 2048

B_HEADS = 8
B_HEAD_DIM = 128
B_WIDTH = B_HEADS * B_HEAD_DIM
B_CONV = 4
B_CHUNK = 128
B_AB_PAD = 128

ADAM_LR = 0.001
ADAM_B1 = 0.9
ADAM_B2 = 0.999
ADAM_EPS = 1e-08
ADAM_WD = 0.01
ADAM_STEP = 10

LANES = 128
VMEM_LIMIT = 48 * 1024 * 1024
VMEM_LIMIT_BIG = 60 * 1024 * 1024
RS_DTYPE = BF16

MESH = pl.DeviceIdType.MESH


def _cparams(sem=None, vmem=VMEM_LIMIT):
    return pltpu.CompilerParams(dimension_semantics=sem, vmem_limit_bytes=vmem)


def _sigmoid(x):
    return 0.5 * (jnp.tanh(0.5 * x) + 1.0)


def _silu(x):
    return x * _sigmoid(x)


def _softplus(x):
    return jnp.maximum(x, 0.0) + jnp.log1p(jnp.exp(-jnp.abs(x)))


def _rms(x, g):
    return x * lax.rsqrt(jnp.mean(x * x, axis=-1, keepdims=True) + NORM_EPS) * g


def _dg(a, b, ca, cb, hp):
    dims = (((ca,), (cb,)), ((), ()))
    if hp:
        return lax.dot_general(a.astype(F32), b.astype(F32), dims, precision=lax.Precision.HIGH,
                               preferred_element_type=F32)
    return lax.dot_general(a.astype(BF16), b.astype(BF16), dims, preferred_element_type=F32)


def _make_dots(hp):
    @jax.custom_vjp
    def nn(a, b):
        return _dg(a, b, 1, 0, hp)

    def nn_f(a, b):
        return nn(a, b), (a, b)

    def nn_b(r, ct):
        a, b = r
        return _dg(ct, b, 1, 1, hp), _dg(a, ct, 0, 0, hp)

    nn.defvjp(nn_f, nn_b)

    @jax.custom_vjp
    def nt(a, b):
        return _dg(a, b, 1, 1, hp)

    def nt_f(a, b):
        return nt(a, b), (a, b)

    def nt_b(r, ct):
        a, b = r
        return _dg(ct, b, 1, 0, hp), _dg(ct, a, 0, 0, hp)

    nt.defvjp(nt_f, nt_b)

    @jax.custom_vjp
    def tn(a, b):
        return _dg(a, b, 0, 0, hp)

    def tn_f(a, b):
        return tn(a, b), (a, b)

    def tn_b(r, ct):
        a, b = r
        return _dg(b, ct, 1, 1, hp), _dg(a, ct, 1, 0, hp)

    tn.defvjp(tn_f, tn_b)
    return nn, nt, tn


bdot_nn, bdot_nt, bdot_tn = _make_dots(False)


def rmsnorm_fwd(x, g, tm):
    M, D = x.shape

    def body(x_ref, g_ref, h_ref):
        h_ref[...] = _rms(x_ref[...], g_ref[...]).astype(h_ref.dtype)

    return pl.pallas_call(
        body, name="rmsnorm_fwd", grid=(M // tm,),
        in_specs=[pl.BlockSpec((tm, D), lambda i: (i, 0)), pl.BlockSpec((1, D), lambda i: (0, 0))],
        out_specs=pl.BlockSpec((tm, D), lambda i: (i, 0)),
        out_shape=jax.ShapeDtypeStruct((M, D), BF16),
        compiler_params=_cparams(("arbitrary",)),
    )(x, g)


def rmsnorm_bwd(x, g, dh, dres, tm):
    M, D = x.shape

    def body(x_ref, g_ref, dh_ref, dres_ref, dx_ref, dg_ref):
        _, vjp = jax.vjp(_rms, x_ref[...], g_ref[...])
        dx, dg = vjp(dh_ref[...])
        dx_ref[...] = dres_ref[...] + dx

        @pl.when(pl.program_id(0) == 0)
        def _():
            dg_ref[...] = jnp.zeros_like(dg_ref)

        dg_ref[...] += dg

    row = pl.BlockSpec((tm, D), lambda i: (i, 0))
    vec = pl.BlockSpec((1, D), lambda i: (0, 0))
    return pl.pallas_call(
        body, name="rmsnorm_bwd", grid=(M // tm,),
        in_specs=[row, vec, row, row], out_specs=[row, vec],
        out_shape=[jax.ShapeDtypeStruct((M, D), F32), jax.ShapeDtypeStruct((1, D), F32)],
        compiler_params=_cparams(("arbitrary",)),
    )(x, g, dh, dres)


def loss_head(x, g, target, tm):
    M, D = x.shape

    def body(x_ref, g_ref, t_ref, dx_ref, dg_ref, loss_ref):
        t = t_ref[...]

        def f(xv, gv):
            err = _rms(xv, gv) - t
            return 0.5 * jnp.sum(jnp.mean(err * err, axis=-1, keepdims=True), axis=0, keepdims=True)

        val, vjp = jax.vjp(f, x_ref[...], g_ref[...])
        dx, dg = vjp(jnp.ones((1, 1), F32))
        dx_ref[...] = dx

        @pl.when(pl.program_id(0) == 0)
        def _():
            dg_ref[...] = jnp.zeros_like(dg_ref)
            loss_ref[...] = jnp.zeros_like(loss_ref)

        dg_ref[...] += dg
        loss_ref[...] += jnp.broadcast_to(val, loss_ref.shape)

    row = pl.BlockSpec((tm, D), lambda i: (i, 0))
    vec = pl.BlockSpec((1, D), lambda i: (0, 0))
    return pl.pallas_call(
        body, name="loss_head", grid=(M // tm,),
        in_specs=[row, vec, row],
        out_specs=[row, vec, pl.BlockSpec((1, LANES), lambda i: (0, 0))],
        out_shape=[jax.ShapeDtypeStruct((M, D), F32), jax.ShapeDtypeStruct((1, D), F32),
                   jax.ShapeDtypeStruct((1, LANES), F32)],
        compiler_params=_cparams(("arbitrary",)),
    )(x, g, target)


def _pick(n, cands):
    for c in cands:
        if n % c == 0:
            return c
    return n


def mm_nn(a, b, *, out_dtype=F32, res=None, res_scale=1.0, name="mm_nn"):
    M, K = a.shape
    N = b.shape[1]
    tm = _pick(M, (512, 256, 128))
    tn = _pick(N, (1408, 1536, 1024, 512, 256, 128))

    def body(*refs):
        if res is None:
            a_ref, b_ref, o_ref = refs
        else:
            a_ref, b_ref, r_ref, o_ref = refs
        acc = _dg(a_ref[...], b_ref[...], 1, 0, False)
        if res is not None:
            acc = r_ref[...] + res_scale * acc
        o_ref[...] = acc.astype(o_ref.dtype)

    in_specs = [pl.BlockSpec((tm, K), lambda j, i: (i, 0)), pl.BlockSpec((K, tn), lambda j, i: (0, j))]
    args = [a, b]
    if res is not None:
        in_specs.append(pl.BlockSpec((tm, tn), lambda j, i: (i, j)))
        args.append(res)
    return pl.pallas_call(
        body, name=name, grid=(N // tn, M // tm), in_specs=in_specs,
        out_specs=pl.BlockSpec((tm, tn), lambda j, i: (i, j)),
        out_shape=jax.ShapeDtypeStruct((M, N), out_dtype),
        compiler_params=_cparams(("arbitrary", "arbitrary")),
    )(*args)


def mm_nt(pairs, *, out_dtype=F32, name="mm_nt"):
    M = pairs[0][0].shape[0]
    N = pairs[0][1].shape[0]
    tm = _pick(M, (512, 256, 128))
    tn = _pick(N, (1024, 512, 256, 128))
    np_ = len(pairs)

    def body(*refs):
        o_ref = refs[-1]
        acc = None
        for p in range(np_):
            t = _dg(refs[2 * p][...], refs[2 * p + 1][...], 1, 1, False)
            acc = t if acc is None else acc + t
        o_ref[...] = acc.astype(o_ref.dtype)

    in_specs, args = [], []
    for a, b in pairs:
        K = a.shape[1]
        in_specs += [pl.BlockSpec((tm, K), lambda j, i: (i, 0)), pl.BlockSpec((tn, K), lambda j, i: (j, 0))]
        args += [a, b]
    return pl.pallas_call(
        body, name=name, grid=(N // tn, M // tm), in_specs=in_specs,
        out_specs=pl.BlockSpec((tm, tn), lambda j, i: (i, j)),
        out_shape=jax.ShapeDtypeStruct((M, N), out_dtype),
        compiler_params=_cparams(("arbitrary", "arbitrary")),
    )(*args)


def mm_tn(a, b, *, name="mm_tn"):
    M, K = a.shape
    N = b.shape[1]
    tm = _pick(M, (512, 256, 128))
    tk = _pick(K, (1408, 1536, 1024, 512, 256, 128))
    tn = _pick(N, (1408, 1536, 1024, 1056, 512, 256, 128))

    def body(a_ref, b_ref, o_ref):
        @pl.when(pl.program_id(2) == 0)
        def _():
            o_ref[...] = jnp.zeros_like(o_ref)

        o_ref[...] += _dg(a_ref[...], b_ref[...], 0, 0, False)

    return pl.pallas_call(
        body, name=name, grid=(K // tk, N // tn, M // tm),
        in_specs=[pl.BlockSpec((tm, tk), lambda k, j, m: (m, k)), pl.BlockSpec((tm, tn), lambda k, j, m: (m, j))],
        out_specs=pl.BlockSpec((tk, tn), lambda k, j, m: (k, j)),
        out_shape=jax.ShapeDtypeStruct((K, N), F32),
        compiler_params=_cparams(("arbitrary", "arbitrary", "arbitrary")),
    )(a, b)


def ffn_up(h, wg, wu, gather=()):
    M, K = h.shape
    N = wg.shape[1]
    tm = _pick(M, (512, 256, 128))
    tn = _pick(N, (1408, 1536, 1024, 512, 256, 128))
    T = len(gather)
    ni = M // tm
    nsteps = (N // tn) * ni
    fwd_step = (3 * nsteps) // 4

    def body(*refs):
        h_ref, wg_ref, wu_ref = refs[:3]
        g_ref, u_ref, a_ref = refs[3 + T:6 + T]
        if T:
            start, forward, finish = _gather_phases(refs[3:3 + T], refs[6 + T:6 + 2 * T], *refs[6 + 2 * T:])
            step = pl.program_id(0) * ni + pl.program_id(1)
            pl.when(step == 0)(start)
        hv = h_ref[...]
        g = _dg(hv, wg_ref[...], 1, 0, False)
        u = _dg(hv, wu_ref[...], 1, 0, False)
        g_ref[...] = g
        u_ref[...] = u
        a_ref[...] = (_silu(g) * u).astype(a_ref.dtype)
        if T:
            pl.when(step == fwd_step)(forward)
            pl.when(step == nsteps - 1)(finish)

    wspec = pl.BlockSpec((K, tn), lambda j, i: (0, j))
    ospec = pl.BlockSpec((tm, tn), lambda j, i: (i, j))
    return pl.pallas_call(
        body, name="ffn_up_gather" if T else "ffn_up", grid=(N // tn, ni),
        in_specs=[pl.BlockSpec((tm, K), lambda j, i: (i, 0)), wspec, wspec] + [HBM_SPEC] * T,
        out_specs=[ospec, ospec, ospec] + [HBM_SPEC] * T,
        out_shape=[jax.ShapeDtypeStruct((M, N), F32), jax.ShapeDtypeStruct((M, N), F32),
                   jax.ShapeDtypeStruct((M, N), BF16)]
        + [jax.ShapeDtypeStruct((N_DEV,) + s.shape, s.dtype) for s in gather],
        scratch_shapes=_gather_scratch(T) if T else [],
        compiler_params=_cparams(("arbitrary", "arbitrary")),
    )(h, wg, wu, *gather)


def ffn_bwd_act(dout, wd, gate, up, exchange=()):
    M, D = dout.shape
    N = wd.shape[0]
    tm = _pick(M, (512, 256, 128))
    tn = _pick(N, (1408, 1536, 1024, 512, 256, 128))
    T = len(exchange)
    ni = M // tm
    nsteps = (N // tn) * ni

    def body(*refs):
        d_ref, w_ref, g_ref, u_ref = refs[:4]
        dg_ref, du_ref = refs[4 + T:6 + T]
        if T:
            start, finish = _exchange_phases(refs[4:4 + T], refs[6 + T:6 + 2 * T], *refs[6 + 2 * T:])
            step = pl.program_id(0) * ni + pl.program_id(1)
            pl.when(step == 0)(start)
        dact = _dg(d_ref[...], w_ref[...], 1, 1, False)
        _, vjp = jax.vjp(lambda g, u: _silu(g) * u, g_ref[...], u_ref[...])
        dg, du = vjp(dact)
        dg_ref[...] = dg.astype(dg_ref.dtype)
        du_ref[...] = du.astype(du_ref.dtype)
        if T:
            pl.when(step == nsteps - 1)(finish)

    ospec = pl.BlockSpec((tm, tn), lambda j, i: (i, j))
    return pl.pallas_call(
        body, name="ffn_bwd_act_exchange" if T else "ffn_bwd_act", grid=(N // tn, ni),
        in_specs=[pl.BlockSpec((tm, D), lambda j, i: (i, 0)), pl.BlockSpec((tn, D), lambda j, i: (j, 0)), ospec, ospec]
        + [HBM_SPEC] * T,
        out_specs=[ospec, ospec] + [HBM_SPEC] * T,
        out_shape=[jax.ShapeDtypeStruct((M, N), BF16), jax.ShapeDtypeStruct((M, N), BF16)]
        + [jax.ShapeDtypeStruct(p.shape, p.dtype) for p in exchange],
        scratch_shapes=_exchange_scratch(T) if T else [],
        compiler_params=_cparams(("arbitrary", "arbitrary")),
    )(dout, wd, gate, up, *exchange)


def scale_cast(x, scale, tm):
    M, D = x.shape

    def body(x_ref, o_ref):
        o_ref[...] = (scale * x_ref[...]).astype(o_ref.dtype)

    row = pl.BlockSpec((tm, D), lambda i: (i, 0))
    return pl.pallas_call(
        body, name="scale_cast", grid=(M // tm,), in_specs=[row], out_specs=row,
        out_shape=jax.ShapeDtypeStruct((M, D), BF16), compiler_params=_cparams(("arbitrary",)),
    )(x)


def _t5_bucket(distance):
    max_exact = NUM_BUCKETS // 2
    n = distance.astype(jnp.float32)
    large = max_exact + (jnp.log(jnp.maximum(n, 1.0) / max_exact)
                         / math.log(MAX_DISTANCE / max_exact) * (NUM_BUCKETS - max_exact))
    large = jnp.minimum(large.astype(jnp.int32), NUM_BUCKETS - 1)
    return jnp.where(distance < max_exact, distance, large)


def _attn_geometry(blk, nb):
    nk = 2 * blk if nb > 1 else blk
    off = blk if nb > 1 else 0
    return nk, off


def _bucket_index(blk, nb, dilation):
    nk, off = _attn_geometry(blk, nb)
    rel = jnp.arange(blk)[:, None] + off - jnp.arange(nk)[None, :]
    return _t5_bucket(jnp.maximum(rel, 0) * dilation)


def _attn_block_pair(q2, kk2, vv2, bias2, valid2, scale):
    nk = kk2.shape[0]
    blk = q2.shape[0]
    s = pdot_nt(q2, kk2) * scale + bias2
    s = jnp.where(valid2, s, NEG_INF)
    ma = jnp.max(s[:, :nk], axis=-1, keepdims=True)
    mb = jnp.max(s[:, nk:], axis=-1, keepdims=True)
    m = lax.stop_gradient(jnp.where(_lane_half(s.shape), ma, mb))
    p = jnp.exp(s - m)
    da = jnp.sum(p[:, :nk], axis=-1, keepdims=True)
    db = jnp.sum(p[:, nk:], axis=-1, keepdims=True)
    first = _lane_half((blk, q2.shape[1]))
    o = pdot_nn(p, vv2) / jnp.where(first, da, db)
    lse = jnp.where(first, lax.stop_gradient(ma) + jnp.log(da), lax.stop_gradient(mb) + jnp.log(db))
    return o, lse


def _attn_valid(blk, nk, off, steps, first):
    q_loc = lax.broadcasted_iota(jnp.int32, (blk, 2 * nk), 0)
    k_loc = lax.broadcasted_iota(jnp.int32, (blk, 2 * nk), 1) % nk
    rel = q_loc + off - k_loc
    valid = (rel >= 0) & (rel <= steps)
    if off:
        valid = valid & ((k_loc >= off) | jnp.logical_not(first))
    return valid


def _sub(ref, r, pp, blk, d):
    if d == 1:
        return ref.at[0, :, pl.ds(pp * LANES, LANES)]
    return ref.at[0, pl.ds(r, blk, stride=d), :]


def _pairs_per_step(d):
    return 1 if d > 1 else A_GROUP_WIDTH // LANES


def _for_units(d, fn):
    for r in range(d):
        for pp in range(_pairs_per_step(d)):
            fn(r, pp)


def attn_fwd(proj, bias2, gi, d, steps):
    B, S, _ = proj.shape
    blk = A_BLOCK
    T = blk * d
    nb = S // T
    nk, off = _attn_geometry(blk, nb)
    scale = A_HEAD_DIM ** -0.5
    PP = _pairs_per_step(d)
    HP = A_GROUP_WIDTH // LANES // PP
    Wb = PP * LANES

    def body(*refs):
        if nb > 1:
            q_ref, kp_ref, kc_ref, vp_ref, vc_ref, b_ref, o_ref, l_ref = refs
        else:
            q_ref, kc_ref, vc_ref, b_ref, o_ref, l_ref = refs
        valid = _attn_valid(blk, nk, off, steps, pl.program_id(2) == 0)

        def one(r, pp):
            sub = lambda ref: _sub(ref, r, pp, blk, d)
            if nb > 1:
                kk = jnp.concatenate([sub(kp_ref)[...], sub(kc_ref)[...]], axis=0)
                vv = jnp.concatenate([sub(vp_ref)[...], sub(vc_ref)[...]], axis=0)
            else:
                kk, vv = sub(kc_ref)[...], sub(vc_ref)[...]
            o, lse = _attn_block_pair(sub(q_ref)[...], kk, vv, b_ref[pp], valid, scale)
            sub(o_ref)[...] = o
            sub(l_ref)[...] = lse

        _for_units(d, one)

    def cur(t):
        return pl.BlockSpec((1, T, Wb), lambda b, h, i: (b, i, (gi * 3 + t) * HP + h))

    def prev(t):
        return pl.BlockSpec((1, T, Wb), lambda b, h, i: (b, jnp.maximum(i - 1, 0), (gi * 3 + t) * HP + h))

    bspec = pl.BlockSpec((PP, blk, 2 * nk), lambda b, h, i: (h, 0, 0))
    out = pl.BlockSpec((1, T, Wb), lambda b, h, i: (b, i, h))
    if nb > 1:
        in_specs, args = [cur(0), prev(1), cur(1), prev(2), cur(2), bspec], [proj] * 5 + [bias2]
    else:
        in_specs, args = [cur(0), cur(1), cur(2), bspec], [proj] * 3 + [bias2]
    osh = jax.ShapeDtypeStruct((B, S, A_GROUP_WIDTH), F32)
    return pl.pallas_call(
        body, name="attn_fwd", grid=(B, HP, nb), in_specs=in_specs, out_specs=[out, out], out_shape=[osh, osh],
        compiler_params=_cparams(("arbitrary", "arbitrary", "arbitrary")),
    )(*args)


def attn_bwd(proj, bias2, do, dlse, gi, d, steps):
    B, S, _ = proj.shape
    blk = A_BLOCK
    T = blk * d
    nb = S // T
    nk, off = _attn_geometry(blk, nb)
    scale = A_HEAD_DIM ** -0.5
    PP = _pairs_per_step(d)
    HP = A_GROUP_WIDTH // LANES // PP
    Wb = PP * LANES
    nsteps = nb + 1 if nb > 1 else 1

    def body(*refs):
        if nb > 1:
            (q_ref, kp_ref, kc_ref, vp_ref, vc_ref, b_ref, do_ref, dl_ref,
             dq_ref, dk_ref, dv_ref, db_ref, ck, cv) = refs
        else:
            q_ref, kc_ref, vc_ref, b_ref, do_ref, dl_ref, dq_ref, dk_ref, dv_ref, db_ref = refs
        b = pl.program_id(0)
        h = pl.program_id(1)
        i = pl.program_id(2)

        @pl.when((b == 0) & (h == 0) & (i == 0))
        def _():
            db_ref[...] = jnp.zeros_like(db_ref)

        if nb > 1:
            @pl.when(i == 0)
            def _():
                ck[...] = jnp.zeros_like(ck)
                cv[...] = jnp.zeros_like(cv)

        def work():
            valid = _attn_valid(blk, nk, off, steps, i == 0)

            def one(r, pp):
                sub = lambda ref: _sub(ref, r, pp, blk, d)
                if nb > 1:
                    kk = jnp.concatenate([sub(kp_ref)[...], sub(kc_ref)[...]], axis=0)
                    vv = jnp.concatenate([sub(vp_ref)[...], sub(vc_ref)[...]], axis=0)
                else:
                    kk, vv = sub(kc_ref)[...], sub(vc_ref)[...]
                _, vjp = jax.vjp(lambda a, c, e, f: _attn_block_pair(a, c, e, f, valid, scale),
                                 sub(q_ref)[...], kk, vv, b_ref[h * PP + pp])
                dq, dkk, dvv, dbias = vjp((sub(do_ref)[...], sub(dl_ref)[...]))
                sub(dq_ref)[...] = dq
                db_ref[h * PP + pp] += dbias
                if nb > 1:
                    idx = (slice(None), pl.ds(pp * LANES, LANES)) if d == 1 else (pl.ds(r, blk, stride=d), slice(None))
                    sub(dk_ref)[...] = ck[idx] + dkk[:blk]
                    sub(dv_ref)[...] = cv[idx] + dvv[:blk]
                    ck[idx] = dkk[blk:]
                    cv[idx] = dvv[blk:]
                else:
                    sub(dk_ref)[...] = dkk
                    sub(dv_ref)[...] = dvv

            _for_units(d, one)

        if nb > 1:
            pl.when(i < nb)(work)

            @pl.when(i == nb)
            def _():
                dk_ref[0] = ck[...]
                dv_ref[0] = cv[...]
        else:
            work()

    last = nb - 1

    def cur(t):
        return pl.BlockSpec((1, T, Wb), lambda b, h, i: (b, jnp.minimum(i, last), (gi * 3 + t) * HP + h))

    def prev(t):
        return pl.BlockSpec((1, T, Wb), lambda b, h, i: (b, jnp.clip(i - 1, 0, last), (gi * 3 + t) * HP + h))

    ocur = pl.BlockSpec((1, T, Wb), lambda b, h, i: (b, jnp.minimum(i, last), h))
    oprev = pl.BlockSpec((1, T, Wb), lambda b, h, i: (b, jnp.clip(i - 1, 0, last), h))
    bspec = pl.BlockSpec((HP * PP, blk, 2 * nk), lambda b, h, i: (0, 0, 0))
    if nb > 1:
        in_specs, args = [cur(0), prev(1), cur(1), prev(2), cur(2), bspec, ocur, ocur], [proj] * 5 + [bias2, do, dlse]
        out_specs = [ocur, oprev, oprev, bspec]
        scratch = [pltpu.VMEM((T, Wb), F32), pltpu.VMEM((T, Wb), F32)]
    else:
        in_specs, args = [cur(0), cur(1), cur(2), bspec, ocur, ocur], [proj] * 3 + [bias2, do, dlse]
        out_specs = [ocur, ocur, ocur, bspec]
        scratch = []
    osh = jax.ShapeDtypeStruct((B, S, A_GROUP_WIDTH), F32)
    return pl.pallas_call(
        body, name="attn_bwd", grid=(B, HP, nsteps), in_specs=in_specs, out_specs=out_specs,
        out_shape=[osh, osh, osh, jax.ShapeDtypeStruct((HP * PP, blk, 2 * nk), F32)],
        scratch_shapes=scratch,
        compiler_params=_cparams(("arbitrary", "arbitrary", "arbitrary")),
    )(*args)


def _merge(o0, o1, o2, l0, l1, l2):
    m = lax.stop_gradient(jnp.maximum(jnp.maximum(l0, l1), l2))
    e0, e1, e2 = jnp.exp(l0 - m), jnp.exp(l1 - m), jnp.exp(l2 - m)
    inv = 1.0 / (e0 + e1 + e2)
    return (e0 * inv) * o0 + (e1 * inv) * o1 + (e2 * inv) * o2


def merge_fwd(os_, ls, tm):
    M, W = os_[0].shape

    def body(o0, o1, o2, l0, l1, l2, out):
        out[...] = _merge(o0[...], o1[...], o2[...], l0[...], l1[...], l2[...]).astype(out.dtype)

    row = pl.BlockSpec((tm, W), lambda i: (i, 0))
    return pl.pallas_call(
        body, name="merge_fwd", grid=(M // tm,), in_specs=[row] * 6, out_specs=row,
        out_shape=jax.ShapeDtypeStruct((M, W), BF16), compiler_params=_cparams(("arbitrary",)),
    )(*os_, *ls)


def merge_bwd(os_, ls, do, tm):
    M, W = os_[0].shape

    def body(o0, o1, o2, l0, l1, l2, d_ref, do0, do1, do2, dl0, dl1, dl2):
        _, vjp = jax.vjp(_merge, o0[...], o1[...], o2[...], l0[...], l1[...], l2[...])
        for r, val in zip((do0, do1, do2, dl0, dl1, dl2), vjp(d_ref[...])):
            r[...] = val

    row = pl.BlockSpec((tm, W), lambda i: (i, 0))
    sh = jax.ShapeDtypeStruct((M, W), F32)
    return pl.pallas_call(
        body, name="merge_bwd", grid=(M // tm,), in_specs=[row] * 7, out_specs=[row] * 6, out_shape=[sh] * 6,
        compiler_params=_cparams(("arbitrary",)),
    )(*os_, *ls, do)


def _split3(x):
    hi = x.astype(BF16)
    r = x - hi.astype(F32)
    mid = r.astype(BF16)
    lo = (r - mid.astype(F32)).astype(BF16)
    return hi, mid, lo


def bias_table(onehot, rb):
    NB, Q = onehot.shape
    H = rb.shape[1]

    def body(oh_ref, rb_ref, o_ref):
        oh = oh_ref[...]
        acc = jnp.zeros((H, Q), F32)
        for part in _split3(rb_ref[...]):
            acc = acc + _dg(part, oh, 0, 0, False)
        o_ref[...] = acc

    full = lambda s: pl.BlockSpec(s, lambda: (0,) * len(s))
    return pl.pallas_call(
        body, name="bias_table", in_specs=[full((NB, Q)), full((NB, H))], out_specs=full((H, Q)),
        out_shape=jax.ShapeDtypeStruct((H, Q), F32), compiler_params=_cparams(),
    )(onehot, rb)


def bias_bucket_grad(onehot, dbias):
    NB, Q = onehot.shape
    H = dbias.shape[0]

    def body(oh_ref, d_ref, o_ref):
        oh = oh_ref[...]
        acc = jnp.zeros((NB, H), F32)
        for part in _split3(d_ref[...]):
            acc = acc + _dg(oh, part, 1, 1, False)
        o_ref[...] = acc

    full = lambda s: pl.BlockSpec(s, lambda: (0,) * len(s))
    return pl.pallas_call(
        body, name="bias_bucket_grad", in_specs=[full((NB, Q)), full((H, Q))], out_specs=full((NB, H)),
        out_shape=jax.ShapeDtypeStruct((NB, H), F32), compiler_params=_cparams(),
    )(onehot, dbias)


def _shift_rows(x, k, S):
    t = lax.broadcasted_iota(jnp.int32, x.shape, 0)
    if k >= 0:
        return jnp.where(t >= k, pltpu.roll(x, k, 0), 0.0)
    return jnp.where(t < S + k, pltpu.roll(x, S + k, 0), 0.0)


def conv_fwd(x, w, ncol, cb):
    B, S, _ = x.shape

    def body(x_ref, w_ref, o_ref):
        xv = x_ref[0]
        wv = w_ref[...]
        acc = xv * wv[B_CONV - 1:B_CONV]
        for k in range(1, B_CONV):
            acc = acc + _shift_rows(xv, k, S) * wv[B_CONV - 1 - k:B_CONV - k]
        o_ref[0] = acc

    blk = pl.BlockSpec((1, S, cb), lambda b, j: (b, 0, j))
    return pl.pallas_call(
        body, name="conv_fwd", grid=(B, ncol // cb),
        in_specs=[blk, pl.BlockSpec((B_CONV, cb), lambda b, j: (0, j))], out_specs=blk,
        out_shape=jax.ShapeDtypeStruct((B, S, ncol), F32),
        compiler_params=_cparams(("arbitrary", "arbitrary")),
    )(x, w)


def conv_bwd(x, w, dc, col0, cb):
    B, S, C = dc.shape
    j0 = col0 // cb

    def body(x_ref, w_ref, dc_ref, dx_ref, dw_ref):
        xv = x_ref[0]
        wv = w_ref[...]
        d = dc_ref[0]
        acc = d * wv[B_CONV - 1:B_CONV]
        rows = [jnp.sum(d * xv, axis=0, keepdims=True)]
        for k in range(1, B_CONV):
            acc = acc + _shift_rows(d, -k, S) * wv[B_CONV - 1 - k:B_CONV - k]
            rows.append(jnp.sum(d * _shift_rows(xv, k, S), axis=0, keepdims=True))
        dx_ref[0] = acc.astype(dx_ref.dtype)

        @pl.when(pl.program_id(1) == 0)
        def _():
            dw_ref[...] = jnp.zeros_like(dw_ref)

        dw_ref[...] += jnp.concatenate(rows[::-1], axis=0)

    return pl.pallas_call(
        body, name="conv_bwd", grid=(C // cb, B),
        in_specs=[pl.BlockSpec((1, S, cb), lambda j, b: (b, 0, j + j0)),
                  pl.BlockSpec((B_CONV, cb), lambda j, b: (0, j)),
                  pl.BlockSpec((1, S, cb), lambda j, b: (b, 0, j))],
        out_specs=[pl.BlockSpec((1, S, cb), lambda j, b: (b, 0, j)), pl.BlockSpec((B_CONV, cb), lambda j, b: (0, j))],
        out_shape=[jax.ShapeDtypeStruct((B, S, C), BF16), jax.ShapeDtypeStruct((B_CONV, C), F32)],
        compiler_params=_cparams(("arbitrary", "arbitrary")),
    )(x, w, dc)


def _lane_half(shape):
    return lax.broadcasted_iota(jnp.int32, shape, len(shape) - 1) < shape[-1] // 2


def _bd(xw):
    first = _lane_half(xw.shape)
    return jnp.concatenate([jnp.where(first, xw, 0.0), jnp.where(first, 0.0, xw)], axis=0)


def _unbd(y):
    r = y.shape[0] // 2
    return jnp.where(_lane_half((r, y.shape[1])), y[:r], y[r:])


def _rs(xw):
    return jnp.concatenate([xw[:, :LANES], xw[:, LANES:]], axis=0)


def _unrs(y):
    r = y.shape[0] // 2
    return jnp.concatenate([y[:r], y[r:]], axis=1)


def _make_pair_dots(hp):
    @jax.custom_vjp
    def nn(xw, pw):
        return _dg(xw, _bd(pw), 1, 0, hp)

    def nn_f(xw, pw):
        return nn(xw, pw), (xw, pw)

    def nn_b(r, ct):
        xw, pw = r
        return _dg(ct, _bd(pw), 1, 1, hp), _unbd(_dg(xw, ct, 0, 0, hp))

    nn.defvjp(nn_f, nn_b)

    @jax.custom_vjp
    def nt(xw, yw):
        return _dg(xw, _bd(yw), 1, 1, hp)

    def nt_f(xw, yw):
        return nt(xw, yw), (xw, yw)

    def nt_b(r, ct):
        xw, yw = r
        return _dg(ct, _bd(yw), 1, 0, hp), _unbd(_dg(ct, xw, 0, 0, hp))

    nt.defvjp(nt_f, nt_b)

    @jax.custom_vjp
    def tn(xw, yw):
        return _dg(_rs(xw), _bd(yw), 0, 0, hp)

    def tn_f(xw, yw):
        return tn(xw, yw), (xw, yw)

    def tn_b(r, ct):
        xw, yw = r
        return _unrs(_dg(_bd(yw), ct, 1, 1, hp)), _unbd(_dg(_rs(xw), ct, 1, 0, hp))

    tn.defvjp(tn_f, tn_b)
    return nn, nt, tn


pdot_nn, pdot_nt, pdot_tn = _make_pair_dots(False)
phdot_nn, phdot_nt, phdot_tn = _make_pair_dots(True)


@jax.custom_vjp
def _tril_cumsum(gw):
    C = gw.shape[0]
    r = lax.broadcasted_iota(jnp.int32, (C, C), 0)
    c = lax.broadcasted_iota(jnp.int32, (C, C), 1)
    return _dg(jnp.where(r >= c, 1.0, 0.0), gw, 1, 0, True)


def _tril_cumsum_f(gw):
    return _tril_cumsum(gw), gw.shape[0]


def _tril_cumsum_b(C, ct):
    r = lax.broadcasted_iota(jnp.int32, (C, C), 0)
    c = lax.broadcasted_iota(jnp.int32, (C, C), 1)
    return (_dg(jnp.where(r <= c, 1.0, 0.0), ct, 1, 0, True),)


_tril_cumsum.defvjp(_tril_cumsum_f, _tril_cumsum_b)


def _pair(fa, fb, shape):
    return jnp.where(_lane_half(shape), fa, fb)


def _half_sums(xw):
    sa = jnp.sum(xw[:, :LANES], axis=-1, keepdims=True)
    sb = jnp.sum(xw[:, LANES:], axis=-1, keepdims=True)
    return _pair(sa, sb, xw.shape)


def _neumann_inverse_pair(low):
    C = low.shape[0]
    r = lax.broadcasted_iota(jnp.int32, low.shape, 0)
    c = lax.broadcasted_iota(jnp.int32, low.shape, 1) % LANES
    x = jnp.where(r == c, 1.0, 0.0) - low
    p = phdot_nn(low, low)
    n = 2
    while n < C:
        x = x + phdot_nn(x, p)
        n *= 2
        if n < C:
            p = phdot_nn(p, p)
    return x


@jax.custom_vjp
def _saved_inverse(low, tinv):
    return tinv


def _saved_inverse_f(low, tinv):
    return tinv, tinv


def _saved_inverse_b(tinv, ct):
    return -phdot_nt(phdot_tn(tinv, ct), tinv), jnp.zeros_like(tinv)


_saved_inverse.defvjp(_saved_inverse_f, _saved_inverse_b)


def _gdn_chunk_pair(cq, ck, cv, z, ab, prm_a, prm_b, nw, state, h0, tinv_saved=None):
    C, W2 = cq.shape
    lane = lax.broadcasted_iota(jnp.int32, ab.shape, 1)
    col = lambda j: jnp.sum(jnp.where(lane == j, ab, 0.0), axis=1, keepdims=True)
    a = _pair(col(h0), col(h0 + 1), (C, W2))
    b = _pair(col(h0 + B_HEADS), col(h0 + 1 + B_HEADS), (C, W2))
    a_log = _pair(prm_a[:, 0:1], prm_b[:, 0:1], (1, W2))
    dtb = _pair(prm_a[:, 1:2], prm_b[:, 1:2], (1, W2))
    nw2 = jnp.concatenate([nw, nw], axis=1)
    q = _silu(cq)
    q = q * lax.rsqrt(_half_sums(q * q) + NORM_EPS) * (B_HEAD_DIM ** -0.5)
    k = _silu(ck)
    k = k * lax.rsqrt(_half_sums(k * k) + NORM_EPS)
    v = _silu(cv)
    beta = _sigmoid(b)
    g = -jnp.exp(a_log) * _softplus(a + dtb)

    r = lax.broadcasted_iota(jnp.int32, (C, W2), 0)
    c = lax.broadcasted_iota(jnp.int32, (C, W2), 1) % LANES
    tril = r >= c
    gc = _tril_cumsum(g)
    gcj = jnp.concatenate([gc[:, :LANES].T, gc[:, LANES:].T], axis=1)
    decay = jnp.where(tril, jnp.exp(jnp.where(tril, gc - gcj, 0.0)), 0.0)
    gl = gc[C - 1:C, :]
    egc = jnp.exp(gc)

    kb = k * beta
    kk = pdot_nt(jnp.concatenate([kb, q], axis=0), k)
    low = jnp.where(r > c, kk[:C] * decay, 0.0)
    attn = kk[C:] * decay
    tinv = _neumann_inverse_pair(low) if tinv_saved is None else _saved_inverse(low, tinv_saved)
    u = pdot_nn(tinv, v * beta)
    w = pdot_nn(tinv, kb * egc)
    ws = pdot_nn(jnp.concatenate([w, q * egc], axis=0), state)
    v_new = u - ws[:C]
    o = ws[C:] + pdot_nn(attn, v_new)
    new_state = state * jnp.exp(gl) + pdot_tn(k * jnp.exp(gl - gc), v_new)
    y = o * lax.rsqrt(_half_sums(o * o) * (1.0 / B_HEAD_DIM) + NORM_EPS) * nw2 * _silu(z)
    if tinv_saved is None:
        return y, new_state, tinv
    return y, new_state


def gdn_fwd(c, proj, prm, nw):
    B, S, _ = c.shape
    H, dh, C = B_HEADS, B_HEAD_DIM, B_CHUNK
    N, HP, W2 = S // C, H // 2, 2 * dh

    def body(cq_ref, ck_ref, cv_ref, z_ref, ab_ref, prm_ref, nw_ref, y_ref, st_ref, ti_ref, state):
        h0 = pl.program_id(1) * 2
        state[...] = jnp.zeros_like(state)
        nwv = nw_ref[...]

        def step(n, carry):
            rows = pl.ds(pl.multiple_of(n * C, C), C)
            s_in = state[...]
            st_ref[0, n] = s_in
            y, s_out, tinv = _gdn_chunk_pair(cq_ref[0, rows, :], ck_ref[0, rows, :], cv_ref[0, rows, :],
                                             z_ref[0, rows, :], ab_ref[0, rows, :], prm_ref[h0], prm_ref[h0 + 1],
                                             nwv, s_in, h0)
            ti_ref[0, n] = tinv
            y_ref[0, rows, :] = y.astype(y_ref.dtype)
            state[...] = s_out
            return carry

        lax.fori_loop(0, N, step, 0)

    def col(off):
        return pl.BlockSpec((1, S, W2), lambda b, h: (b, 0, h + off // 2))

    return pl.pallas_call(
        body, name="gdn_fwd", grid=(B, HP),
        in_specs=[col(0), col(H), col(2 * H), col(3 * H), pl.BlockSpec((1, S, B_AB_PAD), lambda b, h: (b, 0, 4 * H)),
                  pl.BlockSpec((H, 1, LANES), lambda b, h: (0, 0, 0)), pl.BlockSpec((1, dh), lambda b, h: (0, 0))],
        out_specs=[col(0), pl.BlockSpec((1, N, dh, W2), lambda b, h: (b * HP + h, 0, 0, 0)),
                   pl.BlockSpec((1, N, C, W2), lambda b, h: (b * HP + h, 0, 0, 0))],
        out_shape=[jax.ShapeDtypeStruct((B, S, H * dh), BF16), jax.ShapeDtypeStruct((B * HP, N, dh, W2), F32),
                   jax.ShapeDtypeStruct((B * HP, N, C, W2), F32)],
        scratch_shapes=[pltpu.VMEM((dh, W2), F32)],
        compiler_params=_cparams(("arbitrary", "arbitrary")),
    )(c, c, c, proj, proj, prm, nw)


def gdn_bwd(c, proj, prm, nw, states, tinvs, dy):
    B, S, _ = c.shape
    H, dh, C = B_HEADS, B_HEAD_DIM, B_CHUNK
    N, HP, W2 = S // C, H // 2, 2 * dh

    def body(cq_ref, ck_ref, cv_ref, z_ref, ab_ref, prm_ref, nw_ref, st_ref, ti_ref, dy_ref,
             dq_ref, dk_ref, dv_ref, dz_ref, dab_ref, dprm_ref, dnw_ref, dstate):
        b = pl.program_id(0)
        hp = pl.program_id(1)
        h0 = hp * 2
        dstate[...] = jnp.zeros_like(dstate)

        @pl.when((b == 0) & (hp == 0))
        def _():
            dprm_ref[...] = jnp.zeros_like(dprm_ref)
            dnw_ref[...] = jnp.zeros_like(dnw_ref)

        @pl.when(hp == 0)
        def _():
            dab_ref[...] = jnp.zeros_like(dab_ref)

        nwv = nw_ref[...]

        def step(t, carry):
            n = N - 1 - t
            rows = pl.ds(pl.multiple_of(n * C, C), C)
            _, vjp = jax.vjp(functools.partial(_gdn_chunk_pair, h0=h0, tinv_saved=ti_ref[0, n]), cq_ref[0, rows, :],
                             ck_ref[0, rows, :], cv_ref[0, rows, :], z_ref[0, rows, :], ab_ref[0, rows, :],
                             prm_ref[h0], prm_ref[h0 + 1], nwv, st_ref[0, n])
            dcq, dck, dcv, dz, dab, dpa, dpb, dnw, ds = vjp((dy_ref[0, rows, :].astype(F32), dstate[...]))
            dq_ref[0, rows, :] = dcq
            dk_ref[0, rows, :] = dck
            dv_ref[0, rows, :] = dcv
            dz_ref[0, rows, :] = dz.astype(dz_ref.dtype)
            dab_ref[0, rows, :] += dab
            dprm_ref[h0] += dpa
            dprm_ref[h0 + 1] += dpb
            dnw_ref[...] += dnw
            dstate[...] = ds
            return carry

        lax.fori_loop(0, N, step, 0)

    def col(off):
        return pl.BlockSpec((1, S, W2), lambda b, h: (b, 0, h + off // 2))

    abspec = pl.BlockSpec((1, S, B_AB_PAD), lambda b, h: (b, 0, 4 * H))
    ab0 = pl.BlockSpec((1, S, B_AB_PAD), lambda b, h: (b, 0, 0))
    prm_full = pl.BlockSpec((H, 1, LANES), lambda b, h: (0, 0, 0))
    nwspec = pl.BlockSpec((1, dh), lambda b, h: (0, 0))
    wsh = jax.ShapeDtypeStruct((B, S, H * dh), F32)
    return pl.pallas_call(
        body, name="gdn_bwd", grid=(B, HP),
        in_specs=[col(0), col(H), col(2 * H), col(3 * H), abspec, prm_full, nwspec,
                  pl.BlockSpec((1, N, dh, W2), lambda b, h: (b * HP + h, 0, 0, 0)),
                  pl.BlockSpec((1, N, C, W2), lambda b, h: (b * HP + h, 0, 0, 0)), col(0)],
        out_specs=[col(0), col(0), col(0), col(0), ab0, prm_full, nwspec],
        out_shape=[wsh, wsh, wsh, jax.ShapeDtypeStruct((B, S, H * dh), BF16),
                   jax.ShapeDtypeStruct((B, S, B_AB_PAD), F32), jax.ShapeDtypeStruct((H, 1, LANES), F32),
                   jax.ShapeDtypeStruct((1, dh), F32)],
        scratch_shapes=[pltpu.VMEM((dh, W2), F32)],
        compiler_params=_cparams(("arbitrary", "arbitrary"), VMEM_LIMIT_BIG),
    )(c, c, c, proj, proj, prm, nw, states, tinvs, dy)


def _my_place():
    return lax.axis_index("x"), lax.axis_index("y"), lax.axis_index("c")


HBM_SPEC = pl.BlockSpec(memory_space=pltpu.HBM)


def _gather_phases(x_refs, out_refs, send_sems, recv_sems, local_sems):
    T = len(x_refs)
    x, y, c = _my_place()
    me, sibling = (x, y, c), (x, y, 1 - c)
    chips = [(1 - x, y), (x, 1 - y), (1 - x, 1 - y)]

    def copy(t, k, block, to, src=None):
        px, py, pc = block
        slot = out_refs[t].at[4 * px + 2 * py + pc]
        return pltpu.make_async_remote_copy(
            src_ref=slot if src is None else src, dst_ref=slot, send_sem=send_sems.at[7 * t + k],
            recv_sem=recv_sems.at[7 * t + k], device_id=to, device_id_type=MESH)

    def local(t):
        return pltpu.make_async_copy(x_refs[t], out_refs[t].at[4 * x + 2 * y + c], local_sems.at[t])

    def first(t):
        return [copy(t, 0, me, sibling, src=x_refs[t])] + [
            copy(t, 1 + j, me, (*chip, c), src=x_refs[t]) for j, chip in enumerate(chips)]

    def start():
        for t in range(T):
            local(t).start()
            for cp in first(t):
                cp.start()

    def forward():
        for j, chip in enumerate(chips):
            for t in range(T):
                copy(t, 1 + j, (*chip, c), me).wait_recv()
                copy(t, 4 + j, (*chip, c), sibling).start()

    def finish():
        for t in range(T):
            copy(t, 0, sibling, me).wait_recv()
            for j, chip in enumerate(chips):
                copy(t, 4 + j, (*chip, 1 - c), me).wait_recv()
        for t in range(T):
            for cp in first(t):
                cp.wait_send()
            for j, chip in enumerate(chips):
                copy(t, 4 + j, (*chip, c), sibling).wait_send()
            local(t).wait()

    return start, forward, finish


def _gather_scratch(T):
    return [pltpu.SemaphoreType.DMA((7 * T,)), pltpu.SemaphoreType.DMA((7 * T,)), pltpu.SemaphoreType.DMA((T,))]


def allgather_shards(shards):
    T = len(shards)

    def body(*refs):
        start, forward, finish = _gather_phases(refs[:T], refs[T:2 * T], *refs[2 * T:])
        start()
        forward()
        finish()

    return pl.pallas_call(
        body, name="allgather_shards",
        out_shape=[jax.ShapeDtypeStruct((N_DEV,) + s.shape, s.dtype) for s in shards],
        in_specs=[HBM_SPEC] * T, out_specs=[HBM_SPEC] * T, scratch_shapes=_gather_scratch(T),
    )(*shards)


def exchange_sibling(gs):
    T = len(gs)

    def body(*refs):
        g_refs, out_refs = refs[:T], refs[T:2 * T]
        send_sems, recv_sems = refs[2 * T:]
        x, y, c = _my_place()
        copies = []
        for t in range(T):
            for k in range(4):
                copies.append(pltpu.make_async_remote_copy(
                    src_ref=g_refs[t].at[2 * k + (1 - c)], dst_ref=out_refs[t].at[k],
                    send_sem=send_sems.at[4 * t + k], recv_sem=recv_sems.at[4 * t + k],
                    device_id=(x, y, 1 - c), device_id_type=MESH))
        for cp in copies:
            cp.start()
        for cp in copies:
            cp.wait()

    return pl.pallas_call(
        body, name="exchange_sibling",
        out_shape=[jax.ShapeDtypeStruct((4,) + g.shape[1:], g.dtype) for g in gs],
        in_specs=[HBM_SPEC] * T, out_specs=[HBM_SPEC] * T,
        scratch_shapes=[pltpu.SemaphoreType.DMA((4 * T,)), pltpu.SemaphoreType.DMA((4 * T,))],
    )(*gs)


def _row_tile(rows, row_bytes, budget):
    if rows * row_bytes <= budget:
        return rows
    best = None
    for t in range(16, rows, 16):
        if rows % t == 0 and t * row_bytes <= budget:
            best = t
    return best if best is not None else rows


def add_sibling(g, got, c_idx):
    _, R, L = g.shape
    tr = _row_tile(R, L * 4, 2 << 20)

    def body(c_ref, g_ref, r_ref, o_ref):
        o_ref[...] = (g_ref[...].astype(F32) + r_ref[...].astype(F32)).astype(o_ref.dtype)

    grid_spec = pltpu.PrefetchScalarGridSpec(
        num_scalar_prefetch=1, grid=(4, R // tr),
        in_specs=[pl.BlockSpec((1, tr, L), lambda k, i, c: (2 * k + c[0], i, 0)),
                  pl.BlockSpec((1, tr, L), lambda k, i, c: (k, i, 0))],
        out_specs=pl.BlockSpec((1, tr, L), lambda k, i, c: (k, i, 0)))
    return pl.pallas_call(
        body, name="add_sibling", grid_spec=grid_spec, out_shape=jax.ShapeDtypeStruct((4, R, L), RS_DTYPE),
        compiler_params=_cparams(("arbitrary", "arbitrary")),
    )(c_idx, g, got)


def _exchange_phases(p_refs, out_refs, send_sems, recv_sems, local_sems):
    T = len(p_refs)
    x, y, c = _my_place()
    mychip = 2 * x + y
    chips = [(1 - x, y), (x, 1 - y), (1 - x, 1 - y)]

    def local(t):
        return pltpu.make_async_copy(p_refs[t].at[mychip], out_refs[t].at[mychip], local_sems.at[t])

    def send(t, j):
        px, py = chips[j]
        return pltpu.make_async_remote_copy(
            src_ref=p_refs[t].at[2 * px + py], dst_ref=out_refs[t].at[mychip], send_sem=send_sems.at[3 * t + j],
            recv_sem=recv_sems.at[3 * t + j], device_id=(px, py, c), device_id_type=MESH)

    def landed(t, j):
        px, py = chips[j]
        return pltpu.make_async_remote_copy(
            src_ref=p_refs[t].at[mychip], dst_ref=out_refs[t].at[2 * px + py], send_sem=send_sems.at[3 * t + j],
            recv_sem=recv_sems.at[3 * t + j], device_id=(px, py, c), device_id_type=MESH)

    def start():
        for t in range(T):
            local(t).start()
            for j in range(3):
                send(t, j).start()

    def finish():
        for t in range(T):
            for j in range(3):
                landed(t, j).wait_recv()
        for t in range(T):
            for j in range(3):
                send(t, j).wait_send()
            local(t).wait()

    return start, finish


def _exchange_scratch(T):
    return [pltpu.SemaphoreType.DMA((3 * T,)), pltpu.SemaphoreType.DMA((3 * T,)), pltpu.SemaphoreType.DMA((T,))]


def exchange_chips(ps):
    T = len(ps)

    def body(*refs):
        start, finish = _exchange_phases(refs[:T], refs[T:2 * T], *refs[2 * T:])
        start()
        finish()

    return pl.pallas_call(
        body, name="exchange_chips", out_shape=[jax.ShapeDtypeStruct(p.shape, p.dtype) for p in ps],
        in_specs=[HBM_SPEC] * T, out_specs=[HBM_SPEC] * T, scratch_shapes=_exchange_scratch(T),
    )(*ps)


def allgather_small(s):
    R, W = s.shape

    def body(s_ref, out_ref, send_sems, recv_sems):
        x, y, c = _my_place()
        me = 4 * x + 2 * y + c
        out_ref[me] = s_ref[...]
        copies = []
        for j in range(1, N_DEV):
            peer = (x ^ (j >> 2), y ^ ((j >> 1) & 1), c ^ (j & 1))
            copies.append(pltpu.make_async_remote_copy(
                src_ref=s_ref, dst_ref=out_ref.at[me], send_sem=send_sems.at[j - 1], recv_sem=recv_sems.at[j - 1],
                device_id=peer, device_id_type=MESH))
        for cp in copies:
            cp.start()
        for j in range(1, N_DEV):
            px, py, pc = x ^ (j >> 2), y ^ ((j >> 1) & 1), c ^ (j & 1)
            pltpu.make_async_remote_copy(
                src_ref=s_ref, dst_ref=out_ref.at[4 * px + 2 * py + pc], send_sem=send_sems.at[j - 1],
                recv_sem=recv_sems.at[j - 1], device_id=(px, py, pc), device_id_type=MESH).wait_recv()
        for cp in copies:
            cp.wait_send()

    vm = pl.BlockSpec(memory_space=pltpu.VMEM)
    return pl.pallas_call(
        body, name="allgather_small", out_shape=jax.ShapeDtypeStruct((N_DEV, R, W), s.dtype),
        in_specs=[vm], out_specs=vm,
        scratch_shapes=[pltpu.SemaphoreType.DMA((7,)), pltpu.SemaphoreType.DMA((7,))],
    )(s)


def _adamw(w, g, m, v):
    m = ADAM_B1 * m + (1.0 - ADAM_B1) * g
    v = ADAM_B2 * v + (1.0 - ADAM_B2) * jnp.square(g)
    m_hat = m / (1.0 - ADAM_B1 ** ADAM_STEP)
    v_hat = v / (1.0 - ADAM_B2 ** ADAM_STEP)
    delta = -ADAM_LR * (m_hat / (jnp.sqrt(v_hat) + ADAM_EPS) + ADAM_WD * w)
    return delta, m, v


def sum_adamw(parts, w, m, v):
    P, G, Rp, Lp = parts.shape
    _, R, L = w.shape
    tr = _row_tile(R, P * Lp * parts.dtype.itemsize + 7 * L * 4, 5 << 20)

    def body(p_ref, w_ref, m_ref, v_ref, g_ref, d_ref, nm_ref, nv_ref):
        g = p_ref[0, 0].astype(F32)
        for k in range(1, P):
            g = g + p_ref[k, 0].astype(F32)
        g = g[:, :L]
        d, nm, nv = _adamw(w_ref[0], g, m_ref[0], v_ref[0])
        g_ref[0] = g
        d_ref[0] = d
        nm_ref[0] = nm
        nv_ref[0] = nv

    row = pl.BlockSpec((1, tr, L), lambda gi, i: (gi, i, 0))
    sh = jax.ShapeDtypeStruct((G, R, L), F32)
    return pl.pallas_call(
        body, name="sum_adamw", grid=(G, R // tr),
        in_specs=[pl.BlockSpec((P, 1, tr, Lp), lambda gi, i: (0, gi, i, 0)), row, row, row],
        out_specs=[row] * 4, out_shape=[sh] * 4, compiler_params=_cparams(("arbitrary", "arbitrary")),
    )(parts, w, m, v)


def _unblock(t, axis):
    sh = t.shape[1:]
    t = jnp.moveaxis(t, 0, axis)
    return t.reshape(sh[:axis] + (N_DEV * sh[axis],) + sh[axis + 1:])


def _block(a, axis):
    sh = a.shape
    t = a.reshape(sh[:axis] + (N_DEV, sh[axis] // N_DEV) + sh[axis + 1:])
    return jnp.moveaxis(t, axis, 0)


def _rows3(t):
    return t.reshape((t.shape[0], -1, t.shape[-1]))


def _pad_to(a, axis, n):
    pad = [(0, 0)] * a.ndim
    pad[axis] = (0, n - a.shape[axis])
    return jnp.pad(a, pad)


def _lane_pad(n):
    return -(-n // LANES) * LANES


def _pack(arrs, row_mult):
    flat = jnp.concatenate([a.astype(F32).reshape(-1) for a in arrs])
    n = flat.shape[0]
    rows = -(-n // LANES)
    rows = -(-rows // row_mult) * row_mult
    return jnp.pad(flat, (0, rows * LANES - n)).reshape(rows, LANES)


def _unpack(buf, shapes):
    flat = buf.reshape(-1)
    out, off = [], 0
    for sh in shapes:
        n = math.prod(sh)
        out.append(flat[off:off + n].reshape(sh))
        off += n
    return out


def _ffn_fwd(x, g, wg, wu, wd, tm, gather=()):
    h = rmsnorm_fwd(x, g, tm)
    gate, up, act, *gathered = ffn_up(h, wg, wu, gather)
    x_new = mm_nn(act, wd, res=x, res_scale=MACARON_WEIGHT, name="ffn_down")
    return x_new, (x, h, gate, up, act), gathered


def _ffn_bwd(dx, saved, g, wg, wu, wd, tm, exchange=()):
    x, h, gate, up, act = saved
    dout = scale_cast(dx, MACARON_WEIGHT, tm)
    dgate, dup, *received = ffn_bwd_act(dout, wd, gate, up, exchange)
    dwd = mm_tn(act, dout, name="ffn_dwd")
    dwg = mm_tn(h, dgate, name="ffn_dwg")
    dwu = mm_tn(h, dup, name="ffn_dwu")
    dh = mm_nt([(dgate, wg), (dup, wu)], name="ffn_dh")
    dx_in, dg = rmsnorm_bwd(x, g, dh, dx, tm)
    return dx_in, dg, dwg, dwu, dwd, received


def _bucket_onehot(nb, d):
    idx = _bucket_index(A_BLOCK, nb, d)
    onehot = (idx.reshape(-1)[None, :] == jnp.arange(NUM_BUCKETS)[:, None]).astype(BF16)
    return onehot, idx.shape


def _pair_tables(t, blk, nk):
    t = t.reshape(A_HEADS // 2, 2, blk, nk)
    return jnp.transpose(t, (0, 2, 1, 3)).reshape(A_HEADS // 2, blk, 2 * nk)


def _unpair_tables(t, blk, nk):
    t = t.reshape(A_HEADS // 2, blk, 2, nk)
    return jnp.transpose(t, (0, 2, 1, 3)).reshape(A_HEADS, blk * nk)


def _mixer_a_fwd(x, g, w_in, w_out, rel_bias, B, S, tm):
    h = rmsnorm_fwd(x, g, tm)
    proj = mm_nn(h, w_in, name="a_proj").reshape(B, S, -1)
    biases, outs, lses = [], [], []
    for gi, (window, d) in enumerate(A_PATTERNS):
        nb = (S // d) // A_BLOCK
        onehot, (blk, nk) = _bucket_onehot(nb, d)
        bias2 = _pair_tables(bias_table(onehot, rel_bias[:, gi * A_HEADS:(gi + 1) * A_HEADS]), blk, nk)
        o, lse = attn_fwd(proj, bias2, gi, d, window // d)
        biases.append(bias2)
        outs.append(o.reshape(B * S, -1))
        lses.append(lse.reshape(B * S, -1))
    om = merge_fwd(outs, lses, tm)
    x_new = mm_nn(om, w_out, res=x, name="a_out")
    return x_new, (x, h, proj, biases, outs, lses, om)


def _mixer_a_bwd(dx, saved, g, w_in, w_out, B, S, tm):
    x, h, proj, biases, outs, lses, om = saved
    dxb = scale_cast(dx, 1.0, tm)
    dw_out = mm_tn(om, dxb, name="a_dwout")
    dom = mm_nt([(dxb, w_out)], name="a_dom")
    g6 = merge_bwd(outs, lses, dom, tm)
    dparts, dbias_cols = [], []
    for gi, (window, d) in enumerate(A_PATTERNS):
        do = g6[gi].reshape(B, S, -1)
        dl = g6[3 + gi].reshape(B, S, -1)
        dq, dk, dv, dbias2 = attn_bwd(proj, biases[gi], do, dl, gi, d, window // d)
        dparts += [dq, dk, dv]
        onehot, (blk, nk) = _bucket_onehot((S // d) // A_BLOCK, d)
        dbias_cols.append(bias_bucket_grad(onehot, _unpair_tables(dbias2, blk, nk)))
    d_rel_bias = jnp.concatenate(dbias_cols, axis=1)
    dproj = jnp.concatenate(dparts, axis=2).astype(BF16).reshape(B * S, -1)
    dw_in = mm_tn(h, dproj, name="a_dwin")
    dh = mm_nt([(dproj, w_in)], name="a_dh")
    dx_in, dg = rmsnorm_bwd(x, g, dh, dx, tm)
    return dx_in, dg, dw_in, dw_out, d_rel_bias


def _mixer_b_fwd(x, g, w_in_p, conv_w, prm, nw, w_out, B, S, tm):
    h = rmsnorm_fwd(x, g, tm)
    proj = mm_nn(h, w_in_p, name="b_proj").reshape(B, S, -1)
    c = conv_fwd(proj, conv_w, 3 * B_WIDTH, 512)
    y, states, tinvs = gdn_fwd(c, proj, prm, nw)
    y2 = y.reshape(B * S, B_WIDTH)
    x_new = mm_nn(y2, w_out, res=x, name="b_out")
    return x_new, (x, h, proj, c, states, tinvs, y2)


def _mixer_b_bwd(dx, saved, g, w_in_p, conv_w, prm, nw, w_out, B, S, tm):
    x, h, proj, c, states, tinvs, y2 = saved
    W = B_WIDTH
    dxb = scale_cast(dx, 1.0, tm)
    dw_out = mm_tn(y2, dxb, name="b_dwout")
    dy = mm_nt([(dxb, w_out)], out_dtype=BF16, name="b_dy").reshape(B, S, W)
    dcq, dck, dcv, dz, dab, dprm, dnw = gdn_bwd(c, proj, prm, nw, states, tinvs, dy)
    dxs, dws = [], []
    for t, dc in enumerate((dcq, dck, dcv)):
        dxp, dwp = conv_bwd(proj, conv_w[:, t * W:(t + 1) * W], dc, t * W, 512)
        dxs.append(dxp)
        dws.append(dwp)
    dconv_w = jnp.concatenate(dws, axis=1)
    dproj = jnp.concatenate(dxs + [dz, dab.astype(BF16)], axis=2).reshape(B * S, -1)
    dw_in_p = mm_tn(h, dproj, name="b_dwin")
    dh = mm_nt([(dproj, w_in_p)], name="b_dh")
    dx_in, dg = rmsnorm_bwd(x, g, dh, dx, tm)
    return dx_in, dg, dw_in_p, dconv_w, dprm, dnw, dw_out


SHARD_NAMES = ("ffn_w_gate", "ffn_w_up", "ffn_w_down", "a_w_in", "a_w_out", "b_w_in", "b_w_out", "norm_g", "b_conv_w")
SMALL_NAMES = ("rel_bias", "b_a_log", "b_dt_bias", "b_norm_w", "final_g")


def kernel(x, norm_g, ffn_w_gate, ffn_w_up, ffn_w_down, rel_bias, a_w_in, a_w_out, b_w_in, b_conv_w, b_a_log, b_dt_bias, b_norm_w, b_w_out, final_g, loss_target, m_norm_g, m_ffn_w_gate, m_ffn_w_up, m_ffn_w_down, m_rel_bias, m_a_w_in, m_a_w_out, m_b_w_in, m_b_conv_w, m_b_a_log, m_b_dt_bias, m_b_norm_w, m_b_w_out, m_final_g, v_norm_g, v_ffn_w_gate, v_ffn_w_up, v_ffn_w_down, v_rel_bias, v_a_w_in, v_a_w_out, v_b_w_in, v_b_conv_w, v_b_a_log, v_b_dt_bias, v_b_norm_w, v_b_w_out, v_final_g):
    W = dict(norm_g=norm_g, ffn_w_gate=ffn_w_gate, ffn_w_up=ffn_w_up, ffn_w_down=ffn_w_down, rel_bias=rel_bias,
             a_w_in=a_w_in, a_w_out=a_w_out, b_w_in=b_w_in, b_conv_w=b_conv_w, b_a_log=b_a_log,
             b_dt_bias=b_dt_bias, b_norm_w=b_norm_w, b_w_out=b_w_out, final_g=final_g)
    Mo = dict(norm_g=m_norm_g, ffn_w_gate=m_ffn_w_gate, ffn_w_up=m_ffn_w_up, ffn_w_down=m_ffn_w_down,
              rel_bias=m_rel_bias, a_w_in=m_a_w_in, a_w_out=m_a_w_out, b_w_in=m_b_w_in, b_conv_w=m_b_conv_w,
              b_a_log=m_b_a_log, b_dt_bias=m_b_dt_bias, b_norm_w=m_b_norm_w, b_w_out=m_b_w_out, final_g=m_final_g)
    Vo = dict(norm_g=v_norm_g, ffn_w_gate=v_ffn_w_gate, ffn_w_up=v_ffn_w_up, ffn_w_down=v_ffn_w_down,
              rel_bias=v_rel_bias, a_w_in=v_a_w_in, a_w_out=v_a_w_out, b_w_in=v_b_w_in, b_conv_w=v_b_conv_w,
              b_a_log=v_b_a_log, b_dt_bias=v_b_dt_bias, b_norm_w=v_b_norm_w, b_w_out=v_b_w_out, final_g=v_final_g)

    B, S, D = x.shape
    M = B * S
    tm = _pick(M, (512, 256, 128))
    c_idx = lax.axis_index("c").astype(jnp.int32).reshape(1)
    fp = _lane_pad(ffn_w_gate.shape[3])

    wg_s, wu_s = (_pad_to(t.astype(BF16), 3, fp) for t in (ffn_w_gate, ffn_w_up))
    wd_s = _pad_to(ffn_w_down.astype(BF16), 2, fp)
    a_in_s, a_out_s, b_in_s, b_out_s = (t.astype(BF16) for t in (a_w_in, a_w_out, b_w_in, b_w_out))
    n_b_in = b_w_in.shape[2] * N_DEV

    def ffn_shards(i, k):
        return [wg_s[i, k], wu_s[i, k], wd_s[i, k]]

    def ffn_full(t):
        return _unblock(t[0], 1), _unblock(t[1], 1), _unblock(t[2], 0)

    def mixer_shards(i):
        return [a_in_s[i // 2], a_out_s[i // 2]] if i % 2 == 0 else [b_in_s[i // 2], b_out_s[i // 2]]

    def mixer_full(i, t):
        if i % 2 == 0:
            return _unblock(t[0], 1), _unblock(t[1], 1)
        return _pad_to(_unblock(t[0], 1), 1, n_b_in + B_AB_PAD - 2 * B_HEADS), _unblock(t[1], 0)

    first = allgather_shards(ffn_shards(0, 0) + [norm_g, b_conv_w, mixer_shards(0)[0]])
    w_ffn = ffn_full(first[:3])
    norm_all, conv_all, w_m_in = _unblock(first[3], 2), _unblock(first[4], 2), first[5]
    prm_all = _pad_to(jnp.stack([b_a_log, b_dt_bias], axis=-1)[:, :, None, :], 3, LANES)

    xs = x.reshape(M, D)
    saved, weights = [], []
    for i in range(DEPTH):
        j = i // 2
        w1 = w_ffn
        xs, s1, got = _ffn_fwd(xs, norm_all[i, 0][None], *w1, tm, gather=mixer_shards(i)[1:] + ffn_shards(i, 1))
        wm, w2 = mixer_full(i, [w_m_in, got[0]]), ffn_full(got[1:])
        if i % 2 == 0:
            xs, s2 = _mixer_a_fwd(xs, norm_all[i, 1][None], *wm, rel_bias, B, S, tm)
        else:
            xs, s2 = _mixer_b_fwd(xs, norm_all[i, 1][None], wm[0], conv_all[j], prm_all[j],
                                  b_norm_w[j][None], wm[1], B, S, tm)
        nxt = ffn_shards(i + 1, 0) + mixer_shards(i + 1)[:1] if i + 1 < DEPTH else []
        xs, s3, got = _ffn_fwd(xs, norm_all[i, 2][None], *w2, tm, gather=nxt)
        if nxt:
            w_ffn, w_m_in = ffn_full(got[:3]), got[3]
        saved.append((s1, s2, s3))
        weights.append((w1, wm, w2))

    dx, d_final_g, loss_row = loss_head(xs, final_g[None], loss_target.reshape(M, D), tm)

    d_prm, d_nw = [None] * 2, [None] * 2
    d_rel = jnp.zeros(rel_bias.shape, F32)
    received = [None] * DEPTH
    pending = []
    for i in reversed(range(DEPTH)):
        j = i // 2
        s1, s2, s3 = saved[i]
        w1, wm, w2 = weights[i]
        dx, dn2, dwg2, dwu2, dwd2, got_a = _ffn_bwd(dx, s3, norm_all[i, 2][None], *w2, tm, exchange=pending[:1])
        if i % 2 == 0:
            dx, dn1, d_in, d_out, drb = _mixer_a_bwd(dx, s2, norm_all[i, 1][None], *wm, B, S, tm)
            d_rel = d_rel + drb
            mixer_grads = [(d_in, 1), (d_out, 1)]
        else:
            dx, dn1, dbp, dconv, d_prm[j], d_nw[j], d_out = _mixer_b_bwd(
                dx, s2, norm_all[i, 1][None], wm[0], conv_all[j], prm_all[j], b_norm_w[j][None], wm[1], B, S, tm)
            mixer_grads = [(dbp[:, :n_b_in], 1), (d_out, 0)]
        dx, dn0, dwg1, dwu1, dwd1, got_b = _ffn_bwd(dx, s1, norm_all[i, 0][None], *w1, tm, exchange=pending[1:])
        if pending:
            received[i + 1] = got_a + got_b
        layer_grads = [(jnp.stack([dwg1, dwg2, dwu1, dwu2]), 2), (jnp.stack([dwd1, dwd2]), 1)] + mixer_grads
        layer_grads.append((jnp.concatenate([dn0, dn1, dn2], axis=0), 1))
        if i % 2 == 1:
            layer_grads.append((dconv, 1))
        gblocks = [_rows3(_block(g, ax).astype(RS_DTYPE)) for g, ax in layer_grads]
        got = exchange_sibling(gblocks)
        pending = [add_sibling(g, r, c_idx) for g, r in zip(gblocks, got)]
    received[0] = exchange_chips(pending)
    grad_x = dx.reshape(B, S, D)

    def over_layers(k, layers, split=None):
        ps = [received[i][k] for i in layers]
        if split is not None:
            ps = [p.reshape((4, split[2], -1, p.shape[2]))[:, split[0]:split[1]] for p in ps]
            return jnp.concatenate(ps, axis=1)
        return jnp.stack(ps, axis=1)

    every, a_layers, b_layers = range(DEPTH), range(0, DEPTH, 2), range(1, DEPTH, 2)
    parts = dict(ffn_w_gate=over_layers(0, every, (0, 2, 4)), ffn_w_up=over_layers(0, every, (2, 4, 4)),
                 ffn_w_down=over_layers(1, every, (0, 2, 2)), a_w_in=over_layers(2, a_layers),
                 a_w_out=over_layers(3, a_layers), b_w_in=over_layers(2, b_layers), b_w_out=over_layers(3, b_layers),
                 norm_g=over_layers(4, every), b_conv_w=over_layers(5, b_layers))
    out_g, out_d, out_m, out_v = {}, {}, {}, {}
    for n in SHARD_NAMES:
        w = W[n]
        w3 = w.reshape((-1,) + w.shape[-2:])
        res = sum_adamw(parts[n], w3, Mo[n].reshape(w3.shape), Vo[n].reshape(w3.shape))
        for dst, r in zip((out_g, out_d, out_m, out_v), res):
            dst[n] = r.reshape(w.shape)

    d_alog = jnp.stack([p[:, 0, 0] for p in d_prm])
    d_dtb = jnp.stack([p[:, 0, 1] for p in d_prm])
    small_g = [d_rel, d_alog, d_dtb, jnp.concatenate(d_nw, axis=0), d_final_g[0], loss_row[0, :1]]
    small_shapes = [W[n].shape for n in SMALL_NAMES]
    sg = allgather_small(_pack(small_g, 8))
    pks = lambda d: _pack([d[n] for n in SMALL_NAMES] + [jnp.zeros((1,), F32)], 8)[None]
    res = sum_adamw(sg[:, None], pks(W), pks(Mo), pks(Vo))
    loss = res[0].reshape(-1)[sum(math.prod(s) for s in small_shapes)]
    for dst, r in zip((out_g, out_d, out_m, out_v), res):
        dst.update(zip(SMALL_NAMES, _unpack(r, small_shapes)))

    order = ("norm_g", "ffn_w_gate", "ffn_w_up", "ffn_w_down", "rel_bias", "a_w_in", "a_w_out", "b_w_in",
             "b_conv_w", "b_a_log", "b_dt_bias", "b_norm_w", "b_w_out", "final_g")
    return (loss, grad_x, *[out_g[n] for n in order], *[out_d[n] for n in order],
            *[out_m[n] for n in order], *[out_v[n] for n in order])
```

```python
import functools
import math

import jax
import jax.numpy as jnp
from jax import lax
from jax.experimental import pallas as pl
from jax.experimental.pallas import tpu as pltpu

F32 = jnp.float32
BF16 = jnp.bfloat16

N_DEV = 8
NORM_EPS = 1e-6
MACARON_WEIGHT = 0.5
DEPTH = 4

A_PATTERNS = ((128, 1), (512, 4), (2048, 16))
A_HEADS = 8
A_HEAD_DIM = 64
A_GROUP_WIDTH = A_HEADS * A_HEAD_DIM
A_BLOCK = 128
NEG_INF = -1e30
NUM_BUCKETS = 32
MAX_DISTANCE = 2048

B_HEADS = 8
B_HEAD_DIM = 128
B_WIDTH = B_HEADS * B_HEAD_DIM
B_CONV = 4
B_CHUNK = 128
B_AB_PAD = 128

ADAM_LR = 0.001
ADAM_B1 = 0.9
ADAM_B2 = 0.999
ADAM_EPS = 1e-08
ADAM_WD = 0.01
ADAM_STEP = 10

LANES = 128
VMEM_LIMIT = 48 * 1024 * 1024
VMEM_LIMIT_BIG = 60 * 1024 * 1024
RS_DTYPE = BF16

MESH = pl.DeviceIdType.MESH


def _cparams(sem=None, vmem=VMEM_LIMIT):
    return pltpu.CompilerParams(dimension_semantics=sem, vmem_limit_bytes=vmem)


def _sigmoid(x):
    return 0.5 * (jnp.tanh(0.5 * x) + 1.0)


def _silu(x):
    return x * _sigmoid(x)


def _softplus(x):
    return jnp.maximum(x, 0.0) + jnp.log1p(jnp.exp(-jnp.abs(x)))


def _rms(x, g):
    return x * lax.rsqrt(jnp.mean(x * x, axis=-1, keepdims=True) + NORM_EPS) * g


def _dg(a, b, ca, cb, hp):
    dims = (((ca,), (cb,)), ((), ()))
    if hp:
        return lax.dot_general(a.astype(F32), b.astype(F32), dims, precision=lax.Precision.HIGH,
                               preferred_element_type=F32)
    return lax.dot_general(a.astype(BF16), b.astype(BF16), dims, preferred_element_type=F32)


def _make_dots(hp):
    @jax.custom_vjp
    def nn(a, b):
        return _dg(a, b, 1, 0, hp)

    def nn_f(a, b):
        return nn(a, b), (a, b)

    def nn_b(r, ct):
        a, b = r
        return _dg(ct, b, 1, 1, hp), _dg(a, ct, 0, 0, hp)

    nn.defvjp(nn_f, nn_b)

    @jax.custom_vjp
    def nt(a, b):
        return _dg(a, b, 1, 1, hp)

    def nt_f(a, b):
        return nt(a, b), (a, b)

    def nt_b(r, ct):
        a, b = r
        return _dg(ct, b, 1, 0, hp), _dg(ct, a, 0, 0, hp)

    nt.defvjp(nt_f, nt_b)

    @jax.custom_vjp
    def tn(a, b):
        return _dg(a, b, 0, 0, hp)

    def tn_f(a, b):
        return tn(a, b), (a, b)

    def tn_b(r, ct):
        a, b = r
        return _dg(b, ct, 1, 1, hp), _dg(a, ct, 1, 0, hp)

    tn.defvjp(tn_f, tn_b)
    return nn, nt, tn


bdot_nn, bdot_nt, bdot_tn = _make_dots(False)


def rmsnorm_fwd(x, g, tm):
    M, D = x.shape

    def body(x_ref, g_ref, h_ref):
        h_ref[...] = _rms(x_ref[...], g_ref[...]).astype(h_ref.dtype)

    return pl.pallas_call(
        body, name="rmsnorm_fwd", grid=(M // tm,),
        in_specs=[pl.BlockSpec((tm, D), lambda i: (i, 0)), pl.BlockSpec((1, D), lambda i: (0, 0))],
        out_specs=pl.BlockSpec((tm, D), lambda i: (i, 0)),
        out_shape=jax.ShapeDtypeStruct((M, D), BF16),
        compiler_params=_cparams(("arbitrary",)),
    )(x, g)


def rmsnorm_bwd(x, g, dh, dres, tm, next_scale):
    M, D = x.shape

    def body(x_ref, g_ref, dh_ref, dres_ref, dx_ref, dxb_ref, dg_ref):
        _, vjp = jax.vjp(_rms, x_ref[...], g_ref[...])
        dx, dg = vjp(dh_ref[...])
        dx = dres_ref[...] + dx
        dx_ref[...] = dx
        dxb_ref[...] = (next_scale * dx).astype(dxb_ref.dtype)

        @pl.when(pl.program_id(0) == 0)
        def _():
            dg_ref[...] = jnp.zeros_like(dg_ref)

        dg_ref[...] += dg

    row = pl.BlockSpec((tm, D), lambda i: (i, 0))
    vec = pl.BlockSpec((1, D), lambda i: (0, 0))
    return pl.pallas_call(
        body, name="rmsnorm_bwd", grid=(M // tm,),
        in_specs=[row, vec, row, row], out_specs=[row, row, vec],
        out_shape=[jax.ShapeDtypeStruct((M, D), F32), jax.ShapeDtypeStruct((M, D), BF16),
                   jax.ShapeDtypeStruct((1, D), F32)],
        compiler_params=_cparams(("arbitrary",)),
    )(x, g, dh, dres)


def loss_head(x, g, target, tm, next_scale):
    M, D = x.shape

    def body(x_ref, g_ref, t_ref, dx_ref, dxb_ref, dg_ref, loss_ref):
        t = t_ref[...]

        def f(xv, gv):
            err = _rms(xv, gv) - t
            return 0.5 * jnp.sum(jnp.mean(err * err, axis=-1, keepdims=True), axis=0, keepdims=True)

        val, vjp = jax.vjp(f, x_ref[...], g_ref[...])
        dx, dg = vjp(jnp.ones((1, 1), F32))
        dx_ref[...] = dx
        dxb_ref[...] = (next_scale * dx).astype(dxb_ref.dtype)

        @pl.when(pl.program_id(0) == 0)
        def _():
            dg_ref[...] = jnp.zeros_like(dg_ref)
            loss_ref[...] = jnp.zeros_like(loss_ref)

        dg_ref[...] += dg
        loss_ref[...] += jnp.broadcast_to(val, loss_ref.shape)

    row = pl.BlockSpec((tm, D), lambda i: (i, 0))
    vec = pl.BlockSpec((1, D), lambda i: (0, 0))
    return pl.pallas_call(
        body, name="loss_head", grid=(M // tm,),
        in_specs=[row, vec, row],
        out_specs=[row, row, vec, pl.BlockSpec((1, LANES), lambda i: (0, 0))],
        out_shape=[jax.ShapeDtypeStruct((M, D), F32), jax.ShapeDtypeStruct((M, D), BF16),
                   jax.ShapeDtypeStruct((1, D), F32), jax.ShapeDtypeStruct((1, LANES), F32)],
        compiler_params=_cparams(("arbitrary",)),
    )(x, g, target)


def _pick(n, cands):
    for c in cands:
        if n % c == 0:
            return c
    return n


def mm_nn(a, b, *, out_dtype=F32, res=None, res_scale=1.0, norm_g=None, name="mm_nn"):
    M, K = a.shape
    N = b.shape[1]
    tm = _pick(M, (512, 256, 128))
    tn = _pick(N, (1408, 1536, 1024, 512, 256, 128))
    assert norm_g is None or tn == N

    def body(*refs):
        a_ref, b_ref = refs[:2]
        rest = list(refs[2:])
        r_ref = rest.pop(0) if res is not None else None
        g_ref = rest.pop(0) if norm_g is not None else None
        o_ref = rest.pop(0)
        acc = _dg(a_ref[...], b_ref[...], 1, 0, False)
        if res is not None:
            acc = r_ref[...] + res_scale * acc
        o_ref[...] = acc.astype(o_ref.dtype)
        if norm_g is not None:
            rest[0][...] = _rms(acc, g_ref[...]).astype(BF16)

    tile = pl.BlockSpec((tm, tn), lambda j, i: (i, j))
    in_specs = [pl.BlockSpec((tm, K), lambda j, i: (i, 0)), pl.BlockSpec((K, tn), lambda j, i: (0, j))]
    args = [a, b]
    if res is not None:
        in_specs.append(tile)
        args.append(res)
    if norm_g is not None:
        in_specs.append(pl.BlockSpec((1, tn), lambda j, i: (0, 0)))
        args.append(norm_g)
    out_specs, out_shape = tile, jax.ShapeDtypeStruct((M, N), out_dtype)
    if norm_g is not None:
        out_specs, out_shape = [tile, tile], [out_shape, jax.ShapeDtypeStruct((M, N), BF16)]
    out = pl.pallas_call(
        body, name=name, grid=(N // tn, M // tm), in_specs=in_specs, out_specs=out_specs, out_shape=out_shape,
        compiler_params=_cparams(("arbitrary", "arbitrary")),
    )(*args)
    return out if norm_g is not None else (out, None)


def mm_nt(pairs, *, out_dtype=F32, name="mm_nt"):
    M = pairs[0][0].shape[0]
    N = pairs[0][1].shape[0]
    tm = _pick(M, (512, 256, 128))
    tn = _pick(N, (1024, 512, 256, 128))
    np_ = len(pairs)

    def body(*refs):
        o_ref = refs[-1]
        acc = None
        for p in range(np_):
            t = _dg(refs[2 * p][...], refs[2 * p + 1][...], 1, 1, False)
            acc = t if acc is None else acc + t
        o_ref[...] = acc.astype(o_ref.dtype)

    in_specs, args = [], []
    for a, b in pairs:
        K = a.shape[1]
        in_specs += [pl.BlockSpec((tm, K), lambda j, i: (i, 0)), pl.BlockSpec((tn, K), lambda j, i: (j, 0))]
        args += [a, b]
    return pl.pallas_call(
        body, name=name, grid=(N // tn, M // tm), in_specs=in_specs,
        out_specs=pl.BlockSpec((tm, tn), lambda j, i: (i, j)),
        out_shape=jax.ShapeDtypeStruct((M, N), out_dtype),
        compiler_params=_cparams(("arbitrary", "arbitrary")),
    )(*args)


def mm_tn(a, b, *, name="mm_tn"):
    M, K = a.shape
    N = b.shape[1]
    tm = _pick(M, (512, 256, 128))
    tk = _pick(K, (1408, 1536, 1024, 512, 256, 128))
    tn = _pick(N, (1408, 1536, 1024, 1056, 512, 256, 128))

    def body(a_ref, b_ref, o_ref):
        @pl.when(pl.program_id(2) == 0)
        def _():
            o_ref[...] = jnp.zeros_like(o_ref)

        o_ref[...] += _dg(a_ref[...], b_ref[...], 0, 0, False)

    return pl.pallas_call(
        body, name=name, grid=(K // tk, N // tn, M // tm),
        in_specs=[pl.BlockSpec((tm, tk), lambda k, j, m: (m, k)), pl.BlockSpec((tm, tn), lambda k, j, m: (m, j))],
        out_specs=pl.BlockSpec((tk, tn), lambda k, j, m: (k, j)),
        out_shape=jax.ShapeDtypeStruct((K, N), F32),
        compiler_params=_cparams(("arbitrary", "arbitrary", "arbitrary")),
    )(a, b)


def ffn_up(h, wg, wu, gather=()):
    M, K = h.shape
    N = wg.shape[1]
    tm = _pick(M, (512, 256, 128))
    tn = _pick(N, (1408, 1536, 1024, 512, 256, 128))
    T = len(gather)
    ni = M // tm
    nsteps = (N // tn) * ni
    fwd_step = (3 * nsteps) // 4

    def body(*refs):
        h_ref, wg_ref, wu_ref = refs[:3]
        g_ref, u_ref, a_ref = refs[3 + T:6 + T]
        if T:
            start, forward, finish = _gather_phases(refs[3:3 + T], refs[6 + T:6 + 2 * T], *refs[6 + 2 * T:])
            step = pl.program_id(0) * ni + pl.program_id(1)
            pl.when(step == 0)(start)
        hv = h_ref[...]
        g = _dg(hv, wg_ref[...], 1, 0, False)
        u = _dg(hv, wu_ref[...], 1, 0, False)
        g_ref[...] = g.astype(g_ref.dtype)
        u_ref[...] = u.astype(u_ref.dtype)
        a_ref[...] = (_silu(g) * u).astype(a_ref.dtype)
        if T:
            pl.when(step == fwd_step)(forward)
            pl.when(step == nsteps - 1)(finish)

    wspec = pl.BlockSpec((K, tn), lambda j, i: (0, j))
    ospec = pl.BlockSpec((tm, tn), lambda j, i: (i, j))
    return pl.pallas_call(
        body, name="ffn_up_gather" if T else "ffn_up", grid=(N // tn, ni),
        in_specs=[pl.BlockSpec((tm, K), lambda j, i: (i, 0)), wspec, wspec] + [HBM_SPEC] * T,
        out_specs=[ospec, ospec, ospec] + [HBM_SPEC] * T,
        out_shape=[jax.ShapeDtypeStruct((M, N), BF16)] * 3
        + [jax.ShapeDtypeStruct((N_DEV,) + s.shape, s.dtype) for s in gather],
        scratch_shapes=_gather_scratch(T) if T else [],
        compiler_params=_cparams(("arbitrary", "arbitrary")),
    )(h, wg, wu, *gather)


def ffn_bwd_act(dout, wd, gate, up, exchange=()):
    M, D = dout.shape
    N = wd.shape[0]
    tm = _pick(M, (512, 256, 128))
    tn = _pick(N, (1408, 1536, 1024, 512, 256, 128))
    T = len(exchange)
    ni = M // tm
    nsteps = (N // tn) * ni

    def body(*refs):
        d_ref, w_ref, g_ref, u_ref = refs[:4]
        dg_ref, du_ref = refs[4 + T:6 + T]
        if T:
            start, finish = _exchange_phases(refs[4:4 + T], refs[6 + T:6 + 2 * T], *refs[6 + 2 * T:])
            step = pl.program_id(0) * ni + pl.program_id(1)
            pl.when(step == 0)(start)
        dact = _dg(d_ref[...], w_ref[...], 1, 1, False)
        _, vjp = jax.vjp(lambda g, u: _silu(g) * u, g_ref[...].astype(F32), u_ref[...].astype(F32))
        dg, du = vjp(dact)
        dg_ref[...] = dg.astype(dg_ref.dtype)
        du_ref[...] = du.astype(du_ref.dtype)
        if T:
            pl.when(step == nsteps - 1)(finish)

    ospec = pl.BlockSpec((tm, tn), lambda j, i: (i, j))
    return pl.pallas_call(
        body, name="ffn_bwd_act_exchange" if T else "ffn_bwd_act", grid=(N // tn, ni),
        in_specs=[pl.BlockSpec((tm, D), lambda j, i: (i, 0)), pl.BlockSpec((tn, D), lambda j, i: (j, 0)), ospec, ospec]
        + [HBM_SPEC] * T,
        out_specs=[ospec, ospec] + [HBM_SPEC] * T,
        out_shape=[jax.ShapeDtypeStruct((M, N), BF16), jax.ShapeDtypeStruct((M, N), BF16)]
        + [jax.ShapeDtypeStruct(p.shape, p.dtype) for p in exchange],
        scratch_shapes=_exchange_scratch(T) if T else [],
        compiler_params=_cparams(("arbitrary", "arbitrary")),
    )(dout, wd, gate, up, *exchange)


def _t5_bucket(distance):
    max_exact = NUM_BUCKETS // 2
    n = distance.astype(jnp.float32)
    large = max_exact + (jnp.log(jnp.maximum(n, 1.0) / max_exact)
                         / math.log(MAX_DISTANCE / max_exact) * (NUM_BUCKETS - max_exact))
    large = jnp.minimum(large.astype(jnp.int32), NUM_BUCKETS - 1)
    return jnp.where(distance < max_exact, distance, large)


def _attn_geometry(blk, nb):
    nk = 2 * blk if nb > 1 else blk
    off = blk if nb > 1 else 0
    return nk, off


def _bucket_index(blk, nb, dilation):
    nk, off = _attn_geometry(blk, nb)
    rel = jnp.arange(blk)[:, None] + off - jnp.arange(nk)[None, :]
    return _t5_bucket(jnp.maximum(rel, 0) * dilation)


def _attn_block_pair(q2, kk2, vv2, bias2, valid2, scale):
    nk = kk2.shape[0]
    blk = q2.shape[0]
    s = pdot_nt(q2, kk2) * scale + bias2
    s = jnp.where(valid2, s, NEG_INF)
    ma = jnp.max(s[:, :nk], axis=-1, keepdims=True)
    mb = jnp.max(s[:, nk:], axis=-1, keepdims=True)
    m = lax.stop_gradient(jnp.where(_lane_half(s.shape), ma, mb))
    p = jnp.exp(s - m)
    da = jnp.sum(p[:, :nk], axis=-1, keepdims=True)
    db = jnp.sum(p[:, nk:], axis=-1, keepdims=True)
    first = _lane_half((blk, q2.shape[1]))
    o = pdot_nn(p, vv2) / jnp.where(first, da, db)
    lse = jnp.where(first, lax.stop_gradient(ma) + jnp.log(da), lax.stop_gradient(mb) + jnp.log(db))
    return o, lse


def _attn_valid(blk, nk, off, steps, first):
    q_loc = lax.broadcasted_iota(jnp.int32, (blk, 2 * nk), 0)
    k_loc = lax.broadcasted_iota(jnp.int32, (blk, 2 * nk), 1) % nk
    rel = q_loc + off - k_loc
    valid = (rel >= 0) & (rel <= steps)
    if off:
        valid = valid & ((k_loc >= off) | jnp.logical_not(first))
    return valid


def _sub(ref, r, pp, blk, d):
    if d == 1:
        return ref.at[0, :, pl.ds(pp * LANES, LANES)]
    return ref.at[0, pl.ds(r, blk, stride=d), :]


def _pairs_per_step(d):
    return 1 if d > 1 else A_GROUP_WIDTH // LANES


def _for_units(d, fn):
    for r in range(d):
        for pp in range(_pairs_per_step(d)):
            fn(r, pp)


def attn_fwd(proj, bias2, gi, d, steps):
    B, S, _ = proj.shape
    blk = A_BLOCK
    T = blk * d
    nb = S // T
    nk, off = _attn_geometry(blk, nb)
    scale = A_HEAD_DIM ** -0.5
    PP = _pairs_per_step(d)
    HP = A_GROUP_WIDTH // LANES // PP
    Wb = PP * LANES

    def body(*refs):
        if nb > 1:
            q_ref, kp_ref, kc_ref, vp_ref, vc_ref, b_ref, o_ref, l_ref = refs
        else:
            q_ref, kc_ref, vc_ref, b_ref, o_ref, l_ref = refs
        valid = _attn_valid(blk, nk, off, steps, pl.program_id(2) == 0)

        def one(r, pp):
            sub = lambda ref: _sub(ref, r, pp, blk, d)
            if nb > 1:
                kk = jnp.concatenate([sub(kp_ref)[...], sub(kc_ref)[...]], axis=0)
                vv = jnp.concatenate([sub(vp_ref)[...], sub(vc_ref)[...]], axis=0)
            else:
                kk, vv = sub(kc_ref)[...], sub(vc_ref)[...]
            o, lse = _attn_block_pair(sub(q_ref)[...], kk, vv, b_ref[pp], valid, scale)
            sub(o_ref)[...] = o
            sub(l_ref)[...] = lse

        _for_units(d, one)

    def cur(t):
        return pl.BlockSpec((1, T, Wb), lambda b, h, i: (b, i, (gi * 3 + t) * HP + h))

    def prev(t):
        return pl.BlockSpec((1, T, Wb), lambda b, h, i: (b, jnp.maximum(i - 1, 0), (gi * 3 + t) * HP + h))

    bspec = pl.BlockSpec((PP, blk, 2 * nk), lambda b, h, i: (h, 0, 0))
    out = pl.BlockSpec((1, T, Wb), lambda b, h, i: (b, i, h))
    if nb > 1:
        in_specs, args = [cur(0), prev(1), cur(1), prev(2), cur(2), bspec], [proj] * 5 + [bias2]
    else:
        in_specs, args = [cur(0), cur(1), cur(2), bspec], [proj] * 3 + [bias2]
    osh = jax.ShapeDtypeStruct((B, S, A_GROUP_WIDTH), F32)
    return pl.pallas_call(
        body, name="attn_fwd", grid=(B, HP, nb), in_specs=in_specs, out_specs=[out, out], out_shape=[osh, osh],
        compiler_params=_cparams(("arbitrary", "arbitrary", "arbitrary")),
    )(*args)


def attn_bwd(proj, bias2, do, dlse, gi, d, steps):
    B, S, _ = proj.shape
    blk = A_BLOCK
    T = blk * d
    nb = S // T
    nk, off = _attn_geometry(blk, nb)
    scale = A_HEAD_DIM ** -0.5
    PP = _pairs_per_step(d)
    HP = A_GROUP_WIDTH // LANES // PP
    Wb = PP * LANES
    nsteps = nb + 1 if nb > 1 else 1

    def body(*refs):
        if nb > 1:
            (q_ref, kp_ref, kc_ref, vp_ref, vc_ref, b_ref, do_ref, dl_ref,
             dq_ref, dk_ref, dv_ref, db_ref, ck, cv) = refs
        else:
            q_ref, kc_ref, vc_ref, b_ref, do_ref, dl_ref, dq_ref, dk_ref, dv_ref, db_ref = refs
        b = pl.program_id(0)
        h = pl.program_id(1)
        i = pl.program_id(2)

        @pl.when((b == 0) & (h == 0) & (i == 0))
        def _():
            db_ref[...] = jnp.zeros_like(db_ref)

        if nb > 1:
            @pl.when(i == 0)
            def _():
                ck[...] = jnp.zeros_like(ck)
                cv[...] = jnp.zeros_like(cv)

        def work():
            valid = _attn_valid(blk, nk, off, steps, i == 0)

            def one(r, pp):
                sub = lambda ref: _sub(ref, r, pp, blk, d)
                if nb > 1:
                    kk = jnp.concatenate([sub(kp_ref)[...], sub(kc_ref)[...]], axis=0)
                    vv = jnp.concatenate([sub(vp_ref)[...], sub(vc_ref)[...]], axis=0)
                else:
                    kk, vv = sub(kc_ref)[...], sub(vc_ref)[...]
                _, vjp = jax.vjp(lambda a, c, e, f: _attn_block_pair(a, c, e, f, valid, scale),
                                 sub(q_ref)[...], kk, vv, b_ref[h * PP + pp])
                dq, dkk, dvv, dbias = vjp((sub(do_ref)[...], sub(dl_ref)[...]))
                sub(dq_ref)[...] = dq
                db_ref[h * PP + pp] += dbias
                if nb > 1:
                    idx = (slice(None), pl.ds(pp * LANES, LANES)) if d == 1 else (pl.ds(r, blk, stride=d), slice(None))
                    sub(dk_ref)[...] = ck[idx] + dkk[:blk]
                    sub(dv_ref)[...] = cv[idx] + dvv[:blk]
                    ck[idx] = dkk[blk:]
                    cv[idx] = dvv[blk:]
                else:
                    sub(dk_ref)[...] = dkk
                    sub(dv_ref)[...] = dvv

            _for_units(d, one)

        if nb > 1:
            pl.when(i < nb)(work)

            @pl.when(i == nb)
            def _():
                dk_ref[0] = ck[...]
                dv_ref[0] = cv[...]
        else:
            work()

    last = nb - 1

    def cur(t):
        return pl.BlockSpec((1, T, Wb), lambda b, h, i: (b, jnp.minimum(i, last), (gi * 3 + t) * HP + h))

    def prev(t):
        return pl.BlockSpec((1, T, Wb), lambda b, h, i: (b, jnp.clip(i - 1, 0, last), (gi * 3 + t) * HP + h))

    ocur = pl.BlockSpec((1, T, Wb), lambda b, h, i: (b, jnp.minimum(i, last), h))
    oprev = pl.BlockSpec((1, T, Wb), lambda b, h, i: (b, jnp.clip(i - 1, 0, last), h))
    bspec = pl.BlockSpec((HP * PP, blk, 2 * nk), lambda b, h, i: (0, 0, 0))
    if nb > 1:
        in_specs, args = [cur(0), prev(1), cur(1), prev(2), cur(2), bspec, ocur, ocur], [proj] * 5 + [bias2, do, dlse]
        out_specs = [ocur, oprev, oprev, bspec]
        scratch = [pltpu.VMEM((T, Wb), F32), pltpu.VMEM((T, Wb), F32)]
    else:
        in_specs, args = [cur(0), cur(1), cur(2), bspec, ocur, ocur], [proj] * 3 + [bias2, do, dlse]
        out_specs = [ocur, ocur, ocur, bspec]
        scratch = []
    osh = jax.ShapeDtypeStruct((B, S, A_GROUP_WIDTH), F32)
    return pl.pallas_call(
        body, name="attn_bwd", grid=(B, HP, nsteps), in_specs=in_specs, out_specs=out_specs,
        out_shape=[osh, osh, osh, jax.ShapeDtypeStruct((HP * PP, blk, 2 * nk), F32)],
        scratch_shapes=scratch,
        compiler_params=_cparams(("arbitrary", "arbitrary", "arbitrary")),
    )(*args)


def _merge(o0, o1, o2, l0, l1, l2):
    m = lax.stop_gradient(jnp.maximum(jnp.maximum(l0, l1), l2))
    e0, e1, e2 = jnp.exp(l0 - m), jnp.exp(l1 - m), jnp.exp(l2 - m)
    inv = 1.0 / (e0 + e1 + e2)
    return (e0 * inv) * o0 + (e1 * inv) * o1 + (e2 * inv) * o2


def merge_fwd(os_, ls, tm):
    M, W = os_[0].shape

    def body(o0, o1, o2, l0, l1, l2, out):
        out[...] = _merge(o0[...], o1[...], o2[...], l0[...], l1[...], l2[...]).astype(out.dtype)

    row = pl.BlockSpec((tm, W), lambda i: (i, 0))
    return pl.pallas_call(
        body, name="merge_fwd", grid=(M // tm,), in_specs=[row] * 6, out_specs=row,
        out_shape=jax.ShapeDtypeStruct((M, W), BF16), compiler_params=_cparams(("arbitrary",)),
    )(*os_, *ls)


def merge_bwd(os_, ls, do, tm):
    M, W = os_[0].shape

    def body(o0, o1, o2, l0, l1, l2, d_ref, do0, do1, do2, dl0, dl1, dl2):
        _, vjp = jax.vjp(_merge, o0[...], o1[...], o2[...], l0[...], l1[...], l2[...])
        for r, val in zip((do0, do1, do2, dl0, dl1, dl2), vjp(d_ref[...])):
            r[...] = val

    row = pl.BlockSpec((tm, W), lambda i: (i, 0))
    sh = jax.ShapeDtypeStruct((M, W), F32)
    return pl.pallas_call(
        body, name="merge_bwd", grid=(M // tm,), in_specs=[row] * 7, out_specs=[row] * 6, out_shape=[sh] * 6,
        compiler_params=_cparams(("arbitrary",)),
    )(*os_, *ls, do)


def _split3(x):
    hi = x.astype(BF16)
    r = x - hi.astype(F32)
    mid = r.astype(BF16)
    lo = (r - mid.astype(F32)).astype(BF16)
    return hi, mid, lo


def bias_table(onehot, rb):
    NB, Q = onehot.shape
    H = rb.shape[1]

    def body(oh_ref, rb_ref, o_ref):
        oh = oh_ref[...]
        acc = jnp.zeros((H, Q), F32)
        for part in _split3(rb_ref[...]):
            acc = acc + _dg(part, oh, 0, 0, False)
        o_ref[...] = acc

    full = lambda s: pl.BlockSpec(s, lambda: (0,) * len(s))
    return pl.pallas_call(
        body, name="bias_table", in_specs=[full((NB, Q)), full((NB, H))], out_specs=full((H, Q)),
        out_shape=jax.ShapeDtypeStruct((H, Q), F32), compiler_params=_cparams(),
    )(onehot, rb)


def bias_bucket_grad(onehot, dbias):
    NB, Q = onehot.shape
    H = dbias.shape[0]

    def body(oh_ref, d_ref, o_ref):
        oh = oh_ref[...]
        acc = jnp.zeros((NB, H), F32)
        for part in _split3(d_ref[...]):
            acc = acc + _dg(oh, part, 1, 1, False)
        o_ref[...] = acc

    full = lambda s: pl.BlockSpec(s, lambda: (0,) * len(s))
    return pl.pallas_call(
        body, name="bias_bucket_grad", in_specs=[full((NB, Q)), full((H, Q))], out_specs=full((NB, H)),
        out_shape=jax.ShapeDtypeStruct((NB, H), F32), compiler_params=_cparams(),
    )(onehot, dbias)


def _shift_rows(x, k, S):
    t = lax.broadcasted_iota(jnp.int32, x.shape, 0)
    if k >= 0:
        return jnp.where(t >= k, pltpu.roll(x, k, 0), 0.0)
    return jnp.where(t < S + k, pltpu.roll(x, S + k, 0), 0.0)


def conv_fwd(x, w, ncol, cb):
    B, S, _ = x.shape

    def body(x_ref, w_ref, o_ref):
        xv = x_ref[0]
        wv = w_ref[...]
        acc = xv * wv[B_CONV - 1:B_CONV]
        for k in range(1, B_CONV):
            acc = acc + _shift_rows(xv, k, S) * wv[B_CONV - 1 - k:B_CONV - k]
        o_ref[0] = acc

    blk = pl.BlockSpec((1, S, cb), lambda b, j: (b, 0, j))
    return pl.pallas_call(
        body, name="conv_fwd", grid=(B, ncol // cb),
        in_specs=[blk, pl.BlockSpec((B_CONV, cb), lambda b, j: (0, j))], out_specs=blk,
        out_shape=jax.ShapeDtypeStruct((B, S, ncol), F32),
        compiler_params=_cparams(("arbitrary", "arbitrary")),
    )(x, w)


def conv_bwd(x, w, dc, col0, cb):
    B, S, C = dc.shape
    j0 = col0 // cb

    def body(x_ref, w_ref, dc_ref, dx_ref, dw_ref):
        xv = x_ref[0]
        wv = w_ref[...]
        d = dc_ref[0]
        acc = d * wv[B_CONV - 1:B_CONV]
        rows = [jnp.sum(d * xv, axis=0, keepdims=True)]
        for k in range(1, B_CONV):
            acc = acc + _shift_rows(d, -k, S) * wv[B_CONV - 1 - k:B_CONV - k]
            rows.append(jnp.sum(d * _shift_rows(xv, k, S), axis=0, keepdims=True))
        dx_ref[0] = acc.astype(dx_ref.dtype)

        @pl.when(pl.program_id(1) == 0)
        def _():
            dw_ref[...] = jnp.zeros_like(dw_ref)

        dw_ref[...] += jnp.concatenate(rows[::-1], axis=0)

    return pl.pallas_call(
        body, name="conv_bwd", grid=(C // cb, B),
        in_specs=[pl.BlockSpec((1, S, cb), lambda j, b: (b, 0, j + j0)),
                  pl.BlockSpec((B_CONV, cb), lambda j, b: (0, j)),
                  pl.BlockSpec((1, S, cb), lambda j, b: (b, 0, j))],
        out_specs=[pl.BlockSpec((1, S, cb), lambda j, b: (b, 0, j)), pl.BlockSpec((B_CONV, cb), lambda j, b: (0, j))],
        out_shape=[jax.ShapeDtypeStruct((B, S, C), BF16), jax.ShapeDtypeStruct((B_CONV, C), F32)],
        compiler_params=_cparams(("arbitrary", "arbitrary")),
    )(x, w, dc)


def _lane_half(shape):
    return lax.broadcasted_iota(jnp.int32, shape, len(shape) - 1) < shape[-1] // 2


def _bd(xw):
    first = _lane_half(xw.shape)
    return jnp.concatenate([jnp.where(first, xw, 0.0), jnp.where(first, 0.0, xw)], axis=0)


def _unbd(y):
    r = y.shape[0] // 2
    return jnp.where(_lane_half((r, y.shape[1])), y[:r], y[r:])


def _rs(xw):
    return jnp.concatenate([xw[:, :LANES], xw[:, LANES:]], axis=0)


def _unrs(y):
    r = y.shape[0] // 2
    return jnp.concatenate([y[:r], y[r:]], axis=1)


def _make_pair_dots(hp):
    @jax.custom_vjp
    def nn(xw, pw):
        return _dg(xw, _bd(pw), 1, 0, hp)

    def nn_f(xw, pw):
        return nn(xw, pw), (xw, pw)

    def nn_b(r, ct):
        xw, pw = r
        return _dg(ct, _bd(pw), 1, 1, hp), _unbd(_dg(xw, ct, 0, 0, hp))

    nn.defvjp(nn_f, nn_b)

    @jax.custom_vjp
    def nt(xw, yw):
        return _dg(xw, _bd(yw), 1, 1, hp)

    def nt_f(xw, yw):
        return nt(xw, yw), (xw, yw)

    def nt_b(r, ct):
        xw, yw = r
        return _dg(ct, _bd(yw), 1, 0, hp), _unbd(_dg(ct, xw, 0, 0, hp))

    nt.defvjp(nt_f, nt_b)

    @jax.custom_vjp
    def tn(xw, yw):
        return _dg(_rs(xw), _bd(yw), 0, 0, hp)

    def tn_f(xw, yw):
        return tn(xw, yw), (xw, yw)

    def tn_b(r, ct):
        xw, yw = r
        return _unrs(_dg(_bd(yw), ct, 1, 1, hp)), _unbd(_dg(_rs(xw), ct, 1, 0, hp))

    tn.defvjp(tn_f, tn_b)
    return nn, nt, tn


pdot_nn, pdot_nt, pdot_tn = _make_pair_dots(False)
phdot_nn, phdot_nt, phdot_tn = _make_pair_dots(True)


@jax.custom_vjp
def _tril_cumsum(gw):
    C = gw.shape[0]
    r = lax.broadcasted_iota(jnp.int32, (C, C), 0)
    c = lax.broadcasted_iota(jnp.int32, (C, C), 1)
    return _dg(jnp.where(r >= c, 1.0, 0.0), gw, 1, 0, True)


def _tril_cumsum_f(gw):
    return _tril_cumsum(gw), gw.shape[0]


def _tril_cumsum_b(C, ct):
    r = lax.broadcasted_iota(jnp.int32, (C, C), 0)
    c = lax.broadcasted_iota(jnp.int32, (C, C), 1)
    return (_dg(jnp.where(r <= c, 1.0, 0.0), ct, 1, 0, True),)


_tril_cumsum.defvjp(_tril_cumsum_f, _tril_cumsum_b)


def _pair(fa, fb, shape):
    return jnp.where(_lane_half(shape), fa, fb)


def _half_sums(xw):
    sa = jnp.sum(xw[:, :LANES], axis=-1, keepdims=True)
    sb = jnp.sum(xw[:, LANES:], axis=-1, keepdims=True)
    return _pair(sa, sb, xw.shape)


def _neumann_inverse_pair(low):
    C = low.shape[0]
    r = lax.broadcasted_iota(jnp.int32, low.shape, 0)
    c = lax.broadcasted_iota(jnp.int32, low.shape, 1) % LANES
    x = jnp.where(r == c, 1.0, 0.0) - low
    p = phdot_nn(low, low)
    n = 2
    while n < C:
        x = x + phdot_nn(x, p)
        n *= 2
        if n < C:
            p = phdot_nn(p, p)
    return x


@jax.custom_vjp
def _saved_inverse(low, tinv):
    return tinv


def _saved_inverse_f(low, tinv):
    return tinv, tinv


def _saved_inverse_b(tinv, ct):
    return -phdot_nt(phdot_tn(tinv, ct), tinv), jnp.zeros_like(tinv)


_saved_inverse.defvjp(_saved_inverse_f, _saved_inverse_b)


def _gdn_chunk_pair(cq, ck, cv, z, ab, prm_a, prm_b, nw, state, h0, tinv_saved=None):
    C, W2 = cq.shape
    lane = lax.broadcasted_iota(jnp.int32, ab.shape, 1)
    col = lambda j: jnp.sum(jnp.where(lane == j, ab, 0.0), axis=1, keepdims=True)
    a = _pair(col(h0), col(h0 + 1), (C, W2))
    b = _pair(col(h0 + B_HEADS), col(h0 + 1 + B_HEADS), (C, W2))
    a_log = _pair(prm_a[:, 0:1], prm_b[:, 0:1], (1, W2))
    dtb = _pair(prm_a[:, 1:2], prm_b[:, 1:2], (1, W2))
    nw2 = jnp.concatenate([nw, nw], axis=1)
    q = _silu(cq)
    q = q * lax.rsqrt(_half_sums(q * q) + NORM_EPS) * (B_HEAD_DIM ** -0.5)
    k = _silu(ck)
    k = k * lax.rsqrt(_half_sums(k * k) + NORM_EPS)
    v = _silu(cv)
    beta = _sigmoid(b)
    g = -jnp.exp(a_log) * _softplus(a + dtb)

    r = lax.broadcasted_iota(jnp.int32, (C, W2), 0)
    c = lax.broadcasted_iota(jnp.int32, (C, W2), 1) % LANES
    tril = r >= c
    gc = _tril_cumsum(g)
    gcj = jnp.concatenate([gc[:, :LANES].T, gc[:, LANES:].T], axis=1)
    decay = jnp.where(tril, jnp.exp(jnp.where(tril, gc - gcj, 0.0)), 0.0)
    gl = gc[C - 1:C, :]
    egc = jnp.exp(gc)

    kb = k * beta
    kk = pdot_nt(jnp.concatenate([kb, q], axis=0), k)
    low = jnp.where(r > c, kk[:C] * decay, 0.0)
    attn = kk[C:] * decay
    tinv = _neumann_inverse_pair(low) if tinv_saved is None else _saved_inverse(low, tinv_saved)
    u = pdot_nn(tinv, v * beta)
    w = pdot_nn(tinv, kb * egc)
    ws = pdot_nn(jnp.concatenate([w, q * egc], axis=0), state)
    v_new = u - ws[:C]
    o = ws[C:] + pdot_nn(attn, v_new)
    new_state = state * jnp.exp(gl) + pdot_tn(k * jnp.exp(gl - gc), v_new)
    y = o * lax.rsqrt(_half_sums(o * o) * (1.0 / B_HEAD_DIM) + NORM_EPS) * nw2 * _silu(z)
    if tinv_saved is None:
        return y, new_state, tinv
    return y, new_state


def gdn_fwd(c, proj, prm, nw):
    B, S, _ = c.shape
    H, dh, C = B_HEADS, B_HEAD_DIM, B_CHUNK
    N, HP, W2 = S // C, H // 2, 2 * dh

    def body(cq_ref, ck_ref, cv_ref, z_ref, ab_ref, prm_ref, nw_ref, y_ref, st_ref, ti_ref, state):
        h0 = pl.program_id(1) * 2
        state[...] = jnp.zeros_like(state)
        nwv = nw_ref[...]

        def step(n, carry):
            rows = pl.ds(pl.multiple_of(n * C, C), C)
            s_in = state[...]
            st_ref[0, n] = s_in
            y, s_out, tinv = _gdn_chunk_pair(cq_ref[0, rows, :], ck_ref[0, rows, :], cv_ref[0, rows, :],
                                             z_ref[0, rows, :], ab_ref[0, rows, :], prm_ref[h0], prm_ref[h0 + 1],
                                             nwv, s_in, h0)
            ti_ref[0, n] = tinv
            y_ref[0, rows, :] = y.astype(y_ref.dtype)
            state[...] = s_out
            return carry

        lax.fori_loop(0, N, step, 0)

    def col(off):
        return pl.BlockSpec((1, S, W2), lambda b, h: (b, 0, h + off // 2))

    return pl.pallas_call(
        body, name="gdn_fwd", grid=(B, HP),
        in_specs=[col(0), col(H), col(2 * H), col(3 * H), pl.BlockSpec((1, S, B_AB_PAD), lambda b, h: (b, 0, 4 * H)),
                  pl.BlockSpec((H, 1, LANES), lambda b, h: (0, 0, 0)), pl.BlockSpec((1, dh), lambda b, h: (0, 0))],
        out_specs=[col(0), pl.BlockSpec((1, N, dh, W2), lambda b, h: (b * HP + h, 0, 0, 0)),
                   pl.BlockSpec((1, N, C, W2), lambda b, h: (b * HP + h, 0, 0, 0))],
        out_shape=[jax.ShapeDtypeStruct((B, S, H * dh), BF16), jax.ShapeDtypeStruct((B * HP, N, dh, W2), F32),
                   jax.ShapeDtypeStruct((B * HP, N, C, W2), F32)],
        scratch_shapes=[pltpu.VMEM((dh, W2), F32)],
        compiler_params=_cparams(("arbitrary", "arbitrary")),
    )(c, c, c, proj, proj, prm, nw)


def gdn_bwd(c, proj, prm, nw, states, tinvs, dy):
    B, S, _ = c.shape
    H, dh, C = B_HEADS, B_HEAD_DIM, B_CHUNK
    N, HP, W2 = S // C, H // 2, 2 * dh

    def body(cq_ref, ck_ref, cv_ref, z_ref, ab_ref, prm_ref, nw_ref, st_ref, ti_ref, dy_ref,
             dq_ref, dk_ref, dv_ref, dz_ref, dab_ref, dprm_ref, dnw_ref, dstate):
        b = pl.program_id(0)
        hp = pl.program_id(1)
        h0 = hp * 2
        dstate[...] = jnp.zeros_like(dstate)

        @pl.when((b == 0) & (hp == 0))
        def _():
            dprm_ref[...] = jnp.zeros_like(dprm_ref)
            dnw_ref[...] = jnp.zeros_like(dnw_ref)

        @pl.when(hp == 0)
        def _():
            dab_ref[...] = jnp.zeros_like(dab_ref)

        nwv = nw_ref[...]

        def step(t, carry):
            n = N - 1 - t
            rows = pl.ds(pl.multiple_of(n * C, C), C)
            _, vjp = jax.vjp(functools.partial(_gdn_chunk_pair, h0=h0, tinv_saved=ti_ref[0, n]), cq_ref[0, rows, :],
                             ck_ref[0, rows, :], cv_ref[0, rows, :], z_ref[0, rows, :], ab_ref[0, rows, :],
                             prm_ref[h0], prm_ref[h0 + 1], nwv, st_ref[0, n])
            dcq, dck, dcv, dz, dab, dpa, dpb, dnw, ds = vjp((dy_ref[0, rows, :].astype(F32), dstate[...]))
            dq_ref[0, rows, :] = dcq
            dk_ref[0, rows, :] = dck
            dv_ref[0, rows, :] = dcv
            dz_ref[0, rows, :] = dz.astype(dz_ref.dtype)
            dab_ref[0, rows, :] += dab
            dprm_ref[h0] += dpa
            dprm_ref[h0 + 1] += dpb
            dnw_ref[...] += dnw
            dstate[...] = ds
            return carry

        lax.fori_loop(0, N, step, 0)

    def col(off):
        return pl.BlockSpec((1, S, W2), lambda b, h: (b, 0, h + off // 2))

    abspec = pl.BlockSpec((1, S, B_AB_PAD), lambda b, h: (b, 0, 4 * H))
    ab0 = pl.BlockSpec((1, S, B_AB_PAD), lambda b, h: (b, 0, 0))
    prm_full = pl.BlockSpec((H, 1, LANES), lambda b, h: (0, 0, 0))
    nwspec = pl.BlockSpec((1, dh), lambda b, h: (0, 0))
    wsh = jax.ShapeDtypeStruct((B, S, H * dh), F32)
    return pl.pallas_call(
        body, name="gdn_bwd", grid=(B, HP),
        in_specs=[col(0), col(H), col(2 * H), col(3 * H), abspec, prm_full, nwspec,
                  pl.BlockSpec((1, N, dh, W2), lambda b, h: (b * HP + h, 0, 0, 0)),
                  pl.BlockSpec((1, N, C, W2), lambda b, h: (b * HP + h, 0, 0, 0)), col(0)],
        out_specs=[col(0), col(0), col(0), col(0), ab0, prm_full, nwspec],
        out_shape=[wsh, wsh, wsh, jax.ShapeDtypeStruct((B, S, H * dh), BF16),
                   jax.ShapeDtypeStruct((B, S, B_AB_PAD), F32), jax.ShapeDtypeStruct((H, 1, LANES), F32),
                   jax.ShapeDtypeStruct((1, dh), F32)],
        scratch_shapes=[pltpu.VMEM((dh, W2), F32)],
        compiler_params=_cparams(("arbitrary", "arbitrary"), VMEM_LIMIT_BIG),
    )(c, c, c, proj, proj, prm, nw, states, tinvs, dy)


def _my_place():
    return lax.axis_index("x"), lax.axis_index("y"), lax.axis_index("c")


HBM_SPEC = pl.BlockSpec(memory_space=pltpu.HBM)


def _gather_phases(x_refs, out_refs, send_sems, recv_sems, local_sems):
    T = len(x_refs)
    x, y, c = _my_place()
    me, sibling = (x, y, c), (x, y, 1 - c)
    chips = [(1 - x, y), (x, 1 - y), (1 - x, 1 - y)]

    def copy(t, k, block, to, src=None):
        px, py, pc = block
        slot = out_refs[t].at[4 * px + 2 * py + pc]
        return pltpu.make_async_remote_copy(
            src_ref=slot if src is None else src, dst_ref=slot, send_sem=send_sems.at[7 * t + k],
            recv_sem=recv_sems.at[7 * t + k], device_id=to, device_id_type=MESH)

    def local(t):
        return pltpu.make_async_copy(x_refs[t], out_refs[t].at[4 * x + 2 * y + c], local_sems.at[t])

    def first(t):
        return [copy(t, 0, me, sibling, src=x_refs[t])] + [
            copy(t, 1 + j, me, (*chip, c), src=x_refs[t]) for j, chip in enumerate(chips)]

    def start():
        for t in range(T):
            local(t).start()
            for cp in first(t):
                cp.start()

    def forward():
        for j, chip in enumerate(chips):
            for t in range(T):
                copy(t, 1 + j, (*chip, c), me).wait_recv()
                copy(t, 4 + j, (*chip, c), sibling).start()

    def finish():
        for t in range(T):
            copy(t, 0, sibling, me).wait_recv()
            for j, chip in enumerate(chips):
                copy(t, 4 + j, (*chip, 1 - c), me).wait_recv()
        for t in range(T):
            for cp in first(t):
                cp.wait_send()
            for j, chip in enumerate(chips):
                copy(t, 4 + j, (*chip, c), sibling).wait_send()
            local(t).wait()

    return start, forward, finish


def _gather_scratch(T):
    return [pltpu.SemaphoreType.DMA((7 * T,)), pltpu.SemaphoreType.DMA((7 * T,)), pltpu.SemaphoreType.DMA((T,))]


def allgather_shards(shards):
    T = len(shards)

    def body(*refs):
        start, forward, finish = _gather_phases(refs[:T], refs[T:2 * T], *refs[2 * T:])
        start()
        forward()
        finish()

    return pl.pallas_call(
        body, name="allgather_shards",
        out_shape=[jax.ShapeDtypeStruct((N_DEV,) + s.shape, s.dtype) for s in shards],
        in_specs=[HBM_SPEC] * T, out_specs=[HBM_SPEC] * T, scratch_shapes=_gather_scratch(T),
    )(*shards)


def exchange_sibling(gs):
    T = len(gs)

    def body(*refs):
        g_refs, out_refs = refs[:T], refs[T:2 * T]
        send_sems, recv_sems = refs[2 * T:]
        x, y, c = _my_place()
        copies = []
        for t in range(T):
            for k in range(4):
                copies.append(pltpu.make_async_remote_copy(
                    src_ref=g_refs[t].at[2 * k + (1 - c)], dst_ref=out_refs[t].at[k],
                    send_sem=send_sems.at[4 * t + k], recv_sem=recv_sems.at[4 * t + k],
                    device_id=(x, y, 1 - c), device_id_type=MESH))
        for cp in copies:
            cp.start()
        for cp in copies:
            cp.wait()

    return pl.pallas_call(
        body, name="exchange_sibling",
        out_shape=[jax.ShapeDtypeStruct((4,) + g.shape[1:], g.dtype) for g in gs],
        in_specs=[HBM_SPEC] * T, out_specs=[HBM_SPEC] * T,
        scratch_shapes=[pltpu.SemaphoreType.DMA((4 * T,)), pltpu.SemaphoreType.DMA((4 * T,))],
    )(*gs)


def _row_tile(rows, row_bytes, budget):
    if rows * row_bytes <= budget:
        return rows
    best = None
    for t in range(16, rows, 16):
        if rows % t == 0 and t * row_bytes <= budget:
            best = t
    return best if best is not None else rows


def add_sibling(g, got, c_idx):
    _, R, L = g.shape
    tr = _row_tile(R, L * 4, 2 << 20)

    def body(c_ref, g_ref, r_ref, o_ref):
        o_ref[...] = (g_ref[...].astype(F32) + r_ref[...].astype(F32)).astype(o_ref.dtype)

    grid_spec = pltpu.PrefetchScalarGridSpec(
        num_scalar_prefetch=1, grid=(4, R // tr),
        in_specs=[pl.BlockSpec((1, tr, L), lambda k, i, c: (2 * k + c[0], i, 0)),
                  pl.BlockSpec((1, tr, L), lambda k, i, c: (k, i, 0))],
        out_specs=pl.BlockSpec((1, tr, L), lambda k, i, c: (k, i, 0)))
    return pl.pallas_call(
        body, name="add_sibling", grid_spec=grid_spec, out_shape=jax.ShapeDtypeStruct((4, R, L), RS_DTYPE),
        compiler_params=_cparams(("arbitrary", "arbitrary")),
    )(c_idx, g, got)


def _exchange_phases(p_refs, out_refs, send_sems, recv_sems, local_sems):
    T = len(p_refs)
    x, y, c = _my_place()
    mychip = 2 * x + y
    chips = [(1 - x, y), (x, 1 - y), (1 - x, 1 - y)]

    def local(t):
        return pltpu.make_async_copy(p_refs[t].at[mychip], out_refs[t].at[mychip], local_sems.at[t])

    def send(t, j):
        px, py = chips[j]
        return pltpu.make_async_remote_copy(
            src_ref=p_refs[t].at[2 * px + py], dst_ref=out_refs[t].at[mychip], send_sem=send_sems.at[3 * t + j],
            recv_sem=recv_sems.at[3 * t + j], device_id=(px, py, c), device_id_type=MESH)

    def landed(t, j):
        px, py = chips[j]
        return pltpu.make_async_remote_copy(
            src_ref=p_refs[t].at[mychip], dst_ref=out_refs[t].at[2 * px + py], send_sem=send_sems.at[3 * t + j],
            recv_sem=recv_sems.at[3 * t + j], device_id=(px, py, c), device_id_type=MESH)

    def start():
        for t in range(T):
            local(t).start()
            for j in range(3):
                send(t, j).start()

    def finish():
        for t in range(T):
            for j in range(3):
                landed(t, j).wait_recv()
        for t in range(T):
            for j in range(3):
                send(t, j).wait_send()
            local(t).wait()

    return start, finish


def _exchange_scratch(T):
    return [pltpu.SemaphoreType.DMA((3 * T,)), pltpu.SemaphoreType.DMA((3 * T,)), pltpu.SemaphoreType.DMA((T,))]


def exchange_chips(ps):
    T = len(ps)

    def body(*refs):
        start, finish = _exchange_phases(refs[:T], refs[T:2 * T], *refs[2 * T:])
        start()
        finish()

    return pl.pallas_call(
        body, name="exchange_chips", out_shape=[jax.ShapeDtypeStruct(p.shape, p.dtype) for p in ps],
        in_specs=[HBM_SPEC] * T, out_specs=[HBM_SPEC] * T, scratch_shapes=_exchange_scratch(T),
    )(*ps)


def allgather_small(s):
    R, W = s.shape

    def body(s_ref, out_ref, send_sems, recv_sems):
        x, y, c = _my_place()
        me = 4 * x + 2 * y + c
        out_ref[me] = s_ref[...]
        copies = []
        for j in range(1, N_DEV):
            peer = (x ^ (j >> 2), y ^ ((j >> 1) & 1), c ^ (j & 1))
            copies.append(pltpu.make_async_remote_copy(
                src_ref=s_ref, dst_ref=out_ref.at[me], send_sem=send_sems.at[j - 1], recv_sem=recv_sems.at[j - 1],
                device_id=peer, device_id_type=MESH))
        for cp in copies:
            cp.start()
        for j in range(1, N_DEV):
            px, py, pc = x ^ (j >> 2), y ^ ((j >> 1) & 1), c ^ (j & 1)
            pltpu.make_async_remote_copy(
                src_ref=s_ref, dst_ref=out_ref.at[4 * px + 2 * py + pc], send_sem=send_sems.at[j - 1],
                recv_sem=recv_sems.at[j - 1], device_id=(px, py, pc), device_id_type=MESH).wait_recv()
        for cp in copies:
            cp.wait_send()

    vm = pl.BlockSpec(memory_space=pltpu.VMEM)
    return pl.pallas_call(
        body, name="allgather_small", out_shape=jax.ShapeDtypeStruct((N_DEV, R, W), s.dtype),
        in_specs=[vm], out_specs=vm,
        scratch_shapes=[pltpu.SemaphoreType.DMA((7,)), pltpu.SemaphoreType.DMA((7,))],
    )(s)


def _adamw(w, g, m, v):
    m = ADAM_B1 * m + (1.0 - ADAM_B1) * g
    v = ADAM_B2 * v + (1.0 - ADAM_B2) * jnp.square(g)
    m_hat = m / (1.0 - ADAM_B1 ** ADAM_STEP)
    v_hat = v / (1.0 - ADAM_B2 ** ADAM_STEP)
    delta = -ADAM_LR * (m_hat / (jnp.sqrt(v_hat) + ADAM_EPS) + ADAM_WD * w)
    return delta, m, v


def sum_adamw(parts, w, m, v):
    P, G, Rp, Lp = parts.shape
    _, R, L = w.shape
    tr = _row_tile(R, P * Lp * parts.dtype.itemsize + 7 * L * 4, 5 << 20)

    def body(p_ref, w_ref, m_ref, v_ref, g_ref, d_ref, nm_ref, nv_ref):
        g = p_ref[0, 0].astype(F32)
        for k in range(1, P):
            g = g + p_ref[k, 0].astype(F32)
        g = g[:, :L]
        d, nm, nv = _adamw(w_ref[0], g, m_ref[0], v_ref[0])
        g_ref[0] = g
        d_ref[0] = d
        nm_ref[0] = nm
        nv_ref[0] = nv

    row = pl.BlockSpec((1, tr, L), lambda gi, i: (gi, i, 0))
    sh = jax.ShapeDtypeStruct((G, R, L), F32)
    return pl.pallas_call(
        body, name="sum_adamw", grid=(G, R // tr),
        in_specs=[pl.BlockSpec((P, 1, tr, Lp), lambda gi, i: (0, gi, i, 0)), row, row, row],
        out_specs=[row] * 4, out_shape=[sh] * 4, compiler_params=_cparams(("arbitrary", "arbitrary")),
    )(parts, w, m, v)


def _unblock(t, axis):
    sh = t.shape[1:]
    t = jnp.moveaxis(t, 0, axis)
    return t.reshape(sh[:axis] + (N_DEV * sh[axis],) + sh[axis + 1:])


def _block(a, axis):
    sh = a.shape
    t = a.reshape(sh[:axis] + (N_DEV, sh[axis] // N_DEV) + sh[axis + 1:])
    return jnp.moveaxis(t, axis, 0)


def _rows3(t):
    return t.reshape((t.shape[0], -1, t.shape[-1]))


def _pad_to(a, axis, n):
    pad = [(0, 0)] * a.ndim
    pad[axis] = (0, n - a.shape[axis])
    return jnp.pad(a, pad)


def _lane_pad(n):
    return -(-n // LANES) * LANES


def _pack(arrs, row_mult):
    flat = jnp.concatenate([a.astype(F32).reshape(-1) for a in arrs])
    n = flat.shape[0]
    rows = -(-n // LANES)
    rows = -(-rows // row_mult) * row_mult
    return jnp.pad(flat, (0, rows * LANES - n)).reshape(rows, LANES)


def _unpack(buf, shapes):
    flat = buf.reshape(-1)
    out, off = [], 0
    for sh in shapes:
        n = math.prod(sh)
        out.append(flat[off:off + n].reshape(sh))
        off += n
    return out


def _ffn_fwd(x, h, wg, wu, wd, next_g, gather=()):
    gate, up, act, *gathered = ffn_up(h, wg, wu, gather)
    x_new, h_next = mm_nn(act, wd, res=x, res_scale=MACARON_WEIGHT, norm_g=next_g, name="ffn_down")
    return x_new, h_next, (x, h, gate, up, act), gathered


def _ffn_bwd(dx, dout, saved, g, wg, wu, wd, tm, next_scale, exchange=()):
    x, h, gate, up, act = saved
    dgate, dup, *received = ffn_bwd_act(dout, wd, gate, up, exchange)
    dwd = mm_tn(act, dout, name="ffn_dwd")
    dwg = mm_tn(h, dgate, name="ffn_dwg")
    dwu = mm_tn(h, dup, name="ffn_dwu")
    dh = mm_nt([(dgate, wg), (dup, wu)], name="ffn_dh")
    dx_in, dxb, dg = rmsnorm_bwd(x, g, dh, dx, tm, next_scale)
    return dx_in, dxb, dg, dwg, dwu, dwd, received


def _bucket_onehot(nb, d):
    idx = _bucket_index(A_BLOCK, nb, d)
    onehot = (idx.reshape(-1)[None, :] == jnp.arange(NUM_BUCKETS)[:, None]).astype(BF16)
    return onehot, idx.shape


def _pair_tables(t, blk, nk):
    t = t.reshape(A_HEADS // 2, 2, blk, nk)
    return jnp.transpose(t, (0, 2, 1, 3)).reshape(A_HEADS // 2, blk, 2 * nk)


def _unpair_tables(t, blk, nk):
    t = t.reshape(A_HEADS // 2, blk, 2, nk)
    return jnp.transpose(t, (0, 2, 1, 3)).reshape(A_HEADS, blk * nk)


def _mixer_a_fwd(x, h, w_in, w_out, rel_bias, B, S, tm, next_g):
    proj = mm_nn(h, w_in, name="a_proj")[0].reshape(B, S, -1)
    biases, outs, lses = [], [], []
    for gi, (window, d) in enumerate(A_PATTERNS):
        nb = (S // d) // A_BLOCK
        onehot, (blk, nk) = _bucket_onehot(nb, d)
        bias2 = _pair_tables(bias_table(onehot, rel_bias[:, gi * A_HEADS:(gi + 1) * A_HEADS]), blk, nk)
        o, lse = attn_fwd(proj, bias2, gi, d, window // d)
        biases.append(bias2)
        outs.append(o.reshape(B * S, -1))
        lses.append(lse.reshape(B * S, -1))
    om = merge_fwd(outs, lses, tm)
    x_new, h_next = mm_nn(om, w_out, res=x, norm_g=next_g, name="a_out")
    return x_new, h_next, (x, h, proj, biases, outs, lses, om)


def _mixer_a_bwd(dx, dxb, saved, g, w_in, w_out, B, S, tm, next_scale):
    x, h, proj, biases, outs, lses, om = saved
    dw_out = mm_tn(om, dxb, name="a_dwout")
    dom = mm_nt([(dxb, w_out)], name="a_dom")
    g6 = merge_bwd(outs, lses, dom, tm)
    dparts, dbias_cols = [], []
    for gi, (window, d) in enumerate(A_PATTERNS):
        do = g6[gi].reshape(B, S, -1)
        dl = g6[3 + gi].reshape(B, S, -1)
        dq, dk, dv, dbias2 = attn_bwd(proj, biases[gi], do, dl, gi, d, window // d)
        dparts += [dq, dk, dv]
        onehot, (blk, nk) = _bucket_onehot((S // d) // A_BLOCK, d)
        dbias_cols.append(bias_bucket_grad(onehot, _unpair_tables(dbias2, blk, nk)))
    d_rel_bias = jnp.concatenate(dbias_cols, axis=1)
    dproj = jnp.concatenate(dparts, axis=2).astype(BF16).reshape(B * S, -1)
    dw_in = mm_tn(h, dproj, name="a_dwin")
    dh = mm_nt([(dproj, w_in)], name="a_dh")
    dx_in, dxb_in, dg = rmsnorm_bwd(x, g, dh, dx, tm, next_scale)
    return dx_in, dxb_in, dg, dw_in, dw_out, d_rel_bias


def _mixer_b_fwd(x, h, w_in_p, conv_w, prm, nw, w_out, B, S, tm, next_g):
    proj = mm_nn(h, w_in_p, name="b_proj")[0].reshape(B, S, -1)
    c = conv_fwd(proj, conv_w, 3 * B_WIDTH, 512)
    y, states, tinvs = gdn_fwd(c, proj, prm, nw)
    y2 = y.reshape(B * S, B_WIDTH)
    x_new, h_next = mm_nn(y2, w_out, res=x, norm_g=next_g, name="b_out")
    return x_new, h_next, (x, h, proj, c, states, tinvs, y2)


def _mixer_b_bwd(dx, dxb, saved, g, w_in_p, conv_w, prm, nw, w_out, B, S, tm, next_scale):
    x, h, proj, c, states, tinvs, y2 = saved
    W = B_WIDTH
    dw_out = mm_tn(y2, dxb, name="b_dwout")
    dy = mm_nt([(dxb, w_out)], out_dtype=BF16, name="b_dy").reshape(B, S, W)
    dcq, dck, dcv, dz, dab, dprm, dnw = gdn_bwd(c, proj, prm, nw, states, tinvs, dy)
    dxs, dws = [], []
    for t, dc in enumerate((dcq, dck, dcv)):
        dxp, dwp = conv_bwd(proj, conv_w[:, t * W:(t + 1) * W], dc, t * W, 512)
        dxs.append(dxp)
        dws.append(dwp)
    dconv_w = jnp.concatenate(dws, axis=1)
    dproj = jnp.concatenate(dxs + [dz, dab.astype(BF16)], axis=2).reshape(B * S, -1)
    dw_in_p = mm_tn(h, dproj, name="b_dwin")
    dh = mm_nt([(dproj, w_in_p)], name="b_dh")
    dx_in, dxb_in, dg = rmsnorm_bwd(x, g, dh, dx, tm, next_scale)
    return dx_in, dxb_in, dg, dw_in_p, dconv_w, dprm, dnw, dw_out


SHARD_NAMES = ("ffn_w_gate", "ffn_w_up", "ffn_w_down", "a_w_in", "a_w_out", "b_w_in", "b_w_out", "norm_g", "b_conv_w")
SMALL_NAMES = ("rel_bias", "b_a_log", "b_dt_bias", "b_norm_w", "final_g")


def kernel(x, norm_g, ffn_w_gate, ffn_w_up, ffn_w_down, rel_bias, a_w_in, a_w_out, b_w_in, b_conv_w, b_a_log, b_dt_bias, b_norm_w, b_w_out, final_g, loss_target, m_norm_g, m_ffn_w_gate, m_ffn_w_up, m_ffn_w_down, m_rel_bias, m_a_w_in, m_a_w_out, m_b_w_in, m_b_conv_w, m_b_a_log, m_b_dt_bias, m_b_norm_w, m_b_w_out, m_final_g, v_norm_g, v_ffn_w_gate, v_ffn_w_up, v_ffn_w_down, v_rel_bias, v_a_w_in, v_a_w_out, v_b_w_in, v_b_conv_w, v_b_a_log, v_b_dt_bias, v_b_norm_w, v_b_w_out, v_final_g):
    W = dict(norm_g=norm_g, ffn_w_gate=ffn_w_gate, ffn_w_up=ffn_w_up, ffn_w_down=ffn_w_down, rel_bias=rel_bias,
             a_w_in=a_w_in, a_w_out=a_w_out, b_w_in=b_w_in, b_conv_w=b_conv_w, b_a_log=b_a_log,
             b_dt_bias=b_dt_bias, b_norm_w=b_norm_w, b_w_out=b_w_out, final_g=final_g)
    Mo = dict(norm_g=m_norm_g, ffn_w_gate=m_ffn_w_gate, ffn_w_up=m_ffn_w_up, ffn_w_down=m_ffn_w_down,
              rel_bias=m_rel_bias, a_w_in=m_a_w_in, a_w_out=m_a_w_out, b_w_in=m_b_w_in, b_conv_w=m_b_conv_w,
              b_a_log=m_b_a_log, b_dt_bias=m_b_dt_bias, b_norm_w=m_b_norm_w, b_w_out=m_b_w_out, final_g=m_final_g)
    Vo = dict(norm_g=v_norm_g, ffn_w_gate=v_ffn_w_gate, ffn_w_up=v_ffn_w_up, ffn_w_down=v_ffn_w_down,
              rel_bias=v_rel_bias, a_w_in=v_a_w_in, a_w_out=v_a_w_out, b_w_in=v_b_w_in, b_conv_w=v_b_conv_w,
              b_a_log=v_b_a_log, b_dt_bias=v_b_dt_bias, b_norm_w=v_b_norm_w, b_w_out=v_b_w_out, final_g=v_final_g)

    B, S, D = x.shape
    M = B * S
    tm = _pick(M, (512, 256, 128))
    c_idx = lax.axis_index("c").astype(jnp.int32).reshape(1)
    fp = _lane_pad(ffn_w_gate.shape[3])

    wg_s, wu_s = (_pad_to(t.astype(BF16), 3, fp) for t in (ffn_w_gate, ffn_w_up))
    wd_s = _pad_to(ffn_w_down.astype(BF16), 2, fp)
    a_in_s, a_out_s, b_in_s, b_out_s = (t.astype(BF16) for t in (a_w_in, a_w_out, b_w_in, b_w_out))
    n_b_in = b_w_in.shape[2] * N_DEV

    def ffn_shards(i, k):
        return [wg_s[i, k], wu_s[i, k], wd_s[i, k]]

    def ffn_full(t):
        return _unblock(t[0], 1), _unblock(t[1], 1), _unblock(t[2], 0)

    def mixer_shards(i):
        return [a_in_s[i // 2], a_out_s[i // 2]] if i % 2 == 0 else [b_in_s[i // 2], b_out_s[i // 2]]

    def mixer_full(i, t):
        if i % 2 == 0:
            return _unblock(t[0], 1), _unblock(t[1], 1)
        return _pad_to(_unblock(t[0], 1), 1, n_b_in + B_AB_PAD - 2 * B_HEADS), _unblock(t[1], 0)

    first = allgather_shards(ffn_shards(0, 0) + [norm_g, b_conv_w, mixer_shards(0)[0]])
    w_ffn = ffn_full(first[:3])
    norm_all, conv_all, w_m_in = _unblock(first[3], 2), _unblock(first[4], 2), first[5]
    prm_all = _pad_to(jnp.stack([b_a_log, b_dt_bias], axis=-1)[:, :, None, :], 3, LANES)

    xs = x.reshape(M, D)
    hs = rmsnorm_fwd(xs, norm_all[0, 0][None], tm)
    saved, weights = [], []
    for i in range(DEPTH):
        j = i // 2
        w1 = w_ffn
        xs, hs, s1, got = _ffn_fwd(xs, hs, *w1, norm_all[i, 1][None],
                                   gather=mixer_shards(i)[1:] + ffn_shards(i, 1))
        wm, w2 = mixer_full(i, [w_m_in, got[0]]), ffn_full(got[1:])
        if i % 2 == 0:
            xs, hs, s2 = _mixer_a_fwd(xs, hs, *wm, rel_bias, B, S, tm, norm_all[i, 2][None])
        else:
            xs, hs, s2 = _mixer_b_fwd(xs, hs, wm[0], conv_all[j], prm_all[j], b_norm_w[j][None], wm[1], B, S, tm,
                                      norm_all[i, 2][None])
        nxt = ffn_shards(i + 1, 0) + mixer_shards(i + 1)[:1] if i + 1 < DEPTH else []
        xs, hs, s3, got = _ffn_fwd(xs, hs, *w2, norm_all[i + 1, 0][None] if nxt else None, gather=nxt)
        if nxt:
            w_ffn, w_m_in = ffn_full(got[:3]), got[3]
        saved.append((s1, s2, s3))
        weights.append((w1, wm, w2))

    dx, dxb, d_final_g, loss_row = loss_head(xs, final_g[None], loss_target.reshape(M, D), tm, MACARON_WEIGHT)

    d_prm, d_nw = [None] * 2, [None] * 2
    d_rel = jnp.zeros(rel_bias.shape, F32)
    received = [None] * DEPTH
    pending = []
    for i in reversed(range(DEPTH)):
        j = i // 2
        s1, s2, s3 = saved[i]
        w1, wm, w2 = weights[i]
        dx, dxb, dn2, dwg2, dwu2, dwd2, got_a = _ffn_bwd(dx, dxb, s3, norm_all[i, 2][None], *w2, tm, 1.0,
                                                         exchange=pending[:1])
        if i % 2 == 0:
            dx, dxb, dn1, d_in, d_out, drb = _mixer_a_bwd(dx, dxb, s2, norm_all[i, 1][None], *wm, B, S, tm,
                                                          MACARON_WEIGHT)
            d_rel = d_rel + drb
            mixer_grads = [(d_in, 1), (d_out, 1)]
        else:
            dx, dxb, dn1, dbp, dconv, d_prm[j], d_nw[j], d_out = _mixer_b_bwd(
                dx, dxb, s2, norm_all[i, 1][None], wm[0], conv_all[j], prm_all[j], b_norm_w[j][None], wm[1], B, S, tm,
                MACARON_WEIGHT)
            mixer_grads = [(dbp[:, :n_b_in], 1), (d_out, 0)]
        dx, dxb, dn0, dwg1, dwu1, dwd1, got_b = _ffn_bwd(dx, dxb, s1, norm_all[i, 0][None], *w1, tm, MACARON_WEIGHT,
                                                         exchange=pending[1:])
        if pending:
            received[i + 1] = got_a + got_b
        layer_grads = [(jnp.stack([dwg1, dwg2, dwu1, dwu2]), 2), (jnp.stack([dwd1, dwd2]), 1)] + mixer_grads
        layer_grads.append((jnp.concatenate([dn0, dn1, dn2], axis=0), 1))
        if i % 2 == 1:
            layer_grads.append((dconv, 1))
        gblocks = [_rows3(_block(g, ax).astype(RS_DTYPE)) for g, ax in layer_grads]
        got = exchange_sibling(gblocks)
        pending = [add_sibling(g, r, c_idx) for g, r in zip(gblocks, got)]
    received[0] = exchange_chips(pending)
    grad_x = dx.reshape(B, S, D)

    def over_layers(k, layers, split=None):
        ps = [received[i][k] for i in layers]
        if split is not None:
            ps = [p.reshape((4, split[2], -1, p.shape[2]))[:, split[0]:split[1]] for p in ps]
            return jnp.concatenate(ps, axis=1)
        return jnp.stack(ps, axis=1)

    every, a_layers, b_layers = range(DEPTH), range(0, DEPTH, 2), range(1, DEPTH, 2)
    parts = dict(ffn_w_gate=over_layers(0, every, (0, 2, 4)), ffn_w_up=over_layers(0, every, (2, 4, 4)),
                 ffn_w_down=over_layers(1, every, (0, 2, 2)), a_w_in=over_layers(2, a_layers),
                 a_w_out=over_layers(3, a_layers), b_w_in=over_layers(2, b_layers), b_w_out=over_layers(3, b_layers),
                 norm_g=over_layers(4, every), b_conv_w=over_layers(5, b_layers))
    out_g, out_d, out_m, out_v = {}, {}, {}, {}
    for n in SHARD_NAMES:
        w = W[n]
        w3 = w.reshape((-1,) + w.shape[-2:])
        res = sum_adamw(parts[n], w3, Mo[n].reshape(w3.shape), Vo[n].reshape(w3.shape))
        for dst, r in zip((out_g, out_d, out_m, out_v), res):
            dst[n] = r.reshape(w.shape)

    d_alog = jnp.stack([p[:, 0, 0] for p in d_prm])
    d_dtb = jnp.stack([p[:, 0, 1] for p in d_prm])
    small_g = [d_rel, d_alog, d_dtb, jnp.concatenate(d_nw, axis=0), d_final_g[0], loss_row[0, :1]]
    small_shapes = [W[n].shape for n in SMALL_NAMES]
    sg = allgather_small(_pack(small_g, 8))
    pks = lambda d: _pack([d[n] for n in SMALL_NAMES] + [jnp.zeros((1,), F32)], 8)[None]
    res = sum_adamw(sg[:, None], pks(W), pks(Mo), pks(Vo))
    loss = res[0].reshape(-1)[sum(math.prod(s) for s in small_shapes)]
    for dst, r in zip((out_g, out_d, out_m, out_v), res):
        dst.update(zip(SMALL_NAMES, _unpack(r, small_shapes)))

    order = ("norm_g", "ffn_w_gate", "ffn_w_up", "ffn_w_down", "rel_bias", "a_w_in", "a_w_out", "b_w_in",
             "b_conv_w", "b_a_log", "b_dt_bias", "b_norm_w", "b_w_out", "final_g")
    return (loss, grad_x, *[out_g[n] for n in order], *[out_d[n] for n in order],
            *[out_m[n] for n in order], *[out_v[n] for n in order])
```

```python
import functools
import math

import jax
import jax.numpy as jnp
from jax import lax
from jax.experimental import pallas as pl
from jax.experimental.pallas import tpu as pltpu

F32 = jnp.float32
BF16 = jnp.bfloat16

N_DEV = 8
NORM_EPS = 1e-6
MACARON_WEIGHT = 0.5
DEPTH = 4

A_PATTERNS = ((128, 1), (512, 4), (2048, 16))
A_HEADS = 8
A_HEAD_DIM = 64
A_GROUP_WIDTH = A_HEADS * A_HEAD_DIM
A_BLOCK = 128
NEG_INF = -1e30
NUM_BUCKETS = 32
MAX_DISTANCE = 2048

B_HEADS = 8
B_HEAD_DIM = 128
B_WIDTH = B_HEADS * B_HEAD_DIM
B_CONV = 4
B_CHUNK = 128
B_AB_PAD = 128

ADAM_LR = 0.001
ADAM_B1 = 0.9
ADAM_B2 = 0.999
ADAM_EPS = 1e-08
ADAM_WD = 0.01
ADAM_STEP = 10

LANES = 128
VMEM_LIMIT = 48 * 1024 * 1024
VMEM_LIMIT_BIG = 60 * 1024 * 1024
RS_DTYPE = BF16

MESH = pl.DeviceIdType.MESH


def _cparams(sem=None, vmem=VMEM_LIMIT):
    return pltpu.CompilerParams(dimension_semantics=sem, vmem_limit_bytes=vmem)


def _sigmoid(x):
    return 0.5 * (jnp.tanh(0.5 * x) + 1.0)


def _silu(x):
    return x * _sigmoid(x)


def _softplus(x):
    return jnp.maximum(x, 0.0) + jnp.log1p(jnp.exp(-jnp.abs(x)))


def _rms(x, g):
    return x * lax.rsqrt(jnp.mean(x * x, axis=-1, keepdims=True) + NORM_EPS) * g


def _dg(a, b, ca, cb, hp):
    dims = (((ca,), (cb,)), ((), ()))
    if hp:
        return lax.dot_general(a.astype(F32), b.astype(F32), dims, precision=lax.Precision.HIGH,
                               preferred_element_type=F32)
    return lax.dot_general(a.astype(BF16), b.astype(BF16), dims, preferred_element_type=F32)


def _make_dots(hp):
    @jax.custom_vjp
    def nn(a, b):
        return _dg(a, b, 1, 0, hp)

    def nn_f(a, b):
        return nn(a, b), (a, b)

    def nn_b(r, ct):
        a, b = r
        return _dg(ct, b, 1, 1, hp), _dg(a, ct, 0, 0, hp)

    nn.defvjp(nn_f, nn_b)

    @jax.custom_vjp
    def nt(a, b):
        return _dg(a, b, 1, 1, hp)

    def nt_f(a, b):
        return nt(a, b), (a, b)

    def nt_b(r, ct):
        a, b = r
        return _dg(ct, b, 1, 0, hp), _dg(ct, a, 0, 0, hp)

    nt.defvjp(nt_f, nt_b)

    @jax.custom_vjp
    def tn(a, b):
        return _dg(a, b, 0, 0, hp)

    def tn_f(a, b):
        return tn(a, b), (a, b)

    def tn_b(r, ct):
        a, b = r
        return _dg(b, ct, 1, 1, hp), _dg(a, ct, 1, 0, hp)

    tn.defvjp(tn_f, tn_b)
    return nn, nt, tn


bdot_nn, bdot_nt, bdot_tn = _make_dots(False)


def rmsnorm_fwd(x, g, tm):
    M, D = x.shape

    def body(x_ref, g_ref, h_ref):
        h_ref[...] = _rms(x_ref[...], g_ref[...]).astype(h_ref.dtype)

    return pl.pallas_call(
        body, name="rmsnorm_fwd", grid=(M // tm,),
        in_specs=[pl.BlockSpec((tm, D), lambda i: (i, 0)), pl.BlockSpec((1, D), lambda i: (0, 0))],
        out_specs=pl.BlockSpec((tm, D), lambda i: (i, 0)),
        out_shape=jax.ShapeDtypeStruct((M, D), BF16),
        compiler_params=_cparams(("arbitrary",)),
    )(x, g)


def rmsnorm_bwd(x, g, dh, dres, tm, next_scale):
    M, D = x.shape

    def body(x_ref, g_ref, dh_ref, dres_ref, dx_ref, dxb_ref, dg_ref):
        _, vjp = jax.vjp(_rms, x_ref[...], g_ref[...])
        dx, dg = vjp(dh_ref[...])
        dx = dres_ref[...] + dx
        dx_ref[...] = dx
        dxb_ref[...] = (next_scale * dx).astype(dxb_ref.dtype)

        @pl.when(pl.program_id(0) == 0)
        def _():
            dg_ref[...] = jnp.zeros_like(dg_ref)

        dg_ref[...] += dg

    row = pl.BlockSpec((tm, D), lambda i: (i, 0))
    vec = pl.BlockSpec((1, D), lambda i: (0, 0))
    return pl.pallas_call(
        body, name="rmsnorm_bwd", grid=(M // tm,),
        in_specs=[row, vec, row, row], out_specs=[row, row, vec],
        out_shape=[jax.ShapeDtypeStruct((M, D), F32), jax.ShapeDtypeStruct((M, D), BF16),
                   jax.ShapeDtypeStruct((1, D), F32)],
        compiler_params=_cparams(("arbitrary",)),
    )(x, g, dh, dres)


def loss_head(x, g, target, tm, next_scale):
    M, D = x.shape

    def body(x_ref, g_ref, t_ref, dx_ref, dxb_ref, dg_ref, loss_ref):
        t = t_ref[...]

        def f(xv, gv):
            err = _rms(xv, gv) - t
            return 0.5 * jnp.sum(jnp.mean(err * err, axis=-1, keepdims=True), axis=0, keepdims=True)

        val, vjp = jax.vjp(f, x_ref[...], g_ref[...])
        dx, dg = vjp(jnp.ones((1, 1), F32))
        dx_ref[...] = dx
        dxb_ref[...] = (next_scale * dx).astype(dxb_ref.dtype)

        @pl.when(pl.program_id(0) == 0)
        def _():
            dg_ref[...] = jnp.zeros_like(dg_ref)
            loss_ref[...] = jnp.zeros_like(loss_ref)

        dg_ref[...] += dg
        loss_ref[...] += jnp.broadcast_to(val, loss_ref.shape)

    row = pl.BlockSpec((tm, D), lambda i: (i, 0))
    vec = pl.BlockSpec((1, D), lambda i: (0, 0))
    return pl.pallas_call(
        body, name="loss_head", grid=(M // tm,),
        in_specs=[row, vec, row],
        out_specs=[row, row, vec, pl.BlockSpec((1, LANES), lambda i: (0, 0))],
        out_shape=[jax.ShapeDtypeStruct((M, D), F32), jax.ShapeDtypeStruct((M, D), BF16),
                   jax.ShapeDtypeStruct((1, D), F32), jax.ShapeDtypeStruct((1, LANES), F32)],
        compiler_params=_cparams(("arbitrary",)),
    )(x, g, target)


def _pick(n, cands):
    for c in cands:
        if n % c == 0:
            return c
    return n


def mm_nn(a, b, *, out_dtype=F32, res=None, res_scale=1.0, norm_g=None, name="mm_nn"):
    M, K = a.shape
    N = b.shape[1]
    tm = _pick(M, (512, 256, 128))
    tn = _pick(N, (1408, 1536, 1024, 512, 256, 128))
    assert norm_g is None or tn == N

    def body(*refs):
        a_ref, b_ref = refs[:2]
        rest = list(refs[2:])
        r_ref = rest.pop(0) if res is not None else None
        g_ref = rest.pop(0) if norm_g is not None else None
        o_ref = rest.pop(0)
        acc = _dg(a_ref[...], b_ref[...], 1, 0, False)
        if res is not None:
            acc = r_ref[...] + res_scale * acc
        o_ref[...] = acc.astype(o_ref.dtype)
        if norm_g is not None:
            rest[0][...] = _rms(acc, g_ref[...]).astype(BF16)

    tile = pl.BlockSpec((tm, tn), lambda j, i: (i, j))
    in_specs = [pl.BlockSpec((tm, K), lambda j, i: (i, 0)), pl.BlockSpec((K, tn), lambda j, i: (0, j))]
    args = [a, b]
    if res is not None:
        in_specs.append(tile)
        args.append(res)
    if norm_g is not None:
        in_specs.append(pl.BlockSpec((1, tn), lambda j, i: (0, 0)))
        args.append(norm_g)
    out_specs, out_shape = tile, jax.ShapeDtypeStruct((M, N), out_dtype)
    if norm_g is not None:
        out_specs, out_shape = [tile, tile], [out_shape, jax.ShapeDtypeStruct((M, N), BF16)]
    out = pl.pallas_call(
        body, name=name, grid=(N // tn, M // tm), in_specs=in_specs, out_specs=out_specs, out_shape=out_shape,
        compiler_params=_cparams(("arbitrary", "arbitrary")),
    )(*args)
    return out if norm_g is not None else (out, None)


def mm_nt(pairs, *, out_dtype=F32, name="mm_nt"):
    M = pairs[0][0].shape[0]
    N = pairs[0][1].shape[0]
    tm = _pick(M, (512, 256, 128))
    tn = _pick(N, (1024, 512, 256, 128))
    np_ = len(pairs)

    def body(*refs):
        o_ref = refs[-1]
        acc = None
        for p in range(np_):
            t = _dg(refs[2 * p][...], refs[2 * p + 1][...], 1, 1, False)
            acc = t if acc is None else acc + t
        o_ref[...] = acc.astype(o_ref.dtype)

    in_specs, args = [], []
    for a, b in pairs:
        K = a.shape[1]
        in_specs += [pl.BlockSpec((tm, K), lambda j, i: (i, 0)), pl.BlockSpec((tn, K), lambda j, i: (j, 0))]
        args += [a, b]
    return pl.pallas_call(
        body, name=name, grid=(N // tn, M // tm), in_specs=in_specs,
        out_specs=pl.BlockSpec((tm, tn), lambda j, i: (i, j)),
        out_shape=jax.ShapeDtypeStruct((M, N), out_dtype),
        compiler_params=_cparams(("arbitrary", "arbitrary")),
    )(*args)


def mm_tn(a, b, *, name="mm_tn"):
    M, K = a.shape
    N = b.shape[1]
    tm = _pick(M, (1024, 512, 256, 128))
    tk = _pick(K, (1408, 1536, 1024, 512, 256, 128))
    tn = _pick(N, (1408, 1536, 1024, 1056, 512, 256, 128))

    def body(a_ref, b_ref, o_ref):
        @pl.when(pl.program_id(2) == 0)
        def _():
            o_ref[...] = jnp.zeros_like(o_ref)

        o_ref[...] += _dg(a_ref[...], b_ref[...], 0, 0, False)

    return pl.pallas_call(
        body, name=name, grid=(K // tk, N // tn, M // tm),
        in_specs=[pl.BlockSpec((tm, tk), lambda k, j, m: (m, k)), pl.BlockSpec((tm, tn), lambda k, j, m: (m, j))],
        out_specs=pl.BlockSpec((tk, tn), lambda k, j, m: (k, j)),
        out_shape=jax.ShapeDtypeStruct((K, N), F32),
        compiler_params=_cparams(("arbitrary", "arbitrary", "arbitrary")),
    )(a, b)


def ffn_up(h, wg, wu, gather=()):
    M, K = h.shape
    N = wg.shape[1]
    tm = _pick(M, (512, 256, 128))
    tn = _pick(N, (1408, 1536, 1024, 512, 256, 128))
    T = len(gather)
    ni = M // tm
    nsteps = (N // tn) * ni
    fwd_step = (3 * nsteps) // 4

    def body(*refs):
        h_ref, wg_ref, wu_ref = refs[:3]
        g_ref, u_ref, a_ref = refs[3 + T:6 + T]
        if T:
            start, forward, finish = _gather_phases(refs[3:3 + T], refs[6 + T:6 + 2 * T], *refs[6 + 2 * T:])
            step = pl.program_id(0) * ni + pl.program_id(1)
            pl.when(step == 0)(start)
        hv = h_ref[...]
        g = _dg(hv, wg_ref[...], 1, 0, False)
        u = _dg(hv, wu_ref[...], 1, 0, False)
        g_ref[...] = g.astype(g_ref.dtype)
        u_ref[...] = u.astype(u_ref.dtype)
        a_ref[...] = (_silu(g) * u).astype(a_ref.dtype)
        if T:
            pl.when(step == fwd_step)(forward)
            pl.when(step == nsteps - 1)(finish)

    wspec = pl.BlockSpec((K, tn), lambda j, i: (0, j))
    ospec = pl.BlockSpec((tm, tn), lambda j, i: (i, j))
    return pl.pallas_call(
        body, name="ffn_up_gather" if T else "ffn_up", grid=(N // tn, ni),
        in_specs=[pl.BlockSpec((tm, K), lambda j, i: (i, 0)), wspec, wspec] + [HBM_SPEC] * T,
        out_specs=[ospec, ospec, ospec] + [HBM_SPEC] * T,
        out_shape=[jax.ShapeDtypeStruct((M, N), BF16)] * 3
        + [jax.ShapeDtypeStruct((N_DEV,) + s.shape, s.dtype) for s in gather],
        scratch_shapes=_gather_scratch(T) if T else [],
        compiler_params=_cparams(("arbitrary", "arbitrary")),
    )(h, wg, wu, *gather)


def ffn_bwd_act(dout, wd, gate, up, exchange=()):
    M, D = dout.shape
    N = wd.shape[0]
    tm = _pick(M, (512, 256, 128))
    tn = _pick(N, (1408, 1536, 1024, 512, 256, 128))
    T = len(exchange)
    ni = M // tm
    nsteps = (N // tn) * ni

    def body(*refs):
        d_ref, w_ref, g_ref, u_ref = refs[:4]
        dg_ref, du_ref = refs[4 + T:6 + T]
        if T:
            start, finish = _exchange_phases(refs[4:4 + T], refs[6 + T:6 + 2 * T], *refs[6 + 2 * T:])
            step = pl.program_id(0) * ni + pl.program_id(1)
            pl.when(step == 0)(start)
        dact = _dg(d_ref[...], w_ref[...], 1, 1, False)
        _, vjp = jax.vjp(lambda g, u: _silu(g) * u, g_ref[...].astype(F32), u_ref[...].astype(F32))
        dg, du = vjp(dact)
        dg_ref[...] = dg.astype(dg_ref.dtype)
        du_ref[...] = du.astype(du_ref.dtype)
        if T:
            pl.when(step == nsteps - 1)(finish)

    ospec = pl.BlockSpec((tm, tn), lambda j, i: (i, j))
    return pl.pallas_call(
        body, name="ffn_bwd_act_exchange" if T else "ffn_bwd_act", grid=(N // tn, ni),
        in_specs=[pl.BlockSpec((tm, D), lambda j, i: (i, 0)), pl.BlockSpec((tn, D), lambda j, i: (j, 0)), ospec, ospec]
        + [HBM_SPEC] * T,
        out_specs=[ospec, ospec] + [HBM_SPEC] * T,
        out_shape=[jax.ShapeDtypeStruct((M, N), BF16), jax.ShapeDtypeStruct((M, N), BF16)]
        + [jax.ShapeDtypeStruct(p.shape, p.dtype) for p in exchange],
        scratch_shapes=_exchange_scratch(T) if T else [],
        compiler_params=_cparams(("arbitrary", "arbitrary")),
    )(dout, wd, gate, up, *exchange)


def _t5_bucket(distance):
    max_exact = NUM_BUCKETS // 2
    n = distance.astype(jnp.float32)
    large = max_exact + (jnp.log(jnp.maximum(n, 1.0) / max_exact)
                         / math.log(MAX_DISTANCE / max_exact) * (NUM_BUCKETS - max_exact))
    large = jnp.minimum(large.astype(jnp.int32), NUM_BUCKETS - 1)
    return jnp.where(distance < max_exact, distance, large)


def _attn_geometry(blk, nb):
    nk = 2 * blk if nb > 1 else blk
    off = blk if nb > 1 else 0
    return nk, off


def _bucket_index(blk, nb, dilation):
    nk, off = _attn_geometry(blk, nb)
    rel = jnp.arange(blk)[:, None] + off - jnp.arange(nk)[None, :]
    return _t5_bucket(jnp.maximum(rel, 0) * dilation)


def _attn_block_pair(q2, kk2, vv2, bias2, valid2, scale):
    nk = kk2.shape[0]
    blk = q2.shape[0]
    s = pdot_nt(q2, kk2) * scale + bias2
    s = jnp.where(valid2, s, NEG_INF)
    ma = jnp.max(s[:, :nk], axis=-1, keepdims=True)
    mb = jnp.max(s[:, nk:], axis=-1, keepdims=True)
    m = lax.stop_gradient(jnp.where(_lane_half(s.shape), ma, mb))
    p = jnp.exp(s - m)
    da = jnp.sum(p[:, :nk], axis=-1, keepdims=True)
    db = jnp.sum(p[:, nk:], axis=-1, keepdims=True)
    first = _lane_half((blk, q2.shape[1]))
    o = pdot_nn(p, vv2) / jnp.where(first, da, db)
    lse = jnp.where(first, lax.stop_gradient(ma) + jnp.log(da), lax.stop_gradient(mb) + jnp.log(db))
    return o, lse


def _attn_valid(blk, nk, off, steps, first):
    q_loc = lax.broadcasted_iota(jnp.int32, (blk, 2 * nk), 0)
    k_loc = lax.broadcasted_iota(jnp.int32, (blk, 2 * nk), 1) % nk
    rel = q_loc + off - k_loc
    valid = (rel >= 0) & (rel <= steps)
    if off:
        valid = valid & ((k_loc >= off) | jnp.logical_not(first))
    return valid


def _sub(ref, r, pp, blk, d):
    if d == 1:
        return ref.at[0, :, pl.ds(pp * LANES, LANES)]
    return ref.at[0, pl.ds(r, blk, stride=d), :]


def _pairs_per_step(d):
    return 1 if d > 1 else A_GROUP_WIDTH // LANES


def _for_units(d, fn):
    for r in range(d):
        for pp in range(_pairs_per_step(d)):
            fn(r, pp)


def attn_fwd(proj, bias2, gi, d, steps):
    B, S, _ = proj.shape
    blk = A_BLOCK
    T = blk * d
    nb = S // T
    nk, off = _attn_geometry(blk, nb)
    scale = A_HEAD_DIM ** -0.5
    PP = _pairs_per_step(d)
    HP = A_GROUP_WIDTH // LANES // PP
    Wb = PP * LANES

    def body(*refs):
        if nb > 1:
            q_ref, kp_ref, kc_ref, vp_ref, vc_ref, b_ref, o_ref, l_ref = refs
        else:
            q_ref, kc_ref, vc_ref, b_ref, o_ref, l_ref = refs
        valid = _attn_valid(blk, nk, off, steps, pl.program_id(2) == 0)

        def one(r, pp):
            sub = lambda ref: _sub(ref, r, pp, blk, d)
            if nb > 1:
                kk = jnp.concatenate([sub(kp_ref)[...], sub(kc_ref)[...]], axis=0)
                vv = jnp.concatenate([sub(vp_ref)[...], sub(vc_ref)[...]], axis=0)
            else:
                kk, vv = sub(kc_ref)[...], sub(vc_ref)[...]
            o, lse = _attn_block_pair(sub(q_ref)[...], kk, vv, b_ref[pp], valid, scale)
            sub(o_ref)[...] = o
            sub(l_ref)[...] = lse

        _for_units(d, one)

    def cur(t):
        return pl.BlockSpec((1, T, Wb), lambda b, h, i: (b, i, (gi * 3 + t) * HP + h))

    def prev(t):
        return pl.BlockSpec((1, T, Wb), lambda b, h, i: (b, jnp.maximum(i - 1, 0), (gi * 3 + t) * HP + h))

    bspec = pl.BlockSpec((PP, blk, 2 * nk), lambda b, h, i: (h, 0, 0))
    out = pl.BlockSpec((1, T, Wb), lambda b, h, i: (b, i, h))
    if nb > 1:
        in_specs, args = [cur(0), prev(1), cur(1), prev(2), cur(2), bspec], [proj] * 5 + [bias2]
    else:
        in_specs, args = [cur(0), cur(1), cur(2), bspec], [proj] * 3 + [bias2]
    osh = jax.ShapeDtypeStruct((B, S, A_GROUP_WIDTH), F32)
    return pl.pallas_call(
        body, name="attn_fwd", grid=(B, HP, nb), in_specs=in_specs, out_specs=[out, out], out_shape=[osh, osh],
        compiler_params=_cparams(("arbitrary", "arbitrary", "arbitrary")),
    )(*args)


def attn_bwd(proj, bias2, do, dlse, gi, d, steps):
    B, S, _ = proj.shape
    blk = A_BLOCK
    T = blk * d
    nb = S // T
    nk, off = _attn_geometry(blk, nb)
    scale = A_HEAD_DIM ** -0.5
    PP = _pairs_per_step(d)
    HP = A_GROUP_WIDTH // LANES // PP
    Wb = PP * LANES
    nsteps = nb + 1 if nb > 1 else 1

    def body(*refs):
        if nb > 1:
            (q_ref, kp_ref, kc_ref, vp_ref, vc_ref, b_ref, do_ref, dl_ref,
             dq_ref, dk_ref, dv_ref, db_ref, ck, cv) = refs
        else:
            q_ref, kc_ref, vc_ref, b_ref, do_ref, dl_ref, dq_ref, dk_ref, dv_ref, db_ref = refs
        b = pl.program_id(0)
        h = pl.program_id(1)
        i = pl.program_id(2)

        @pl.when((b == 0) & (h == 0) & (i == 0))
        def _():
            db_ref[...] = jnp.zeros_like(db_ref)

        if nb > 1:
            @pl.when(i == 0)
            def _():
                ck[...] = jnp.zeros_like(ck)
                cv[...] = jnp.zeros_like(cv)

        def work():
            valid = _attn_valid(blk, nk, off, steps, i == 0)

            def one(r, pp):
                sub = lambda ref: _sub(ref, r, pp, blk, d)
                if nb > 1:
                    kk = jnp.concatenate([sub(kp_ref)[...], sub(kc_ref)[...]], axis=0)
                    vv = jnp.concatenate([sub(vp_ref)[...], sub(vc_ref)[...]], axis=0)
                else:
                    kk, vv = sub(kc_ref)[...], sub(vc_ref)[...]
                _, vjp = jax.vjp(lambda a, c, e, f: _attn_block_pair(a, c, e, f, valid, scale),
                                 sub(q_ref)[...], kk, vv, b_ref[h * PP + pp])
                dq, dkk, dvv, dbias = vjp((sub(do_ref)[...], sub(dl_ref)[...]))
                sub(dq_ref)[...] = dq
                db_ref[h * PP + pp] += dbias
                if nb > 1:
                    idx = (slice(None), pl.ds(pp * LANES, LANES)) if d == 1 else (pl.ds(r, blk, stride=d), slice(None))
                    sub(dk_ref)[...] = ck[idx] + dkk[:blk]
                    sub(dv_ref)[...] = cv[idx] + dvv[:blk]
                    ck[idx] = dkk[blk:]
                    cv[idx] = dvv[blk:]
                else:
                    sub(dk_ref)[...] = dkk
                    sub(dv_ref)[...] = dvv

            _for_units(d, one)

        if nb > 1:
            pl.when(i < nb)(work)

            @pl.when(i == nb)
            def _():
                dk_ref[0] = ck[...]
                dv_ref[0] = cv[...]
        else:
            work()

    last = nb - 1

    def cur(t):
        return pl.BlockSpec((1, T, Wb), lambda b, h, i: (b, jnp.minimum(i, last), (gi * 3 + t) * HP + h))

    def prev(t):
        return pl.BlockSpec((1, T, Wb), lambda b, h, i: (b, jnp.clip(i - 1, 0, last), (gi * 3 + t) * HP + h))

    ocur = pl.BlockSpec((1, T, Wb), lambda b, h, i: (b, jnp.minimum(i, last), h))
    oprev = pl.BlockSpec((1, T, Wb), lambda b, h, i: (b, jnp.clip(i - 1, 0, last), h))
    bspec = pl.BlockSpec((HP * PP, blk, 2 * nk), lambda b, h, i: (0, 0, 0))
    if nb > 1:
        in_specs, args = [cur(0), prev(1), cur(1), prev(2), cur(2), bspec, ocur, ocur], [proj] * 5 + [bias2, do, dlse]
        out_specs = [ocur, oprev, oprev, bspec]
        scratch = [pltpu.VMEM((T, Wb), F32), pltpu.VMEM((T, Wb), F32)]
    else:
        in_specs, args = [cur(0), cur(1), cur(2), bspec, ocur, ocur], [proj] * 3 + [bias2, do, dlse]
        out_specs = [ocur, ocur, ocur, bspec]
        scratch = []
    osh = jax.ShapeDtypeStruct((B, S, A_GROUP_WIDTH), F32)
    return pl.pallas_call(
        body, name="attn_bwd", grid=(B, HP, nsteps), in_specs=in_specs, out_specs=out_specs,
        out_shape=[osh, osh, osh, jax.ShapeDtypeStruct((HP * PP, blk, 2 * nk), F32)],
        scratch_shapes=scratch,
        compiler_params=_cparams(("arbitrary", "arbitrary", "arbitrary")),
    )(*args)


def _merge(o0, o1, o2, l0, l1, l2):
    m = lax.stop_gradient(jnp.maximum(jnp.maximum(l0, l1), l2))
    e0, e1, e2 = jnp.exp(l0 - m), jnp.exp(l1 - m), jnp.exp(l2 - m)
    inv = 1.0 / (e0 + e1 + e2)
    return (e0 * inv) * o0 + (e1 * inv) * o1 + (e2 * inv) * o2


def merge_fwd(os_, ls, tm):
    M, W = os_[0].shape

    def body(o0, o1, o2, l0, l1, l2, out):
        out[...] = _merge(o0[...], o1[...], o2[...], l0[...], l1[...], l2[...]).astype(out.dtype)

    row = pl.BlockSpec((tm, W), lambda i: (i, 0))
    return pl.pallas_call(
        body, name="merge_fwd", grid=(M // tm,), in_specs=[row] * 6, out_specs=row,
        out_shape=jax.ShapeDtypeStruct((M, W), BF16), compiler_params=_cparams(("arbitrary",)),
    )(*os_, *ls)


def merge_bwd(os_, ls, do, tm):
    M, W = os_[0].shape

    def body(o0, o1, o2, l0, l1, l2, d_ref, do0, do1, do2, dl0, dl1, dl2):
        _, vjp = jax.vjp(_merge, o0[...], o1[...], o2[...], l0[...], l1[...], l2[...])
        for r, val in zip((do0, do1, do2, dl0, dl1, dl2), vjp(d_ref[...])):
            r[...] = val

    row = pl.BlockSpec((tm, W), lambda i: (i, 0))
    sh = jax.ShapeDtypeStruct((M, W), F32)
    return pl.pallas_call(
        body, name="merge_bwd", grid=(M // tm,), in_specs=[row] * 7, out_specs=[row] * 6, out_shape=[sh] * 6,
        compiler_params=_cparams(("arbitrary",)),
    )(*os_, *ls, do)


def _split3(x):
    hi = x.astype(BF16)
    r = x - hi.astype(F32)
    mid = r.astype(BF16)
    lo = (r - mid.astype(F32)).astype(BF16)
    return hi, mid, lo


def bias_table(onehot, rb):
    NB, Q = onehot.shape
    H = rb.shape[1]

    def body(oh_ref, rb_ref, o_ref):
        oh = oh_ref[...]
        acc = jnp.zeros((H, Q), F32)
        for part in _split3(rb_ref[...]):
            acc = acc + _dg(part, oh, 0, 0, False)
        o_ref[...] = acc

    full = lambda s: pl.BlockSpec(s, lambda: (0,) * len(s))
    return pl.pallas_call(
        body, name="bias_table", in_specs=[full((NB, Q)), full((NB, H))], out_specs=full((H, Q)),
        out_shape=jax.ShapeDtypeStruct((H, Q), F32), compiler_params=_cparams(),
    )(onehot, rb)


def bias_bucket_grad(onehot, dbias):
    NB, Q = onehot.shape
    H = dbias.shape[0]

    def body(oh_ref, d_ref, o_ref):
        oh = oh_ref[...]
        acc = jnp.zeros((NB, H), F32)
        for part in _split3(d_ref[...]):
            acc = acc + _dg(oh, part, 1, 1, False)
        o_ref[...] = acc

    full = lambda s: pl.BlockSpec(s, lambda: (0,) * len(s))
    return pl.pallas_call(
        body, name="bias_bucket_grad", in_specs=[full((NB, Q)), full((H, Q))], out_specs=full((NB, H)),
        out_shape=jax.ShapeDtypeStruct((NB, H), F32), compiler_params=_cparams(),
    )(onehot, dbias)


def _shift_rows(x, k, S):
    t = lax.broadcasted_iota(jnp.int32, x.shape, 0)
    if k >= 0:
        return jnp.where(t >= k, pltpu.roll(x, k, 0), 0.0)
    return jnp.where(t < S + k, pltpu.roll(x, S + k, 0), 0.0)


def conv_fwd(x, w, ncol, cb):
    B, S, _ = x.shape

    def body(x_ref, w_ref, o_ref):
        xv = x_ref[0]
        wv = w_ref[...]
        acc = xv * wv[B_CONV - 1:B_CONV]
        for k in range(1, B_CONV):
            acc = acc + _shift_rows(xv, k, S) * wv[B_CONV - 1 - k:B_CONV - k]
        o_ref[0] = acc

    blk = pl.BlockSpec((1, S, cb), lambda b, j: (b, 0, j))
    return pl.pallas_call(
        body, name="conv_fwd", grid=(B, ncol // cb),
        in_specs=[blk, pl.BlockSpec((B_CONV, cb), lambda b, j: (0, j))], out_specs=blk,
        out_shape=jax.ShapeDtypeStruct((B, S, ncol), F32),
        compiler_params=_cparams(("arbitrary", "arbitrary")),
    )(x, w)


def conv_bwd(x, w, dc, col0, cb):
    B, S, C = dc.shape
    j0 = col0 // cb

    def body(x_ref, w_ref, dc_ref, dx_ref, dw_ref):
        xv = x_ref[0]
        wv = w_ref[...]
        d = dc_ref[0]
        acc = d * wv[B_CONV - 1:B_CONV]
        rows = [jnp.sum(d * xv, axis=0, keepdims=True)]
        for k in range(1, B_CONV):
            acc = acc + _shift_rows(d, -k, S) * wv[B_CONV - 1 - k:B_CONV - k]
            rows.append(jnp.sum(d * _shift_rows(xv, k, S), axis=0, keepdims=True))
        dx_ref[0] = acc.astype(dx_ref.dtype)

        @pl.when(pl.program_id(1) == 0)
        def _():
            dw_ref[...] = jnp.zeros_like(dw_ref)

        dw_ref[...] += jnp.concatenate(rows[::-1], axis=0)

    return pl.pallas_call(
        body, name="conv_bwd", grid=(C // cb, B),
        in_specs=[pl.BlockSpec((1, S, cb), lambda j, b: (b, 0, j + j0)),
                  pl.BlockSpec((B_CONV, cb), lambda j, b: (0, j)),
                  pl.BlockSpec((1, S, cb), lambda j, b: (b, 0, j))],
        out_specs=[pl.BlockSpec((1, S, cb), lambda j, b: (b, 0, j)), pl.BlockSpec((B_CONV, cb), lambda j, b: (0, j))],
        out_shape=[jax.ShapeDtypeStruct((B, S, C), BF16), jax.ShapeDtypeStruct((B_CONV, C), F32)],
        compiler_params=_cparams(("arbitrary", "arbitrary")),
    )(x, w, dc)


def _lane_half(shape):
    return lax.broadcasted_iota(jnp.int32, shape, len(shape) - 1) < shape[-1] // 2


def _bd(xw):
    first = _lane_half(xw.shape)
    return jnp.concatenate([jnp.where(first, xw, 0.0), jnp.where(first, 0.0, xw)], axis=0)


def _unbd(y):
    r = y.shape[0] // 2
    return jnp.where(_lane_half((r, y.shape[1])), y[:r], y[r:])


def _rs(xw):
    return jnp.concatenate([xw[:, :LANES], xw[:, LANES:]], axis=0)


def _unrs(y):
    r = y.shape[0] // 2
    return jnp.concatenate([y[:r], y[r:]], axis=1)


def _make_pair_dots(hp):
    @jax.custom_vjp
    def nn(xw, pw):
        return _dg(xw, _bd(pw), 1, 0, hp)

    def nn_f(xw, pw):
        return nn(xw, pw), (xw, pw)

    def nn_b(r, ct):
        xw, pw = r
        return _dg(ct, _bd(pw), 1, 1, hp), _unbd(_dg(xw, ct, 0, 0, hp))

    nn.defvjp(nn_f, nn_b)

    @jax.custom_vjp
    def nt(xw, yw):
        return _dg(xw, _bd(yw), 1, 1, hp)

    def nt_f(xw, yw):
        return nt(xw, yw), (xw, yw)

    def nt_b(r, ct):
        xw, yw = r
        return _dg(ct, _bd(yw), 1, 0, hp), _unbd(_dg(ct, xw, 0, 0, hp))

    nt.defvjp(nt_f, nt_b)

    @jax.custom_vjp
    def tn(xw, yw):
        return _dg(_rs(xw), _bd(yw), 0, 0, hp)

    def tn_f(xw, yw):
        return tn(xw, yw), (xw, yw)

    def tn_b(r, ct):
        xw, yw = r
        return _unrs(_dg(_bd(yw), ct, 1, 1, hp)), _unbd(_dg(_rs(xw), ct, 1, 0, hp))

    tn.defvjp(tn_f, tn_b)
    return nn, nt, tn


pdot_nn, pdot_nt, pdot_tn = _make_pair_dots(False)
phdot_nn, phdot_nt, phdot_tn = _make_pair_dots(True)


@jax.custom_vjp
def _tril_cumsum(gw):
    C = gw.shape[0]
    r = lax.broadcasted_iota(jnp.int32, (C, C), 0)
    c = lax.broadcasted_iota(jnp.int32, (C, C), 1)
    return _dg(jnp.where(r >= c, 1.0, 0.0), gw, 1, 0, True)


def _tril_cumsum_f(gw):
    return _tril_cumsum(gw), gw.shape[0]


def _tril_cumsum_b(C, ct):
    r = lax.broadcasted_iota(jnp.int32, (C, C), 0)
    c = lax.broadcasted_iota(jnp.int32, (C, C), 1)
    return (_dg(jnp.where(r <= c, 1.0, 0.0), ct, 1, 0, True),)


_tril_cumsum.defvjp(_tril_cumsum_f, _tril_cumsum_b)


def _pair(fa, fb, shape):
    return jnp.where(_lane_half(shape), fa, fb)


def _half_sums(xw):
    sa = jnp.sum(xw[:, :LANES], axis=-1, keepdims=True)
    sb = jnp.sum(xw[:, LANES:], axis=-1, keepdims=True)
    return _pair(sa, sb, xw.shape)


def _neumann_inverse_pair(low):
    C = low.shape[0]
    r = lax.broadcasted_iota(jnp.int32, low.shape, 0)
    c = lax.broadcasted_iota(jnp.int32, low.shape, 1) % LANES
    x = jnp.where(r == c, 1.0, 0.0) - low
    p = phdot_nn(low, low)
    n = 2
    while n < C:
        x = x + phdot_nn(x, p)
        n *= 2
        if n < C:
            p = phdot_nn(p, p)
    return x


@jax.custom_vjp
def _saved_inverse(low, tinv):
    return tinv


def _saved_inverse_f(low, tinv):
    return tinv, tinv


def _saved_inverse_b(tinv, ct):
    return -phdot_nt(phdot_tn(tinv, ct), tinv), jnp.zeros_like(tinv)


_saved_inverse.defvjp(_saved_inverse_f, _saved_inverse_b)


def _gdn_chunk_pair(cq, ck, cv, z, ab, prm_a, prm_b, nw, state, h0, tinv_saved=None):
    C, W2 = cq.shape
    lane = lax.broadcasted_iota(jnp.int32, ab.shape, 1)
    col = lambda j: jnp.sum(jnp.where(lane == j, ab, 0.0), axis=1, keepdims=True)
    a = _pair(col(h0), col(h0 + 1), (C, W2))
    b = _pair(col(h0 + B_HEADS), col(h0 + 1 + B_HEADS), (C, W2))
    a_log = _pair(prm_a[:, 0:1], prm_b[:, 0:1], (1, W2))
    dtb = _pair(prm_a[:, 1:2], prm_b[:, 1:2], (1, W2))
    nw2 = jnp.concatenate([nw, nw], axis=1)
    q = _silu(cq)
    q = q * lax.rsqrt(_half_sums(q * q) + NORM_EPS) * (B_HEAD_DIM ** -0.5)
    k = _silu(ck)
    k = k * lax.rsqrt(_half_sums(k * k) + NORM_EPS)
    v = _silu(cv)
    beta = _sigmoid(b)
    g = -jnp.exp(a_log) * _softplus(a + dtb)

    r = lax.broadcasted_iota(jnp.int32, (C, W2), 0)
    c = lax.broadcasted_iota(jnp.int32, (C, W2), 1) % LANES
    tril = r >= c
    gc = _tril_cumsum(g)
    gcj = jnp.concatenate([gc[:, :LANES].T, gc[:, LANES:].T], axis=1)
    decay = jnp.where(tril, jnp.exp(jnp.where(tril, gc - gcj, 0.0)), 0.0)
    gl = gc[C - 1:C, :]
    egc = jnp.exp(gc)

    kb = k * beta
    kk = pdot_nt(jnp.concatenate([kb, q], axis=0), k)
    low = jnp.where(r > c, kk[:C] * decay, 0.0)
    attn = kk[C:] * decay
    tinv = _neumann_inverse_pair(low) if tinv_saved is None else _saved_inverse(low, tinv_saved)
    u = pdot_nn(tinv, v * beta)
    w = pdot_nn(tinv, kb * egc)
    ws = pdot_nn(jnp.concatenate([w, q * egc], axis=0), state)
    v_new = u - ws[:C]
    o = ws[C:] + pdot_nn(attn, v_new)
    new_state = state * jnp.exp(gl) + pdot_tn(k * jnp.exp(gl - gc), v_new)
    y = o * lax.rsqrt(_half_sums(o * o) * (1.0 / B_HEAD_DIM) + NORM_EPS) * nw2 * _silu(z)
    if tinv_saved is None:
        return y, new_state, tinv
    return y, new_state


def gdn_fwd(c, proj, prm, nw):
    B, S, _ = c.shape
    H, dh, C = B_HEADS, B_HEAD_DIM, B_CHUNK
    N, HP, W2 = S // C, H // 2, 2 * dh

    def body(cq_ref, ck_ref, cv_ref, z_ref, ab_ref, prm_ref, nw_ref, y_ref, st_ref, ti_ref, state):
        h0 = pl.program_id(1) * 2
        state[...] = jnp.zeros_like(state)
        nwv = nw_ref[...]

        def step(n, carry):
            rows = pl.ds(pl.multiple_of(n * C, C), C)
            s_in = state[...]
            st_ref[0, n] = s_in
            y, s_out, tinv = _gdn_chunk_pair(cq_ref[0, rows, :], ck_ref[0, rows, :], cv_ref[0, rows, :],
                                             z_ref[0, rows, :], ab_ref[0, rows, :], prm_ref[h0], prm_ref[h0 + 1],
                                             nwv, s_in, h0)
            ti_ref[0, n] = tinv
            y_ref[0, rows, :] = y.astype(y_ref.dtype)
            state[...] = s_out
            return carry

        lax.fori_loop(0, N, step, 0)

    def col(off):
        return pl.BlockSpec((1, S, W2), lambda b, h: (b, 0, h + off // 2))

    return pl.pallas_call(
        body, name="gdn_fwd", grid=(B, HP),
        in_specs=[col(0), col(H), col(2 * H), col(3 * H), pl.BlockSpec((1, S, B_AB_PAD), lambda b, h: (b, 0, 4 * H)),
                  pl.BlockSpec((H, 1, LANES), lambda b, h: (0, 0, 0)), pl.BlockSpec((1, dh), lambda b, h: (0, 0))],
        out_specs=[col(0), pl.BlockSpec((1, N, dh, W2), lambda b, h: (b * HP + h, 0, 0, 0)),
                   pl.BlockSpec((1, N, C, W2), lambda b, h: (b * HP + h, 0, 0, 0))],
        out_shape=[jax.ShapeDtypeStruct((B, S, H * dh), BF16), jax.ShapeDtypeStruct((B * HP, N, dh, W2), F32),
                   jax.ShapeDtypeStruct((B * HP, N, C, W2), F32)],
        scratch_shapes=[pltpu.VMEM((dh, W2), F32)],
        compiler_params=_cparams(("arbitrary", "arbitrary")),
    )(c, c, c, proj, proj, prm, nw)


def gdn_bwd(c, proj, prm, nw, states, tinvs, dy):
    B, S, _ = c.shape
    H, dh, C = B_HEADS, B_HEAD_DIM, B_CHUNK
    N, HP, W2 = S // C, H // 2, 2 * dh

    def body(cq_ref, ck_ref, cv_ref, z_ref, ab_ref, prm_ref, nw_ref, st_ref, ti_ref, dy_ref,
             dq_ref, dk_ref, dv_ref, dz_ref, dab_ref, dprm_ref, dnw_ref, dstate):
        b = pl.program_id(0)
        hp = pl.program_id(1)
        h0 = hp * 2
        dstate[...] = jnp.zeros_like(dstate)

        @pl.when((b == 0) & (hp == 0))
        def _():
            dprm_ref[...] = jnp.zeros_like(dprm_ref)
            dnw_ref[...] = jnp.zeros_like(dnw_ref)

        @pl.when(hp == 0)
        def _():
            dab_ref[...] = jnp.zeros_like(dab_ref)

        nwv = nw_ref[...]

        def step(t, carry):
            n = N - 1 - t
            rows = pl.ds(pl.multiple_of(n * C, C), C)
            _, vjp = jax.vjp(functools.partial(_gdn_chunk_pair, h0=h0, tinv_saved=ti_ref[0, n]), cq_ref[0, rows, :],
                             ck_ref[0, rows, :], cv_ref[0, rows, :], z_ref[0, rows, :], ab_ref[0, rows, :],
                             prm_ref[h0], prm_ref[h0 + 1], nwv, st_ref[0, n])
            dcq, dck, dcv, dz, dab, dpa, dpb, dnw, ds = vjp((dy_ref[0, rows, :].astype(F32), dstate[...]))
            dq_ref[0, rows, :] = dcq
            dk_ref[0, rows, :] = dck
            dv_ref[0, rows, :] = dcv
            dz_ref[0, rows, :] = dz.astype(dz_ref.dtype)
            dab_ref[0, rows, :] += dab
            dprm_ref[h0] += dpa
            dprm_ref[h0 + 1] += dpb
            dnw_ref[...] += dnw
            dstate[...] = ds
            return carry

        lax.fori_loop(0, N, step, 0)

    def col(off):
        return pl.BlockSpec((1, S, W2), lambda b, h: (b, 0, h + off // 2))

    abspec = pl.BlockSpec((1, S, B_AB_PAD), lambda b, h: (b, 0, 4 * H))
    ab0 = pl.BlockSpec((1, S, B_AB_PAD), lambda b, h: (b, 0, 0))
    prm_full = pl.BlockSpec((H, 1, LANES), lambda b, h: (0, 0, 0))
    nwspec = pl.BlockSpec((1, dh), lambda b, h: (0, 0))
    wsh = jax.ShapeDtypeStruct((B, S, H * dh), F32)
    return pl.pallas_call(
        body, name="gdn_bwd", grid=(B, HP),
        in_specs=[col(0), col(H), col(2 * H), col(3 * H), abspec, prm_full, nwspec,
                  pl.BlockSpec((1, N, dh, W2), lambda b, h: (b * HP + h, 0, 0, 0)),
                  pl.BlockSpec((1, N, C, W2), lambda b, h: (b * HP + h, 0, 0, 0)), col(0)],
        out_specs=[col(0), col(0), col(0), col(0), ab0, prm_full, nwspec],
        out_shape=[wsh, wsh, wsh, jax.ShapeDtypeStruct((B, S, H * dh), BF16),
                   jax.ShapeDtypeStruct((B, S, B_AB_PAD), F32), jax.ShapeDtypeStruct((H, 1, LANES), F32),
                   jax.ShapeDtypeStruct((1, dh), F32)],
        scratch_shapes=[pltpu.VMEM((dh, W2), F32)],
        compiler_params=_cparams(("arbitrary", "arbitrary"), VMEM_LIMIT_BIG),
    )(c, c, c, proj, proj, prm, nw, states, tinvs, dy)


def _my_place():
    return lax.axis_index("x"), lax.axis_index("y"), lax.axis_index("c")


HBM_SPEC = pl.BlockSpec(memory_space=pltpu.HBM)


def _gather_phases(x_refs, out_refs, send_sems, recv_sems, local_sems):
    T = len(x_refs)
    x, y, c = _my_place()
    me, sibling = (x, y, c), (x, y, 1 - c)
    chips = [(1 - x, y), (x, 1 - y), (1 - x, 1 - y)]

    def copy(t, k, block, to, src=None):
        px, py, pc = block
        slot = out_refs[t].at[4 * px + 2 * py + pc]
        return pltpu.make_async_remote_copy(
            src_ref=slot if src is None else src, dst_ref=slot, send_sem=send_sems.at[7 * t + k],
            recv_sem=recv_sems.at[7 * t + k], device_id=to, device_id_type=MESH)

    def local(t):
        return pltpu.make_async_copy(x_refs[t], out_refs[t].at[4 * x + 2 * y + c], local_sems.at[t])

    def first(t):
        return [copy(t, 0, me, sibling, src=x_refs[t])] + [
            copy(t, 1 + j, me, (*chip, c), src=x_refs[t]) for j, chip in enumerate(chips)]

    def start():
        for t in range(T):
            local(t).start()
            for cp in first(t):
                cp.start()

    def forward():
        for j, chip in enumerate(chips):
            for t in range(T):
                copy(t, 1 + j, (*chip, c), me).wait_recv()
                copy(t, 4 + j, (*chip, c), sibling).start()

    def finish():
        for t in range(T):
            copy(t, 0, sibling, me).wait_recv()
            for j, chip in enumerate(chips):
                copy(t, 4 + j, (*chip, 1 - c), me).wait_recv()
        for t in range(T):
            for cp in first(t):
                cp.wait_send()
            for j, chip in enumerate(chips):
                copy(t, 4 + j, (*chip, c), sibling).wait_send()
            local(t).wait()

    return start, forward, finish


def _gather_scratch(T):
    return [pltpu.SemaphoreType.DMA((7 * T,)), pltpu.SemaphoreType.DMA((7 * T,)), pltpu.SemaphoreType.DMA((T,))]


def allgather_shards(shards):
    T = len(shards)

    def body(*refs):
        start, forward, finish = _gather_phases(refs[:T], refs[T:2 * T], *refs[2 * T:])
        start()
        forward()
        finish()

    return pl.pallas_call(
        body, name="allgather_shards",
        out_shape=[jax.ShapeDtypeStruct((N_DEV,) + s.shape, s.dtype) for s in shards],
        in_specs=[HBM_SPEC] * T, out_specs=[HBM_SPEC] * T, scratch_shapes=_gather_scratch(T),
    )(*shards)


def exchange_sibling(gs):
    T = len(gs)

    def body(*refs):
        g_refs, out_refs = refs[:T], refs[T:2 * T]
        send_sems, recv_sems = refs[2 * T:]
        x, y, c = _my_place()
        copies = []
        for t in range(T):
            for k in range(4):
                copies.append(pltpu.make_async_remote_copy(
                    src_ref=g_refs[t].at[2 * k + (1 - c)], dst_ref=out_refs[t].at[k],
                    send_sem=send_sems.at[4 * t + k], recv_sem=recv_sems.at[4 * t + k],
                    device_id=(x, y, 1 - c), device_id_type=MESH))
        for cp in copies:
            cp.start()
        for cp in copies:
            cp.wait()

    return pl.pallas_call(
        body, name="exchange_sibling",
        out_shape=[jax.ShapeDtypeStruct((4,) + g.shape[1:], g.dtype) for g in gs],
        in_specs=[HBM_SPEC] * T, out_specs=[HBM_SPEC] * T,
        scratch_shapes=[pltpu.SemaphoreType.DMA((4 * T,)), pltpu.SemaphoreType.DMA((4 * T,))],
    )(*gs)


def _row_tile(rows, row_bytes, budget):
    if rows * row_bytes <= budget:
        return rows
    best = None
    for t in range(16, rows, 16):
        if rows % t == 0 and t * row_bytes <= budget:
            best = t
    return best if best is not None else rows


def add_sibling(g, got, c_idx):
    _, R, L = g.shape
    tr = _row_tile(R, L * 4, 2 << 20)

    def body(c_ref, g_ref, r_ref, o_ref):
        o_ref[...] = (g_ref[...].astype(F32) + r_ref[...].astype(F32)).astype(o_ref.dtype)

    grid_spec = pltpu.PrefetchScalarGridSpec(
        num_scalar_prefetch=1, grid=(4, R // tr),
        in_specs=[pl.BlockSpec((1, tr, L), lambda k, i, c: (2 * k + c[0], i, 0)),
                  pl.BlockSpec((1, tr, L), lambda k, i, c: (k, i, 0))],
        out_specs=pl.BlockSpec((1, tr, L), lambda k, i, c: (k, i, 0)))
    return pl.pallas_call(
        body, name="add_sibling", grid_spec=grid_spec, out_shape=jax.ShapeDtypeStruct((4, R, L), RS_DTYPE),
        compiler_params=_cparams(("arbitrary", "arbitrary")),
    )(c_idx, g, got)


def _exchange_phases(p_refs, out_refs, send_sems, recv_sems, local_sems):
    T = len(p_refs)
    x, y, c = _my_place()
    mychip = 2 * x + y
    chips = [(1 - x, y), (x, 1 - y), (1 - x, 1 - y)]

    def local(t):
        return pltpu.make_async_copy(p_refs[t].at[mychip], out_refs[t].at[mychip], local_sems.at[t])

    def send(t, j):
        px, py = chips[j]
        return pltpu.make_async_remote_copy(
            src_ref=p_refs[t].at[2 * px + py], dst_ref=out_refs[t].at[mychip], send_sem=send_sems.at[3 * t + j],
            recv_sem=recv_sems.at[3 * t + j], device_id=(px, py, c), device_id_type=MESH)

    def landed(t, j):
        px, py = chips[j]
        return pltpu.make_async_remote_copy(
            src_ref=p_refs[t].at[mychip], dst_ref=out_refs[t].at[2 * px + py], send_sem=send_sems.at[3 * t + j],
            recv_sem=recv_sems.at[3 * t + j], device_id=(px, py, c), device_id_type=MESH)

    def start():
        for t in range(T):
            local(t).start()
            for j in range(3):
                send(t, j).start()

    def finish():
        for t in range(T):
            for j in range(3):
                landed(t, j).wait_recv()
        for t in range(T):
            for j in range(3):
                send(t, j).wait_send()
            local(t).wait()

    return start, finish


def _exchange_scratch(T):
    return [pltpu.SemaphoreType.DMA((3 * T,)), pltpu.SemaphoreType.DMA((3 * T,)), pltpu.SemaphoreType.DMA((T,))]


def exchange_chips(ps):
    T = len(ps)

    def body(*refs):
        start, finish = _exchange_phases(refs[:T], refs[T:2 * T], *refs[2 * T:])
        start()
        finish()

    return pl.pallas_call(
        body, name="exchange_chips", out_shape=[jax.ShapeDtypeStruct(p.shape, p.dtype) for p in ps],
        in_specs=[HBM_SPEC] * T, out_specs=[HBM_SPEC] * T, scratch_shapes=_exchange_scratch(T),
    )(*ps)


def allgather_small(s):
    R, W = s.shape

    def body(s_ref, out_ref, send_sems, recv_sems):
        x, y, c = _my_place()
        me = 4 * x + 2 * y + c
        out_ref[me] = s_ref[...]
        copies = []
        for j in range(1, N_DEV):
            peer = (x ^ (j >> 2), y ^ ((j >> 1) & 1), c ^ (j & 1))
            copies.append(pltpu.make_async_remote_copy(
                src_ref=s_ref, dst_ref=out_ref.at[me], send_sem=send_sems.at[j - 1], recv_sem=recv_sems.at[j - 1],
                device_id=peer, device_id_type=MESH))
        for cp in copies:
            cp.start()
        for j in range(1, N_DEV):
            px, py, pc = x ^ (j >> 2), y ^ ((j >> 1) & 1), c ^ (j & 1)
            pltpu.make_async_remote_copy(
                src_ref=s_ref, dst_ref=out_ref.at[4 * px + 2 * py + pc], send_sem=send_sems.at[j - 1],
                recv_sem=recv_sems.at[j - 1], device_id=(px, py, pc), device_id_type=MESH).wait_recv()
        for cp in copies:
            cp.wait_send()

    vm = pl.BlockSpec(memory_space=pltpu.VMEM)
    return pl.pallas_call(
        body, name="allgather_small", out_shape=jax.ShapeDtypeStruct((N_DEV, R, W), s.dtype),
        in_specs=[vm], out_specs=vm,
        scratch_shapes=[pltpu.SemaphoreType.DMA((7,)), pltpu.SemaphoreType.DMA((7,))],
    )(s)


def _adamw(w, g, m, v):
    m = ADAM_B1 * m + (1.0 - ADAM_B1) * g
    v = ADAM_B2 * v + (1.0 - ADAM_B2) * jnp.square(g)
    m_hat = m / (1.0 - ADAM_B1 ** ADAM_STEP)
    v_hat = v / (1.0 - ADAM_B2 ** ADAM_STEP)
    delta = -ADAM_LR * (m_hat / (jnp.sqrt(v_hat) + ADAM_EPS) + ADAM_WD * w)
    return delta, m, v


def sum_adamw(parts, w, m, v):
    P, G, Rp, Lp = parts.shape
    _, R, L = w.shape
    tr = _row_tile(R, P * Lp * parts.dtype.itemsize + 7 * L * 4, 5 << 20)

    def body(p_ref, w_ref, m_ref, v_ref, g_ref, d_ref, nm_ref, nv_ref):
        g = p_ref[0, 0].astype(F32)
        for k in range(1, P):
            g = g + p_ref[k, 0].astype(F32)
        g = g[:, :L]
        d, nm, nv = _adamw(w_ref[0], g, m_ref[0], v_ref[0])
        g_ref[0] = g
        d_ref[0] = d
        nm_ref[0] = nm
        nv_ref[0] = nv

    row = pl.BlockSpec((1, tr, L), lambda gi, i: (gi, i, 0))
    sh = jax.ShapeDtypeStruct((G, R, L), F32)
    return pl.pallas_call(
        body, name="sum_adamw", grid=(G, R // tr),
        in_specs=[pl.BlockSpec((P, 1, tr, Lp), lambda gi, i: (0, gi, i, 0)), row, row, row],
        out_specs=[row] * 4, out_shape=[sh] * 4, compiler_params=_cparams(("arbitrary", "arbitrary")),
    )(parts, w, m, v)


def _unblock(t, axis):
    sh = t.shape[1:]
    t = jnp.moveaxis(t, 0, axis)
    return t.reshape(sh[:axis] + (N_DEV * sh[axis],) + sh[axis + 1:])


def _block(a, axis):
    sh = a.shape
    t = a.reshape(sh[:axis] + (N_DEV, sh[axis] // N_DEV) + sh[axis + 1:])
    return jnp.moveaxis(t, axis, 0)


def _rows3(t):
    return t.reshape((t.shape[0], -1, t.shape[-1]))


def _pad_to(a, axis, n):
    pad = [(0, 0)] * a.ndim
    pad[axis] = (0, n - a.shape[axis])
    return jnp.pad(a, pad)


def _lane_pad(n):
    return -(-n // LANES) * LANES


def _pack(arrs, row_mult):
    flat = jnp.concatenate([a.astype(F32).reshape(-1) for a in arrs])
    n = flat.shape[0]
    rows = -(-n // LANES)
    rows = -(-rows // row_mult) * row_mult
    return jnp.pad(flat, (0, rows * LANES - n)).reshape(rows, LANES)


def _unpack(buf, shapes):
    flat = buf.reshape(-1)
    out, off = [], 0
    for sh in shapes:
        n = math.prod(sh)
        out.append(flat[off:off + n].reshape(sh))
        off += n
    return out


def _ffn_fwd(x, h, wg, wu, wd, next_g, gather=()):
    gate, up, act, *gathered = ffn_up(h, wg, wu, gather)
    x_new, h_next = mm_nn(act, wd, res=x, res_scale=MACARON_WEIGHT, norm_g=next_g, name="ffn_down")
    return x_new, h_next, (x, h, gate, up, act), gathered


def _ffn_bwd(dx, dout, saved, g, wg, wu, wd, tm, next_scale, exchange=()):
    x, h, gate, up, act = saved
    dgate, dup, *received = ffn_bwd_act(dout, wd, gate, up, exchange)
    dwd = mm_tn(act, dout, name="ffn_dwd")
    dwg = mm_tn(h, dgate, name="ffn_dwg")
    dwu = mm_tn(h, dup, name="ffn_dwu")
    dh = mm_nt([(dgate, wg), (dup, wu)], name="ffn_dh")
    dx_in, dxb, dg = rmsnorm_bwd(x, g, dh, dx, tm, next_scale)
    return dx_in, dxb, dg, dwg, dwu, dwd, received


def _bucket_onehot(nb, d):
    idx = _bucket_index(A_BLOCK, nb, d)
    onehot = (idx.reshape(-1)[None, :] == jnp.arange(NUM_BUCKETS)[:, None]).astype(BF16)
    return onehot, idx.shape


def _pair_tables(t, blk, nk):
    t = t.reshape(A_HEADS // 2, 2, blk, nk)
    return jnp.transpose(t, (0, 2, 1, 3)).reshape(A_HEADS // 2, blk, 2 * nk)


def _unpair_tables(t, blk, nk):
    t = t.reshape(A_HEADS // 2, blk, 2, nk)
    return jnp.transpose(t, (0, 2, 1, 3)).reshape(A_HEADS, blk * nk)


def _mixer_a_fwd(x, h, w_in, w_out, rel_bias, B, S, tm, next_g):
    proj = mm_nn(h, w_in, name="a_proj")[0].reshape(B, S, -1)
    biases, outs, lses = [], [], []
    for gi, (window, d) in enumerate(A_PATTERNS):
        nb = (S // d) // A_BLOCK
        onehot, (blk, nk) = _bucket_onehot(nb, d)
        bias2 = _pair_tables(bias_table(onehot, rel_bias[:, gi * A_HEADS:(gi + 1) * A_HEADS]), blk, nk)
        o, lse = attn_fwd(proj, bias2, gi, d, window // d)
        biases.append(bias2)
        outs.append(o.reshape(B * S, -1))
        lses.append(lse.reshape(B * S, -1))
    om = merge_fwd(outs, lses, tm)
    x_new, h_next = mm_nn(om, w_out, res=x, norm_g=next_g, name="a_out")
    return x_new, h_next, (x, h, proj, biases, outs, lses, om)


def _mixer_a_bwd(dx, dxb, saved, g, w_in, w_out, B, S, tm, next_scale):
    x, h, proj, biases, outs, lses, om = saved
    dw_out = mm_tn(om, dxb, name="a_dwout")
    dom = mm_nt([(dxb, w_out)], name="a_dom")
    g6 = merge_bwd(outs, lses, dom, tm)
    dparts, dbias_cols = [], []
    for gi, (window, d) in enumerate(A_PATTERNS):
        do = g6[gi].reshape(B, S, -1)
        dl = g6[3 + gi].reshape(B, S, -1)
        dq, dk, dv, dbias2 = attn_bwd(proj, biases[gi], do, dl, gi, d, window // d)
        dparts += [dq, dk, dv]
        onehot, (blk, nk) = _bucket_onehot((S // d) // A_BLOCK, d)
        dbias_cols.append(bias_bucket_grad(onehot, _unpair_tables(dbias2, blk, nk)))
    d_rel_bias = jnp.concatenate(dbias_cols, axis=1)
    dproj = jnp.concatenate(dparts, axis=2).astype(BF16).reshape(B * S, -1)
    dw_in = mm_tn(h, dproj, name="a_dwin")
    dh = mm_nt([(dproj, w_in)], name="a_dh")
    dx_in, dxb_in, dg = rmsnorm_bwd(x, g, dh, dx, tm, next_scale)
    return dx_in, dxb_in, dg, dw_in, dw_out, d_rel_bias


def _mixer_b_fwd(x, h, w_in_p, conv_w, prm, nw, w_out, B, S, tm, next_g):
    proj = mm_nn(h, w_in_p, name="b_proj")[0].reshape(B, S, -1)
    c = conv_fwd(proj, conv_w, 3 * B_WIDTH, 512)
    y, states, tinvs = gdn_fwd(c, proj, prm, nw)
    y2 = y.reshape(B * S, B_WIDTH)
    x_new, h_next = mm_nn(y2, w_out, res=x, norm_g=next_g, name="b_out")
    return x_new, h_next, (x, h, proj, c, states, tinvs, y2)


def _mixer_b_bwd(dx, dxb, saved, g, w_in_p, conv_w, prm, nw, w_out, B, S, tm, next_scale):
    x, h, proj, c, states, tinvs, y2 = saved
    W = B_WIDTH
    dw_out = mm_tn(y2, dxb, name="b_dwout")
    dy = mm_nt([(dxb, w_out)], out_dtype=BF16, name="b_dy").reshape(B, S, W)
    dcq, dck, dcv, dz, dab, dprm, dnw = gdn_bwd(c, proj, prm, nw, states, tinvs, dy)
    dxs, dws = [], []
    for t, dc in enumerate((dcq, dck, dcv)):
        dxp, dwp = conv_bwd(proj, conv_w[:, t * W:(t + 1) * W], dc, t * W, 512)
        dxs.append(dxp)
        dws.append(dwp)
    dconv_w = jnp.concatenate(dws, axis=1)
    dproj = jnp.concatenate(dxs + [dz, dab.astype(BF16)], axis=2).reshape(B * S, -1)
    dw_in_p = mm_tn(h, dproj, name="b_dwin")
    dh = mm_nt([(dproj, w_in_p)], name="b_dh")
    dx_in, dxb_in, dg = rmsnorm_bwd(x, g, dh, dx, tm, next_scale)
    return dx_in, dxb_in, dg, dw_in_p, dconv_w, dprm, dnw, dw_out


SHARD_NAMES = ("ffn_w_gate", "ffn_w_up", "ffn_w_down", "a_w_in", "a_w_out", "b_w_in", "b_w_out", "norm_g", "b_conv_w")
SMALL_NAMES = ("rel_bias", "b_a_log", "b_dt_bias", "b_norm_w", "final_g")


def kernel(x, norm_g, ffn_w_gate, ffn_w_up, ffn_w_down, rel_bias, a_w_in, a_w_out, b_w_in, b_conv_w, b_a_log, b_dt_bias, b_norm_w, b_w_out, final_g, loss_target, m_norm_g, m_ffn_w_gate, m_ffn_w_up, m_ffn_w_down, m_rel_bias, m_a_w_in, m_a_w_out, m_b_w_in, m_b_conv_w, m_b_a_log, m_b_dt_bias, m_b_norm_w, m_b_w_out, m_final_g, v_norm_g, v_ffn_w_gate, v_ffn_w_up, v_ffn_w_down, v_rel_bias, v_a_w_in, v_a_w_out, v_b_w_in, v_b_conv_w, v_b_a_log, v_b_dt_bias, v_b_norm_w, v_b_w_out, v_final_g):
    W = dict(norm_g=norm_g, ffn_w_gate=ffn_w_gate, ffn_w_up=ffn_w_up, ffn_w_down=ffn_w_down, rel_bias=rel_bias,
             a_w_in=a_w_in, a_w_out=a_w_out, b_w_in=b_w_in, b_conv_w=b_conv_w, b_a_log=b_a_log,
             b_dt_bias=b_dt_bias, b_norm_w=b_norm_w, b_w_out=b_w_out, final_g=final_g)
    Mo = dict(norm_g=m_norm_g, ffn_w_gate=m_ffn_w_gate, ffn_w_up=m_ffn_w_up, ffn_w_down=m_ffn_w_down,
              rel_bias=m_rel_bias, a_w_in=m_a_w_in, a_w_out=m_a_w_out, b_w_in=m_b_w_in, b_conv_w=m_b_conv_w,
              b_a_log=m_b_a_log, b_dt_bias=m_b_dt_bias, b_norm_w=m_b_norm_w, b_w_out=m_b_w_out, final_g=m_final_g)
    Vo = dict(norm_g=v_norm_g, ffn_w_gate=v_ffn_w_gate, ffn_w_up=v_ffn_w_up, ffn_w_down=v_ffn_w_down,
              rel_bias=v_rel_bias, a_w_in=v_a_w_in, a_w_out=v_a_w_out, b_w_in=v_b_w_in, b_conv_w=v_b_conv_w,
              b_a_log=v_b_a_log, b_dt_bias=v_b_dt_bias, b_norm_w=v_b_norm_w, b_w_out=v_b_w_out, final_g=v_final_g)

    B, S, D = x.shape
    M = B * S
    tm = _pick(M, (512, 256, 128))
    c_idx = lax.axis_index("c").astype(jnp.int32).reshape(1)
    fp = _lane_pad(ffn_w_gate.shape[3])

    wg_s, wu_s = (_pad_to(t.astype(BF16), 3, fp) for t in (ffn_w_gate, ffn_w_up))
    wd_s = _pad_to(ffn_w_down.astype(BF16), 2, fp)
    a_in_s, a_out_s, b_in_s, b_out_s = (t.astype(BF16) for t in (a_w_in, a_w_out, b_w_in, b_w_out))
    n_b_in = b_w_in.shape[2] * N_DEV

    def ffn_shards(i, k):
        return [wg_s[i, k], wu_s[i, k], wd_s[i, k]]

    def ffn_full(t):
        return _unblock(t[0], 1), _unblock(t[1], 1), _unblock(t[2], 0)

    def mixer_shards(i):
        return [a_in_s[i // 2], a_out_s[i // 2]] if i % 2 == 0 else [b_in_s[i // 2], b_out_s[i // 2]]

    def mixer_full(i, t):
        if i % 2 == 0:
            return _unblock(t[0], 1), _unblock(t[1], 1)
        return _pad_to(_unblock(t[0], 1), 1, n_b_in + B_AB_PAD - 2 * B_HEADS), _unblock(t[1], 0)

    first = allgather_shards(ffn_shards(0, 0) + [norm_g, b_conv_w, mixer_shards(0)[0]])
    w_ffn = ffn_full(first[:3])
    norm_all, conv_all, w_m_in = _unblock(first[3], 2), _unblock(first[4], 2), first[5]
    prm_all = _pad_to(jnp.stack([b_a_log, b_dt_bias], axis=-1)[:, :, None, :], 3, LANES)

    xs = x.reshape(M, D)
    hs = rmsnorm_fwd(xs, norm_all[0, 0][None], tm)
    saved, weights = [], []
    for i in range(DEPTH):
        j = i // 2
        w1 = w_ffn
        xs, hs, s1, got = _ffn_fwd(xs, hs, *w1, norm_all[i, 1][None],
                                   gather=mixer_shards(i)[1:] + ffn_shards(i, 1))
        wm, w2 = mixer_full(i, [w_m_in, got[0]]), ffn_full(got[1:])
        if i % 2 == 0:
            xs, hs, s2 = _mixer_a_fwd(xs, hs, *wm, rel_bias, B, S, tm, norm_all[i, 2][None])
        else:
            xs, hs, s2 = _mixer_b_fwd(xs, hs, wm[0], conv_all[j], prm_all[j], b_norm_w[j][None], wm[1], B, S, tm,
                                      norm_all[i, 2][None])
        nxt = ffn_shards(i + 1, 0) + mixer_shards(i + 1)[:1] if i + 1 < DEPTH else []
        xs, hs, s3, got = _ffn_fwd(xs, hs, *w2, norm_all[i + 1, 0][None] if nxt else None, gather=nxt)
        if nxt:
            w_ffn, w_m_in = ffn_full(got[:3]), got[3]
        saved.append((s1, s2, s3))
        weights.append((w1, wm, w2))

    dx, dxb, d_final_g, loss_row = loss_head(xs, final_g[None], loss_target.reshape(M, D), tm, MACARON_WEIGHT)

    d_prm, d_nw = [None] * 2, [None] * 2
    d_rel = jnp.zeros(rel_bias.shape, F32)
    received = [None] * DEPTH
    pending = []
    for i in reversed(range(DEPTH)):
        j = i // 2
        s1, s2, s3 = saved[i]
        w1, wm, w2 = weights[i]
        dx, dxb, dn2, dwg2, dwu2, dwd2, got_a = _ffn_bwd(dx, dxb, s3, norm_all[i, 2][None], *w2, tm, 1.0,
                                                         exchange=pending[:1])
        if i % 2 == 0:
            dx, dxb, dn1, d_in, d_out, drb = _mixer_a_bwd(dx, dxb, s2, norm_all[i, 1][None], *wm, B, S, tm,
                                                          MACARON_WEIGHT)
            d_rel = d_rel + drb
            mixer_grads = [(d_in, 1), (d_out, 1)]
        else:
            dx, dxb, dn1, dbp, dconv, d_prm[j], d_nw[j], d_out = _mixer_b_bwd(
                dx, dxb, s2, norm_all[i, 1][None], wm[0], conv_all[j], prm_all[j], b_norm_w[j][None], wm[1], B, S, tm,
                MACARON_WEIGHT)
            mixer_grads = [(dbp[:, :n_b_in], 1), (d_out, 0)]
        dx, dxb, dn0, dwg1, dwu1, dwd1, got_b = _ffn_bwd(dx, dxb, s1, norm_all[i, 0][None], *w1, tm, MACARON_WEIGHT,
                                                         exchange=pending[1:])
        if pending:
            received[i + 1] = got_a + got_b
        layer_grads = [(jnp.stack([dwg1, dwg2, dwu1, dwu2]), 2), (jnp.stack([dwd1, dwd2]), 1)] + mixer_grads
        layer_grads.append((jnp.concatenate([dn0, dn1, dn2], axis=0), 1))
        if i % 2 == 1:
            layer_grads.append((dconv, 1))
        gblocks = [_rows3(_block(g, ax).astype(RS_DTYPE)) for g, ax in layer_grads]
        got = exchange_sibling(gblocks)
        pending = [add_sibling(g, r, c_idx) for g, r in zip(gblocks, got)]
    received[0] = exchange_chips(pending)
    grad_x = dx.reshape(B, S, D)

    def over_layers(k, layers, split=None):
        ps = [received[i][k] for i in layers]
        if split is not None:
            ps = [p.reshape((4, split[2], -1, p.shape[2]))[:, split[0]:split[1]] for p in ps]
            return jnp.concatenate(ps, axis=1)
        return jnp.stack(ps, axis=1)

    every, a_layers, b_layers = range(DEPTH), range(0, DEPTH, 2), range(1, DEPTH, 2)
    parts = dict(ffn_w_gate=over_layers(0, every, (0, 2, 4)), ffn_w_up=over_layers(0, every, (2, 4, 4)),
                 ffn_w_down=over_layers(1, every, (0, 2, 2)), a_w_in=over_layers(2, a_layers),
                 a_w_out=over_layers(3, a_layers), b_w_in=over_layers(2, b_layers), b_w_out=over_layers(3, b_layers),
                 norm_g=over_layers(4, every), b_conv_w=over_layers(5, b_layers))
    out_g, out_d, out_m, out_v = {}, {}, {}, {}
    for n in SHARD_NAMES:
        w = W[n]
        w3 = w.reshape((-1,) + w.shape[-2:])
        res = sum_adamw(parts[n], w3, Mo[n].reshape(w3.shape), Vo[n].reshape(w3.shape))
        for dst, r in zip((out_g, out_d, out_m, out_v), res):
            dst[n] = r.reshape(w.shape)

    d_alog = jnp.stack([p[:, 0, 0] for p in d_prm])
    d_dtb = jnp.stack([p[:, 0, 1] for p in d_prm])
    small_g = [d_rel, d_alog, d_dtb, jnp.concatenate(d_nw, axis=0), d_final_g[0], loss_row[0, :1]]
    small_shapes = [W[n].shape for n in SMALL_NAMES]
    sg = allgather_small(_pack(small_g, 8))
    pks = lambda d: _pack([d[n] for n in SMALL_NAMES] + [jnp.zeros((1,), F32)], 8)[None]
    res = sum_adamw(sg[:, None], pks(W), pks(Mo), pks(Vo))
    loss = res[0].reshape(-1)[sum(math.prod(s) for s in small_shapes)]
    for dst, r in zip((out_g, out_d, out_m, out_v), res):
        dst.update(zip(SMALL_NAMES, _unpack(r, small_shapes)))

    order = ("norm_g", "ffn_w_gate", "ffn_w_up", "ffn_w_down", "rel_bias", "a_w_in", "a_w_out", "b_w_in",
             "b_conv_w", "b_a_log", "b_dt_bias", "b_norm_w", "b_w_out", "final_g")
    return (loss, grad_x, *[out_g[n] for n in order], *[out_d[n] for n in order],
            *[out_m[n] for n in order], *[out_v[n] for n in order])
```

```python
import functools
import math

import jax
import jax.numpy as jnp
from jax import lax
from jax.experimental import pallas as pl
from jax.experimental.pallas import tpu as pltpu

F32 = jnp.float32
BF16 = jnp.bfloat16

N_DEV = 8
NORM_EPS = 1e-6
MACARON_WEIGHT = 0.5
DEPTH = 4

A_PATTERNS = ((128, 1), (512, 4), (2048, 16))
A_HEADS = 8
A_HEAD_DIM = 64
A_GROUP_WIDTH = A_HEADS * A_HEAD_DIM
A_BLOCK = 128
NEG_INF = -1e30
NUM_BUCKETS = 32
MAX_DISTANCE = 2048

B_HEADS = 8
B_HEAD_DIM = 128
B_WIDTH = B_HEADS * B_HEAD_DIM
B_CONV = 4
B_CHUNK = 128
B_AB_PAD = 128

ADAM_LR = 0.001
ADAM_B1 = 0.9
ADAM_B2 = 0.999
ADAM_EPS = 1e-08
ADAM_WD = 0.01
ADAM_STEP = 10

LANES = 128
VMEM_LIMIT = 48 * 1024 * 1024
VMEM_LIMIT_BIG = 60 * 1024 * 1024
RS_DTYPE = BF16

MESH = pl.DeviceIdType.MESH


def _cparams(sem=None, vmem=VMEM_LIMIT):
    return pltpu.CompilerParams(dimension_semantics=sem, vmem_limit_bytes=vmem)


def _sigmoid(x):
    return 0.5 * (jnp.tanh(0.5 * x) + 1.0)


def _silu(x):
    return x * _sigmoid(x)


def _softplus(x):
    return jnp.maximum(x, 0.0) + jnp.log1p(jnp.exp(-jnp.abs(x)))


def _rms(x, g):
    return x * lax.rsqrt(jnp.mean(x * x, axis=-1, keepdims=True) + NORM_EPS) * g


def _dg(a, b, ca, cb, hp):
    dims = (((ca,), (cb,)), ((), ()))
    if hp:
        return lax.dot_general(a.astype(F32), b.astype(F32), dims, precision=lax.Precision.HIGH,
                               preferred_element_type=F32)
    return lax.dot_general(a.astype(BF16), b.astype(BF16), dims, preferred_element_type=F32)


def _make_dots(hp):
    @jax.custom_vjp
    def nn(a, b):
        return _dg(a, b, 1, 0, hp)

    def nn_f(a, b):
        return nn(a, b), (a, b)

    def nn_b(r, ct):
        a, b = r
        return _dg(ct, b, 1, 1, hp), _dg(a, ct, 0, 0, hp)

    nn.defvjp(nn_f, nn_b)

    @jax.custom_vjp
    def nt(a, b):
        return _dg(a, b, 1, 1, hp)

    def nt_f(a, b):
        return nt(a, b), (a, b)

    def nt_b(r, ct):
        a, b = r
        return _dg(ct, b, 1, 0, hp), _dg(ct, a, 0, 0, hp)

    nt.defvjp(nt_f, nt_b)

    @jax.custom_vjp
    def tn(a, b):
        return _dg(a, b, 0, 0, hp)

    def tn_f(a, b):
        return tn(a, b), (a, b)

    def tn_b(r, ct):
        a, b = r
        return _dg(b, ct, 1, 1, hp), _dg(a, ct, 1, 0, hp)

    tn.defvjp(tn_f, tn_b)
    return nn, nt, tn


bdot_nn, bdot_nt, bdot_tn = _make_dots(False)


def rmsnorm_fwd(x, g, tm):
    M, D = x.shape

    def body(x_ref, g_ref, h_ref):
        h_ref[...] = _rms(x_ref[...], g_ref[...]).astype(h_ref.dtype)

    return pl.pallas_call(
        body, name="rmsnorm_fwd", grid=(M // tm,),
        in_specs=[pl.BlockSpec((tm, D), lambda i: (i, 0)), pl.BlockSpec((1, D), lambda i: (0, 0))],
        out_specs=pl.BlockSpec((tm, D), lambda i: (i, 0)),
        out_shape=jax.ShapeDtypeStruct((M, D), BF16),
        compiler_params=_cparams(("arbitrary",)),
    )(x, g)


def rmsnorm_bwd(x, g, dh, dres, tm, next_scale):
    M, D = x.shape

    def body(x_ref, g_ref, dh_ref, dres_ref, dx_ref, dxb_ref, dg_ref):
        _, vjp = jax.vjp(_rms, x_ref[...], g_ref[...])
        dx, dg = vjp(dh_ref[...])
        dx = dres_ref[...] + dx
        dx_ref[...] = dx
        dxb_ref[...] = (next_scale * dx).astype(dxb_ref.dtype)

        @pl.when(pl.program_id(0) == 0)
        def _():
            dg_ref[...] = jnp.zeros_like(dg_ref)

        dg_ref[...] += dg

    row = pl.BlockSpec((tm, D), lambda i: (i, 0))
    vec = pl.BlockSpec((1, D), lambda i: (0, 0))
    return pl.pallas_call(
        body, name="rmsnorm_bwd", grid=(M // tm,),
        in_specs=[row, vec, row, row], out_specs=[row, row, vec],
        out_shape=[jax.ShapeDtypeStruct((M, D), F32), jax.ShapeDtypeStruct((M, D), BF16),
                   jax.ShapeDtypeStruct((1, D), F32)],
        compiler_params=_cparams(("arbitrary",)),
    )(x, g, dh, dres)


def loss_head(x, g, target, tm, next_scale):
    M, D = x.shape

    def body(x_ref, g_ref, t_ref, dx_ref, dxb_ref, dg_ref, loss_ref):
        t = t_ref[...]

        def f(xv, gv):
            err = _rms(xv, gv) - t
            return 0.5 * jnp.sum(jnp.mean(err * err, axis=-1, keepdims=True), axis=0, keepdims=True)

        val, vjp = jax.vjp(f, x_ref[...], g_ref[...])
        dx, dg = vjp(jnp.ones((1, 1), F32))
        dx_ref[...] = dx
        dxb_ref[...] = (next_scale * dx).astype(dxb_ref.dtype)

        @pl.when(pl.program_id(0) == 0)
        def _():
            dg_ref[...] = jnp.zeros_like(dg_ref)
            loss_ref[...] = jnp.zeros_like(loss_ref)

        dg_ref[...] += dg
        loss_ref[...] += jnp.broadcast_to(val, loss_ref.shape)

    row = pl.BlockSpec((tm, D), lambda i: (i, 0))
    vec = pl.BlockSpec((1, D), lambda i: (0, 0))
    return pl.pallas_call(
        body, name="loss_head", grid=(M // tm,),
        in_specs=[row, vec, row],
        out_specs=[row, row, vec, pl.BlockSpec((1, LANES), lambda i: (0, 0))],
        out_shape=[jax.ShapeDtypeStruct((M, D), F32), jax.ShapeDtypeStruct((M, D), BF16),
                   jax.ShapeDtypeStruct((1, D), F32), jax.ShapeDtypeStruct((1, LANES), F32)],
        compiler_params=_cparams(("arbitrary",)),
    )(x, g, target)


def _pick(n, cands):
    for c in cands:
        if n % c == 0:
            return c
    return n


def mm_nn(a, b, *, out_dtype=F32, res=None, res_scale=1.0, norm_g=None, name="mm_nn"):
    M, K = a.shape
    N = b.shape[1]
    tm = _pick(M, (512, 256, 128))
    tn = _pick(N, (1408, 1536, 1024, 512, 256, 128))
    assert norm_g is None or tn == N

    def body(*refs):
        a_ref, b_ref = refs[:2]
        rest = list(refs[2:])
        r_ref = rest.pop(0) if res is not None else None
        g_ref = rest.pop(0) if norm_g is not None else None
        o_ref = rest.pop(0)
        acc = _dg(a_ref[...], b_ref[...], 1, 0, False)
        if res is not None:
            acc = r_ref[...] + res_scale * acc
        o_ref[...] = acc.astype(o_ref.dtype)
        if norm_g is not None:
            rest[0][...] = _rms(acc, g_ref[...]).astype(BF16)

    tile = pl.BlockSpec((tm, tn), lambda j, i: (i, j))
    in_specs = [pl.BlockSpec((tm, K), lambda j, i: (i, 0)), pl.BlockSpec((K, tn), lambda j, i: (0, j))]
    args = [a, b]
    if res is not None:
        in_specs.append(tile)
        args.append(res)
    if norm_g is not None:
        in_specs.append(pl.BlockSpec((1, tn), lambda j, i: (0, 0)))
        args.append(norm_g)
    out_specs, out_shape = tile, jax.ShapeDtypeStruct((M, N), out_dtype)
    if norm_g is not None:
        out_specs, out_shape = [tile, tile], [out_shape, jax.ShapeDtypeStruct((M, N), BF16)]
    out = pl.pallas_call(
        body, name=name, grid=(N // tn, M // tm), in_specs=in_specs, out_specs=out_specs, out_shape=out_shape,
        compiler_params=_cparams(("arbitrary", "arbitrary")),
    )(*args)
    return out if norm_g is not None else (out, None)


def mm_nt(pairs, *, out_dtype=F32, name="mm_nt"):
    M = pairs[0][0].shape[0]
    N = pairs[0][1].shape[0]
    tm = _pick(M, (512, 256, 128))
    tn = _pick(N, (1024, 512, 256, 128))
    np_ = len(pairs)

    def body(*refs):
        o_ref = refs[-1]
        acc = None
        for p in range(np_):
            t = _dg(refs[2 * p][...], refs[2 * p + 1][...], 1, 1, False)
            acc = t if acc is None else acc + t
        o_ref[...] = acc.astype(o_ref.dtype)

    in_specs, args = [], []
    for a, b in pairs:
        K = a.shape[1]
        in_specs += [pl.BlockSpec((tm, K), lambda j, i: (i, 0)), pl.BlockSpec((tn, K), lambda j, i: (j, 0))]
        args += [a, b]
    return pl.pallas_call(
        body, name=name, grid=(N // tn, M // tm), in_specs=in_specs,
        out_specs=pl.BlockSpec((tm, tn), lambda j, i: (i, j)),
        out_shape=jax.ShapeDtypeStruct((M, N), out_dtype),
        compiler_params=_cparams(("arbitrary", "arbitrary")),
    )(*args)


def mm_tn(a, b, *, name="mm_tn"):
    M, K = a.shape
    N = b.shape[1]
    tm = _pick(M, (2048, 1024, 512, 256, 128))
    tk = _pick(K, (1408, 1536, 1024, 512, 256, 128))
    tn = _pick(N, (1408, 1536, 1024, 1056, 512, 256, 128))

    def body(a_ref, b_ref, o_ref):
        @pl.when(pl.program_id(2) == 0)
        def _():
            o_ref[...] = jnp.zeros_like(o_ref)

        o_ref[...] += _dg(a_ref[...], b_ref[...], 0, 0, False)

    return pl.pallas_call(
        body, name=name, grid=(K // tk, N // tn, M // tm),
        in_specs=[pl.BlockSpec((tm, tk), lambda k, j, m: (m, k)), pl.BlockSpec((tm, tn), lambda k, j, m: (m, j))],
        out_specs=pl.BlockSpec((tk, tn), lambda k, j, m: (k, j)),
        out_shape=jax.ShapeDtypeStruct((K, N), F32),
        compiler_params=_cparams(("arbitrary", "arbitrary", "arbitrary")),
    )(a, b)


def ffn_up(h, wg, wu, gather=()):
    M, K = h.shape
    N = wg.shape[1]
    tm = _pick(M, (512, 256, 128))
    tn = _pick(N, (1408, 1536, 1024, 512, 256, 128))
    T = len(gather)
    ni = M // tm
    nsteps = (N // tn) * ni
    fwd_step = (3 * nsteps) // 4

    def body(*refs):
        h_ref, wg_ref, wu_ref = refs[:3]
        g_ref, u_ref, a_ref = refs[3 + T:6 + T]
        if T:
            start, forward, finish = _gather_phases(refs[3:3 + T], refs[6 + T:6 + 2 * T], *refs[6 + 2 * T:])
            step = pl.program_id(0) * ni + pl.program_id(1)
            pl.when(step == 0)(start)
        hv = h_ref[...]
        g = _dg(hv, wg_ref[...], 1, 0, False)
        u = _dg(hv, wu_ref[...], 1, 0, False)
        g_ref[...] = g.astype(g_ref.dtype)
        u_ref[...] = u.astype(u_ref.dtype)
        a_ref[...] = (_silu(g) * u).astype(a_ref.dtype)
        if T:
            pl.when(step == fwd_step)(forward)
            pl.when(step == nsteps - 1)(finish)

    wspec = pl.BlockSpec((K, tn), lambda j, i: (0, j))
    ospec = pl.BlockSpec((tm, tn), lambda j, i: (i, j))
    return pl.pallas_call(
        body, name="ffn_up_gather" if T else "ffn_up", grid=(N // tn, ni),
        in_specs=[pl.BlockSpec((tm, K), lambda j, i: (i, 0)), wspec, wspec] + [HBM_SPEC] * T,
        out_specs=[ospec, ospec, ospec] + [HBM_SPEC] * T,
        out_shape=[jax.ShapeDtypeStruct((M, N), BF16)] * 3
        + [jax.ShapeDtypeStruct((N_DEV,) + s.shape, s.dtype) for s in gather],
        scratch_shapes=_gather_scratch(T) if T else [],
        compiler_params=_cparams(("arbitrary", "arbitrary")),
    )(h, wg, wu, *gather)


def ffn_bwd_act(dout, wd, gate, up, exchange=()):
    M, D = dout.shape
    N = wd.shape[0]
    tm = _pick(M, (512, 256, 128))
    tn = _pick(N, (1408, 1536, 1024, 512, 256, 128))
    T = len(exchange)
    ni = M // tm
    nsteps = (N // tn) * ni

    def body(*refs):
        d_ref, w_ref, g_ref, u_ref = refs[:4]
        dg_ref, du_ref = refs[4 + T:6 + T]
        if T:
            start, finish = _exchange_phases(refs[4:4 + T], refs[6 + T:6 + 2 * T], *refs[6 + 2 * T:])
            step = pl.program_id(0) * ni + pl.program_id(1)
            pl.when(step == 0)(start)
        dact = _dg(d_ref[...], w_ref[...], 1, 1, False)
        _, vjp = jax.vjp(lambda g, u: _silu(g) * u, g_ref[...].astype(F32), u_ref[...].astype(F32))
        dg, du = vjp(dact)
        dg_ref[...] = dg.astype(dg_ref.dtype)
        du_ref[...] = du.astype(du_ref.dtype)
        if T:
            pl.when(step == nsteps - 1)(finish)

    ospec = pl.BlockSpec((tm, tn), lambda j, i: (i, j))
    return pl.pallas_call(
        body, name="ffn_bwd_act_exchange" if T else "ffn_bwd_act", grid=(N // tn, ni),
        in_specs=[pl.BlockSpec((tm, D), lambda j, i: (i, 0)), pl.BlockSpec((tn, D), lambda j, i: (j, 0)), ospec, ospec]
        + [HBM_SPEC] * T,
        out_specs=[ospec, ospec] + [HBM_SPEC] * T,
        out_shape=[jax.ShapeDtypeStruct((M, N), BF16), jax.ShapeDtypeStruct((M, N), BF16)]
        + [jax.ShapeDtypeStruct(p.shape, p.dtype) for p in exchange],
        scratch_shapes=_exchange_scratch(T) if T else [],
        compiler_params=_cparams(("arbitrary", "arbitrary")),
    )(dout, wd, gate, up, *exchange)


def _t5_bucket(distance):
    max_exact = NUM_BUCKETS // 2
    n = distance.astype(jnp.float32)
    large = max_exact + (jnp.log(jnp.maximum(n, 1.0) / max_exact)
                         / math.log(MAX_DISTANCE / max_exact) * (NUM_BUCKETS - max_exact))
    large = jnp.minimum(large.astype(jnp.int32), NUM_BUCKETS - 1)
    return jnp.where(distance < max_exact, distance, large)


def _attn_geometry(blk, nb):
    nk = 2 * blk if nb > 1 else blk
    off = blk if nb > 1 else 0
    return nk, off


def _bucket_index(blk, nb, dilation):
    nk, off = _attn_geometry(blk, nb)
    rel = jnp.arange(blk)[:, None] + off - jnp.arange(nk)[None, :]
    return _t5_bucket(jnp.maximum(rel, 0) * dilation)


def _attn_block_pair(q2, kk2, vv2, bias2, valid2, scale):
    nk = kk2.shape[0]
    blk = q2.shape[0]
    s = pdot_nt(q2, kk2) * scale + bias2
    s = jnp.where(valid2, s, NEG_INF)
    ma = jnp.max(s[:, :nk], axis=-1, keepdims=True)
    mb = jnp.max(s[:, nk:], axis=-1, keepdims=True)
    m = lax.stop_gradient(jnp.where(_lane_half(s.shape), ma, mb))
    p = jnp.exp(s - m)
    da = jnp.sum(p[:, :nk], axis=-1, keepdims=True)
    db = jnp.sum(p[:, nk:], axis=-1, keepdims=True)
    first = _lane_half((blk, q2.shape[1]))
    o = pdot_nn(p, vv2) / jnp.where(first, da, db)
    lse = jnp.where(first, lax.stop_gradient(ma) + jnp.log(da), lax.stop_gradient(mb) + jnp.log(db))
    return o, lse


def _attn_valid(blk, nk, off, steps, first):
    q_loc = lax.broadcasted_iota(jnp.int32, (blk, 2 * nk), 0)
    k_loc = lax.broadcasted_iota(jnp.int32, (blk, 2 * nk), 1) % nk
    rel = q_loc + off - k_loc
    valid = (rel >= 0) & (rel <= steps)
    if off:
        valid = valid & ((k_loc >= off) | jnp.logical_not(first))
    return valid


def _sub(ref, r, pp, blk, d):
    if d == 1:
        return ref.at[0, :, pl.ds(pp * LANES, LANES)]
    return ref.at[0, pl.ds(r, blk, stride=d), :]


def _pairs_per_step(d):
    return 1 if d > 1 else A_GROUP_WIDTH // LANES


def _for_units(d, fn):
    for r in range(d):
        for pp in range(_pairs_per_step(d)):
            fn(r, pp)


def attn_fwd(proj, bias2, gi, d, steps):
    B, S, _ = proj.shape
    blk = A_BLOCK
    T = blk * d
    nb = S // T
    nk, off = _attn_geometry(blk, nb)
    scale = A_HEAD_DIM ** -0.5
    PP = _pairs_per_step(d)
    HP = A_GROUP_WIDTH // LANES // PP
    Wb = PP * LANES

    def body(*refs):
        if nb > 1:
            q_ref, kp_ref, kc_ref, vp_ref, vc_ref, b_ref, o_ref, l_ref = refs
        else:
            q_ref, kc_ref, vc_ref, b_ref, o_ref, l_ref = refs
        valid = _attn_valid(blk, nk, off, steps, pl.program_id(2) == 0)

        def one(r, pp):
            sub = lambda ref: _sub(ref, r, pp, blk, d)
            if nb > 1:
                kk = jnp.concatenate([sub(kp_ref)[...], sub(kc_ref)[...]], axis=0)
                vv = jnp.concatenate([sub(vp_ref)[...], sub(vc_ref)[...]], axis=0)
            else:
                kk, vv = sub(kc_ref)[...], sub(vc_ref)[...]
            o, lse = _attn_block_pair(sub(q_ref)[...], kk, vv, b_ref[pp], valid, scale)
            sub(o_ref)[...] = o
            sub(l_ref)[...] = lse

        _for_units(d, one)

    def cur(t):
        return pl.BlockSpec((1, T, Wb), lambda b, h, i: (b, i, (gi * 3 + t) * HP + h))

    def prev(t):
        return pl.BlockSpec((1, T, Wb), lambda b, h, i: (b, jnp.maximum(i - 1, 0), (gi * 3 + t) * HP + h))

    bspec = pl.BlockSpec((PP, blk, 2 * nk), lambda b, h, i: (h, 0, 0))
    out = pl.BlockSpec((1, T, Wb), lambda b, h, i: (b, i, h))
    if nb > 1:
        in_specs, args = [cur(0), prev(1), cur(1), prev(2), cur(2), bspec], [proj] * 5 + [bias2]
    else:
        in_specs, args = [cur(0), cur(1), cur(2), bspec], [proj] * 3 + [bias2]
    osh = jax.ShapeDtypeStruct((B, S, A_GROUP_WIDTH), F32)
    return pl.pallas_call(
        body, name="attn_fwd", grid=(B, HP, nb), in_specs=in_specs, out_specs=[out, out], out_shape=[osh, osh],
        compiler_params=_cparams(("arbitrary", "arbitrary", "arbitrary")),
    )(*args)


def attn_bwd(proj, bias2, do, dlse, gi, d, steps):
    B, S, _ = proj.shape
    blk = A_BLOCK
    T = blk * d
    nb = S // T
    nk, off = _attn_geometry(blk, nb)
    scale = A_HEAD_DIM ** -0.5
    PP = _pairs_per_step(d)
    HP = A_GROUP_WIDTH // LANES // PP
    Wb = PP * LANES
    nsteps = nb + 1 if nb > 1 else 1

    def body(*refs):
        if nb > 1:
            (q_ref, kp_ref, kc_ref, vp_ref, vc_ref, b_ref, do_ref, dl_ref,
             dq_ref, dk_ref, dv_ref, db_ref, ck, cv) = refs
        else:
            q_ref, kc_ref, vc_ref, b_ref, do_ref, dl_ref, dq_ref, dk_ref, dv_ref, db_ref = refs
        b = pl.program_id(0)
        h = pl.program_id(1)
        i = pl.program_id(2)

        @pl.when((b == 0) & (h == 0) & (i == 0))
        def _():
            db_ref[...] = jnp.zeros_like(db_ref)

        if nb > 1:
            @pl.when(i == 0)
            def _():
                ck[...] = jnp.zeros_like(ck)
                cv[...] = jnp.zeros_like(cv)

        def work():
            valid = _attn_valid(blk, nk, off, steps, i == 0)

            def one(r, pp):
                sub = lambda ref: _sub(ref, r, pp, blk, d)
                if nb > 1:
                    kk = jnp.concatenate([sub(kp_ref)[...], sub(kc_ref)[...]], axis=0)
                    vv = jnp.concatenate([sub(vp_ref)[...], sub(vc_ref)[...]], axis=0)
                else:
                    kk, vv = sub(kc_ref)[...], sub(vc_ref)[...]
                _, vjp = jax.vjp(lambda a, c, e, f: _attn_block_pair(a, c, e, f, valid, scale),
                                 sub(q_ref)[...], kk, vv, b_ref[h * PP + pp])
                dq, dkk, dvv, dbias = vjp((sub(do_ref)[...], sub(dl_ref)[...]))
                sub(dq_ref)[...] = dq
                db_ref[h * PP + pp] += dbias
                if nb > 1:
                    idx = (slice(None), pl.ds(pp * LANES, LANES)) if d == 1 else (pl.ds(r, blk, stride=d), slice(None))
                    sub(dk_ref)[...] = ck[idx] + dkk[:blk]
                    sub(dv_ref)[...] = cv[idx] + dvv[:blk]
                    ck[idx] = dkk[blk:]
                    cv[idx] = dvv[blk:]
                else:
                    sub(dk_ref)[...] = dkk
                    sub(dv_ref)[...] = dvv

            _for_units(d, one)

        if nb > 1:
            pl.when(i < nb)(work)

            @pl.when(i == nb)
            def _():
                dk_ref[0] = ck[...]
                dv_ref[0] = cv[...]
        else:
            work()

    last = nb - 1

    def cur(t):
        return pl.BlockSpec((1, T, Wb), lambda b, h, i: (b, jnp.minimum(i, last), (gi * 3 + t) * HP + h))

    def prev(t):
        return pl.BlockSpec((1, T, Wb), lambda b, h, i: (b, jnp.clip(i - 1, 0, last), (gi * 3 + t) * HP + h))

    ocur = pl.BlockSpec((1, T, Wb), lambda b, h, i: (b, jnp.minimum(i, last), h))
    oprev = pl.BlockSpec((1, T, Wb), lambda b, h, i: (b, jnp.clip(i - 1, 0, last), h))
    bspec = pl.BlockSpec((HP * PP, blk, 2 * nk), lambda b, h, i: (0, 0, 0))
    if nb > 1:
        in_specs, args = [cur(0), prev(1), cur(1), prev(2), cur(2), bspec, ocur, ocur], [proj] * 5 + [bias2, do, dlse]
        out_specs = [ocur, oprev, oprev, bspec]
        scratch = [pltpu.VMEM((T, Wb), F32), pltpu.VMEM((T, Wb), F32)]
    else:
        in_specs, args = [cur(0), cur(1), cur(2), bspec, ocur, ocur], [proj] * 3 + [bias2, do, dlse]
        out_specs = [ocur, ocur, ocur, bspec]
        scratch = []
    osh = jax.ShapeDtypeStruct((B, S, A_GROUP_WIDTH), F32)
    return pl.pallas_call(
        body, name="attn_bwd", grid=(B, HP, nsteps), in_specs=in_specs, out_specs=out_specs,
        out_shape=[osh, osh, osh, jax.ShapeDtypeStruct((HP * PP, blk, 2 * nk), F32)],
        scratch_shapes=scratch,
        compiler_params=_cparams(("arbitrary", "arbitrary", "arbitrary")),
    )(*args)


def _merge(o0, o1, o2, l0, l1, l2):
    m = lax.stop_gradient(jnp.maximum(jnp.maximum(l0, l1), l2))
    e0, e1, e2 = jnp.exp(l0 - m), jnp.exp(l1 - m), jnp.exp(l2 - m)
    inv = 1.0 / (e0 + e1 + e2)
    return (e0 * inv) * o0 + (e1 * inv) * o1 + (e2 * inv) * o2


def merge_fwd(os_, ls, tm):
    M, W = os_[0].shape

    def body(o0, o1, o2, l0, l1, l2, out):
        out[...] = _merge(o0[...], o1[...], o2[...], l0[...], l1[...], l2[...]).astype(out.dtype)

    row = pl.BlockSpec((tm, W), lambda i: (i, 0))
    return pl.pallas_call(
        body, name="merge_fwd", grid=(M // tm,), in_specs=[row] * 6, out_specs=row,
        out_shape=jax.ShapeDtypeStruct((M, W), BF16), compiler_params=_cparams(("arbitrary",)),
    )(*os_, *ls)


def merge_bwd(os_, ls, do, tm):
    M, W = os_[0].shape

    def body(o0, o1, o2, l0, l1, l2, d_ref, do0, do1, do2, dl0, dl1, dl2):
        _, vjp = jax.vjp(_merge, o0[...], o1[...], o2[...], l0[...], l1[...], l2[...])
        for r, val in zip((do0, do1, do2, dl0, dl1, dl2), vjp(d_ref[...])):
            r[...] = val

    row = pl.BlockSpec((tm, W), lambda i: (i, 0))
    sh = jax.ShapeDtypeStruct((M, W), F32)
    return pl.pallas_call(
        body, name="merge_bwd", grid=(M // tm,), in_specs=[row] * 7, out_specs=[row] * 6, out_shape=[sh] * 6,
        compiler_params=_cparams(("arbitrary",)),
    )(*os_, *ls, do)


def _split3(x):
    hi = x.astype(BF16)
    r = x - hi.astype(F32)
    mid = r.astype(BF16)
    lo = (r - mid.astype(F32)).astype(BF16)
    return hi, mid, lo


def bias_table(onehot, rb):
    NB, Q = onehot.shape
    H = rb.shape[1]

    def body(oh_ref, rb_ref, o_ref):
        oh = oh_ref[...]
        acc = jnp.zeros((H, Q), F32)
        for part in _split3(rb_ref[...]):
            acc = acc + _dg(part, oh, 0, 0, False)
        o_ref[...] = acc

    full = lambda s: pl.BlockSpec(s, lambda: (0,) * len(s))
    return pl.pallas_call(
        body, name="bias_table", in_specs=[full((NB, Q)), full((NB, H))], out_specs=full((H, Q)),
        out_shape=jax.ShapeDtypeStruct((H, Q), F32), compiler_params=_cparams(),
    )(onehot, rb)


def bias_bucket_grad(onehot, dbias):
    NB, Q = onehot.shape
    H = dbias.shape[0]

    def body(oh_ref, d_ref, o_ref):
        oh = oh_ref[...]
        acc = jnp.zeros((NB, H), F32)
        for part in _split3(d_ref[...]):
            acc = acc + _dg(oh, part, 1, 1, False)
        o_ref[...] = acc

    full = lambda s: pl.BlockSpec(s, lambda: (0,) * len(s))
    return pl.pallas_call(
        body, name="bias_bucket_grad", in_specs=[full((NB, Q)), full((H, Q))], out_specs=full((NB, H)),
        out_shape=jax.ShapeDtypeStruct((NB, H), F32), compiler_params=_cparams(),
    )(onehot, dbias)


def _shift_rows(x, k, S):
    t = lax.broadcasted_iota(jnp.int32, x.shape, 0)
    if k >= 0:
        return jnp.where(t >= k, pltpu.roll(x, k, 0), 0.0)
    return jnp.where(t < S + k, pltpu.roll(x, S + k, 0), 0.0)


def conv_fwd(x, w, ncol, cb):
    B, S, _ = x.shape

    def body(x_ref, w_ref, o_ref):
        xv = x_ref[0]
        wv = w_ref[...]
        acc = xv * wv[B_CONV - 1:B_CONV]
        for k in range(1, B_CONV):
            acc = acc + _shift_rows(xv, k, S) * wv[B_CONV - 1 - k:B_CONV - k]
        o_ref[0] = acc

    blk = pl.BlockSpec((1, S, cb), lambda b, j: (b, 0, j))
    return pl.pallas_call(
        body, name="conv_fwd", grid=(B, ncol // cb),
        in_specs=[blk, pl.BlockSpec((B_CONV, cb), lambda b, j: (0, j))], out_specs=blk,
        out_shape=jax.ShapeDtypeStruct((B, S, ncol), F32),
        compiler_params=_cparams(("arbitrary", "arbitrary")),
    )(x, w)


def conv_bwd(x, w, dc, col0, cb):
    B, S, C = dc.shape
    j0 = col0 // cb

    def body(x_ref, w_ref, dc_ref, dx_ref, dw_ref):
        xv = x_ref[0]
        wv = w_ref[...]
        d = dc_ref[0]
        acc = d * wv[B_CONV - 1:B_CONV]
        rows = [jnp.sum(d * xv, axis=0, keepdims=True)]
        for k in range(1, B_CONV):
            acc = acc + _shift_rows(d, -k, S) * wv[B_CONV - 1 - k:B_CONV - k]
            rows.append(jnp.sum(d * _shift_rows(xv, k, S), axis=0, keepdims=True))
        dx_ref[0] = acc.astype(dx_ref.dtype)

        @pl.when(pl.program_id(1) == 0)
        def _():
            dw_ref[...] = jnp.zeros_like(dw_ref)

        dw_ref[...] += jnp.concatenate(rows[::-1], axis=0)

    return pl.pallas_call(
        body, name="conv_bwd", grid=(C // cb, B),
        in_specs=[pl.BlockSpec((1, S, cb), lambda j, b: (b, 0, j + j0)),
                  pl.BlockSpec((B_CONV, cb), lambda j, b: (0, j)),
                  pl.BlockSpec((1, S, cb), lambda j, b: (b, 0, j))],
        out_specs=[pl.BlockSpec((1, S, cb), lambda j, b: (b, 0, j)), pl.BlockSpec((B_CONV, cb), lambda j, b: (0, j))],
        out_shape=[jax.ShapeDtypeStruct((B, S, C), BF16), jax.ShapeDtypeStruct((B_CONV, C), F32)],
        compiler_params=_cparams(("arbitrary", "arbitrary")),
    )(x, w, dc)


def _lane_half(shape):
    return lax.broadcasted_iota(jnp.int32, shape, len(shape) - 1) < shape[-1] // 2


def _bd(xw):
    first = _lane_half(xw.shape)
    return jnp.concatenate([jnp.where(first, xw, 0.0), jnp.where(first, 0.0, xw)], axis=0)


def _unbd(y):
    r = y.shape[0] // 2
    return jnp.where(_lane_half((r, y.shape[1])), y[:r], y[r:])


def _rs(xw):
    return jnp.concatenate([xw[:, :LANES], xw[:, LANES:]], axis=0)


def _unrs(y):
    r = y.shape[0] // 2
    return jnp.concatenate([y[:r], y[r:]], axis=1)


def _make_pair_dots(hp):
    @jax.custom_vjp
    def nn(xw, pw):
        return _dg(xw, _bd(pw), 1, 0, hp)

    def nn_f(xw, pw):
        return nn(xw, pw), (xw, pw)

    def nn_b(r, ct):
        xw, pw = r
        return _dg(ct, _bd(pw), 1, 1, hp), _unbd(_dg(xw, ct, 0, 0, hp))

    nn.defvjp(nn_f, nn_b)

    @jax.custom_vjp
    def nt(xw, yw):
        return _dg(xw, _bd(yw), 1, 1, hp)

    def nt_f(xw, yw):
        return nt(xw, yw), (xw, yw)

    def nt_b(r, ct):
        xw, yw = r
        return _dg(ct, _bd(yw), 1, 0, hp), _unbd(_dg(ct, xw, 0, 0, hp))

    nt.defvjp(nt_f, nt_b)

    @jax.custom_vjp
    def tn(xw, yw):
        return _dg(_rs(xw), _bd(yw), 0, 0, hp)

    def tn_f(xw, yw):
        return tn(xw, yw), (xw, yw)

    def tn_b(r, ct):
        xw, yw = r
        return _unrs(_dg(_bd(yw), ct, 1, 1, hp)), _unbd(_dg(_rs(xw), ct, 1, 0, hp))

    tn.defvjp(tn_f, tn_b)
    return nn, nt, tn


pdot_nn, pdot_nt, pdot_tn = _make_pair_dots(False)
phdot_nn, phdot_nt, phdot_tn = _make_pair_dots(True)


@jax.custom_vjp
def _tril_cumsum(gw):
    C = gw.shape[0]
    r = lax.broadcasted_iota(jnp.int32, (C, C), 0)
    c = lax.broadcasted_iota(jnp.int32, (C, C), 1)
    return _dg(jnp.where(r >= c, 1.0, 0.0), gw, 1, 0, True)


def _tril_cumsum_f(gw):
    return _tril_cumsum(gw), gw.shape[0]


def _tril_cumsum_b(C, ct):
    r = lax.broadcasted_iota(jnp.int32, (C, C), 0)
    c = lax.broadcasted_iota(jnp.int32, (C, C), 1)
    return (_dg(jnp.where(r <= c, 1.0, 0.0), ct, 1, 0, True),)


_tril_cumsum.defvjp(_tril_cumsum_f, _tril_cumsum_b)


def _pair(fa, fb, shape):
    return jnp.where(_lane_half(shape), fa, fb)


def _half_sums(xw):
    sa = jnp.sum(xw[:, :LANES], axis=-1, keepdims=True)
    sb = jnp.sum(xw[:, LANES:], axis=-1, keepdims=True)
    return _pair(sa, sb, xw.shape)


def _neumann_inverse_pair(low):
    C = low.shape[0]
    r = lax.broadcasted_iota(jnp.int32, low.shape, 0)
    c = lax.broadcasted_iota(jnp.int32, low.shape, 1) % LANES
    x = jnp.where(r == c, 1.0, 0.0) - low
    p = phdot_nn(low, low)
    n = 2
    while n < C:
        x = x + phdot_nn(x, p)
        n *= 2
        if n < C:
            p = phdot_nn(p, p)
    return x


@jax.custom_vjp
def _saved_inverse(low, tinv):
    return tinv


def _saved_inverse_f(low, tinv):
    return tinv, tinv


def _saved_inverse_b(tinv, ct):
    return -phdot_nt(phdot_tn(tinv, ct), tinv), jnp.zeros_like(tinv)


_saved_inverse.defvjp(_saved_inverse_f, _saved_inverse_b)


def _gdn_chunk_pair(cq, ck, cv, z, ab, prm_a, prm_b, nw, state, h0, tinv_saved=None):
    C, W2 = cq.shape
    lane = lax.broadcasted_iota(jnp.int32, ab.shape, 1)
    col = lambda j: jnp.sum(jnp.where(lane == j, ab, 0.0), axis=1, keepdims=True)
    a = _pair(col(h0), col(h0 + 1), (C, W2))
    b = _pair(col(h0 + B_HEADS), col(h0 + 1 + B_HEADS), (C, W2))
    a_log = _pair(prm_a[:, 0:1], prm_b[:, 0:1], (1, W2))
    dtb = _pair(prm_a[:, 1:2], prm_b[:, 1:2], (1, W2))
    nw2 = jnp.concatenate([nw, nw], axis=1)
    q = _silu(cq)
    q = q * lax.rsqrt(_half_sums(q * q) + NORM_EPS) * (B_HEAD_DIM ** -0.5)
    k = _silu(ck)
    k = k * lax.rsqrt(_half_sums(k * k) + NORM_EPS)
    v = _silu(cv)
    beta = _sigmoid(b)
    g = -jnp.exp(a_log) * _softplus(a + dtb)

    r = lax.broadcasted_iota(jnp.int32, (C, W2), 0)
    c = lax.broadcasted_iota(jnp.int32, (C, W2), 1) % LANES
    tril = r >= c
    gc = _tril_cumsum(g)
    gcj = jnp.concatenate([gc[:, :LANES].T, gc[:, LANES:].T], axis=1)
    decay = jnp.where(tril, jnp.exp(jnp.where(tril, gc - gcj, 0.0)), 0.0)
    gl = gc[C - 1:C, :]
    egc = jnp.exp(gc)

    kb = k * beta
    kk = pdot_nt(jnp.concatenate([kb, q], axis=0), k)
    low = jnp.where(r > c, kk[:C] * decay, 0.0)
    attn = kk[C:] * decay
    tinv = _neumann_inverse_pair(low) if tinv_saved is None else _saved_inverse(low, tinv_saved)
    u = pdot_nn(tinv, v * beta)
    w = pdot_nn(tinv, kb * egc)
    ws = pdot_nn(jnp.concatenate([w, q * egc], axis=0), state)
    v_new = u - ws[:C]
    o = ws[C:] + pdot_nn(attn, v_new)
    new_state = state * jnp.exp(gl) + pdot_tn(k * jnp.exp(gl - gc), v_new)
    y = o * lax.rsqrt(_half_sums(o * o) * (1.0 / B_HEAD_DIM) + NORM_EPS) * nw2 * _silu(z)
    if tinv_saved is None:
        return y, new_state, tinv
    return y, new_state


def gdn_fwd(c, proj, prm, nw):
    B, S, _ = c.shape
    H, dh, C = B_HEADS, B_HEAD_DIM, B_CHUNK
    N, HP, W2 = S // C, H // 2, 2 * dh

    def body(cq_ref, ck_ref, cv_ref, z_ref, ab_ref, prm_ref, nw_ref, y_ref, st_ref, ti_ref, state):
        h0 = pl.program_id(1) * 2
        state[...] = jnp.zeros_like(state)
        nwv = nw_ref[...]

        def step(n, carry):
            rows = pl.ds(pl.multiple_of(n * C, C), C)
            s_in = state[...]
            st_ref[0, n] = s_in
            y, s_out, tinv = _gdn_chunk_pair(cq_ref[0, rows, :], ck_ref[0, rows, :], cv_ref[0, rows, :],
                                             z_ref[0, rows, :], ab_ref[0, rows, :], prm_ref[h0], prm_ref[h0 + 1],
                                             nwv, s_in, h0)
            ti_ref[0, n] = tinv
            y_ref[0, rows, :] = y.astype(y_ref.dtype)
            state[...] = s_out
            return carry

        lax.fori_loop(0, N, step, 0)

    def col(off):
        return pl.BlockSpec((1, S, W2), lambda b, h: (b, 0, h + off // 2))

    return pl.pallas_call(
        body, name="gdn_fwd", grid=(B, HP),
        in_specs=[col(0), col(H), col(2 * H), col(3 * H), pl.BlockSpec((1, S, B_AB_PAD), lambda b, h: (b, 0, 4 * H)),
                  pl.BlockSpec((H, 1, LANES), lambda b, h: (0, 0, 0)), pl.BlockSpec((1, dh), lambda b, h: (0, 0))],
        out_specs=[col(0), pl.BlockSpec((1, N, dh, W2), lambda b, h: (b * HP + h, 0, 0, 0)),
                   pl.BlockSpec((1, N, C, W2), lambda b, h: (b * HP + h, 0, 0, 0))],
        out_shape=[jax.ShapeDtypeStruct((B, S, H * dh), BF16), jax.ShapeDtypeStruct((B * HP, N, dh, W2), F32),
                   jax.ShapeDtypeStruct((B * HP, N, C, W2), F32)],
        scratch_shapes=[pltpu.VMEM((dh, W2), F32)],
        compiler_params=_cparams(("arbitrary", "arbitrary")),
    )(c, c, c, proj, proj, prm, nw)


def gdn_bwd(c, proj, prm, nw, states, tinvs, dy):
    B, S, _ = c.shape
    H, dh, C = B_HEADS, B_HEAD_DIM, B_CHUNK
    N, HP, W2 = S // C, H // 2, 2 * dh

    def body(cq_ref, ck_ref, cv_ref, z_ref, ab_ref, prm_ref, nw_ref, st_ref, ti_ref, dy_ref,
             dq_ref, dk_ref, dv_ref, dz_ref, dab_ref, dprm_ref, dnw_ref, dstate):
        b = pl.program_id(0)
        hp = pl.program_id(1)
        h0 = hp * 2
        dstate[...] = jnp.zeros_like(dstate)

        @pl.when((b == 0) & (hp == 0))
        def _():
            dprm_ref[...] = jnp.zeros_like(dprm_ref)
            dnw_ref[...] = jnp.zeros_like(dnw_ref)

        @pl.when(hp == 0)
        def _():
            dab_ref[...] = jnp.zeros_like(dab_ref)

        nwv = nw_ref[...]

        def step(t, carry):
            n = N - 1 - t
            rows = pl.ds(pl.multiple_of(n * C, C), C)
            _, vjp = jax.vjp(functools.partial(_gdn_chunk_pair, h0=h0, tinv_saved=ti_ref[0, n]), cq_ref[0, rows, :],
                             ck_ref[0, rows, :], cv_ref[0, rows, :], z_ref[0, rows, :], ab_ref[0, rows, :],
                             prm_ref[h0], prm_ref[h0 + 1], nwv, st_ref[0, n])
            dcq, dck, dcv, dz, dab, dpa, dpb, dnw, ds = vjp((dy_ref[0, rows, :].astype(F32), dstate[...]))
            dq_ref[0, rows, :] = dcq
            dk_ref[0, rows, :] = dck
            dv_ref[0, rows, :] = dcv
            dz_ref[0, rows, :] = dz.astype(dz_ref.dtype)
            dab_ref[0, rows, :] += dab
            dprm_ref[h0] += dpa
            dprm_ref[h0 + 1] += dpb
            dnw_ref[...] += dnw
            dstate[...] = ds
            return carry

        lax.fori_loop(0, N, step, 0)

    def col(off):
        return pl.BlockSpec((1, S, W2), lambda b, h: (b, 0, h + off // 2))

    abspec = pl.BlockSpec((1, S, B_AB_PAD), lambda b, h: (b, 0, 4 * H))
    ab0 = pl.BlockSpec((1, S, B_AB_PAD), lambda b, h: (b, 0, 0))
    prm_full = pl.BlockSpec((H, 1, LANES), lambda b, h: (0, 0, 0))
    nwspec = pl.BlockSpec((1, dh), lambda b, h: (0, 0))
    wsh = jax.ShapeDtypeStruct((B, S, H * dh), F32)
    return pl.pallas_call(
        body, name="gdn_bwd", grid=(B, HP),
        in_specs=[col(0), col(H), col(2 * H), col(3 * H), abspec, prm_full, nwspec,
                  pl.BlockSpec((1, N, dh, W2), lambda b, h: (b * HP + h, 0, 0, 0)),
                  pl.BlockSpec((1, N, C, W2), lambda b, h: (b * HP + h, 0, 0, 0)), col(0)],
        out_specs=[col(0), col(0), col(0), col(0), ab0, prm_full, nwspec],
        out_shape=[wsh, wsh, wsh, jax.ShapeDtypeStruct((B, S, H * dh), BF16),
                   jax.ShapeDtypeStruct((B, S, B_AB_PAD), F32), jax.ShapeDtypeStruct((H, 1, LANES), F32),
                   jax.ShapeDtypeStruct((1, dh), F32)],
        scratch_shapes=[pltpu.VMEM((dh, W2), F32)],
        compiler_params=_cparams(("arbitrary", "arbitrary"), VMEM_LIMIT_BIG),
    )(c, c, c, proj, proj, prm, nw, states, tinvs, dy)


def _my_place():
    return lax.axis_index("x"), lax.axis_index("y"), lax.axis_index("c")


HBM_SPEC = pl.BlockSpec(memory_space=pltpu.HBM)


def _gather_phases(x_refs, out_refs, send_sems, recv_sems, local_sems):
    T = len(x_refs)
    x, y, c = _my_place()
    me, sibling = (x, y, c), (x, y, 1 - c)
    chips = [(1 - x, y), (x, 1 - y), (1 - x, 1 - y)]

    def copy(t, k, block, to, src=None):
        px, py, pc = block
        slot = out_refs[t].at[4 * px + 2 * py + pc]
        return pltpu.make_async_remote_copy(
            src_ref=slot if src is None else src, dst_ref=slot, send_sem=send_sems.at[7 * t + k],
            recv_sem=recv_sems.at[7 * t + k], device_id=to, device_id_type=MESH)

    def local(t):
        return pltpu.make_async_copy(x_refs[t], out_refs[t].at[4 * x + 2 * y + c], local_sems.at[t])

    def first(t):
        return [copy(t, 0, me, sibling, src=x_refs[t])] + [
            copy(t, 1 + j, me, (*chip, c), src=x_refs[t]) for j, chip in enumerate(chips)]

    def start():
        for t in range(T):
            local(t).start()
            for cp in first(t):
                cp.start()

    def forward():
        for j, chip in enumerate(chips):
            for t in range(T):
                copy(t, 1 + j, (*chip, c), me).wait_recv()
                copy(t, 4 + j, (*chip, c), sibling).start()

    def finish():
        for t in range(T):
            copy(t, 0, sibling, me).wait_recv()
            for j, chip in enumerate(chips):
                copy(t, 4 + j, (*chip, 1 - c), me).wait_recv()
        for t in range(T):
            for cp in first(t):
                cp.wait_send()
            for j, chip in enumerate(chips):
                copy(t, 4 + j, (*chip, c), sibling).wait_send()
            local(t).wait()

    return start, forward, finish


def _gather_scratch(T):
    return [pltpu.SemaphoreType.DMA((7 * T,)), pltpu.SemaphoreType.DMA((7 * T,)), pltpu.SemaphoreType.DMA((T,))]


def allgather_shards(shards):
    T = len(shards)

    def body(*refs):
        start, forward, finish = _gather_phases(refs[:T], refs[T:2 * T], *refs[2 * T:])
        start()
        forward()
        finish()

    return pl.pallas_call(
        body, name="allgather_shards",
        out_shape=[jax.ShapeDtypeStruct((N_DEV,) + s.shape, s.dtype) for s in shards],
        in_specs=[HBM_SPEC] * T, out_specs=[HBM_SPEC] * T, scratch_shapes=_gather_scratch(T),
    )(*shards)


def exchange_sibling(gs):
    T = len(gs)

    def body(*refs):
        g_refs, out_refs = refs[:T], refs[T:2 * T]
        send_sems, recv_sems = refs[2 * T:]
        x, y, c = _my_place()
        copies = []
        for t in range(T):
            for k in range(4):
                copies.append(pltpu.make_async_remote_copy(
                    src_ref=g_refs[t].at[2 * k + (1 - c)], dst_ref=out_refs[t].at[k],
                    send_sem=send_sems.at[4 * t + k], recv_sem=recv_sems.at[4 * t + k],
                    device_id=(x, y, 1 - c), device_id_type=MESH))
        for cp in copies:
            cp.start()
        for cp in copies:
            cp.wait()

    return pl.pallas_call(
        body, name="exchange_sibling",
        out_shape=[jax.ShapeDtypeStruct((4,) + g.shape[1:], g.dtype) for g in gs],
        in_specs=[HBM_SPEC] * T, out_specs=[HBM_SPEC] * T,
        scratch_shapes=[pltpu.SemaphoreType.DMA((4 * T,)), pltpu.SemaphoreType.DMA((4 * T,))],
    )(*gs)


def _row_tile(rows, row_bytes, budget):
    if rows * row_bytes <= budget:
        return rows
    best = None
    for t in range(16, rows, 16):
        if rows % t == 0 and t * row_bytes <= budget:
            best = t
    return best if best is not None else rows


def add_sibling(g, got, c_idx):
    _, R, L = g.shape
    tr = _row_tile(R, L * 4, 2 << 20)

    def body(c_ref, g_ref, r_ref, o_ref):
        o_ref[...] = (g_ref[...].astype(F32) + r_ref[...].astype(F32)).astype(o_ref.dtype)

    grid_spec = pltpu.PrefetchScalarGridSpec(
        num_scalar_prefetch=1, grid=(4, R // tr),
        in_specs=[pl.BlockSpec((1, tr, L), lambda k, i, c: (2 * k + c[0], i, 0)),
                  pl.BlockSpec((1, tr, L), lambda k, i, c: (k, i, 0))],
        out_specs=pl.BlockSpec((1, tr, L), lambda k, i, c: (k, i, 0)))
    return pl.pallas_call(
        body, name="add_sibling", grid_spec=grid_spec, out_shape=jax.ShapeDtypeStruct((4, R, L), RS_DTYPE),
        compiler_params=_cparams(("arbitrary", "arbitrary")),
    )(c_idx, g, got)


def _exchange_phases(p_refs, out_refs, send_sems, recv_sems, local_sems):
    T = len(p_refs)
    x, y, c = _my_place()
    mychip = 2 * x + y
    chips = [(1 - x, y), (x, 1 - y), (1 - x, 1 - y)]

    def local(t):
        return pltpu.make_async_copy(p_refs[t].at[mychip], out_refs[t].at[mychip], local_sems.at[t])

    def send(t, j):
        px, py = chips[j]
        return pltpu.make_async_remote_copy(
            src_ref=p_refs[t].at[2 * px + py], dst_ref=out_refs[t].at[mychip], send_sem=send_sems.at[3 * t + j],
            recv_sem=recv_sems.at[3 * t + j], device_id=(px, py, c), device_id_type=MESH)

    def landed(t, j):
        px, py = chips[j]
        return pltpu.make_async_remote_copy(
            src_ref=p_refs[t].at[mychip], dst_ref=out_refs[t].at[2 * px + py], send_sem=send_sems.at[3 * t + j],
            recv_sem=recv_sems.at[3 * t + j], device_id=(px, py, c), device_id_type=MESH)

    def start():
        for t in range(T):
            local(t).start()
            for j in range(3):
                send(t, j).start()

    def finish():
        for t in range(T):
            for j in range(3):
                landed(t, j).wait_recv()
        for t in range(T):
            for j in range(3):
                send(t, j).wait_send()
            local(t).wait()

    return start, finish


def _exchange_scratch(T):
    return [pltpu.SemaphoreType.DMA((3 * T,)), pltpu.SemaphoreType.DMA((3 * T,)), pltpu.SemaphoreType.DMA((T,))]


def exchange_chips(ps):
    T = len(ps)

    def body(*refs):
        start, finish = _exchange_phases(refs[:T], refs[T:2 * T], *refs[2 * T:])
        start()
        finish()

    return pl.pallas_call(
        body, name="exchange_chips", out_shape=[jax.ShapeDtypeStruct(p.shape, p.dtype) for p in ps],
        in_specs=[HBM_SPEC] * T, out_specs=[HBM_SPEC] * T, scratch_shapes=_exchange_scratch(T),
    )(*ps)


def allgather_small(s):
    R, W = s.shape

    def body(s_ref, out_ref, send_sems, recv_sems):
        x, y, c = _my_place()
        me = 4 * x + 2 * y + c
        out_ref[me] = s_ref[...]
        copies = []
        for j in range(1, N_DEV):
            peer = (x ^ (j >> 2), y ^ ((j >> 1) & 1), c ^ (j & 1))
            copies.append(pltpu.make_async_remote_copy(
                src_ref=s_ref, dst_ref=out_ref.at[me], send_sem=send_sems.at[j - 1], recv_sem=recv_sems.at[j - 1],
                device_id=peer, device_id_type=MESH))
        for cp in copies:
            cp.start()
        for j in range(1, N_DEV):
            px, py, pc = x ^ (j >> 2), y ^ ((j >> 1) & 1), c ^ (j & 1)
            pltpu.make_async_remote_copy(
                src_ref=s_ref, dst_ref=out_ref.at[4 * px + 2 * py + pc], send_sem=send_sems.at[j - 1],
                recv_sem=recv_sems.at[j - 1], device_id=(px, py, pc), device_id_type=MESH).wait_recv()
        for cp in copies:
            cp.wait_send()

    vm = pl.BlockSpec(memory_space=pltpu.VMEM)
    return pl.pallas_call(
        body, name="allgather_small", out_shape=jax.ShapeDtypeStruct((N_DEV, R, W), s.dtype),
        in_specs=[vm], out_specs=vm,
        scratch_shapes=[pltpu.SemaphoreType.DMA((7,)), pltpu.SemaphoreType.DMA((7,))],
    )(s)


def _adamw(w, g, m, v):
    m = ADAM_B1 * m + (1.0 - ADAM_B1) * g
    v = ADAM_B2 * v + (1.0 - ADAM_B2) * jnp.square(g)
    m_hat = m / (1.0 - ADAM_B1 ** ADAM_STEP)
    v_hat = v / (1.0 - ADAM_B2 ** ADAM_STEP)
    delta = -ADAM_LR * (m_hat / (jnp.sqrt(v_hat) + ADAM_EPS) + ADAM_WD * w)
    return delta, m, v


def sum_adamw(parts, w, m, v):
    P, G, Rp, Lp = parts.shape
    _, R, L = w.shape
    tr = _row_tile(R, P * Lp * parts.dtype.itemsize + 7 * L * 4, 5 << 20)

    def body(p_ref, w_ref, m_ref, v_ref, g_ref, d_ref, nm_ref, nv_ref):
        g = p_ref[0, 0].astype(F32)
        for k in range(1, P):
            g = g + p_ref[k, 0].astype(F32)
        g = g[:, :L]
        d, nm, nv = _adamw(w_ref[0], g, m_ref[0], v_ref[0])
        g_ref[0] = g
        d_ref[0] = d
        nm_ref[0] = nm
        nv_ref[0] = nv

    row = pl.BlockSpec((1, tr, L), lambda gi, i: (gi, i, 0))
    sh = jax.ShapeDtypeStruct((G, R, L), F32)
    return pl.pallas_call(
        body, name="sum_adamw", grid=(G, R // tr),
        in_specs=[pl.BlockSpec((P, 1, tr, Lp), lambda gi, i: (0, gi, i, 0)), row, row, row],
        out_specs=[row] * 4, out_shape=[sh] * 4, compiler_params=_cparams(("arbitrary", "arbitrary")),
    )(parts, w, m, v)


def _unblock(t, axis):
    sh = t.shape[1:]
    t = jnp.moveaxis(t, 0, axis)
    return t.reshape(sh[:axis] + (N_DEV * sh[axis],) + sh[axis + 1:])


def _block(a, axis):
    sh = a.shape
    t = a.reshape(sh[:axis] + (N_DEV, sh[axis] // N_DEV) + sh[axis + 1:])
    return jnp.moveaxis(t, axis, 0)


def _rows3(t):
    return t.reshape((t.shape[0], -1, t.shape[-1]))


def _pad_to(a, axis, n):
    pad = [(0, 0)] * a.ndim
    pad[axis] = (0, n - a.shape[axis])
    return jnp.pad(a, pad)


def _lane_pad(n):
    return -(-n // LANES) * LANES


def _pack(arrs, row_mult):
    flat = jnp.concatenate([a.astype(F32).reshape(-1) for a in arrs])
    n = flat.shape[0]
    rows = -(-n // LANES)
    rows = -(-rows // row_mult) * row_mult
    return jnp.pad(flat, (0, rows * LANES - n)).reshape(rows, LANES)


def _unpack(buf, shapes):
    flat = buf.reshape(-1)
    out, off = [], 0
    for sh in shapes:
        n = math.prod(sh)
        out.append(flat[off:off + n].reshape(sh))
        off += n
    return out


def _ffn_fwd(x, h, wg, wu, wd, next_g, gather=()):
    gate, up, act, *gathered = ffn_up(h, wg, wu, gather)
    x_new, h_next = mm_nn(act, wd, res=x, res_scale=MACARON_WEIGHT, norm_g=next_g, name="ffn_down")
    return x_new, h_next, (x, h, gate, up, act), gathered


def _ffn_bwd(dx, dout, saved, g, wg, wu, wd, tm, next_scale, exchange=()):
    x, h, gate, up, act = saved
    dgate, dup, *received = ffn_bwd_act(dout, wd, gate, up, exchange)
    dwd = mm_tn(act, dout, name="ffn_dwd")
    dwg = mm_tn(h, dgate, name="ffn_dwg")
    dwu = mm_tn(h, dup, name="ffn_dwu")
    dh = mm_nt([(dgate, wg), (dup, wu)], name="ffn_dh")
    dx_in, dxb, dg = rmsnorm_bwd(x, g, dh, dx, tm, next_scale)
    return dx_in, dxb, dg, dwg, dwu, dwd, received


def _bucket_onehot(nb, d):
    idx = _bucket_index(A_BLOCK, nb, d)
    onehot = (idx.reshape(-1)[None, :] == jnp.arange(NUM_BUCKETS)[:, None]).astype(BF16)
    return onehot, idx.shape


def _pair_tables(t, blk, nk):
    t = t.reshape(A_HEADS // 2, 2, blk, nk)
    return jnp.transpose(t, (0, 2, 1, 3)).reshape(A_HEADS // 2, blk, 2 * nk)


def _unpair_tables(t, blk, nk):
    t = t.reshape(A_HEADS // 2, blk, 2, nk)
    return jnp.transpose(t, (0, 2, 1, 3)).reshape(A_HEADS, blk * nk)


def _mixer_a_fwd(x, h, w_in, w_out, rel_bias, B, S, tm, next_g):
    proj = mm_nn(h, w_in, name="a_proj")[0].reshape(B, S, -1)
    biases, outs, lses = [], [], []
    for gi, (window, d) in enumerate(A_PATTERNS):
        nb = (S // d) // A_BLOCK
        onehot, (blk, nk) = _bucket_onehot(nb, d)
        bias2 = _pair_tables(bias_table(onehot, rel_bias[:, gi * A_HEADS:(gi + 1) * A_HEADS]), blk, nk)
        o, lse = attn_fwd(proj, bias2, gi, d, window // d)
        biases.append(bias2)
        outs.append(o.reshape(B * S, -1))
        lses.append(lse.reshape(B * S, -1))
    om = merge_fwd(outs, lses, tm)
    x_new, h_next = mm_nn(om, w_out, res=x, norm_g=next_g, name="a_out")
    return x_new, h_next, (x, h, proj, biases, outs, lses, om)


def _mixer_a_bwd(dx, dxb, saved, g, w_in, w_out, B, S, tm, next_scale):
    x, h, proj, biases, outs, lses, om = saved
    dw_out = mm_tn(om, dxb, name="a_dwout")
    dom = mm_nt([(dxb, w_out)], name="a_dom")
    g6 = merge_bwd(outs, lses, dom, tm)
    dparts, dbias_cols = [], []
    for gi, (window, d) in enumerate(A_PATTERNS):
        do = g6[gi].reshape(B, S, -1)
        dl = g6[3 + gi].reshape(B, S, -1)
        dq, dk, dv, dbias2 = attn_bwd(proj, biases[gi], do, dl, gi, d, window // d)
        dparts += [dq, dk, dv]
        onehot, (blk, nk) = _bucket_onehot((S // d) // A_BLOCK, d)
        dbias_cols.append(bias_bucket_grad(onehot, _unpair_tables(dbias2, blk, nk)))
    d_rel_bias = jnp.concatenate(dbias_cols, axis=1)
    dproj = jnp.concatenate(dparts, axis=2).astype(BF16).reshape(B * S, -1)
    dw_in = mm_tn(h, dproj, name="a_dwin")
    dh = mm_nt([(dproj, w_in)], name="a_dh")
    dx_in, dxb_in, dg = rmsnorm_bwd(x, g, dh, dx, tm, next_scale)
    return dx_in, dxb_in, dg, dw_in, dw_out, d_rel_bias


def _mixer_b_fwd(x, h, w_in_p, conv_w, prm, nw, w_out, B, S, tm, next_g):
    proj = mm_nn(h, w_in_p, name="b_proj")[0].reshape(B, S, -1)
    c = conv_fwd(proj, conv_w, 3 * B_WIDTH, 512)
    y, states, tinvs = gdn_fwd(c, proj, prm, nw)
    y2 = y.reshape(B * S, B_WIDTH)
    x_new, h_next = mm_nn(y2, w_out, res=x, norm_g=next_g, name="b_out")
    return x_new, h_next, (x, h, proj, c, states, tinvs, y2)


def _mixer_b_bwd(dx, dxb, saved, g, w_in_p, conv_w, prm, nw, w_out, B, S, tm, next_scale):
    x, h, proj, c, states, tinvs, y2 = saved
    W = B_WIDTH
    dw_out = mm_tn(y2, dxb, name="b_dwout")
    dy = mm_nt([(dxb, w_out)], out_dtype=BF16, name="b_dy").reshape(B, S, W)
    dcq, dck, dcv, dz, dab, dprm, dnw = gdn_bwd(c, proj, prm, nw, states, tinvs, dy)
    dxs, dws = [], []
    for t, dc in enumerate((dcq, dck, dcv)):
        dxp, dwp = conv_bwd(proj, conv_w[:, t * W:(t + 1) * W], dc, t * W, 512)
        dxs.append(dxp)
        dws.append(dwp)
    dconv_w = jnp.concatenate(dws, axis=1)
    dproj = jnp.concatenate(dxs + [dz, dab.astype(BF16)], axis=2).reshape(B * S, -1)
    dw_in_p = mm_tn(h, dproj, name="b_dwin")
    dh = mm_nt([(dproj, w_in_p)], name="b_dh")
    dx_in, dxb_in, dg = rmsnorm_bwd(x, g, dh, dx, tm, next_scale)
    return dx_in, dxb_in, dg, dw_in_p, dconv_w, dprm, dnw, dw_out


SHARD_NAMES = ("ffn_w_gate", "ffn_w_up", "ffn_w_down", "a_w_in", "a_w_out", "b_w_in", "b_w_out", "norm_g", "b_conv_w")
SMALL_NAMES = ("rel_bias", "b_a_log", "b_dt_bias", "b_norm_w", "final_g")


def kernel(x, norm_g, ffn_w_gate, ffn_w_up, ffn_w_down, rel_bias, a_w_in, a_w_out, b_w_in, b_conv_w, b_a_log, b_dt_bias, b_norm_w, b_w_out, final_g, loss_target, m_norm_g, m_ffn_w_gate, m_ffn_w_up, m_ffn_w_down, m_rel_bias, m_a_w_in, m_a_w_out, m_b_w_in, m_b_conv_w, m_b_a_log, m_b_dt_bias, m_b_norm_w, m_b_w_out, m_final_g, v_norm_g, v_ffn_w_gate, v_ffn_w_up, v_ffn_w_down, v_rel_bias, v_a_w_in, v_a_w_out, v_b_w_in, v_b_conv_w, v_b_a_log, v_b_dt_bias, v_b_norm_w, v_b_w_out, v_final_g):
    W = dict(norm_g=norm_g, ffn_w_gate=ffn_w_gate, ffn_w_up=ffn_w_up, ffn_w_down=ffn_w_down, rel_bias=rel_bias,
             a_w_in=a_w_in, a_w_out=a_w_out, b_w_in=b_w_in, b_conv_w=b_conv_w, b_a_log=b_a_log,
             b_dt_bias=b_dt_bias, b_norm_w=b_norm_w, b_w_out=b_w_out, final_g=final_g)
    Mo = dict(norm_g=m_norm_g, ffn_w_gate=m_ffn_w_gate, ffn_w_up=m_ffn_w_up, ffn_w_down=m_ffn_w_down,
              rel_bias=m_rel_bias, a_w_in=m_a_w_in, a_w_out=m_a_w_out, b_w_in=m_b_w_in, b_conv_w=m_b_conv_w,
              b_a_log=m_b_a_log, b_dt_bias=m_b_dt_bias, b_norm_w=m_b_norm_w, b_w_out=m_b_w_out, final_g=m_final_g)
    Vo = dict(norm_g=v_norm_g, ffn_w_gate=v_ffn_w_gate, ffn_w_up=v_ffn_w_up, ffn_w_down=v_ffn_w_down,
              rel_bias=v_rel_bias, a_w_in=v_a_w_in, a_w_out=v_a_w_out, b_w_in=v_b_w_in, b_conv_w=v_b_conv_w,
              b_a_log=v_b_a_log, b_dt_bias=v_b_dt_bias, b_norm_w=v_b_norm_w, b_w_out=v_b_w_out, final_g=v_final_g)

    B, S, D = x.shape
    M = B * S
    tm = _pick(M, (512, 256, 128))
    c_idx = lax.axis_index("c").astype(jnp.int32).reshape(1)
    fp = _lane_pad(ffn_w_gate.shape[3])

    wg_s, wu_s = (_pad_to(t.astype(BF16), 3, fp) for t in (ffn_w_gate, ffn_w_up))
    wd_s = _pad_to(ffn_w_down.astype(BF16), 2, fp)
    a_in_s, a_out_s, b_in_s, b_out_s = (t.astype(BF16) for t in (a_w_in, a_w_out, b_w_in, b_w_out))
    n_b_in = b_w_in.shape[2] * N_DEV

    def ffn_shards(i, k):
        return [wg_s[i, k], wu_s[i, k], wd_s[i, k]]

    def ffn_full(t):
        return _unblock(t[0], 1), _unblock(t[1], 1), _unblock(t[2], 0)

    def mixer_shards(i):
        return [a_in_s[i // 2], a_out_s[i // 2]] if i % 2 == 0 else [b_in_s[i // 2], b_out_s[i // 2]]

    def mixer_full(i, t):
        if i % 2 == 0:
            return _unblock(t[0], 1), _unblock(t[1], 1)
        return _pad_to(_unblock(t[0], 1), 1, n_b_in + B_AB_PAD - 2 * B_HEADS), _unblock(t[1], 0)

    first = allgather_shards(ffn_shards(0, 0) + [norm_g, b_conv_w, mixer_shards(0)[0]])
    w_ffn = ffn_full(first[:3])
    norm_all, conv_all, w_m_in = _unblock(first[3], 2), _unblock(first[4], 2), first[5]
    prm_all = _pad_to(jnp.stack([b_a_log, b_dt_bias], axis=-1)[:, :, None, :], 3, LANES)

    xs = x.reshape(M, D)
    hs = rmsnorm_fwd(xs, norm_all[0, 0][None], tm)
    saved, weights = [], []
    for i in range(DEPTH):
        j = i // 2
        w1 = w_ffn
        xs, hs, s1, got = _ffn_fwd(xs, hs, *w1, norm_all[i, 1][None],
                                   gather=mixer_shards(i)[1:] + ffn_shards(i, 1))
        wm, w2 = mixer_full(i, [w_m_in, got[0]]), ffn_full(got[1:])
        if i % 2 == 0:
            xs, hs, s2 = _mixer_a_fwd(xs, hs, *wm, rel_bias, B, S, tm, norm_all[i, 2][None])
        else:
            xs, hs, s2 = _mixer_b_fwd(xs, hs, wm[0], conv_all[j], prm_all[j], b_norm_w[j][None], wm[1], B, S, tm,
                                      norm_all[i, 2][None])
        nxt = ffn_shards(i + 1, 0) + mixer_shards(i + 1)[:1] if i + 1 < DEPTH else []
        xs, hs, s3, got = _ffn_fwd(xs, hs, *w2, norm_all[i + 1, 0][None] if nxt else None, gather=nxt)
        if nxt:
            w_ffn, w_m_in = ffn_full(got[:3]), got[3]
        saved.append((s1, s2, s3))
        weights.append((w1, wm, w2))

    dx, dxb, d_final_g, loss_row = loss_head(xs, final_g[None], loss_target.reshape(M, D), tm, MACARON_WEIGHT)

    d_prm, d_nw = [None] * 2, [None] * 2
    d_rel = jnp.zeros(rel_bias.shape, F32)
    received = [None] * DEPTH
    pending = []
    for i in reversed(range(DEPTH)):
        j = i // 2
        s1, s2, s3 = saved[i]
        w1, wm, w2 = weights[i]
        dx, dxb, dn2, dwg2, dwu2, dwd2, got_a = _ffn_bwd(dx, dxb, s3, norm_all[i, 2][None], *w2, tm, 1.0,
                                                         exchange=pending[:1])
        if i % 2 == 0:
            dx, dxb, dn1, d_in, d_out, drb = _mixer_a_bwd(dx, dxb, s2, norm_all[i, 1][None], *wm, B, S, tm,
                                                          MACARON_WEIGHT)
            d_rel = d_rel + drb
            mixer_grads = [(d_in, 1), (d_out, 1)]
        else:
            dx, dxb, dn1, dbp, dconv, d_prm[j], d_nw[j], d_out = _mixer_b_bwd(
                dx, dxb, s2, norm_all[i, 1][None], wm[0], conv_all[j], prm_all[j], b_norm_w[j][None], wm[1], B, S, tm,
                MACARON_WEIGHT)
            mixer_grads = [(dbp[:, :n_b_in], 1), (d_out, 0)]
        dx, dxb, dn0, dwg1, dwu1, dwd1, got_b = _ffn_bwd(dx, dxb, s1, norm_all[i, 0][None], *w1, tm, MACARON_WEIGHT,
                                                         exchange=pending[1:])
        if pending:
            received[i + 1] = got_a + got_b
        layer_grads = [(jnp.stack([dwg1, dwg2, dwu1, dwu2]), 2), (jnp.stack([dwd1, dwd2]), 1)] + mixer_grads
        layer_grads.append((jnp.concatenate([dn0, dn1, dn2], axis=0), 1))
        if i % 2 == 1:
            layer_grads.append((dconv, 1))
        gblocks = [_rows3(_block(g, ax).astype(RS_DTYPE)) for g, ax in layer_grads]
        got = exchange_sibling(gblocks)
        pending = [add_sibling(g, r, c_idx) for g, r in zip(gblocks, got)]
    received[0] = exchange_chips(pending)
    grad_x = dx.reshape(B, S, D)

    def over_layers(k, layers, split=None):
        ps = [received[i][k] for i in layers]
        if split is not None:
            ps = [p.reshape((4, split[2], -1, p.shape[2]))[:, split[0]:split[1]] for p in ps]
            return jnp.concatenate(ps, axis=1)
        return jnp.stack(ps, axis=1)

    every, a_layers, b_layers = range(DEPTH), range(0, DEPTH, 2), range(1, DEPTH, 2)
    parts = dict(ffn_w_gate=over_layers(0, every, (0, 2, 4)), ffn_w_up=over_layers(0, every, (2, 4, 4)),
                 ffn_w_down=over_layers(1, every, (0, 2, 2)), a_w_in=over_layers(2, a_layers),
                 a_w_out=over_layers(3, a_layers), b_w_in=over_layers(2, b_layers), b_w_out=over_layers(3, b_layers),
                 norm_g=over_layers(4, every), b_conv_w=over_layers(5, b_layers))
    out_g, out_d, out_m, out_v = {}, {}, {}, {}
    for n in SHARD_NAMES:
        w = W[n]
        w3 = w.reshape((-1,) + w.shape[-2:])
        res = sum_adamw(parts[n], w3, Mo[n].reshape(w3.shape), Vo[n].reshape(w3.shape))
        for dst, r in zip((out_g, out_d, out_m, out_v), res):
            dst[n] = r.reshape(w.shape)

    d_alog = jnp.stack([p[:, 0, 0] for p in d_prm])
    d_dtb = jnp.stack([p[:, 0, 1] for p in d_prm])
    small_g = [d_rel, d_alog, d_dtb, jnp.concatenate(d_nw, axis=0), d_final_g[0], loss_row[0, :1]]
    small_shapes = [W[n].shape for n in SMALL_NAMES]
    sg = allgather_small(_pack(small_g, 8))
    pks = lambda d: _pack([d[n] for n in SMALL_NAMES] + [jnp.zeros((1,), F32)], 8)[None]
    res = sum_adamw(sg[:, None], pks(W), pks(Mo), pks(Vo))
    loss = res[0].reshape(-1)[sum(math.prod(s) for s in small_shapes)]
    for dst, r in zip((out_g, out_d, out_m, out_v), res):
        dst.update(zip(SMALL_NAMES, _unpack(r, small_shapes)))

    order = ("norm_g", "ffn_w_gate", "ffn_w_up", "ffn_w_down", "rel_bias", "a_w_in", "a_w_out", "b_w_in",
             "b_conv_w", "b_a_log", "b_dt_bias", "b_norm_w", "b_w_out", "final_g")
    return (loss, grad_x, *[out_g[n] for n in order], *[out_d[n] for n in order],
            *[out_m[n] for n in order], *[out_v[n] for n in order])
```
